```python
import math
import jax, jax.numpy as jnp
from jax import lax
import numpy as np

D_MODEL = 1024
BATCH = 4
SEQ = 4096
DEPTH = 2

N_MIXERS = 2
CONV_WIDTH = 31
SSM_GROUP = 16
SSM_GROUPS = D_MODEL // SSM_GROUP
SSM_STATE = 64
DT_MIN = 1e-3
DT_MAX = 1e-1
D_FF_DENSE = 2816
N_EXPERTS = 8
TOP_K = 2
D_FF_EXPERT = 3584
NORM_EPS = 1e-6
N_CONV_LAYERS = (DEPTH + 1) // 2
N_SSM_LAYERS = DEPTH // 2

kernel_name = "hybrid_conformer_s5_moe_adaln"

F32 = jnp.float32


def rms_norm(x, g):
    x32 = x.astype(F32)
    y = x32 * lax.rsqrt(jnp.mean(x32 * x32, axis=-1, keepdims=True) + NORM_EPS)
    return (y * g.astype(F32)).astype(x.dtype)


def layer_norm(x, g, b):
    x32 = x.astype(F32)
    mu = jnp.mean(x32, axis=-1, keepdims=True)
    xc = x32 - mu
    var = jnp.mean(xc * xc, axis=-1, keepdims=True)
    return (xc * lax.rsqrt(var + NORM_EPS) * g.astype(F32) + b.astype(F32)).astype(x.dtype)


def conformer_conv(h, w_pw1, b_pw1, w_dw, b_dw, ln_g, ln_b, w_pw2, b_pw2):
    u = h @ w_pw1 + b_pw1
    a, g = jnp.split(u, 2, axis=-1)
    u = a * jax.nn.sigmoid(g)
    u = lax.conv_general_dilated(
        u, w_dw[:, None, :].astype(u.dtype),
        window_strides=(1,), padding=[(CONV_WIDTH - 1, 0)],
        dimension_numbers=("NWC", "WIO", "NWC"),
        feature_group_count=D_MODEL) + b_dw
    u = jax.nn.silu(layer_norm(u, ln_g, ln_b))
    return u @ w_pw2 + b_pw2


def _ssm_combine(left, right):
    ar_i, ai_i, br_i, bi_i = left
    ar_j, ai_j, br_j, bi_j = right
    ar = ar_j * ar_i - ai_j * ai_i
    ai = ar_j * ai_i + ai_j * ar_i
    br = ar_j * br_i - ai_j * bi_i + br_j
    bi = ar_j * bi_i + ai_j * br_i + bi_j
    return (ar, ai, br, bi)


def s5_layer(h, a_re, a_im, log_dt, b_re, b_im, c_re, c_im, d_skip, w_glu, b_glu):
    nb, ns, _ = h.shape
    h32 = h.astype(F32)
    u = h32.reshape(nb, ns, SSM_GROUPS, SSM_GROUP)
    dt = jnp.exp(log_dt.astype(F32))[:, None]
    lam_re = a_re.astype(F32)
    lam_im = a_im.astype(F32)
    mag = jnp.exp(dt * lam_re)
    ab_re = mag * jnp.cos(dt * lam_im)
    ab_im = mag * jnp.sin(dt * lam_im)
    den = lam_re * lam_re + lam_im * lam_im
    f_re = ((ab_re - 1.0) * lam_re + ab_im * lam_im) / den
    f_im = (ab_im * lam_re - (ab_re - 1.0) * lam_im) / den
    br = b_re.astype(F32)
    bim = b_im.astype(F32)
    bb_re = f_re[..., None] * br - f_im[..., None] * bim
    bb_im = f_re[..., None] * bim + f_im[..., None] * br
    bu_re = jnp.einsum("bsgc,gpc->bsgp", u, bb_re)
    bu_im = jnp.einsum("bsgc,gpc->bsgp", u, bb_im)
    a_seq_re = jnp.broadcast_to(ab_re[None, None], (1, ns, SSM_GROUPS, SSM_STATE))
    a_seq_im = jnp.broadcast_to(ab_im[None, None], (1, ns, SSM_GROUPS, SSM_STATE))
    _, _, st_re, st_im = lax.associative_scan(
        _ssm_combine, (a_seq_re, a_seq_im, bu_re, bu_im), axis=1)
    y = (jnp.einsum("bsgp,gcp->bsgc", st_re, c_re.astype(F32))
         - jnp.einsum("bsgp,gcp->bsgc", st_im, c_im.astype(F32)))
    y = y.reshape(nb, ns, D_MODEL) + d_skip.astype(F32) * h32
    y = jax.nn.gelu(y)
    z = y @ w_glu.astype(F32) + b_glu.astype(F32)
    za, zg = jnp.split(z, 2, axis=-1)
    return (za * jax.nn.sigmoid(zg)).astype(h.dtype)


def swiglu(t, w_gate, w_up, w_down):
    return (jax.nn.silu(t @ w_gate) * (t @ w_up)) @ w_down


def moe_swiglu(h, w_router, b_router, w_gate, w_up, w_down):
    nb, ns, d = h.shape
    t = h.reshape(nb * ns, d)
    logits = (t @ w_router + b_router).astype(F32)
    top_vals, top_idx = lax.top_k(logits, TOP_K)
    top_w = jax.nn.softmax(top_vals, axis=-1)
    gates = jnp.sum(jax.nn.one_hot(top_idx, N_EXPERTS, dtype=F32) * top_w[..., None], axis=1)
    out = jnp.zeros((nb * ns, d), F32)
    for e in range(N_EXPERTS):
        out = out + gates[:, e:e + 1] * swiglu(t, w_gate[e], w_up[e], w_down[e]).astype(F32)
    return out.reshape(nb, ns, d).astype(h.dtype)


def setup_inputs(seed: int = 0) -> dict:
    key = jax.random.key(seed)
    ks = iter(jax.random.split(key, 40))
    D, G, P = D_MODEL, SSM_GROUPS, SSM_STATE

    def nrm(shape, scale):
        return jax.random.normal(next(ks), shape, F32) * scale

    def gain(shape):
        return 1.0 + nrm(shape, 0.01)

    nA, nB = N_CONV_LAYERS, N_SSM_LAYERS
    inp = {}
    inp["x"] = nrm((BATCH, SEQ, D), 1.0)
    inp["c"] = nrm((BATCH, D), 1.0)
    inp["mod_w"] = nrm((DEPTH, D, 6 * D), D ** -0.5)
    inp["mod_b"] = nrm((DEPTH, 6 * D), 0.01)
    inp["norm1_g"] = gain((DEPTH, D))
    inp["norm2_g"] = gain((DEPTH, D))
    inp["conv_w_pw1"] = nrm((nA, D, 2 * D), D ** -0.5)
    inp["conv_b_pw1"] = nrm((nA, 2 * D), 0.01)
    inp["conv_w_dw"] = nrm((nA, CONV_WIDTH, D), CONV_WIDTH ** -0.5)
    inp["conv_b_dw"] = nrm((nA, D), 0.01)
    inp["conv_ln_g"] = gain((nA, D))
    inp["conv_ln_b"] = nrm((nA, D), 0.01)
    inp["conv_w_pw2"] = nrm((nA, D, D), D ** -0.5)
    inp["conv_b_pw2"] = nrm((nA, D), 0.01)
    inp["ssm_a_re"] = -0.5 * jnp.exp(nrm((nB, G, P), 0.01))
    inp["ssm_a_im"] = math.pi * jnp.arange(P, dtype=F32)[None, None, :] + nrm((nB, G, P), 0.01)
    inp["ssm_log_dt"] = jax.random.uniform(next(ks), (nB, G), F32,
                                           math.log(DT_MIN), math.log(DT_MAX))
    inp["ssm_b_re"] = nrm((nB, G, P, SSM_GROUP), (2 * SSM_GROUP) ** -0.5)
    inp["ssm_b_im"] = nrm((nB, G, P, SSM_GROUP), (2 * SSM_GROUP) ** -0.5)
    inp["ssm_c_re"] = nrm((nB, G, SSM_GROUP, P), (2 * P) ** -0.5)
    inp["ssm_c_im"] = nrm((nB, G, SSM_GROUP, P), (2 * P) ** -0.5)
    inp["ssm_d"] = nrm((nB, D), 1.0)
    inp["ssm_w_glu"] = nrm((nB, D, 2 * D), D ** -0.5)
    inp["ssm_b_glu"] = nrm((nB, 2 * D), 0.01)
    inp["ffn_w_gate"] = nrm((nA, D, D_FF_DENSE), D ** -0.5)
    inp["ffn_w_up"] = nrm((nA, D, D_FF_DENSE), D ** -0.5)
    inp["ffn_w_down"] = nrm((nA, D_FF_DENSE, D), D_FF_DENSE ** -0.5)
    inp["moe_w_router"] = nrm((nB, D, N_EXPERTS), D ** -0.5)
    inp["moe_b_router"] = nrm((nB, N_EXPERTS), 0.01)
    inp["moe_w_gate"] = nrm((nB, N_EXPERTS, D, D_FF_EXPERT), D ** -0.5)
    inp["moe_w_up"] = nrm((nB, N_EXPERTS, D, D_FF_EXPERT), D ** -0.5)
    inp["moe_w_down"] = nrm((nB, N_EXPERTS, D_FF_EXPERT, D), D_FF_EXPERT ** -0.5)
    inp["final_norm_g"] = gain((D,))
    return inp


def reference(x, c, mod_w, mod_b, norm1_g, norm2_g,
              conv_w_pw1, conv_b_pw1, conv_w_dw, conv_b_dw, conv_ln_g, conv_ln_b,
              conv_w_pw2, conv_b_pw2,
              ssm_a_re, ssm_a_im, ssm_log_dt, ssm_b_re, ssm_b_im, ssm_c_re, ssm_c_im,
              ssm_d, ssm_w_glu, ssm_b_glu,
              ffn_w_gate, ffn_w_up, ffn_w_down,
              moe_w_router, moe_b_router, moe_w_gate, moe_w_up, moe_w_down,
              final_norm_g):
    cond = jax.nn.silu(c.astype(F32))
    h = x
    for i in range(DEPTH):
        j = i // N_MIXERS
        mod = cond @ mod_w[i].astype(F32) + mod_b[i].astype(F32)
        sh1, sc1, g1, sh2, sc2, g2 = jnp.split(mod[:, None, :], 6, axis=-1)

        y = (rms_norm(h, norm1_g[i]).astype(F32) * (1.0 + sc1) + sh1).astype(h.dtype)
        if i % N_MIXERS == 0:
            y = conformer_conv(y, conv_w_pw1[j], conv_b_pw1[j], conv_w_dw[j], conv_b_dw[j],
                               conv_ln_g[j], conv_ln_b[j], conv_w_pw2[j], conv_b_pw2[j])
        else:
            y = s5_layer(y, ssm_a_re[j], ssm_a_im[j], ssm_log_dt[j], ssm_b_re[j], ssm_b_im[j],
                         ssm_c_re[j], ssm_c_im[j], ssm_d[j], ssm_w_glu[j], ssm_b_glu[j])
        h = (h.astype(F32) + g1 * y.astype(F32)).astype(x.dtype)

        y = (rms_norm(h, norm2_g[i]).astype(F32) * (1.0 + sc2) + sh2).astype(h.dtype)
        if i % 2 == 0:
            y = swiglu(y, ffn_w_gate[j], ffn_w_up[j], ffn_w_down[j])
        else:
            y = moe_swiglu(y, moe_w_router[j], moe_b_router[j],
                           moe_w_gate[j], moe_w_up[j], moe_w_down[j])
        h = (h.astype(F32) + g2 * y.astype(F32)).astype(x.dtype)
    return rms_norm(h, final_norm_g)
```

```python
import functools

import jax
import jax.numpy as jnp
from jax import lax
from jax.experimental import pallas as pl
from jax.experimental.pallas import tpu as pltpu

F32 = jnp.float32
BF16 = jnp.bfloat16
HIGHEST = lax.Precision.HIGHEST

NORM_EPS = 1e-6
TOP_K = 2
LANES = 128
S5_CHUNK = 16
CONV_HALO = 32
CONV_ROWS = 32
VMEM_LIMIT = 56 * 2**20


def _params(sem):
    return pltpu.CompilerParams(dimension_semantics=sem, vmem_limit_bytes=VMEM_LIMIT)


def _divisor(n, cap, mult):
    best = None
    for d in range(mult, min(n, cap) + 1, mult):
        if n % d == 0:
            best = d
    assert best is not None, (n, cap, mult)
    return best


def _resident(shape):
    nd = len(shape)
    return pl.BlockSpec(shape, lambda *_: (0,) * nd, pipeline_mode=pl.Buffered(1))


def _rms_mod(x, g, sc, sh):
    ms = jnp.mean(x * x, axis=-1, keepdims=True)
    return (x * lax.rsqrt(ms + NORM_EPS) * g) * (1.0 + sc) + sh


def _silu(x):
    return x * jax.nn.sigmoid(x)


def _gelu_tanh(x):
    c = 0.7978845608028654
    return 0.5 * x * (1.0 + jnp.tanh(c * (x + 0.044715 * (x * x * x))))


def _mod_kernel(c_ref, w_ref, b_ref, o_ref):
    cond = _silu(c_ref[...])
    o_ref[0] = jnp.dot(cond, w_ref[0], preferred_element_type=F32, precision=HIGHEST) + b_ref[0]


def _modulation(c, mod_w, mod_b):
    depth, d, d6 = mod_w.shape
    nb = c.shape[0]
    rows = -(-nb // 8) * 8
    c8 = jnp.pad(c, ((0, rows - nb), (0, 0)))
    tn = _divisor(d6, 1536, LANES)
    out = pl.pallas_call(
        _mod_kernel,
        grid=(depth, d6 // tn),
        in_specs=[
            pl.BlockSpec((rows, d), lambda i, j: (0, 0)),
            pl.BlockSpec((1, d, tn), lambda i, j: (i, 0, j)),
            pl.BlockSpec((1, 1, tn), lambda i, j: (i, 0, j)),
        ],
        out_specs=pl.BlockSpec((1, rows, tn), lambda i, j: (i, 0, j)),
        out_shape=jax.ShapeDtypeStruct((depth, rows, d6), F32),
        compiler_params=_params(("arbitrary", "arbitrary")),
        name="adaln_mod",
    )(c8, mod_w, mod_b.reshape(depth, 1, d6))
    return out[:, :nb, :]


def _pw1_kernel(x_ref, g_ref, sc_ref, sh_ref, w_ref, b_ref, o_ref):
    x = x_ref[...]
    d = x.shape[1]
    y = _rms_mod(x, g_ref[...], sc_ref[0], sh_ref[0])
    u = jnp.dot(y.astype(BF16), w_ref[...], preferred_element_type=F32) + b_ref[...]
    o_ref[...] = u[:, :d] * jax.nn.sigmoid(u[:, d:])


def _pw1(x2, norm_g, sc, sh, w, b, seq):
    n, d = x2.shape
    tm = _divisor(seq, 512, 8)
    tpb = seq // tm
    bmap = lambda i: (i // tpb, 0, 0)
    return pl.pallas_call(
        _pw1_kernel,
        grid=(n // tm,),
        in_specs=[
            pl.BlockSpec((tm, d), lambda i: (i, 0)),
            _resident((1, d)),
            pl.BlockSpec((1, 1, d), bmap),
            pl.BlockSpec((1, 1, d), bmap),
            _resident((d, 2 * d)),
            _resident((1, 2 * d)),
        ],
        out_specs=pl.BlockSpec((tm, d), lambda i: (i, 0)),
        out_shape=jax.ShapeDtypeStruct((n, d), F32),
        compiler_params=_params(("arbitrary",)),
        name="conv_pw1_glu",
    )(x2, norm_g.reshape(1, d), sc, sh, w.astype(BF16), b.reshape(1, 2 * d))


def _conv_kernel(cur_ref, prev_ref, wdw_ref, bdw_ref, lng_ref, lnb_ref, w2_ref, b2_ref, h_ref, g1_ref,
                 o_ref, buf_ref, cv_ref, *, taps):
    tm, d = cv_ref.shape
    i = pl.program_id(1)
    buf_ref[0:CONV_HALO, :] = jnp.where(i > 0, prev_ref[0], 0.0)
    buf_ref[CONV_HALO:, :] = cur_ref[0]
    off0 = CONV_HALO - (taps - 1)
    span = CONV_ROWS + CONV_HALO
    for c in range(d // LANES):
        lanes = slice(c * LANES, (c + 1) * LANES)

        def body(k, carry, lanes=lanes):
            r0 = pl.multiple_of(k * CONV_ROWS, CONV_ROWS)
            v = buf_ref[pl.ds(r0, span), lanes]
            acc = jnp.broadcast_to(bdw_ref[:, lanes], (CONV_ROWS, LANES))
            for s in range(8):
                xs = v if s == 0 else pltpu.roll(v, span - s, axis=0)
                for q in range(span // 8):
                    o = 8 * q + s
                    if o < off0 or o > off0 + taps - 1:
                        continue
                    wrow = wdw_ref[o - off0:o - off0 + 1, lanes]
                    acc = acc + wrow * xs[8 * q:8 * q + CONV_ROWS, :]
            cv_ref[pl.ds(r0, CONV_ROWS), lanes] = acc
            return carry

        lax.fori_loop(0, tm // CONV_ROWS, body, 0)
    v = cv_ref[...]
    mu = jnp.mean(v, axis=-1, keepdims=True)
    xc = v - mu
    var = jnp.mean(xc * xc, axis=-1, keepdims=True)
    y = _silu(xc * lax.rsqrt(var + NORM_EPS) * lng_ref[...] + lnb_ref[...])
    z = jnp.dot(y.astype(BF16), w2_ref[...], preferred_element_type=F32) + b2_ref[...]
    o_ref[0] = h_ref[0] + g1_ref[0] * z


def _conv_block(u3, x3, w_dw, b_dw, ln_g, ln_b, w2, b2, g1):
    nb, seq, d = x3.shape
    taps = w_dw.shape[0]
    assert taps - 1 <= CONV_HALO and d % LANES == 0
    tm = _divisor(seq, 512, CONV_ROWS)
    hb = tm // CONV_HALO
    wpad = jnp.pad(w_dw, ((0, -taps % 8), (0, 0)))
    row = lambda a: a.reshape(1, d)
    return pl.pallas_call(
        functools.partial(_conv_kernel, taps=taps),
        grid=(nb, seq // tm),
        in_specs=[
            pl.BlockSpec((1, tm, d), lambda b, i: (b, i, 0)),
            pl.BlockSpec((1, CONV_HALO, d), lambda b, i: (b, jnp.maximum(i * hb - 1, 0), 0)),
            _resident(wpad.shape),
            _resident((1, d)),
            _resident((1, d)),
            _resident((1, d)),
            _resident((d, d)),
            _resident((1, d)),
            pl.BlockSpec((1, tm, d), lambda b, i: (b, i, 0)),
            pl.BlockSpec((1, 1, d), lambda b, i: (b, 0, 0)),
        ],
        out_specs=pl.BlockSpec((1, tm, d), lambda b, i: (b, i, 0)),
        out_shape=jax.ShapeDtypeStruct((nb, seq, d), F32),
        scratch_shapes=[pltpu.VMEM((tm + CONV_HALO, d), F32), pltpu.VMEM((tm, d), F32)],
        compiler_params=_params(("arbitrary", "arbitrary")),
        name="conv_dw_ln_pw2",
    )(u3, u3, wpad, row(b_dw), row(ln_g), row(ln_b), w2.astype(BF16), row(b2), x3, g1)


def _ffn_kernel(h_ref, n2g_ref, sc_ref, sh_ref, g2_ref, wg_ref, wu_ref, wd_ref, n1g_ref, sc1_ref, sh1_ref,
                h2_ref, u_ref, *, fchunk):
    h = h_ref[...]
    t = _rms_mod(h, n2g_ref[...], sc_ref[0], sh_ref[0]).astype(BF16)
    dff = wg_ref.shape[1]
    acc = None
    for f0 in range(0, dff, fchunk):
        f1 = min(f0 + fchunk, dff)
        g = jnp.dot(t, wg_ref[:, f0:f1], preferred_element_type=F32)
        u = jnp.dot(t, wu_ref[:, f0:f1], preferred_element_type=F32)
        a = (_silu(g) * u).astype(BF16)
        y = jnp.dot(a, wd_ref[f0:f1, :], preferred_element_type=F32)
        acc = y if acc is None else acc + y
    h2 = h + g2_ref[0] * acc
    h2_ref[...] = h2
    u_ref[0] = _rms_mod(h2, n1g_ref[...], sc1_ref[0], sh1_ref[0])


def _dense_ffn(hv, n2g, sc2, sh2, g2, wg, wu, wd, n1g, sc1, sh1, rows_per_batch):
    nr, ld = hv.shape
    d = ld // S5_CHUNK
    dff = wg.shape[1]
    tmr = _divisor(rows_per_batch, 256, 8)
    tpb = rows_per_batch // tmr
    bmap = lambda i, l: (i // tpb, 0, 0)
    return pl.pallas_call(
        functools.partial(_ffn_kernel, fchunk=512),
        grid=(nr // tmr, S5_CHUNK),
        in_specs=[
            pl.BlockSpec((tmr, d), lambda i, l: (i, l)),
            _resident((1, d)),
            pl.BlockSpec((1, 1, d), bmap),
            pl.BlockSpec((1, 1, d), bmap),
            pl.BlockSpec((1, 1, d), bmap),
            _resident((d, dff)),
            _resident((d, dff)),
            _resident((dff, d)),
            _resident((1, d)),
            pl.BlockSpec((1, 1, d), bmap),
            pl.BlockSpec((1, 1, d), bmap),
        ],
        out_specs=[
            pl.BlockSpec((tmr, d), lambda i, l: (i, l)),
            pl.BlockSpec((1, tmr, d), lambda i, l: (l, i, 0)),
        ],
        out_shape=[
            jax.ShapeDtypeStruct((nr, ld), F32),
            jax.ShapeDtypeStruct((S5_CHUNK, nr, d), F32),
        ],
        compiler_params=_params(("arbitrary", "arbitrary")),
        name="dense_swiglu",
    )(hv, n2g.reshape(1, d), sc2, sh2, g2, wg.astype(BF16), wu.astype(BF16), wd.astype(BF16),
      n1g.reshape(1, d), sc1, sh1)


def _s5_tables(a_re, a_im, log_dt, b_re, b_im, c_re, c_im, d_skip):
    ng, p = a_re.shape
    c = b_re.shape[-1]
    gpb = LANES // c
    nblk = ng // gpb
    L = S5_CHUNK
    dt = jnp.exp(log_dt)[:, None]
    mag = jnp.exp(dt * a_re)
    ab_re = mag * jnp.cos(dt * a_im)
    ab_im = mag * jnp.sin(dt * a_im)
    den = a_re * a_re + a_im * a_im
    f_re = ((ab_re - 1.0) * a_re + ab_im * a_im) / den
    f_im = (ab_im * a_re - (ab_re - 1.0) * a_im) / den
    bb_re = f_re[..., None] * b_re - f_im[..., None] * b_im
    bb_im = f_re[..., None] * b_im + f_im[..., None] * b_re
    pr, pi = [jnp.ones_like(ab_re)], [jnp.zeros_like(ab_re)]
    for _ in range(L):
        pr, pi = pr + [pr[-1] * ab_re - pi[-1] * ab_im], pi + [pr[-1] * ab_im + pi[-1] * ab_re]
    pw_re, pw_im = jnp.stack(pr), jnp.stack(pi)
    ca_re = c_re[None] * pw_re[:, :, None, :] - c_im[None] * pw_im[:, :, None, :]
    ca_im = c_re[None] * pw_im[:, :, None, :] + c_im[None] * pw_re[:, :, None, :]
    kt = (jnp.einsum("jgop,gpc->jgco", ca_re[:L], bb_re, precision=HIGHEST)
          - jnp.einsum("jgop,gpc->jgco", ca_im[:L], bb_im, precision=HIGHEST))
    lag = jnp.arange(L)[None, :] - jnp.arange(L)[:, None]
    ktoe = jnp.where((lag >= 0)[:, :, None, None, None], kt[jnp.clip(lag, 0, L - 1)], 0.0)
    eye = jnp.eye(gpb, dtype=F32)
    ktoe = ktoe.reshape(L, L, nblk, gpb, c, c)
    t_blk = jnp.einsum("mlbgic,gh->bmgilhc", ktoe, eye).reshape(nblk, L * LANES, L * LANES)
    qr = jnp.stack([pr[L - 1 - l] for l in range(L)])
    qi = jnp.stack([pi[L - 1 - l] for l in range(L)])
    we_re = qr[..., None] * bb_re[None] - qi[..., None] * bb_im[None]
    we_im = qr[..., None] * bb_im[None] + qi[..., None] * bb_re[None]
    we = jnp.stack([we_re, we_im]).reshape(2, L, nblk, gpb, p, c)
    we_blk = jnp.einsum("rlbgpc,gh->blgcrhp", we, eye).reshape(nblk, L * LANES, 2 * gpb * p)
    vv = jnp.stack([ca_re[1:], -ca_im[1:]]).reshape(2, L, nblk, gpb, c, p)
    v_blk = jnp.einsum("rlbgop,gh->brgplho", vv, eye).reshape(nblk, 2 * gpb * p, L * LANES)
    a_blk = jnp.concatenate([pw_re[L].reshape(nblk, gpb * p), pw_im[L].reshape(nblk, gpb * p)], axis=-1)
    d_blk = jnp.tile(d_skip.reshape(nblk, LANES), (1, L))
    return (t_blk.astype(BF16), we_blk.astype(BF16), v_blk.astype(BF16),
            a_blk.reshape(nblk, 1, -1), d_blk.reshape(nblk, 1, -1))


def _s5_kernel(u_ref, t_ref, we_ref, v_ref, a_ref, d_ref, o_ref, e_ref, hin_ref):
    L = u_ref.shape[0]
    rows, ew = e_ref.shape
    ph = ew // 2
    xcat = jnp.concatenate([u_ref[l] for l in range(L)], axis=1)
    xb = xcat.astype(BF16)
    e_ref[...] = jnp.dot(xb, we_ref[0], preferred_element_type=F32)
    ar = a_ref[0][:, :ph]
    ai = a_ref[0][:, ph:]

    def body(r, carry):
        hr, hi = carry
        hin_ref[pl.ds(r, 1), 0:ph] = hr
        hin_ref[pl.ds(r, 1), ph:ew] = hi
        er = e_ref[pl.ds(r, 1), 0:ph]
        ei = e_ref[pl.ds(r, 1), ph:ew]
        return (ar * hr - ai * hi + er, ar * hi + ai * hr + ei)

    zero = jnp.zeros((1, ph), F32)
    lax.fori_loop(0, rows, body, (zero, zero))
    y = jnp.dot(xb, t_ref[0], preferred_element_type=F32)
    y = y + jnp.dot(hin_ref[...].astype(BF16), v_ref[0], preferred_element_type=F32)
    y = _gelu_tanh(y + d_ref[0] * xcat)
    for l in range(L):
        o_ref[l] = y[:, l * LANES:(l + 1) * LANES].astype(o_ref.dtype)


def _s5_mix(u16, tables, nbatch):
    L, nr, d = u16.shape
    t_blk, we_blk, v_blk, a_blk, d_blk = tables
    nblk = t_blk.shape[0]
    rows = nr // nbatch
    ew = we_blk.shape[2]
    return pl.pallas_call(
        _s5_kernel,
        grid=(nblk, nbatch),
        in_specs=[
            pl.BlockSpec((L, rows, LANES), lambda g, b: (0, b, g)),
            pl.BlockSpec((1,) + t_blk.shape[1:], lambda g, b: (g, 0, 0)),
            pl.BlockSpec((1,) + we_blk.shape[1:], lambda g, b: (g, 0, 0)),
            pl.BlockSpec((1,) + v_blk.shape[1:], lambda g, b: (g, 0, 0)),
            pl.BlockSpec((1, 1, ew), lambda g, b: (g, 0, 0)),
            pl.BlockSpec((1, 1, L * LANES), lambda g, b: (g, 0, 0)),
        ],
        out_specs=pl.BlockSpec((L, rows, LANES), lambda g, b: (0, b, g)),
        out_shape=jax.ShapeDtypeStruct((L, nr, d), BF16),
        scratch_shapes=[pltpu.VMEM((rows, ew), F32), pltpu.VMEM((rows, ew), F32)],
        compiler_params=_params(("arbitrary", "arbitrary")),
        name="s5_chunked_scan",
    )(u16, t_blk, we_blk, v_blk, a_blk, d_blk)


def _glu_kernel(y_ref, w_ref, b_ref, h_ref, g1_ref, n2g_ref, sc_ref, sh_ref, wr_ref, br_ref,
                h3_ref, t_ref, rt_ref):
    d = h_ref.shape[1]
    z = jnp.dot(y_ref[0], w_ref[...], preferred_element_type=F32) + b_ref[...]
    h3 = h_ref[...] + g1_ref[0] * (z[:, :d] * jax.nn.sigmoid(z[:, d:]))
    h3_ref[...] = h3
    t = _rms_mod(h3, n2g_ref[...], sc_ref[0], sh_ref[0])
    t_ref[...] = t
    logits = jnp.dot(t, wr_ref[...], preferred_element_type=F32, precision=HIGHEST) + br_ref[...]
    lane = lax.broadcasted_iota(jnp.int32, logits.shape, 1).astype(F32)
    big = float(LANES)
    m1 = jnp.max(logits, axis=-1, keepdims=True)
    i1 = jnp.min(jnp.where(logits == m1, lane, big), axis=-1, keepdims=True)
    rest = jnp.where(lane == i1, -jnp.inf, logits)
    m2 = jnp.max(rest, axis=-1, keepdims=True)
    i2 = jnp.min(jnp.where(rest == m2, lane, big), axis=-1, keepdims=True)
    e2 = jnp.exp(m2 - m1)
    den = 1.0 + e2
    rt_ref[...] = jnp.where(lane == 0.0, 1.0 / den,
                            jnp.where(lane == 1.0, e2 / den,
                                      jnp.where(lane == 2.0, i1, jnp.where(lane == 3.0, i2, 0.0))))


def _glu_router(y16, w_glu, b_glu, hv, g1, n2g, sc2, sh2, w_router, b_router, rows_per_batch):
    L, nr, d = y16.shape
    ne = w_router.shape[1]
    assert ne <= LANES
    tmr = _divisor(rows_per_batch, 256, 8)
    tpb = rows_per_batch // tmr
    bmap = lambda i, l: (i // tpb, 0, 0)
    wr = jnp.pad(w_router, ((0, 0), (0, LANES - ne)))
    br = jnp.pad(b_router, (0, LANES - ne), constant_values=-1e30).reshape(1, LANES)
    return pl.pallas_call(
        _glu_kernel,
        grid=(nr // tmr, L),
        in_specs=[
            pl.BlockSpec((1, tmr, d), lambda i, l: (l, i, 0)),
            _resident((d, 2 * d)),
            _resident((1, 2 * d)),
            pl.BlockSpec((tmr, d), lambda i, l: (i, l)),
            pl.BlockSpec((1, 1, d), bmap),
            _resident((1, d)),
            pl.BlockSpec((1, 1, d), bmap),
            pl.BlockSpec((1, 1, d), bmap),
            _resident((d, LANES)),
            _resident((1, LANES)),
        ],
        out_specs=[
            pl.BlockSpec((tmr, d), lambda i, l: (i, l)),
            pl.BlockSpec((tmr, d), lambda i, l: (i, l)),
            pl.BlockSpec((tmr, LANES), lambda i, l: (i, l)),
        ],
        out_shape=[
            jax.ShapeDtypeStruct((nr, L * d), F32),
            jax.ShapeDtypeStruct((nr, L * d), F32),
            jax.ShapeDtypeStruct((nr, L * LANES), F32),
        ],
        compiler_params=_params(("arbitrary", "arbitrary")),
        name="s5_glu_router",
    )(y16, w_glu.astype(BF16), b_glu.reshape(1, 2 * d), hv, g1, n2g.reshape(1, d), sc2, sh2, wr, br)


def _dispatch_kernel(dest_ref, t_ref, xs_in_ref, xs_ref, sem):
    del xs_in_ref
    td = t_ref.shape[0]

    def row_copy(r, k):
        dst = dest_ref[0, 0, TOP_K * r + k]
        return pltpu.make_async_copy(t_ref.at[pl.ds(r, 1), :], xs_ref.at[pl.ds(dst, 1), :], sem)

    def start(r, c):
        for k in range(TOP_K):
            row_copy(r, k).start()
        return c

    def wait(r, c):
        for k in range(TOP_K):
            row_copy(r, k).wait()
        return c

    lax.fori_loop(0, td, start, 0)
    lax.fori_loop(0, td, wait, 0)


def _dispatch(t2, dest, nrows):
    n, d = t2.shape
    td = _divisor(n, 256, 8)
    return pl.pallas_call(
        _dispatch_kernel,
        grid=(n // td,),
        in_specs=[
            pl.BlockSpec((1, 1, TOP_K * td), lambda i: (i, 0, 0), memory_space=pltpu.SMEM),
            pl.BlockSpec((td, d), lambda i: (i, 0)),
            pl.BlockSpec(memory_space=pl.ANY),
        ],
        out_specs=pl.BlockSpec(memory_space=pl.ANY),
        out_shape=jax.ShapeDtypeStruct((nrows, d), F32),
        scratch_shapes=[pltpu.SemaphoreType.DMA(())],
        input_output_aliases={2: 0},
        compiler_params=_params(("arbitrary",)),
        name="moe_dispatch",
    )(dest.reshape(n // td, 1, TOP_K * td), t2, jnp.zeros((nrows, d), F32))


def _expert_kernel(te_ref, tv_ref, x_ref, wg_ref, wu_ref, wd_ref, o_ref, xb_ref, acc_ref, *, fchunk):
    del te_ref
    i = pl.program_id(0)
    f = pl.program_id(1)
    nf = pl.num_programs(1)

    @pl.when(jnp.logical_and(tv_ref[i] == 0, f == nf - 1))
    def _():
        o_ref[...] = jnp.zeros_like(o_ref)

    @pl.when(tv_ref[i] == 1)
    def _():
        @pl.when(f == 0)
        def _():
            xb_ref[...] = x_ref[...].astype(BF16)

        x = xb_ref[...]
        tf = wg_ref.shape[2]
        y = None
        for f0 in range(0, tf, fchunk):
            f1 = min(f0 + fchunk, tf)
            g = jnp.dot(x, wg_ref[0, :, f0:f1], preferred_element_type=F32)
            u = jnp.dot(x, wu_ref[0, :, f0:f1], preferred_element_type=F32)
            a = (_silu(g) * u).astype(BF16)
            yy = jnp.dot(a, wd_ref[0, f0:f1, :], preferred_element_type=F32)
            y = yy if y is None else y + yy

        @pl.when(f == 0)
        def _():
            acc_ref[...] = y

        @pl.when(f > 0)
        def _():
            acc_ref[...] += y

        @pl.when(f == nf - 1)
        def _():
            o_ref[...] = acc_ref[...]


def _experts(xs, tile_expert, tile_valid, wg, wu, wd, tm):
    nrows, d = xs.shape
    dff = wg.shape[2]
    tf = _divisor(dff, 1792, 256)
    nf = dff // tf
    n_tiles = nrows // tm
    fidx = lambda i, f, tv: f * tv[i] + (nf - 1) * (1 - tv[i])
    grid_spec = pltpu.PrefetchScalarGridSpec(
        num_scalar_prefetch=2,
        grid=(n_tiles, nf),
        in_specs=[
            pl.BlockSpec((tm, d), lambda i, f, te, tv: (i, 0)),
            pl.BlockSpec((1, d, tf), lambda i, f, te, tv: (te[i], 0, fidx(i, f, tv))),
            pl.BlockSpec((1, d, tf), lambda i, f, te, tv: (te[i], 0, fidx(i, f, tv))),
            pl.BlockSpec((1, tf, d), lambda i, f, te, tv: (te[i], fidx(i, f, tv), 0)),
        ],
        out_specs=pl.BlockSpec((tm, d), lambda i, f, te, tv: (i, 0)),
        scratch_shapes=[pltpu.VMEM((tm, d), BF16), pltpu.VMEM((tm, d), F32)],
    )
    return pl.pallas_call(
        functools.partial(_expert_kernel, fchunk=1024),
        grid_spec=grid_spec,
        out_shape=jax.ShapeDtypeStruct((nrows, d), F32),
        compiler_params=_params(("arbitrary", "arbitrary")),
        name="moe_experts",
    )(tile_expert, tile_valid, xs, wg, wu, wd)


def _combine_kernel(dest_ref, ys_ref, h_ref, rt_ref, g2_ref, fg_ref, o_ref, ybuf_ref, sem):
    tc = h_ref.shape[0]

    def row_copy(r, k):
        src = dest_ref[0, 0, TOP_K * r + k]
        return pltpu.make_async_copy(ys_ref.at[pl.ds(src, 1), :], ybuf_ref.at[pl.ds(k * tc + r, 1), :], sem)

    def start(r, c):
        for k in range(TOP_K):
            row_copy(r, k).start()
        return c

    def wait(r, c):
        for k in range(TOP_K):
            row_copy(r, k).wait()
        return c

    lax.fori_loop(0, tc, start, 0)
    lax.fori_loop(0, tc, wait, 0)
    rt = rt_ref[...]
    y = rt[:, 0:1] * ybuf_ref[0:tc, :]
    for k in range(1, TOP_K):
        y = y + rt[:, k:k + 1] * ybuf_ref[k * tc:(k + 1) * tc, :]
    h4 = h_ref[...] + g2_ref[0] * y
    ms = jnp.mean(h4 * h4, axis=-1, keepdims=True)
    o_ref[...] = h4 * lax.rsqrt(ms + NORM_EPS) * fg_ref[...]


def _combine(ys, dest, h2d, rt, g2, final_g, seq):
    n, d = h2d.shape
    tc = _divisor(seq, 256, 8)
    tpb = seq // tc
    return pl.pallas_call(
        _combine_kernel,
        grid=(n // tc,),
        in_specs=[
            pl.BlockSpec((1, 1, TOP_K * tc), lambda i: (i, 0, 0), memory_space=pltpu.SMEM),
            pl.BlockSpec(memory_space=pl.ANY),
            pl.BlockSpec((tc, d), lambda i: (i, 0)),
            pl.BlockSpec((tc, LANES), lambda i: (i, 0)),
            pl.BlockSpec((1, 1, d), lambda i: (i // tpb, 0, 0)),
            _resident((1, d)),
        ],
        out_specs=pl.BlockSpec((tc, d), lambda i: (i, 0)),
        out_shape=jax.ShapeDtypeStruct((n, d), F32),
        scratch_shapes=[pltpu.VMEM((TOP_K * tc, d), F32), pltpu.SemaphoreType.DMA(())],
        compiler_params=_params(("arbitrary",)),
        name="moe_combine_norm",
    )(dest.reshape(n // tc, 1, TOP_K * tc), ys, h2d, rt, g2, final_g.reshape(1, d))


def _route(rt, n_experts, tm):
    n = rt.shape[0]
    pair_expert = rt[:, 2:2 + TOP_K].astype(jnp.int32).reshape(n * TOP_K)
    onehot = (pair_expert[:, None] == jnp.arange(n_experts, dtype=jnp.int32)[None, :]).astype(jnp.int32)
    csum = jnp.cumsum(onehot, axis=0)
    rank = jnp.sum(onehot * csum, axis=1) - 1
    counts = csum[-1]
    tiles = (counts + tm - 1) // tm
    tile_end = jnp.cumsum(tiles)
    tile_start = tile_end - tiles
    dest = tile_start[pair_expert] * tm + rank
    n_tiles = (n * TOP_K) // tm + n_experts
    ti = jnp.arange(n_tiles, dtype=jnp.int32)
    n_active = tile_end[-1]
    tile_valid = (ti < n_active).astype(jnp.int32)
    last_used = jnp.minimum(ti, n_active - 1)
    expert_of = jnp.sum((last_used[:, None] >= tile_end[None, :]).astype(jnp.int32), axis=1)
    tile_expert = jnp.minimum(expert_of, n_experts - 1)
    return dest.astype(jnp.int32), tile_expert, tile_valid, n_tiles


def kernel(x, c, mod_w, mod_b, norm1_g, norm2_g, conv_w_pw1, conv_b_pw1, conv_w_dw, conv_b_dw, conv_ln_g, conv_ln_b, conv_w_pw2, conv_b_pw2, ssm_a_re, ssm_a_im, ssm_log_dt, ssm_b_re, ssm_b_im, ssm_c_re, ssm_c_im, ssm_d, ssm_w_glu, ssm_b_glu, ffn_w_gate, ffn_w_up, ffn_w_down, moe_w_router, moe_b_router, moe_w_gate, moe_w_up, moe_w_down, final_norm_g):
    nb, seq, d = x.shape
    n = nb * seq
    assert mod_w.shape[0] == 2 and seq % S5_CHUNK == 0 and d % LANES == 0
    n_experts = moe_w_router.shape[-1]
    rows_per_batch = seq // S5_CHUNK
    nr = n // S5_CHUNK

    mod = _modulation(c, mod_w, mod_b)
    parts = [[mod[i, :, k * d:(k + 1) * d].reshape(nb, 1, d) for k in range(6)] for i in range(2)]
    sh1a, sc1a, g1a, sh2a, sc2a, g2a = parts[0]
    sh1b, sc1b, g1b, sh2b, sc2b, g2b = parts[1]

    u = _pw1(x.reshape(n, d), norm1_g[0], sc1a, sh1a, conv_w_pw1[0], conv_b_pw1[0], seq)
    h1 = _conv_block(u.reshape(nb, seq, d), x, conv_w_dw[0], conv_b_dw[0], conv_ln_g[0], conv_ln_b[0],
                     conv_w_pw2[0], conv_b_pw2[0], g1a)
    h2v, u16 = _dense_ffn(h1.reshape(nr, S5_CHUNK * d), norm2_g[0], sc2a, sh2a, g2a,
                          ffn_w_gate[0], ffn_w_up[0], ffn_w_down[0], norm1_g[1], sc1b, sh1b, rows_per_batch)

    tables = _s5_tables(ssm_a_re[0], ssm_a_im[0], ssm_log_dt[0], ssm_b_re[0], ssm_b_im[0],
                        ssm_c_re[0], ssm_c_im[0], ssm_d[0])
    y16 = _s5_mix(u16, tables, nb)
    h3v, tv, rtv = _glu_router(y16, ssm_w_glu[0], ssm_b_glu[0], h2v, g1b, norm2_g[1], sc2b, sh2b,
                               moe_w_router[0], moe_b_router[0], rows_per_batch)
    rt = rtv.reshape(n, LANES)
    tm = 512
    dest, tile_expert, tile_valid, n_tiles = _route(rt, n_experts, tm)
    xs = _dispatch(tv.reshape(n, d), dest, n_tiles * tm)
    ys = _experts(xs, tile_expert, tile_valid, moe_w_gate[0].astype(BF16), moe_w_up[0].astype(BF16),
                  moe_w_down[0].astype(BF16), tm)
    out = _combine(ys, dest, h3v.reshape(n, d), rt, g2b, final_norm_g, seq)
    return out.reshape(nb, seq, d)
```

```python
import functools

import jax
import jax.numpy as jnp
from jax import lax
from jax.experimental import pallas as pl
from jax.experimental.pallas import tpu as pltpu

F32 = jnp.float32
BF16 = jnp.bfloat16
HIGHEST = lax.Precision.HIGHEST

NORM_EPS = 1e-6
TOP_K = 2
LANES = 128
S5_CHUNK = 16
CONV_HALO = 32
CONV_ROWS = 32
DMA_UNROLL = 8
VMEM_LIMIT = 56 * 2**20


def _params(sem):
    return pltpu.CompilerParams(dimension_semantics=sem, vmem_limit_bytes=VMEM_LIMIT)


def _divisor(n, cap, mult):
    best = None
    for d in range(mult, min(n, cap) + 1, mult):
        if n % d == 0:
            best = d
    assert best is not None, (n, cap, mult)
    return best


def _resident(shape):
    nd = len(shape)
    return pl.BlockSpec(shape, lambda *_: (0,) * nd, pipeline_mode=pl.Buffered(1))


def _rms_mod(x, g, sc, sh):
    ms = jnp.mean(x * x, axis=-1, keepdims=True)
    return (x * lax.rsqrt(ms + NORM_EPS) * g) * (1.0 + sc) + sh


def _silu(x):
    return x * jax.nn.sigmoid(x)


def _gelu_tanh(x):
    c = 0.7978845608028654
    return 0.5 * x * (1.0 + jnp.tanh(c * (x + 0.044715 * (x * x * x))))


def _split_bf16(x):
    hi = x.astype(BF16)
    return hi, (x - hi.astype(F32)).astype(BF16)


def _mod_kernel(c_ref, w_ref, b_ref, o_ref):
    cond = _silu(c_ref[...])
    o_ref[0] = jnp.dot(cond, w_ref[0], preferred_element_type=F32, precision=HIGHEST) + b_ref[0]


def _modulation(c, mod_w, mod_b):
    depth, d, d6 = mod_w.shape
    nb = c.shape[0]
    rows = -(-nb // 8) * 8
    c8 = jnp.pad(c, ((0, rows - nb), (0, 0)))
    tn = _divisor(d6, 1536, LANES)
    out = pl.pallas_call(
        _mod_kernel,
        grid=(depth, d6 // tn),
        in_specs=[
            pl.BlockSpec((rows, d), lambda i, j: (0, 0)),
            pl.BlockSpec((1, d, tn), lambda i, j: (i, 0, j)),
            pl.BlockSpec((1, 1, tn), lambda i, j: (i, 0, j)),
        ],
        out_specs=pl.BlockSpec((1, rows, tn), lambda i, j: (i, 0, j)),
        out_shape=jax.ShapeDtypeStruct((depth, rows, d6), F32),
        compiler_params=_params(("arbitrary", "arbitrary")),
        name="adaln_mod",
    )(c8, mod_w, mod_b.reshape(depth, 1, d6))
    return out[:, :nb, :]


def _pw1_kernel(x_ref, g_ref, sc_ref, sh_ref, w_ref, b_ref, o_ref):
    x = x_ref[...]
    d = x.shape[1]
    y = _rms_mod(x, g_ref[...], sc_ref[0], sh_ref[0])
    u = jnp.dot(y.astype(BF16), w_ref[...], preferred_element_type=F32) + b_ref[...]
    o_ref[...] = u[:, :d] * jax.nn.sigmoid(u[:, d:])


def _pw1(x2, norm_g, sc, sh, w, b, seq):
    n, d = x2.shape
    tm = _divisor(seq, 512, 8)
    tpb = seq // tm
    bmap = lambda i: (i // tpb, 0, 0)
    return pl.pallas_call(
        _pw1_kernel,
        grid=(n // tm,),
        in_specs=[
            pl.BlockSpec((tm, d), lambda i: (i, 0)),
            _resident((1, d)),
            pl.BlockSpec((1, 1, d), bmap),
            pl.BlockSpec((1, 1, d), bmap),
            _resident((d, 2 * d)),
            _resident((1, 2 * d)),
        ],
        out_specs=pl.BlockSpec((tm, d), lambda i: (i, 0)),
        out_shape=jax.ShapeDtypeStruct((n, d), F32),
        compiler_params=_params(("arbitrary",)),
        name="conv_pw1_glu",
    )(x2, norm_g.reshape(1, d), sc, sh, w.astype(BF16), b.reshape(1, 2 * d))


def _conv_kernel(cur_ref, prev_ref, wdw_ref, bdw_ref, lng_ref, lnb_ref, w2_ref, b2_ref, h_ref, g1_ref,
                 o_ref, buf_ref, cv_ref, *, taps):
    tm, d = cv_ref.shape
    i = pl.program_id(1)
    buf_ref[0:CONV_HALO, :] = jnp.where(i > 0, prev_ref[0], 0.0)
    buf_ref[CONV_HALO:, :] = cur_ref[0]
    off0 = CONV_HALO - (taps - 1)
    span = CONV_ROWS + CONV_HALO
    for c in range(d // LANES):
        lanes = slice(c * LANES, (c + 1) * LANES)

        def body(k, carry, lanes=lanes):
            r0 = pl.multiple_of(k * CONV_ROWS, CONV_ROWS)
            v = buf_ref[pl.ds(r0, span), lanes]
            acc = jnp.broadcast_to(bdw_ref[:, lanes], (CONV_ROWS, LANES))
            for s in range(8):
                xs = v if s == 0 else pltpu.roll(v, span - s, axis=0)
                for q in range(span // 8):
                    o = 8 * q + s
                    if o < off0 or o > off0 + taps - 1:
                        continue
                    wrow = wdw_ref[o - off0:o - off0 + 1, lanes]
                    acc = acc + wrow * xs[8 * q:8 * q + CONV_ROWS, :]
            cv_ref[pl.ds(r0, CONV_ROWS), lanes] = acc
            return carry

        lax.fori_loop(0, tm // CONV_ROWS, body, 0)
    v = cv_ref[...]
    mu = jnp.mean(v, axis=-1, keepdims=True)
    xc = v - mu
    var = jnp.mean(xc * xc, axis=-1, keepdims=True)
    y = _silu(xc * lax.rsqrt(var + NORM_EPS) * lng_ref[...] + lnb_ref[...])
    z = jnp.dot(y.astype(BF16), w2_ref[...], preferred_element_type=F32) + b2_ref[...]
    o_ref[0] = h_ref[0] + g1_ref[0] * z


def _conv_block(u3, x3, w_dw, b_dw, ln_g, ln_b, w2, b2, g1):
    nb, seq, d = x3.shape
    taps = w_dw.shape[0]
    assert taps - 1 <= CONV_HALO and d % LANES == 0
    tm = _divisor(seq, 512, CONV_ROWS)
    hb = tm // CONV_HALO
    wpad = jnp.pad(w_dw, ((0, -taps % 8), (0, 0)))
    row = lambda a: a.reshape(1, d)
    return pl.pallas_call(
        functools.partial(_conv_kernel, taps=taps),
        grid=(nb, seq // tm),
        in_specs=[
            pl.BlockSpec((1, tm, d), lambda b, i: (b, i, 0)),
            pl.BlockSpec((1, CONV_HALO, d), lambda b, i: (b, jnp.maximum(i * hb - 1, 0), 0)),
            _resident(wpad.shape),
            _resident((1, d)),
            _resident((1, d)),
            _resident((1, d)),
            _resident((d, d)),
            _resident((1, d)),
            pl.BlockSpec((1, tm, d), lambda b, i: (b, i, 0)),
            pl.BlockSpec((1, 1, d), lambda b, i: (b, 0, 0)),
        ],
        out_specs=pl.BlockSpec((1, tm, d), lambda b, i: (b, i, 0)),
        out_shape=jax.ShapeDtypeStruct((nb, seq, d), F32),
        scratch_shapes=[pltpu.VMEM((tm + CONV_HALO, d), F32), pltpu.VMEM((tm, d), F32)],
        compiler_params=_params(("arbitrary", "arbitrary")),
        name="conv_dw_ln_pw2",
    )(u3, u3, wpad, row(b_dw), row(ln_g), row(ln_b), w2.astype(BF16), row(b2), x3, g1)


def _ffn_kernel(h_ref, n2g_ref, sc_ref, sh_ref, g2_ref, wg_ref, wu_ref, wd_ref, n1g_ref, sc1_ref, sh1_ref,
                h2_ref, u_ref, *, fchunk):
    h = h_ref[...]
    t = _rms_mod(h, n2g_ref[...], sc_ref[0], sh_ref[0]).astype(BF16)
    dff = wg_ref.shape[1]
    acc = None
    for f0 in range(0, dff, fchunk):
        f1 = min(f0 + fchunk, dff)
        g = jnp.dot(t, wg_ref[:, f0:f1], preferred_element_type=F32)
        u = jnp.dot(t, wu_ref[:, f0:f1], preferred_element_type=F32)
        a = (_silu(g) * u).astype(BF16)
        y = jnp.dot(a, wd_ref[f0:f1, :], preferred_element_type=F32)
        acc = y if acc is None else acc + y
    h2 = h + g2_ref[0] * acc
    h2_ref[...] = h2
    u_ref[...] = _rms_mod(h2, n1g_ref[...], sc1_ref[0], sh1_ref[0])


def _dense_ffn(h, n2g, sc2, sh2, g2, wg, wu, wd, n1g, sc1, sh1, seq):
    n, d = h.shape
    dff = wg.shape[1]
    tm = _divisor(seq, 512, 8)
    tpb = seq // tm
    bmap = lambda i: (i // tpb, 0, 0)
    tile = pl.BlockSpec((tm, d), lambda i: (i, 0))
    return pl.pallas_call(
        functools.partial(_ffn_kernel, fchunk=512),
        grid=(n // tm,),
        in_specs=[
            tile,
            _resident((1, d)),
            pl.BlockSpec((1, 1, d), bmap),
            pl.BlockSpec((1, 1, d), bmap),
            pl.BlockSpec((1, 1, d), bmap),
            _resident((d, dff)),
            _resident((d, dff)),
            _resident((dff, d)),
            _resident((1, d)),
            pl.BlockSpec((1, 1, d), bmap),
            pl.BlockSpec((1, 1, d), bmap),
        ],
        out_specs=[tile, tile],
        out_shape=[jax.ShapeDtypeStruct((n, d), F32), jax.ShapeDtypeStruct((n, d), F32)],
        compiler_params=_params(("arbitrary",)),
        name="dense_swiglu",
    )(h, n2g.reshape(1, d), sc2, sh2, g2, wg.astype(BF16), wu.astype(BF16), wd.astype(BF16),
      n1g.reshape(1, d), sc1, sh1)


def _s5_tables(a_re, a_im, log_dt, b_re, b_im, c_re, c_im, d_skip):
    ng, p = a_re.shape
    c = b_re.shape[-1]
    gpb = LANES // c
    nblk = ng // gpb
    L = S5_CHUNK
    dt = jnp.exp(log_dt)[:, None]
    mag = jnp.exp(dt * a_re)
    ab_re = mag * jnp.cos(dt * a_im)
    ab_im = mag * jnp.sin(dt * a_im)
    den = a_re * a_re + a_im * a_im
    f_re = ((ab_re - 1.0) * a_re + ab_im * a_im) / den
    f_im = (ab_im * a_re - (ab_re - 1.0) * a_im) / den
    bb_re = f_re[..., None] * b_re - f_im[..., None] * b_im
    bb_im = f_re[..., None] * b_im + f_im[..., None] * b_re
    pr, pi = [jnp.ones_like(ab_re)], [jnp.zeros_like(ab_re)]
    for _ in range(L):
        pr, pi = pr + [pr[-1] * ab_re - pi[-1] * ab_im], pi + [pr[-1] * ab_im + pi[-1] * ab_re]
    eye = jnp.eye(gpb, dtype=F32)
    blk = lambda a: a.reshape((nblk, gpb) + a.shape[1:])
    bbx_re = jnp.einsum("bgpc,gh->bgchp", blk(bb_re), eye).reshape(nblk, LANES, gpb * p)
    bbx_im = jnp.einsum("bgpc,gh->bgchp", blk(bb_im), eye).reshape(nblk, LANES, gpb * p)
    ctx_re = jnp.einsum("bgop,gh->bgohp", blk(c_re), eye).reshape(nblk, LANES, gpb * p)
    ctx_im = jnp.einsum("bgop,gh->bgohp", blk(c_im), eye).reshape(nblk, LANES, gpb * p)
    ccm = jnp.einsum("rbgop,gh->brgpho", jnp.stack([blk(c_re), -blk(c_im)]), eye).reshape(nblk, 2 * gpb * p, LANES)
    pw_re = jnp.stack(pr).reshape(L + 1, nblk, gpb * p).transpose(1, 0, 2)
    pw_im = jnp.stack(pi).reshape(L + 1, nblk, gpb * p).transpose(1, 0, 2)
    d_blk = jnp.tile(d_skip.reshape(nblk, LANES), (1, L)).reshape(nblk, 1, L * LANES)
    return bbx_re, bbx_im, ctx_re, ctx_im, ccm, pw_re, pw_im, d_blk


def _s5_kernel(u_ref, bbr_ref, bbi_ref, ctr_ref, cti_ref, ccm_ref, pwr_ref, pwi_ref, d_ref, o_ref,
               t_ref, we_ref, vt_ref, e_ref, hin_ref):
    L = S5_CHUNK
    rows, ew = e_ref.shape
    ph = ew // 2

    @pl.when(pl.program_id(1) == 0)
    def _build_operands():
        t_ref[...] = jnp.zeros_like(t_ref)
        bbr, bbi = bbr_ref[0], bbi_ref[0]
        ctr, cti = ctr_ref[0], cti_ref[0]
        cc_hi, cc_lo = _split_bf16(ccm_ref[0])
        for lp in range(L):
            j = L - 1 - lp
            p_re, p_im = pwr_ref[0, j:j + 1, :], pwi_ref[0, j:j + 1, :]
            w = jnp.concatenate([bbr * p_re - bbi * p_im, bbr * p_im + bbi * p_re], axis=1)
            w_hi, w_lo = _split_bf16(w)
            we_ref[lp * LANES:(lp + 1) * LANES, :] = w_hi
            k = (jnp.dot(w_hi, cc_hi, preferred_element_type=F32)
                 + jnp.dot(w_lo, cc_hi, preferred_element_type=F32)
                 + jnp.dot(w_hi, cc_lo, preferred_element_type=F32)).astype(BF16)
            for l1 in range(L - j):
                t_ref[l1 * LANES:(l1 + 1) * LANES, (l1 + j) * LANES:(l1 + j + 1) * LANES] = k
        for l in range(L):
            p_re, p_im = pwr_ref[0, l + 1:l + 2, :], pwi_ref[0, l + 1:l + 2, :]
            v = jnp.concatenate([ctr * p_re - cti * p_im, -(ctr * p_im + cti * p_re)], axis=1)
            vt_ref[l * LANES:(l + 1) * LANES, :] = v.astype(BF16)

    xcat = jnp.concatenate([u_ref[pl.ds(l, rows, stride=L), :] for l in range(L)], axis=1)
    xb = xcat.astype(BF16)
    e_ref[...] = jnp.dot(xb, we_ref[...], preferred_element_type=F32)
    ar = pwr_ref[0, L:L + 1, :]
    ai = pwi_ref[0, L:L + 1, :]

    def body(r, carry):
        hr, hi = carry
        hin_ref[pl.ds(r, 1), 0:ph] = hr
        hin_ref[pl.ds(r, 1), ph:ew] = hi
        er = e_ref[pl.ds(r, 1), 0:ph]
        ei = e_ref[pl.ds(r, 1), ph:ew]
        return (ar * hr - ai * hi + er, ar * hi + ai * hr + ei)

    zero = jnp.zeros((1, ph), F32)
    lax.fori_loop(0, rows, body, (zero, zero))
    y = jnp.dot(xb, t_ref[...], preferred_element_type=F32)
    y = y + lax.dot_general(hin_ref[...].astype(BF16), vt_ref[...], (((1,), (1,)), ((), ())),
                            preferred_element_type=F32)
    y = _gelu_tanh(y + d_ref[0] * xcat)
    for l in range(L):
        o_ref[pl.ds(l, rows, stride=L), :] = y[:, l * LANES:(l + 1) * LANES]


def _s5_mix(u, tables, nbatch):
    n, d = u.shape
    L = S5_CHUNK
    bbx_re, bbx_im, ctx_re, ctx_im, ccm, pw_re, pw_im, d_blk = tables
    nblk, _, sw = bbx_re.shape
    seq = n // nbatch
    rows = seq // L
    slab = lambda a: pl.BlockSpec((1,) + a.shape[1:], lambda g, b: (g, 0, 0))
    return pl.pallas_call(
        _s5_kernel,
        grid=(nblk, nbatch),
        in_specs=[pl.BlockSpec((seq, LANES), lambda g, b: (b, g))] + [slab(a) for a in tables],
        out_specs=pl.BlockSpec((seq, LANES), lambda g, b: (b, g)),
        out_shape=jax.ShapeDtypeStruct((n, d), F32),
        scratch_shapes=[
            pltpu.VMEM((L * LANES, L * LANES), BF16),
            pltpu.VMEM((L * LANES, 2 * sw), BF16),
            pltpu.VMEM((L * LANES, 2 * sw), BF16),
            pltpu.VMEM((rows, 2 * sw), F32),
            pltpu.VMEM((rows, 2 * sw), F32),
        ],
        compiler_params=_params(("arbitrary", "arbitrary")),
        name="s5_chunked_scan",
    )(u, *tables)


def _glu_kernel(y_ref, w_ref, b_ref, h_ref, g1_ref, n2g_ref, sc_ref, sh_ref, wr_ref, br_ref,
                h3_ref, t_ref, rt_ref):
    d = h_ref.shape[1]
    ne = wr_ref.shape[0]
    z = jnp.dot(y_ref[...].astype(BF16), w_ref[...], preferred_element_type=F32) + b_ref[...]
    h3 = h_ref[...] + g1_ref[0] * (z[:, :d] * jax.nn.sigmoid(z[:, d:]))
    h3_ref[...] = h3
    t = _rms_mod(h3, n2g_ref[...], sc_ref[0], sh_ref[0])
    t_ref[...] = t
    logit = [jnp.sum(t * wr_ref[e:e + 1, :], axis=-1, keepdims=True) + br_ref[e:e + 1, :] for e in range(ne)]

    def top1(cols):
        m = cols[0]
        for col in cols[1:]:
            m = jnp.maximum(m, col)
        idx = jnp.full_like(m, float(ne))
        for e in reversed(range(ne)):
            idx = jnp.where(cols[e] == m, float(e), idx)
        return m, idx

    m1, i1 = top1(logit)
    m2, i2 = top1([jnp.where(i1 == float(e), -jnp.inf, logit[e]) for e in range(ne)])
    e2 = jnp.exp(m2 - m1)
    den = 1.0 + e2
    lane = lax.broadcasted_iota(jnp.int32, rt_ref.shape, 1)
    rt_ref[...] = jnp.where(lane == 0, 1.0 / den,
                            jnp.where(lane == 1, e2 / den,
                                      jnp.where(lane == 2, i1, jnp.where(lane == 3, i2, 0.0))))


def _glu_router(y, w_glu, b_glu, h, g1, n2g, sc2, sh2, w_router, b_router, seq):
    n, d = h.shape
    ne = w_router.shape[1]
    tm = _divisor(seq, 512, 8)
    tpb = seq // tm
    bmap = lambda i: (i // tpb, 0, 0)
    tile = pl.BlockSpec((tm, d), lambda i: (i, 0))
    return pl.pallas_call(
        _glu_kernel,
        grid=(n // tm,),
        in_specs=[
            tile,
            _resident((d, 2 * d)),
            _resident((1, 2 * d)),
            tile,
            pl.BlockSpec((1, 1, d), bmap),
            _resident((1, d)),
            pl.BlockSpec((1, 1, d), bmap),
            pl.BlockSpec((1, 1, d), bmap),
            _resident((ne, d)),
            _resident((ne, 1)),
        ],
        out_specs=[tile, tile, pl.BlockSpec((tm, LANES), lambda i: (i, 0))],
        out_shape=[
            jax.ShapeDtypeStruct((n, d), F32),
            jax.ShapeDtypeStruct((n, d), F32),
            jax.ShapeDtypeStruct((n, LANES), F32),
        ],
        compiler_params=_params(("arbitrary",)),
        name="s5_glu_router",
    )(y, w_glu.astype(BF16), b_glu.reshape(1, 2 * d), h, g1, n2g.reshape(1, d), sc2, sh2,
      w_router.T, b_router.reshape(ne, 1))


def _dispatch_kernel(dest_ref, t_ref, xs_in_ref, xs_ref, sem):
    del xs_in_ref
    td = t_ref.shape[0]

    def row_copy(r, k):
        dst = dest_ref[0, 0, TOP_K * r + k]
        return pltpu.make_async_copy(t_ref.at[pl.ds(r, 1), :], xs_ref.at[pl.ds(dst, 1), :], sem)

    def rows(fn):
        def body(j, c):
            for u in range(DMA_UNROLL):
                for k in range(TOP_K):
                    fn(row_copy(j * DMA_UNROLL + u, k))
            return c
        lax.fori_loop(0, td // DMA_UNROLL, body, 0)

    rows(lambda cp: cp.start())
    rows(lambda cp: cp.wait())


def _dispatch(t2, dest, nrows):
    n, d = t2.shape
    td = _divisor(n, 256, DMA_UNROLL)
    return pl.pallas_call(
        _dispatch_kernel,
        grid=(n // td,),
        in_specs=[
            pl.BlockSpec((1, 1, TOP_K * td), lambda i: (i, 0, 0), memory_space=pltpu.SMEM),
            pl.BlockSpec((td, d), lambda i: (i, 0)),
            pl.BlockSpec(memory_space=pl.ANY),
        ],
        out_specs=pl.BlockSpec(memory_space=pl.ANY),
        out_shape=jax.ShapeDtypeStruct((nrows, d), F32),
        scratch_shapes=[pltpu.SemaphoreType.DMA(())],
        input_output_aliases={2: 0},
        compiler_params=_params(("arbitrary",)),
        name="moe_dispatch",
    )(dest.reshape(n // td, 1, TOP_K * td), t2, jnp.zeros((nrows, d), F32))


def _expert_kernel(te_ref, tv_ref, x_ref, wg_ref, wu_ref, wd_ref, o_ref, xb_ref, acc_ref, *, fchunk):
    del te_ref
    i = pl.program_id(0)
    f = pl.program_id(1)
    nf = pl.num_programs(1)

    @pl.when(jnp.logical_and(tv_ref[i] == 0, f == nf - 1))
    def _():
        o_ref[...] = jnp.zeros_like(o_ref)

    @pl.when(tv_ref[i] == 1)
    def _():
        @pl.when(f == 0)
        def _():
            xb_ref[...] = x_ref[...].astype(BF16)

        x = xb_ref[...]
        tf = wg_ref.shape[2]
        y = None
        for f0 in range(0, tf, fchunk):
            f1 = min(f0 + fchunk, tf)
            g = jnp.dot(x, wg_ref[0, :, f0:f1], preferred_element_type=F32)
            u = jnp.dot(x, wu_ref[0, :, f0:f1], preferred_element_type=F32)
            a = (_silu(g) * u).astype(BF16)
            yy = jnp.dot(a, wd_ref[0, f0:f1, :], preferred_element_type=F32)
            y = yy if y is None else y + yy

        @pl.when(f == 0)
        def _():
            acc_ref[...] = y

        @pl.when(f > 0)
        def _():
            acc_ref[...] += y

        @pl.when(f == nf - 1)
        def _():
            o_ref[...] = acc_ref[...]


def _experts(xs, tile_expert, tile_valid, wg, wu, wd, tm):
    nrows, d = xs.shape
    dff = wg.shape[2]
    tf = _divisor(dff, 1792, 256)
    nf = dff // tf
    n_tiles = nrows // tm
    fidx = lambda i, f, tv: f * tv[i] + (nf - 1) * (1 - tv[i])
    grid_spec = pltpu.PrefetchScalarGridSpec(
        num_scalar_prefetch=2,
        grid=(n_tiles, nf),
        in_specs=[
            pl.BlockSpec((tm, d), lambda i, f, te, tv: (i, 0)),
            pl.BlockSpec((1, d, tf), lambda i, f, te, tv: (te[i], 0, fidx(i, f, tv))),
            pl.BlockSpec((1, d, tf), lambda i, f, te, tv: (te[i], 0, fidx(i, f, tv))),
            pl.BlockSpec((1, tf, d), lambda i, f, te, tv: (te[i], fidx(i, f, tv), 0)),
        ],
        out_specs=pl.BlockSpec((tm, d), lambda i, f, te, tv: (i, 0)),
        scratch_shapes=[pltpu.VMEM((tm, d), BF16), pltpu.VMEM((tm, d), F32)],
    )
    return pl.pallas_call(
        functools.partial(_expert_kernel, fchunk=1024),
        grid_spec=grid_spec,
        out_shape=jax.ShapeDtypeStruct((nrows, d), F32),
        compiler_params=_params(("arbitrary", "arbitrary")),
        name="moe_experts",
    )(tile_expert, tile_valid, xs, wg, wu, wd)


def _combine_kernel(dest_ref, ys_ref, h_ref, rt_ref, g2_ref, fg_ref, o_ref, ybuf_ref, sem):
    tc = h_ref.shape[0]

    def row_copy(r, k):
        src = dest_ref[0, 0, TOP_K * r + k]
        return pltpu.make_async_copy(ys_ref.at[pl.ds(src, 1), :], ybuf_ref.at[pl.ds(k * tc + r, 1), :], sem)

    def rows(fn):
        def body(j, c):
            for u in range(DMA_UNROLL):
                for k in range(TOP_K):
                    fn(row_copy(j * DMA_UNROLL + u, k))
            return c
        lax.fori_loop(0, tc // DMA_UNROLL, body, 0)

    rows(lambda cp: cp.start())
    rows(lambda cp: cp.wait())
    rt = rt_ref[...]
    y = rt[:, 0:1] * ybuf_ref[0:tc, :]
    for k in range(1, TOP_K):
        y = y + rt[:, k:k + 1] * ybuf_ref[k * tc:(k + 1) * tc, :]
    h4 = h_ref[...] + g2_ref[0] * y
    ms = jnp.mean(h4 * h4, axis=-1, keepdims=True)
    o_ref[...] = h4 * lax.rsqrt(ms + NORM_EPS) * fg_ref[...]


def _combine(ys, dest, h2d, rt, g2, final_g, seq):
    n, d = h2d.shape
    tc = _divisor(seq, 256, DMA_UNROLL)
    tpb = seq // tc
    return pl.pallas_call(
        _combine_kernel,
        grid=(n // tc,),
        in_specs=[
            pl.BlockSpec((1, 1, TOP_K * tc), lambda i: (i, 0, 0), memory_space=pltpu.SMEM),
            pl.BlockSpec(memory_space=pl.ANY),
            pl.BlockSpec((tc, d), lambda i: (i, 0)),
            pl.BlockSpec((tc, LANES), lambda i: (i, 0)),
            pl.BlockSpec((1, 1, d), lambda i: (i // tpb, 0, 0)),
            _resident((1, d)),
        ],
        out_specs=pl.BlockSpec((tc, d), lambda i: (i, 0)),
        out_shape=jax.ShapeDtypeStruct((n, d), F32),
        scratch_shapes=[pltpu.VMEM((TOP_K * tc, d), F32), pltpu.SemaphoreType.DMA(())],
        compiler_params=_params(("arbitrary",)),
        name="moe_combine_norm",
    )(dest.reshape(n // tc, 1, TOP_K * tc), ys, h2d, rt, g2, final_g.reshape(1, d))


def _route(rt, n_experts, tm):
    n = rt.shape[0]
    pair_expert = rt[:, 2:2 + TOP_K].astype(jnp.int32).reshape(n * TOP_K)
    onehot = (pair_expert[:, None] == jnp.arange(n_experts, dtype=jnp.int32)[None, :]).astype(jnp.int32)
    csum = jnp.cumsum(onehot, axis=0)
    rank = jnp.sum(onehot * csum, axis=1) - 1
    counts = csum[-1]
    tiles = (counts + tm - 1) // tm
    tile_end = jnp.cumsum(tiles)
    tile_start = tile_end - tiles
    dest = tile_start[pair_expert] * tm + rank
    n_tiles = (n * TOP_K) // tm + n_experts
    ti = jnp.arange(n_tiles, dtype=jnp.int32)
    n_active = tile_end[-1]
    tile_valid = (ti < n_active).astype(jnp.int32)
    last_used = jnp.minimum(ti, n_active - 1)
    expert_of = jnp.sum((last_used[:, None] >= tile_end[None, :]).astype(jnp.int32), axis=1)
    tile_expert = jnp.minimum(expert_of, n_experts - 1)
    return dest.astype(jnp.int32), tile_expert, tile_valid, n_tiles


def kernel(x, c, mod_w, mod_b, norm1_g, norm2_g, conv_w_pw1, conv_b_pw1, conv_w_dw, conv_b_dw, conv_ln_g, conv_ln_b, conv_w_pw2, conv_b_pw2, ssm_a_re, ssm_a_im, ssm_log_dt, ssm_b_re, ssm_b_im, ssm_c_re, ssm_c_im, ssm_d, ssm_w_glu, ssm_b_glu, ffn_w_gate, ffn_w_up, ffn_w_down, moe_w_router, moe_b_router, moe_w_gate, moe_w_up, moe_w_down, final_norm_g):
    nb, seq, d = x.shape
    n = nb * seq
    assert mod_w.shape[0] == 2 and seq % S5_CHUNK == 0 and d % LANES == 0
    n_experts = moe_w_router.shape[-1]

    mod = _modulation(c, mod_w, mod_b)
    parts = [[mod[i, :, k * d:(k + 1) * d].reshape(nb, 1, d) for k in range(6)] for i in range(2)]
    sh1a, sc1a, g1a, sh2a, sc2a, g2a = parts[0]
    sh1b, sc1b, g1b, sh2b, sc2b, g2b = parts[1]

    u = _pw1(x.reshape(n, d), norm1_g[0], sc1a, sh1a, conv_w_pw1[0], conv_b_pw1[0], seq)
    h1 = _conv_block(u.reshape(nb, seq, d), x, conv_w_dw[0], conv_b_dw[0], conv_ln_g[0], conv_ln_b[0],
                     conv_w_pw2[0], conv_b_pw2[0], g1a)
    h2, u1 = _dense_ffn(h1.reshape(n, d), norm2_g[0], sc2a, sh2a, g2a,
                        ffn_w_gate[0], ffn_w_up[0], ffn_w_down[0], norm1_g[1], sc1b, sh1b, seq)

    tables = _s5_tables(ssm_a_re[0], ssm_a_im[0], ssm_log_dt[0], ssm_b_re[0], ssm_b_im[0],
                        ssm_c_re[0], ssm_c_im[0], ssm_d[0])
    y1 = _s5_mix(u1, tables, nb)
    h3, t, rt = _glu_router(y1, ssm_w_glu[0], ssm_b_glu[0], h2, g1b, norm2_g[1], sc2b, sh2b,
                            moe_w_router[0], moe_b_router[0], seq)
    tm = 512
    dest, tile_expert, tile_valid, n_tiles = _route(rt, n_experts, tm)
    xs = _dispatch(t, dest, n_tiles * tm)
    ys = _experts(xs, tile_expert, tile_valid, moe_w_gate[0].astype(BF16), moe_w_up[0].astype(BF16),
                  moe_w_down[0].astype(BF16), tm)
    out = _combine(ys, dest, h3, rt, g2b, final_norm_g, seq)
    return out.reshape(nb, seq, d)
```

```python
import functools

import jax
import jax.numpy as jnp
from jax import lax
from jax.experimental import pallas as pl
from jax.experimental.pallas import tpu as pltpu

F32 = jnp.float32
BF16 = jnp.bfloat16
HIGHEST = lax.Precision.HIGHEST

NORM_EPS = 1e-6
TOP_K = 2
LANES = 128
S5_CHUNK = 16
CONV_HALO = 32
CONV_ROWS = 64
TOKEN_ROWS = 8
DMA_UNROLL = 8
VMEM_LIMIT = 56 * 2**20


def _params(sem):
    return pltpu.CompilerParams(dimension_semantics=sem, vmem_limit_bytes=VMEM_LIMIT)


def _divisor(n, cap, mult):
    best = None
    for d in range(mult, min(n, cap) + 1, mult):
        if n % d == 0:
            best = d
    assert best is not None, (n, cap, mult)
    return best


def _resident(shape):
    nd = len(shape)
    return pl.BlockSpec(shape, lambda *_: (0,) * nd, pipeline_mode=pl.Buffered(1))


def _rms_mod(x, g, sc, sh):
    ms = jnp.mean(x * x, axis=-1, keepdims=True)
    return (x * lax.rsqrt(ms + NORM_EPS) * g) * (1.0 + sc) + sh


def _silu(x):
    return x * jax.nn.sigmoid(x)


def _gelu_tanh(x):
    c = 0.7978845608028654
    return 0.5 * x * (1.0 + jnp.tanh(c * (x + 0.044715 * (x * x * x))))


def _split_bf16(x):
    hi = x.astype(BF16)
    return hi, (x - hi.astype(F32)).astype(BF16)


def _mod_kernel(c_ref, w_ref, b_ref, o_ref):
    cond = _silu(c_ref[...])
    o_ref[0] = jnp.dot(cond, w_ref[0], preferred_element_type=F32, precision=HIGHEST) + b_ref[0]


def _modulation(c, mod_w, mod_b):
    depth, d, d6 = mod_w.shape
    nb = c.shape[0]
    rows = -(-nb // 8) * 8
    c8 = jnp.pad(c, ((0, rows - nb), (0, 0)))
    tn = _divisor(d6, 1536, LANES)
    out = pl.pallas_call(
        _mod_kernel,
        grid=(depth, d6 // tn),
        in_specs=[
            pl.BlockSpec((rows, d), lambda i, j: (0, 0)),
            pl.BlockSpec((1, d, tn), lambda i, j: (i, 0, j)),
            pl.BlockSpec((1, 1, tn), lambda i, j: (i, 0, j)),
        ],
        out_specs=pl.BlockSpec((1, rows, tn), lambda i, j: (i, 0, j)),
        out_shape=jax.ShapeDtypeStruct((depth, rows, d6), F32),
        compiler_params=_params(("arbitrary", "arbitrary")),
        name="adaln_mod",
    )(c8, mod_w, mod_b.reshape(depth, 1, d6))
    return out[:, :nb, :]


def _pw1_kernel(x_ref, g_ref, sc_ref, sh_ref, w_ref, b_ref, o_ref):
    x = x_ref[...]
    d = x.shape[1]
    y = _rms_mod(x, g_ref[...], sc_ref[0], sh_ref[0])
    u = jnp.dot(y.astype(BF16), w_ref[...], preferred_element_type=F32) + b_ref[...]
    o_ref[...] = u[:, :d] * jax.nn.sigmoid(u[:, d:])


def _pw1(x2, norm_g, sc, sh, w, b, seq):
    n, d = x2.shape
    tm = _divisor(seq, 512, 8)
    tpb = seq // tm
    bmap = lambda i: (i // tpb, 0, 0)
    return pl.pallas_call(
        _pw1_kernel,
        grid=(n // tm,),
        in_specs=[
            pl.BlockSpec((tm, d), lambda i: (i, 0)),
            _resident((1, d)),
            pl.BlockSpec((1, 1, d), bmap),
            pl.BlockSpec((1, 1, d), bmap),
            _resident((d, 2 * d)),
            _resident((1, 2 * d)),
        ],
        out_specs=pl.BlockSpec((tm, d), lambda i: (i, 0)),
        out_shape=jax.ShapeDtypeStruct((n, d), F32),
        compiler_params=_params(("arbitrary",)),
        name="conv_pw1_glu",
    )(x2, norm_g.reshape(1, d), sc, sh, w.astype(BF16), b.reshape(1, 2 * d))


def _conv_kernel(cur_ref, prev_ref, wdw_ref, bdw_ref, lng_ref, lnb_ref, w2_ref, b2_ref, h_ref, g1_ref,
                 o_ref, buf_ref, cv_ref, *, taps):
    tm, d = cv_ref.shape
    i = pl.program_id(1)
    buf_ref[0:CONV_HALO, :] = jnp.where(i > 0, prev_ref[0], 0.0)
    buf_ref[CONV_HALO:, :] = cur_ref[0]
    off0 = CONV_HALO - (taps - 1)
    span = CONV_ROWS + CONV_HALO
    for c in range(d // LANES):
        lanes = slice(c * LANES, (c + 1) * LANES)

        def body(k, carry, lanes=lanes):
            r0 = pl.multiple_of(k * CONV_ROWS, CONV_ROWS)
            v = buf_ref[pl.ds(r0, span), lanes]
            acc = jnp.broadcast_to(bdw_ref[:, lanes], (CONV_ROWS, LANES))
            for s in range(8):
                xs = v if s == 0 else pltpu.roll(v, span - s, axis=0)
                for q in range(span // 8):
                    o = 8 * q + s
                    if o < off0 or o > off0 + taps - 1:
                        continue
                    wrow = wdw_ref[o - off0:o - off0 + 1, lanes]
                    acc = acc + wrow * xs[8 * q:8 * q + CONV_ROWS, :]
            cv_ref[pl.ds(r0, CONV_ROWS), lanes] = acc
            return carry

        lax.fori_loop(0, tm // CONV_ROWS, body, 0)
    v = cv_ref[...]
    mu = jnp.mean(v, axis=-1, keepdims=True)
    xc = v - mu
    var = jnp.mean(xc * xc, axis=-1, keepdims=True)
    y = _silu(xc * lax.rsqrt(var + NORM_EPS) * lng_ref[...] + lnb_ref[...])
    z = jnp.dot(y.astype(BF16), w2_ref[...], preferred_element_type=F32) + b2_ref[...]
    o_ref[0] = h_ref[0] + g1_ref[0] * z


def _conv_block(u3, x3, w_dw, b_dw, ln_g, ln_b, w2, b2, g1):
    nb, seq, d = x3.shape
    taps = w_dw.shape[0]
    assert taps - 1 <= CONV_HALO and d % LANES == 0
    tm = _divisor(seq, 512, CONV_ROWS)
    hb = tm // CONV_HALO
    wpad = jnp.pad(w_dw, ((0, -taps % 8), (0, 0)))
    row = lambda a: a.reshape(1, d)
    return pl.pallas_call(
        functools.partial(_conv_kernel, taps=taps),
        grid=(nb, seq // tm),
        in_specs=[
            pl.BlockSpec((1, tm, d), lambda b, i: (b, i, 0)),
            pl.BlockSpec((1, CONV_HALO, d), lambda b, i: (b, jnp.maximum(i * hb - 1, 0), 0)),
            _resident(wpad.shape),
            _resident((1, d)),
            _resident((1, d)),
            _resident((1, d)),
            _resident((d, d)),
            _resident((1, d)),
            pl.BlockSpec((1, tm, d), lambda b, i: (b, i, 0)),
            pl.BlockSpec((1, 1, d), lambda b, i: (b, 0, 0)),
        ],
        out_specs=pl.BlockSpec((1, tm, d), lambda b, i: (b, i, 0)),
        out_shape=jax.ShapeDtypeStruct((nb, seq, d), F32),
        scratch_shapes=[pltpu.VMEM((tm + CONV_HALO, d), F32), pltpu.VMEM((tm, d), F32)],
        compiler_params=_params(("arbitrary", "arbitrary")),
        name="conv_dw_ln_pw2",
    )(u3, u3, wpad, row(b_dw), row(ln_g), row(ln_b), w2.astype(BF16), row(b2), x3, g1)


def _ffn_kernel(h_ref, n2g_ref, sc_ref, sh_ref, g2_ref, wg_ref, wu_ref, wd_ref, n1g_ref, sc1_ref, sh1_ref,
                h2_ref, u_ref, *, fchunk):
    h = h_ref[...]
    t = _rms_mod(h, n2g_ref[...], sc_ref[0], sh_ref[0]).astype(BF16)
    dff = wg_ref.shape[1]
    acc = None
    for f0 in range(0, dff, fchunk):
        f1 = min(f0 + fchunk, dff)
        g = jnp.dot(t, wg_ref[:, f0:f1], preferred_element_type=F32)
        u = jnp.dot(t, wu_ref[:, f0:f1], preferred_element_type=F32)
        a = (_silu(g) * u).astype(BF16)
        y = jnp.dot(a, wd_ref[f0:f1, :], preferred_element_type=F32)
        acc = y if acc is None else acc + y
    h2 = h + g2_ref[0] * acc
    h2_ref[...] = h2
    u_ref[...] = _rms_mod(h2, n1g_ref[...], sc1_ref[0], sh1_ref[0])


def _dense_ffn(h, n2g, sc2, sh2, g2, wg, wu, wd, n1g, sc1, sh1, seq):
    n, d = h.shape
    dff = wg.shape[1]
    tm = _divisor(seq, 512, 8)
    tpb = seq // tm
    bmap = lambda i: (i // tpb, 0, 0)
    tile = pl.BlockSpec((tm, d), lambda i: (i, 0))
    return pl.pallas_call(
        functools.partial(_ffn_kernel, fchunk=512),
        grid=(n // tm,),
        in_specs=[
            tile,
            _resident((1, d)),
            pl.BlockSpec((1, 1, d), bmap),
            pl.BlockSpec((1, 1, d), bmap),
            pl.BlockSpec((1, 1, d), bmap),
            _resident((d, dff)),
            _resident((d, dff)),
            _resident((dff, d)),
            _resident((1, d)),
            pl.BlockSpec((1, 1, d), bmap),
            pl.BlockSpec((1, 1, d), bmap),
        ],
        out_specs=[tile, tile],
        out_shape=[jax.ShapeDtypeStruct((n, d), F32), jax.ShapeDtypeStruct((n, d), F32)],
        compiler_params=_params(("arbitrary",)),
        name="dense_swiglu",
    )(h, n2g.reshape(1, d), sc2, sh2, g2, wg.astype(BF16), wu.astype(BF16), wd.astype(BF16),
      n1g.reshape(1, d), sc1, sh1)


def _s5_tables(a_re, a_im, log_dt, b_re, b_im, c_re, c_im, d_skip):
    ng, p = a_re.shape
    c = b_re.shape[-1]
    gpb = LANES // c
    nblk = ng // gpb
    L = S5_CHUNK
    dt = jnp.exp(log_dt)[:, None]
    mag = jnp.exp(dt * a_re)
    ab_re = mag * jnp.cos(dt * a_im)
    ab_im = mag * jnp.sin(dt * a_im)
    den = a_re * a_re + a_im * a_im
    f_re = ((ab_re - 1.0) * a_re + ab_im * a_im) / den
    f_im = (ab_im * a_re - (ab_re - 1.0) * a_im) / den
    bb_re = f_re[..., None] * b_re - f_im[..., None] * b_im
    bb_im = f_re[..., None] * b_im + f_im[..., None] * b_re
    pr, pi = [jnp.ones_like(ab_re)], [jnp.zeros_like(ab_re)]
    for _ in range(L):
        pr, pi = pr + [pr[-1] * ab_re - pi[-1] * ab_im], pi + [pr[-1] * ab_im + pi[-1] * ab_re]
    eye = jnp.eye(gpb, dtype=F32)
    blk = lambda a: a.reshape((nblk, gpb) + a.shape[1:])
    bbx_re = jnp.einsum("bgpc,gh->bgchp", blk(bb_re), eye).reshape(nblk, LANES, gpb * p)
    bbx_im = jnp.einsum("bgpc,gh->bgchp", blk(bb_im), eye).reshape(nblk, LANES, gpb * p)
    ctx_re = jnp.einsum("bgop,gh->bgohp", blk(c_re), eye).reshape(nblk, LANES, gpb * p)
    ctx_im = jnp.einsum("bgop,gh->bgohp", blk(c_im), eye).reshape(nblk, LANES, gpb * p)
    ccm = jnp.einsum("rbgop,gh->brgpho", jnp.stack([blk(c_re), -blk(c_im)]), eye).reshape(nblk, 2 * gpb * p, LANES)
    pw_re = jnp.stack(pr).reshape(L + 1, nblk, gpb * p).transpose(1, 0, 2)
    pw_im = jnp.stack(pi).reshape(L + 1, nblk, gpb * p).transpose(1, 0, 2)
    d_blk = jnp.tile(d_skip.reshape(nblk, LANES), (1, L)).reshape(nblk, 1, L * LANES)
    return bbx_re, bbx_im, ctx_re, ctx_im, ccm, pw_re, pw_im, d_blk


def _s5_kernel(u_ref, bbr_ref, bbi_ref, ctr_ref, cti_ref, ccm_ref, pwr_ref, pwi_ref, d_ref, o_ref,
               t_ref, we_ref, vt_ref, e_ref, hin_ref):
    L = S5_CHUNK
    rows, ew = e_ref.shape
    ph = ew // 2

    @pl.when(pl.program_id(1) == 0)
    def _build_operands():
        t_ref[...] = jnp.zeros_like(t_ref)
        bbr, bbi = bbr_ref[0], bbi_ref[0]
        ctr, cti = ctr_ref[0], cti_ref[0]
        cc_hi, cc_lo = _split_bf16(ccm_ref[0])
        for lp in range(L):
            j = L - 1 - lp
            p_re, p_im = pwr_ref[0, j:j + 1, :], pwi_ref[0, j:j + 1, :]
            w = jnp.concatenate([bbr * p_re - bbi * p_im, bbr * p_im + bbi * p_re], axis=1)
            w_hi, w_lo = _split_bf16(w)
            we_ref[lp * LANES:(lp + 1) * LANES, :] = w_hi
            k = (jnp.dot(w_hi, cc_hi, preferred_element_type=F32)
                 + jnp.dot(w_lo, cc_hi, preferred_element_type=F32)
                 + jnp.dot(w_hi, cc_lo, preferred_element_type=F32)).astype(BF16)
            for l1 in range(L - j):
                t_ref[l1 * LANES:(l1 + 1) * LANES, (l1 + j) * LANES:(l1 + j + 1) * LANES] = k
        for l in range(L):
            p_re, p_im = pwr_ref[0, l + 1:l + 2, :], pwi_ref[0, l + 1:l + 2, :]
            v = jnp.concatenate([ctr * p_re - cti * p_im, -(ctr * p_im + cti * p_re)], axis=1)
            vt_ref[l * LANES:(l + 1) * LANES, :] = v.astype(BF16)

    xcat = jnp.concatenate([u_ref[pl.ds(l, rows, stride=L), :] for l in range(L)], axis=1)
    xb = xcat.astype(BF16)
    e_ref[...] = jnp.dot(xb, we_ref[...], preferred_element_type=F32)
    ar = pwr_ref[0, L:L + 1, :]
    ai = pwi_ref[0, L:L + 1, :]

    def body(r, carry):
        hr, hi = carry
        hin_ref[pl.ds(r, 1), 0:ph] = hr
        hin_ref[pl.ds(r, 1), ph:ew] = hi
        er = e_ref[pl.ds(r, 1), 0:ph]
        ei = e_ref[pl.ds(r, 1), ph:ew]
        return (ar * hr - ai * hi + er, ar * hi + ai * hr + ei)

    zero = jnp.zeros((1, ph), F32)
    lax.fori_loop(0, rows, body, (zero, zero))
    tw = 2 * LANES
    y = jnp.concatenate([jnp.dot(xb[:, :c0 + tw], t_ref[0:c0 + tw, c0:c0 + tw], preferred_element_type=F32)
                         for c0 in range(0, L * LANES, tw)], axis=1)
    y = y + lax.dot_general(hin_ref[...].astype(BF16), vt_ref[...], (((1,), (1,)), ((), ())),
                            preferred_element_type=F32)
    y = _gelu_tanh(y + d_ref[0] * xcat)
    for l in range(L):
        o_ref[pl.ds(l, rows, stride=L), :] = y[:, l * LANES:(l + 1) * LANES]


def _s5_mix(u, tables, nbatch):
    n, d = u.shape
    L = S5_CHUNK
    bbx_re, bbx_im, ctx_re, ctx_im, ccm, pw_re, pw_im, d_blk = tables
    nblk, _, sw = bbx_re.shape
    seq = n // nbatch
    rows = seq // L
    slab = lambda a: pl.BlockSpec((1,) + a.shape[1:], lambda g, b: (g, 0, 0))
    return pl.pallas_call(
        _s5_kernel,
        grid=(nblk, nbatch),
        in_specs=[pl.BlockSpec((seq, LANES), lambda g, b: (b, g))] + [slab(a) for a in tables],
        out_specs=pl.BlockSpec((seq, LANES), lambda g, b: (b, g)),
        out_shape=jax.ShapeDtypeStruct((n, d), F32),
        scratch_shapes=[
            pltpu.VMEM((L * LANES, L * LANES), BF16),
            pltpu.VMEM((L * LANES, 2 * sw), BF16),
            pltpu.VMEM((L * LANES, 2 * sw), BF16),
            pltpu.VMEM((rows, 2 * sw), F32),
            pltpu.VMEM((rows, 2 * sw), F32),
        ],
        compiler_params=_params(("arbitrary", "arbitrary")),
        name="s5_chunked_scan",
    )(u, *tables)


def _glu_kernel(y_ref, w_ref, b_ref, h_ref, g1_ref, n2g_ref, sc_ref, sh_ref, wr_ref, br_ref,
                h3_ref, t_ref, rt_ref):
    d = h_ref.shape[1]
    ne = wr_ref.shape[0]
    z = jnp.dot(y_ref[...].astype(BF16), w_ref[...], preferred_element_type=F32) + b_ref[...]
    h3 = h_ref[...] + g1_ref[0] * (z[:, :d] * jax.nn.sigmoid(z[:, d:]))
    h3_ref[...] = h3
    t = _rms_mod(h3, n2g_ref[...], sc_ref[0], sh_ref[0])
    rows = t.shape[0]
    for j in range(TOKEN_ROWS):
        t_ref[pl.ds(j, rows, stride=TOKEN_ROWS), :] = t[:, j * LANES:(j + 1) * LANES]
    logit = [jnp.sum(t * wr_ref[e:e + 1, :], axis=-1, keepdims=True) + br_ref[e:e + 1, :] for e in range(ne)]

    def top1(cols):
        m = cols[0]
        for col in cols[1:]:
            m = jnp.maximum(m, col)
        idx = jnp.full_like(m, float(ne))
        for e in reversed(range(ne)):
            idx = jnp.where(cols[e] == m, float(e), idx)
        return m, idx

    m1, i1 = top1(logit)
    m2, i2 = top1([jnp.where(i1 == float(e), -jnp.inf, logit[e]) for e in range(ne)])
    e2 = jnp.exp(m2 - m1)
    den = 1.0 + e2
    lane = lax.broadcasted_iota(jnp.int32, rt_ref.shape, 1)
    rt_ref[...] = jnp.where(lane == 0, 1.0 / den,
                            jnp.where(lane == 1, e2 / den,
                                      jnp.where(lane == 2, i1, jnp.where(lane == 3, i2, 0.0))))


def _glu_router(y, w_glu, b_glu, h, g1, n2g, sc2, sh2, w_router, b_router, seq):
    n, d = h.shape
    ne = w_router.shape[1]
    tm = _divisor(seq, 512, 8)
    tpb = seq // tm
    bmap = lambda i: (i // tpb, 0, 0)
    tile = pl.BlockSpec((tm, d), lambda i: (i, 0))
    return pl.pallas_call(
        _glu_kernel,
        grid=(n // tm,),
        in_specs=[
            tile,
            _resident((d, 2 * d)),
            _resident((1, 2 * d)),
            tile,
            pl.BlockSpec((1, 1, d), bmap),
            _resident((1, d)),
            pl.BlockSpec((1, 1, d), bmap),
            pl.BlockSpec((1, 1, d), bmap),
            _resident((ne, d)),
            _resident((ne, 1)),
        ],
        out_specs=[tile, pl.BlockSpec((tm * TOKEN_ROWS, LANES), lambda i: (i, 0)),
                   pl.BlockSpec((tm, LANES), lambda i: (i, 0))],
        out_shape=[
            jax.ShapeDtypeStruct((n, d), F32),
            jax.ShapeDtypeStruct((n * TOKEN_ROWS, LANES), F32),
            jax.ShapeDtypeStruct((n, LANES), F32),
        ],
        compiler_params=_params(("arbitrary",)),
        name="s5_glu_router",
    )(y, w_glu.astype(BF16), b_glu.reshape(1, 2 * d), h, g1, n2g.reshape(1, d), sc2, sh2,
      w_router.T, b_router.reshape(ne, 1))


def _token_copy(src_ref, src_tok, dst_ref, dst_tok, sem):
    s = pl.multiple_of(src_tok * TOKEN_ROWS, TOKEN_ROWS)
    t = pl.multiple_of(dst_tok * TOKEN_ROWS, TOKEN_ROWS)
    return pltpu.make_async_copy(src_ref.at[pl.ds(s, TOKEN_ROWS), :], dst_ref.at[pl.ds(t, TOKEN_ROWS), :], sem)


def _tokens_wait(src_ref, dst_ref, dst_tok, ntok, sem):
    t = pl.multiple_of(dst_tok * TOKEN_ROWS, TOKEN_ROWS)
    pltpu.make_async_copy(src_ref.at[pl.ds(0, ntok * TOKEN_ROWS), :],
                          dst_ref.at[pl.ds(t, ntok * TOKEN_ROWS), :], sem).wait()


def _expert_kernel(te_ref, tv_ref, nxt_ref, first_ref, t8_ref, wg_ref, wu_ref, wd_ref, o_ref,
                   xbuf_ref, xb_ref, acc_ref, sems, *, fchunk, nf):
    del te_ref
    i = pl.program_id(0)
    f = pl.program_id(1)
    nt = pl.num_programs(0)
    tm = xb_ref.shape[0]
    per_step = tm // nf
    slot = lax.rem(i, 2)

    def gather(idx_ref, into, r0, count):
        def body(j, c):
            for u in range(DMA_UNROLL):
                r = r0 + j * DMA_UNROLL + u
                _token_copy(t8_ref, idx_ref[0, 0, r], xbuf_ref, into * tm + r, sems.at[into]).start()
            return c
        lax.fori_loop(0, count // DMA_UNROLL, body, 0)

    @pl.when(jnp.logical_and(i == 0, f == 0))
    def _():
        gather(first_ref, 0, 0, tm)

    @pl.when(jnp.logical_and(i + 1 < nt, tv_ref[jnp.minimum(i + 1, nt - 1)] == 1))
    def _():
        gather(nxt_ref, 1 - slot, f * per_step, per_step)

    @pl.when(jnp.logical_and(tv_ref[i] == 0, f == nf - 1))
    def _():
        o_ref[...] = jnp.zeros_like(o_ref)

    @pl.when(tv_ref[i] == 1)
    def _():
        @pl.when(f == 0)
        def _():
            _tokens_wait(t8_ref, xbuf_ref, slot * tm, tm, sems.at[slot])
            base = pl.multiple_of(slot * (tm * TOKEN_ROWS), TOKEN_ROWS)
            for j in range(TOKEN_ROWS):
                xb_ref[:, j * LANES:(j + 1) * LANES] = (
                    xbuf_ref[pl.ds(base + j, tm, stride=TOKEN_ROWS), :].astype(BF16))

        x = xb_ref[...]
        tf = wg_ref.shape[2]
        y = None
        for f0 in range(0, tf, fchunk):
            f1 = min(f0 + fchunk, tf)
            g = jnp.dot(x, wg_ref[0, :, f0:f1], preferred_element_type=F32)
            u = jnp.dot(x, wu_ref[0, :, f0:f1], preferred_element_type=F32)
            a = (_silu(g) * u).astype(BF16)
            yy = jnp.dot(a, wd_ref[0, f0:f1, :], preferred_element_type=F32)
            y = yy if y is None else y + yy

        @pl.when(f == 0)
        def _():
            acc_ref[...] = y

        @pl.when(f > 0)
        def _():
            acc_ref[...] += y

        @pl.when(f == nf - 1)
        def _():
            acc = acc_ref[...]
            for j in range(TOKEN_ROWS):
                o_ref[pl.ds(j, tm, stride=TOKEN_ROWS), :] = acc[:, j * LANES:(j + 1) * LANES]


def _experts(t8, src_tok, tile_expert, tile_valid, wg, wu, wd, tm):
    n_tiles = src_tok.shape[0]
    d = TOKEN_ROWS * LANES
    dff = wg.shape[2]
    tf = _divisor(dff, 1792, 256)
    nf = dff // tf
    assert tm % (nf * DMA_UNROLL) == 0
    fidx = lambda i, f, tv: f * tv[i] + (nf - 1) * (1 - tv[i])
    smem_tile = lambda imap: pl.BlockSpec((1, 1, tm), imap, memory_space=pltpu.SMEM)
    grid_spec = pltpu.PrefetchScalarGridSpec(
        num_scalar_prefetch=2,
        grid=(n_tiles, nf),
        in_specs=[
            smem_tile(lambda i, f, te, tv: (jnp.minimum(i + 1, n_tiles - 1), 0, 0)),
            smem_tile(lambda i, f, te, tv: (0, 0, 0)),
            pl.BlockSpec(memory_space=pl.ANY),
            pl.BlockSpec((1, d, tf), lambda i, f, te, tv: (te[i], 0, fidx(i, f, tv))),
            pl.BlockSpec((1, d, tf), lambda i, f, te, tv: (te[i], 0, fidx(i, f, tv))),
            pl.BlockSpec((1, tf, d), lambda i, f, te, tv: (te[i], fidx(i, f, tv), 0)),
        ],
        out_specs=pl.BlockSpec((tm * TOKEN_ROWS, LANES), lambda i, f, te, tv: (i, 0)),
        scratch_shapes=[
            pltpu.VMEM((2 * tm * TOKEN_ROWS, LANES), F32),
            pltpu.VMEM((tm, d), BF16),
            pltpu.VMEM((tm, d), F32),
            pltpu.SemaphoreType.DMA((2,)),
        ],
    )
    return pl.pallas_call(
        functools.partial(_expert_kernel, fchunk=1024, nf=nf),
        grid_spec=grid_spec,
        out_shape=jax.ShapeDtypeStruct((n_tiles * tm * TOKEN_ROWS, LANES), F32),
        compiler_params=_params(("arbitrary", "arbitrary")),
        name="moe_experts",
    )(tile_expert, tile_valid, src_tok, src_tok, t8, wg, wu, wd)


def _combine_kernel(nxt_ref, first_ref, ys_ref, h_ref, rt_ref, g2_ref, fg_ref, o_ref, ybuf_ref, sems):
    i = pl.program_id(0)
    ns = pl.num_programs(0)
    tc = h_ref.shape[0]
    npair = TOP_K * tc
    slot = lax.rem(i, 2)

    def gather(idx_ref, into):
        def body(j, c):
            for u in range(DMA_UNROLL):
                r = j * DMA_UNROLL + u
                for k in range(TOP_K):
                    _token_copy(ys_ref, idx_ref[0, 0, TOP_K * r + k], ybuf_ref, into * npair + k * tc + r,
                                sems.at[into]).start()
            return c
        lax.fori_loop(0, tc // DMA_UNROLL, body, 0)

    @pl.when(i == 0)
    def _():
        gather(first_ref, 0)

    @pl.when(i + 1 < ns)
    def _():
        gather(nxt_ref, 1 - slot)

    _tokens_wait(ys_ref, ybuf_ref, slot * npair, npair, sems.at[slot])
    rt = rt_ref[...]
    base = pl.multiple_of(slot * (npair * TOKEN_ROWS), TOKEN_ROWS)
    pieces = []
    for j in range(TOKEN_ROWS):
        yj = None
        for k in range(TOP_K):
            v = ybuf_ref[pl.ds(base + k * tc * TOKEN_ROWS + j, tc, stride=TOKEN_ROWS), :]
            yj = rt[:, k:k + 1] * v if yj is None else yj + rt[:, k:k + 1] * v
        pieces.append(yj)
    y = jnp.concatenate(pieces, axis=1)
    h4 = h_ref[...] + g2_ref[0] * y
    ms = jnp.mean(h4 * h4, axis=-1, keepdims=True)
    o_ref[...] = h4 * lax.rsqrt(ms + NORM_EPS) * fg_ref[...]


def _combine(ys8, dest, h2d, rt, g2, final_g, seq):
    n, d = h2d.shape
    tc = _divisor(seq, 256, DMA_UNROLL)
    tpb = seq // tc
    ns = n // tc
    dest3 = dest.reshape(ns, 1, TOP_K * tc)
    smem_tile = lambda imap: pl.BlockSpec((1, 1, TOP_K * tc), imap, memory_space=pltpu.SMEM)
    return pl.pallas_call(
        _combine_kernel,
        grid=(ns,),
        in_specs=[
            smem_tile(lambda i: (jnp.minimum(i + 1, ns - 1), 0, 0)),
            smem_tile(lambda i: (0, 0, 0)),
            pl.BlockSpec(memory_space=pl.ANY),
            pl.BlockSpec((tc, d), lambda i: (i, 0)),
            pl.BlockSpec((tc, LANES), lambda i: (i, 0)),
            pl.BlockSpec((1, 1, d), lambda i: (i // tpb, 0, 0)),
            _resident((1, d)),
        ],
        out_specs=pl.BlockSpec((tc, d), lambda i: (i, 0)),
        out_shape=jax.ShapeDtypeStruct((n, d), F32),
        scratch_shapes=[pltpu.VMEM((2 * TOP_K * tc * TOKEN_ROWS, LANES), F32), pltpu.SemaphoreType.DMA((2,))],
        compiler_params=_params(("arbitrary",)),
        name="moe_combine_norm",
    )(dest3, dest3, ys8, h2d, rt, g2, final_g.reshape(1, d))


def _route(rt, n_experts, tm):
    n = rt.shape[0]
    npairs = n * TOP_K
    pair_expert = rt[:, 2:2 + TOP_K].astype(jnp.int32).reshape(npairs)
    onehot = (pair_expert[:, None] == jnp.arange(n_experts, dtype=jnp.int32)[None, :]).astype(jnp.int32)
    csum = jnp.cumsum(onehot, axis=0)
    rank = jnp.sum(onehot * csum, axis=1) - 1
    counts = csum[-1]
    tiles = (counts + tm - 1) // tm
    tile_end = jnp.cumsum(tiles)
    tile_start = tile_end - tiles
    dest = tile_start[pair_expert] * tm + rank
    n_tiles = npairs // tm + n_experts
    ti = jnp.arange(n_tiles, dtype=jnp.int32)
    n_active = tile_end[-1]
    tile_valid = (ti < n_active).astype(jnp.int32)
    last_used = jnp.minimum(ti, n_active - 1)
    expert_of = jnp.sum((last_used[:, None] >= tile_end[None, :]).astype(jnp.int32), axis=1)
    tile_expert = jnp.minimum(expert_of, n_experts - 1)
    order = jnp.sort(pair_expert * npairs + jnp.arange(npairs, dtype=jnp.int32)) % npairs
    count_start = jnp.cumsum(counts) - counts
    row = jnp.arange(n_tiles * tm, dtype=jnp.int32)
    row_expert = tile_expert[row // tm]
    within = row - tile_start[row_expert] * tm
    used = jnp.logical_and(within < counts[row_expert], tile_valid[row // tm] == 1)
    pair_of_row = order[jnp.clip(count_start[row_expert] + within, 0, npairs - 1)]
    src_tok = jnp.where(used, pair_of_row // TOP_K, 0).astype(jnp.int32).reshape(n_tiles, 1, tm)
    return dest.astype(jnp.int32), src_tok, tile_expert, tile_valid


def kernel(x, c, mod_w, mod_b, norm1_g, norm2_g, conv_w_pw1, conv_b_pw1, conv_w_dw, conv_b_dw, conv_ln_g, conv_ln_b, conv_w_pw2, conv_b_pw2, ssm_a_re, ssm_a_im, ssm_log_dt, ssm_b_re, ssm_b_im, ssm_c_re, ssm_c_im, ssm_d, ssm_w_glu, ssm_b_glu, ffn_w_gate, ffn_w_up, ffn_w_down, moe_w_router, moe_b_router, moe_w_gate, moe_w_up, moe_w_down, final_norm_g):
    nb, seq, d = x.shape
    n = nb * seq
    assert mod_w.shape[0] == 2 and seq % S5_CHUNK == 0 and d == TOKEN_ROWS * LANES
    n_experts = moe_w_router.shape[-1]

    mod = _modulation(c, mod_w, mod_b)
    parts = [[mod[i, :, k * d:(k + 1) * d].reshape(nb, 1, d) for k in range(6)] for i in range(2)]
    sh1a, sc1a, g1a, sh2a, sc2a, g2a = parts[0]
    sh1b, sc1b, g1b, sh2b, sc2b, g2b = parts[1]

    u = _pw1(x.reshape(n, d), norm1_g[0], sc1a, sh1a, conv_w_pw1[0], conv_b_pw1[0], seq)
    h1 = _conv_block(u.reshape(nb, seq, d), x, conv_w_dw[0], conv_b_dw[0], conv_ln_g[0], conv_ln_b[0],
                     conv_w_pw2[0], conv_b_pw2[0], g1a)
    h2, u1 = _dense_ffn(h1.reshape(n, d), norm2_g[0], sc2a, sh2a, g2a,
                        ffn_w_gate[0], ffn_w_up[0], ffn_w_down[0], norm1_g[1], sc1b, sh1b, seq)

    tables = _s5_tables(ssm_a_re[0], ssm_a_im[0], ssm_log_dt[0], ssm_b_re[0], ssm_b_im[0],
                        ssm_c_re[0], ssm_c_im[0], ssm_d[0])
    y1 = _s5_mix(u1, tables, nb)
    h3, t8, rt = _glu_router(y1, ssm_w_glu[0], ssm_b_glu[0], h2, g1b, norm2_g[1], sc2b, sh2b,
                             moe_w_router[0], moe_b_router[0], seq)
    tm = 512
    dest, src_tok, tile_expert, tile_valid = _route(rt, n_experts, tm)
    ys8 = _experts(t8, src_tok, tile_expert, tile_valid, moe_w_gate[0].astype(BF16), moe_w_up[0].astype(BF16),
                   moe_w_down[0].astype(BF16), tm)
    out = _combine(ys8, dest, h3, rt, g2b, final_norm_g, seq)
    return out.reshape(nb, seq, d)
```

```python
import functools

import jax
import jax.numpy as jnp
from jax import lax
from jax.experimental import pallas as pl
from jax.experimental.pallas import tpu as pltpu

F32 = jnp.float32
BF16 = jnp.bfloat16
HIGHEST = lax.Precision.HIGHEST

NORM_EPS = 1e-6
TOP_K = 2
LANES = 128
S5_CHUNK = 16
CONV_HALO = 32
CONV_ROWS = 64
TOKEN_ROWS = 8
DMA_UNROLL = 8
VMEM_LIMIT = 56 * 2**20


def _params(sem):
    return pltpu.CompilerParams(dimension_semantics=sem, vmem_limit_bytes=VMEM_LIMIT)


def _divisor(n, cap, mult):
    best = None
    for d in range(mult, min(n, cap) + 1, mult):
        if n % d == 0:
            best = d
    assert best is not None, (n, cap, mult)
    return best


def _resident(shape):
    nd = len(shape)
    return pl.BlockSpec(shape, lambda *_: (0,) * nd, pipeline_mode=pl.Buffered(1))


def _rms_mod(x, g, sc, sh):
    ms = jnp.mean(x * x, axis=-1, keepdims=True)
    return (x * lax.rsqrt(ms + NORM_EPS) * g) * (1.0 + sc) + sh


def _silu(x):
    return x * jax.nn.sigmoid(x)


def _gelu_tanh(x):
    c = 0.7978845608028654
    return 0.5 * x * (1.0 + jnp.tanh(c * (x + 0.044715 * (x * x * x))))


def _split_bf16(x):
    hi = x.astype(BF16)
    return hi, (x - hi.astype(F32)).astype(BF16)


def _mod_kernel(c_ref, w_ref, b_ref, o_ref):
    cond = _silu(c_ref[...])
    o_ref[0] = jnp.dot(cond, w_ref[0], preferred_element_type=F32, precision=HIGHEST) + b_ref[0]


def _modulation(c, mod_w, mod_b):
    depth, d, d6 = mod_w.shape
    nb = c.shape[0]
    rows = -(-nb // 8) * 8
    c8 = jnp.pad(c, ((0, rows - nb), (0, 0)))
    tn = _divisor(d6, 1536, LANES)
    out = pl.pallas_call(
        _mod_kernel,
        grid=(depth, d6 // tn),
        in_specs=[
            pl.BlockSpec((rows, d), lambda i, j: (0, 0)),
            pl.BlockSpec((1, d, tn), lambda i, j: (i, 0, j)),
            pl.BlockSpec((1, 1, tn), lambda i, j: (i, 0, j)),
        ],
        out_specs=pl.BlockSpec((1, rows, tn), lambda i, j: (i, 0, j)),
        out_shape=jax.ShapeDtypeStruct((depth, rows, d6), F32),
        compiler_params=_params(("arbitrary", "arbitrary")),
        name="adaln_mod",
    )(c8, mod_w, mod_b.reshape(depth, 1, d6))
    return out[:, :nb, :]


def _pw1_kernel(x_ref, g_ref, sc_ref, sh_ref, w_ref, b_ref, o_ref):
    x = x_ref[...]
    d = x.shape[1]
    y = _rms_mod(x, g_ref[...], sc_ref[0], sh_ref[0])
    u = jnp.dot(y.astype(BF16), w_ref[...], preferred_element_type=F32) + b_ref[...]
    o_ref[...] = u[:, :d] * jax.nn.sigmoid(u[:, d:])


def _pw1(x2, norm_g, sc, sh, w, b, seq):
    n, d = x2.shape
    tm = _divisor(seq, 512, 8)
    tpb = seq // tm
    bmap = lambda i: (i // tpb, 0, 0)
    return pl.pallas_call(
        _pw1_kernel,
        grid=(n // tm,),
        in_specs=[
            pl.BlockSpec((tm, d), lambda i: (i, 0)),
            _resident((1, d)),
            pl.BlockSpec((1, 1, d), bmap),
            pl.BlockSpec((1, 1, d), bmap),
            _resident((d, 2 * d)),
            _resident((1, 2 * d)),
        ],
        out_specs=pl.BlockSpec((tm, d), lambda i: (i, 0)),
        out_shape=jax.ShapeDtypeStruct((n, d), F32),
        compiler_params=_params(("arbitrary",)),
        name="conv_pw1_glu",
    )(x2, norm_g.reshape(1, d), sc, sh, w.astype(BF16), b.reshape(1, 2 * d))


def _conv_kernel(cur_ref, prev_ref, wdw_ref, bdw_ref, lng_ref, lnb_ref, w2_ref, b2_ref, h_ref, g1_ref,
                 o_ref, buf_ref, cv_ref, *, taps):
    tm, d = cv_ref.shape
    i = pl.program_id(1)
    buf_ref[0:CONV_HALO, :] = jnp.where(i > 0, prev_ref[0], 0.0)
    buf_ref[CONV_HALO:, :] = cur_ref[0]
    off0 = CONV_HALO - (taps - 1)
    span = CONV_ROWS + CONV_HALO
    for c in range(d // LANES):
        lanes = slice(c * LANES, (c + 1) * LANES)

        def body(k, carry, lanes=lanes):
            r0 = pl.multiple_of(k * CONV_ROWS, CONV_ROWS)
            v = buf_ref[pl.ds(r0, span), lanes]
            acc = jnp.broadcast_to(bdw_ref[:, lanes], (CONV_ROWS, LANES))
            for s in range(8):
                xs = v if s == 0 else pltpu.roll(v, span - s, axis=0)
                for q in range(span // 8):
                    o = 8 * q + s
                    if o < off0 or o > off0 + taps - 1:
                        continue
                    wrow = wdw_ref[o - off0:o - off0 + 1, lanes]
                    acc = acc + wrow * xs[8 * q:8 * q + CONV_ROWS, :]
            cv_ref[pl.ds(r0, CONV_ROWS), lanes] = acc
            return carry

        lax.fori_loop(0, tm // CONV_ROWS, body, 0)
    v = cv_ref[...]
    mu = jnp.mean(v, axis=-1, keepdims=True)
    xc = v - mu
    var = jnp.mean(xc * xc, axis=-1, keepdims=True)
    y = _silu(xc * lax.rsqrt(var + NORM_EPS) * lng_ref[...] + lnb_ref[...])
    z = jnp.dot(y.astype(BF16), w2_ref[...], preferred_element_type=F32) + b2_ref[...]
    o_ref[0] = h_ref[0] + g1_ref[0] * z


def _conv_block(u3, x3, w_dw, b_dw, ln_g, ln_b, w2, b2, g1):
    nb, seq, d = x3.shape
    taps = w_dw.shape[0]
    assert taps - 1 <= CONV_HALO and d % LANES == 0
    tm = _divisor(seq, 512, CONV_ROWS)
    hb = tm // CONV_HALO
    wpad = jnp.pad(w_dw, ((0, -taps % 8), (0, 0)))
    row = lambda a: a.reshape(1, d)
    return pl.pallas_call(
        functools.partial(_conv_kernel, taps=taps),
        grid=(nb, seq // tm),
        in_specs=[
            pl.BlockSpec((1, tm, d), lambda b, i: (b, i, 0)),
            pl.BlockSpec((1, CONV_HALO, d), lambda b, i: (b, jnp.maximum(i * hb - 1, 0), 0)),
            _resident(wpad.shape),
            _resident((1, d)),
            _resident((1, d)),
            _resident((1, d)),
            _resident((d, d)),
            _resident((1, d)),
            pl.BlockSpec((1, tm, d), lambda b, i: (b, i, 0)),
            pl.BlockSpec((1, 1, d), lambda b, i: (b, 0, 0)),
        ],
        out_specs=pl.BlockSpec((1, tm, d), lambda b, i: (b, i, 0)),
        out_shape=jax.ShapeDtypeStruct((nb, seq, d), F32),
        scratch_shapes=[pltpu.VMEM((tm + CONV_HALO, d), F32), pltpu.VMEM((tm, d), F32)],
        compiler_params=_params(("arbitrary", "arbitrary")),
        name="conv_dw_ln_pw2",
    )(u3, u3, wpad, row(b_dw), row(ln_g), row(ln_b), w2.astype(BF16), row(b2), x3, g1)


def _ffn_kernel(h_ref, n2g_ref, sc_ref, sh_ref, g2_ref, wg_ref, wu_ref, wd_ref, n1g_ref, sc1_ref, sh1_ref,
                h2_ref, u_ref, *, fchunk):
    h = h_ref[...]
    t = _rms_mod(h, n2g_ref[...], sc_ref[0], sh_ref[0]).astype(BF16)
    dff = wg_ref.shape[1]
    acc = None
    for f0 in range(0, dff, fchunk):
        f1 = min(f0 + fchunk, dff)
        g = jnp.dot(t, wg_ref[:, f0:f1], preferred_element_type=F32)
        u = jnp.dot(t, wu_ref[:, f0:f1], preferred_element_type=F32)
        a = (_silu(g) * u).astype(BF16)
        y = jnp.dot(a, wd_ref[f0:f1, :], preferred_element_type=F32)
        acc = y if acc is None else acc + y
    h2 = h + g2_ref[0] * acc
    h2_ref[...] = h2
    u_ref[...] = _rms_mod(h2, n1g_ref[...], sc1_ref[0], sh1_ref[0])


def _dense_ffn(h, n2g, sc2, sh2, g2, wg, wu, wd, n1g, sc1, sh1, seq):
    n, d = h.shape
    dff = wg.shape[1]
    tm = _divisor(seq, 512, 8)
    tpb = seq // tm
    bmap = lambda i: (i // tpb, 0, 0)
    tile = pl.BlockSpec((tm, d), lambda i: (i, 0))
    return pl.pallas_call(
        functools.partial(_ffn_kernel, fchunk=512),
        grid=(n // tm,),
        in_specs=[
            tile,
            _resident((1, d)),
            pl.BlockSpec((1, 1, d), bmap),
            pl.BlockSpec((1, 1, d), bmap),
            pl.BlockSpec((1, 1, d), bmap),
            _resident((d, dff)),
            _resident((d, dff)),
            _resident((dff, d)),
            _resident((1, d)),
            pl.BlockSpec((1, 1, d), bmap),
            pl.BlockSpec((1, 1, d), bmap),
        ],
        out_specs=[tile, tile],
        out_shape=[jax.ShapeDtypeStruct((n, d), F32), jax.ShapeDtypeStruct((n, d), F32)],
        compiler_params=_params(("arbitrary",)),
        name="dense_swiglu",
    )(h, n2g.reshape(1, d), sc2, sh2, g2, wg.astype(BF16), wu.astype(BF16), wd.astype(BF16),
      n1g.reshape(1, d), sc1, sh1)


def _s5_tables(a_re, a_im, log_dt, b_re, b_im, c_re, c_im, d_skip):
    ng, p = a_re.shape
    c = b_re.shape[-1]
    gpb = LANES // c
    nblk = ng // gpb
    L = S5_CHUNK
    dt = jnp.exp(log_dt)[:, None]
    mag = jnp.exp(dt * a_re)
    ab_re = mag * jnp.cos(dt * a_im)
    ab_im = mag * jnp.sin(dt * a_im)
    den = a_re * a_re + a_im * a_im
    f_re = ((ab_re - 1.0) * a_re + ab_im * a_im) / den
    f_im = (ab_im * a_re - (ab_re - 1.0) * a_im) / den
    bb_re = f_re[..., None] * b_re - f_im[..., None] * b_im
    bb_im = f_re[..., None] * b_im + f_im[..., None] * b_re
    pr, pi = [jnp.ones_like(ab_re)], [jnp.zeros_like(ab_re)]
    for _ in range(L):
        pr, pi = pr + [pr[-1] * ab_re - pi[-1] * ab_im], pi + [pr[-1] * ab_im + pi[-1] * ab_re]
    eye = jnp.eye(gpb, dtype=F32)
    blk = lambda a: a.reshape((nblk, gpb) + a.shape[1:])
    bbx_re = jnp.einsum("bgpc,gh->bgchp", blk(bb_re), eye).reshape(nblk, LANES, gpb * p)
    bbx_im = jnp.einsum("bgpc,gh->bgchp", blk(bb_im), eye).reshape(nblk, LANES, gpb * p)
    ctx_re = jnp.einsum("bgop,gh->bgohp", blk(c_re), eye).reshape(nblk, LANES, gpb * p)
    ctx_im = jnp.einsum("bgop,gh->bgohp", blk(c_im), eye).reshape(nblk, LANES, gpb * p)
    ccm = jnp.einsum("rbgop,gh->brgpho", jnp.stack([blk(c_re), -blk(c_im)]), eye).reshape(nblk, 2 * gpb * p, LANES)
    pw_re = jnp.stack(pr).reshape(L + 1, nblk, gpb * p).transpose(1, 0, 2)
    pw_im = jnp.stack(pi).reshape(L + 1, nblk, gpb * p).transpose(1, 0, 2)
    d_blk = jnp.tile(d_skip.reshape(nblk, LANES), (1, L)).reshape(nblk, 1, L * LANES)
    return bbx_re, bbx_im, ctx_re, ctx_im, ccm, pw_re, pw_im, d_blk


def _s5_kernel(u_ref, bbr_ref, bbi_ref, ctr_ref, cti_ref, ccm_ref, pwr_ref, pwi_ref, d_ref, o_ref,
               t_ref, we_ref, vt_ref, e_ref, hin_ref):
    L = S5_CHUNK
    rows, ew = e_ref.shape
    ph = ew // 2

    @pl.when(pl.program_id(1) == 0)
    def _build_operands():
        t_ref[...] = jnp.zeros_like(t_ref)
        bbr, bbi = bbr_ref[0], bbi_ref[0]
        ctr, cti = ctr_ref[0], cti_ref[0]
        cc_hi, cc_lo = _split_bf16(ccm_ref[0])
        for lp in range(L):
            j = L - 1 - lp
            p_re, p_im = pwr_ref[0, j:j + 1, :], pwi_ref[0, j:j + 1, :]
            w = jnp.concatenate([bbr * p_re - bbi * p_im, bbr * p_im + bbi * p_re], axis=1)
            w_hi, w_lo = _split_bf16(w)
            we_ref[lp * LANES:(lp + 1) * LANES, :] = w_hi
            k = (jnp.dot(w_hi, cc_hi, preferred_element_type=F32)
                 + jnp.dot(w_lo, cc_hi, preferred_element_type=F32)
                 + jnp.dot(w_hi, cc_lo, preferred_element_type=F32)).astype(BF16)
            for l1 in range(L - j):
                t_ref[l1 * LANES:(l1 + 1) * LANES, (l1 + j) * LANES:(l1 + j + 1) * LANES] = k
        for l in range(L):
            p_re, p_im = pwr_ref[0, l + 1:l + 2, :], pwi_ref[0, l + 1:l + 2, :]
            v = jnp.concatenate([ctr * p_re - cti * p_im, -(ctr * p_im + cti * p_re)], axis=1)
            vt_ref[l * LANES:(l + 1) * LANES, :] = v.astype(BF16)

    xcat = jnp.concatenate([u_ref[pl.ds(l, rows, stride=L), :] for l in range(L)], axis=1)
    xb = xcat.astype(BF16)
    e_ref[...] = jnp.dot(xb, we_ref[...], preferred_element_type=F32)
    ar = pwr_ref[0, L:L + 1, :]
    ai = pwi_ref[0, L:L + 1, :]

    def body(r, carry):
        hr, hi = carry
        hin_ref[pl.ds(r, 1), 0:ph] = hr
        hin_ref[pl.ds(r, 1), ph:ew] = hi
        er = e_ref[pl.ds(r, 1), 0:ph]
        ei = e_ref[pl.ds(r, 1), ph:ew]
        return (ar * hr - ai * hi + er, ar * hi + ai * hr + ei)

    zero = jnp.zeros((1, ph), F32)
    lax.fori_loop(0, rows, body, (zero, zero))
    tw = 2 * LANES
    y = jnp.concatenate([jnp.dot(xb[:, :c0 + tw], t_ref[0:c0 + tw, c0:c0 + tw], preferred_element_type=F32)
                         for c0 in range(0, L * LANES, tw)], axis=1)
    y = y + lax.dot_general(hin_ref[...].astype(BF16), vt_ref[...], (((1,), (1,)), ((), ())),
                            preferred_element_type=F32)
    y = _gelu_tanh(y + d_ref[0] * xcat)
    for l in range(L):
        o_ref[pl.ds(l, rows, stride=L), :] = y[:, l * LANES:(l + 1) * LANES]


def _s5_mix(u, tables, nbatch):
    n, d = u.shape
    L = S5_CHUNK
    bbx_re, bbx_im, ctx_re, ctx_im, ccm, pw_re, pw_im, d_blk = tables
    nblk, _, sw = bbx_re.shape
    seq = n // nbatch
    rows = seq // L
    slab = lambda a: pl.BlockSpec((1,) + a.shape[1:], lambda g, b: (g, 0, 0))
    return pl.pallas_call(
        _s5_kernel,
        grid=(nblk, nbatch),
        in_specs=[pl.BlockSpec((seq, LANES), lambda g, b: (b, g))] + [slab(a) for a in tables],
        out_specs=pl.BlockSpec((seq, LANES), lambda g, b: (b, g)),
        out_shape=jax.ShapeDtypeStruct((n, d), F32),
        scratch_shapes=[
            pltpu.VMEM((L * LANES, L * LANES), BF16),
            pltpu.VMEM((L * LANES, 2 * sw), BF16),
            pltpu.VMEM((L * LANES, 2 * sw), BF16),
            pltpu.VMEM((rows, 2 * sw), F32),
            pltpu.VMEM((rows, 2 * sw), F32),
        ],
        compiler_params=_params(("arbitrary", "arbitrary")),
        name="s5_chunked_scan",
    )(u, *tables)


def _glu_kernel(y_ref, w_ref, b_ref, h_ref, g1_ref, n2g_ref, sc_ref, sh_ref, wr_ref, br_ref,
                h3_ref, t_ref, rt_ref):
    d = h_ref.shape[1]
    ne = wr_ref.shape[0]
    z = jnp.dot(y_ref[...].astype(BF16), w_ref[...], preferred_element_type=F32) + b_ref[...]
    h3 = h_ref[...] + g1_ref[0] * (z[:, :d] * jax.nn.sigmoid(z[:, d:]))
    h3_ref[...] = h3
    t = _rms_mod(h3, n2g_ref[...], sc_ref[0], sh_ref[0])
    rows = t.shape[0]
    for j in range(TOKEN_ROWS):
        t_ref[pl.ds(j, rows, stride=TOKEN_ROWS), :] = t[:, j * LANES:(j + 1) * LANES]
    logit = [jnp.sum(t * wr_ref[e:e + 1, :], axis=-1, keepdims=True) + br_ref[e:e + 1, :] for e in range(ne)]

    def top1(cols):
        m = cols[0]
        for col in cols[1:]:
            m = jnp.maximum(m, col)
        idx = jnp.full_like(m, float(ne))
        for e in reversed(range(ne)):
            idx = jnp.where(cols[e] == m, float(e), idx)
        return m, idx

    m1, i1 = top1(logit)
    m2, i2 = top1([jnp.where(i1 == float(e), -jnp.inf, logit[e]) for e in range(ne)])
    e2 = jnp.exp(m2 - m1)
    den = 1.0 + e2
    lane = lax.broadcasted_iota(jnp.int32, rt_ref.shape, 1)
    rt_ref[...] = jnp.where(lane == 0, 1.0 / den,
                            jnp.where(lane == 1, e2 / den,
                                      jnp.where(lane == 2, i1, jnp.where(lane == 3, i2, 0.0))))


def _glu_router(y, w_glu, b_glu, h, g1, n2g, sc2, sh2, w_router, b_router, seq):
    n, d = h.shape
    ne = w_router.shape[1]
    tm = _divisor(seq, 512, 8)
    tpb = seq // tm
    bmap = lambda i: (i // tpb, 0, 0)
    tile = pl.BlockSpec((tm, d), lambda i: (i, 0))
    return pl.pallas_call(
        _glu_kernel,
        grid=(n // tm,),
        in_specs=[
            tile,
            _resident((d, 2 * d)),
            _resident((1, 2 * d)),
            tile,
            pl.BlockSpec((1, 1, d), bmap),
            _resident((1, d)),
            pl.BlockSpec((1, 1, d), bmap),
            pl.BlockSpec((1, 1, d), bmap),
            _resident((ne, d)),
            _resident((ne, 1)),
        ],
        out_specs=[tile, pl.BlockSpec((tm * TOKEN_ROWS, LANES), lambda i: (i, 0)),
                   pl.BlockSpec((tm, LANES), lambda i: (i, 0))],
        out_shape=[
            jax.ShapeDtypeStruct((n, d), F32),
            jax.ShapeDtypeStruct((n * TOKEN_ROWS, LANES), F32),
            jax.ShapeDtypeStruct((n, LANES), F32),
        ],
        compiler_params=_params(("arbitrary",)),
        name="s5_glu_router",
    )(y, w_glu.astype(BF16), b_glu.reshape(1, 2 * d), h, g1, n2g.reshape(1, d), sc2, sh2,
      w_router.T, b_router.reshape(ne, 1))


def _token_copy(src_ref, src_tok, dst_ref, dst_tok, sem):
    s = pl.multiple_of(src_tok * TOKEN_ROWS, TOKEN_ROWS)
    t = pl.multiple_of(dst_tok * TOKEN_ROWS, TOKEN_ROWS)
    return pltpu.make_async_copy(src_ref.at[pl.ds(s, TOKEN_ROWS), :], dst_ref.at[pl.ds(t, TOKEN_ROWS), :], sem)


def _tokens_wait(src_ref, dst_ref, dst_tok, ntok, sem):
    t = pl.multiple_of(dst_tok * TOKEN_ROWS, TOKEN_ROWS)
    pltpu.make_async_copy(src_ref.at[pl.ds(0, ntok * TOKEN_ROWS), :],
                          dst_ref.at[pl.ds(t, ntok * TOKEN_ROWS), :], sem).wait()


def _expert_kernel(te_ref, tv_ref, nxt_ref, first_ref, t8_ref, wg_ref, wu_ref, wd_ref, o_ref,
                   xbuf_ref, xb_ref, acc_ref, sems, *, fchunk, nf):
    del te_ref
    i = pl.program_id(0)
    f = pl.program_id(1)
    nt = pl.num_programs(0)
    tm = xb_ref.shape[0]
    slot = lax.rem(i, 2)

    def gather(idx_ref, into):
        def body(j, c):
            for u in range(DMA_UNROLL):
                r = j * DMA_UNROLL + u
                _token_copy(t8_ref, idx_ref[0, 0, r], xbuf_ref, into * tm + r, sems.at[into]).start(priority=1)
            return c
        lax.fori_loop(0, tm // DMA_UNROLL, body, 0)

    @pl.when(jnp.logical_and(i == 0, f == 0))
    def _():
        gather(first_ref, 0)

    @pl.when(jnp.logical_and(f == 0, jnp.logical_and(i + 1 < nt, tv_ref[jnp.minimum(i + 1, nt - 1)] == 1)))
    def _():
        gather(nxt_ref, 1 - slot)

    @pl.when(jnp.logical_and(tv_ref[i] == 0, f == nf - 1))
    def _():
        o_ref[...] = jnp.zeros_like(o_ref)

    @pl.when(tv_ref[i] == 1)
    def _():
        @pl.when(f == 0)
        def _():
            _tokens_wait(t8_ref, xbuf_ref, slot * tm, tm, sems.at[slot])
            base = pl.multiple_of(slot * (tm * TOKEN_ROWS), TOKEN_ROWS)
            for j in range(TOKEN_ROWS):
                xb_ref[:, j * LANES:(j + 1) * LANES] = (
                    xbuf_ref[pl.ds(base + j, tm, stride=TOKEN_ROWS), :].astype(BF16))

        x = xb_ref[...]
        tf = wg_ref.shape[2]
        y = None
        for f0 in range(0, tf, fchunk):
            f1 = min(f0 + fchunk, tf)
            g = jnp.dot(x, wg_ref[0, :, f0:f1], preferred_element_type=F32)
            u = jnp.dot(x, wu_ref[0, :, f0:f1], preferred_element_type=F32)
            a = (_silu(g) * u).astype(BF16)
            yy = jnp.dot(a, wd_ref[0, f0:f1, :], preferred_element_type=F32)
            y = yy if y is None else y + yy

        @pl.when(f == 0)
        def _():
            acc_ref[...] = y

        @pl.when(f > 0)
        def _():
            acc_ref[...] += y

        @pl.when(f == nf - 1)
        def _():
            acc = acc_ref[...]
            for j in range(TOKEN_ROWS):
                o_ref[pl.ds(j, tm, stride=TOKEN_ROWS), :] = acc[:, j * LANES:(j + 1) * LANES]


def _experts(t8, src_tok, tile_expert, tile_valid, wg, wu, wd, tm):
    n_tiles = src_tok.shape[0]
    d = TOKEN_ROWS * LANES
    dff = wg.shape[2]
    tf = _divisor(dff, 1792, 256)
    nf = dff // tf
    assert tm % (nf * DMA_UNROLL) == 0
    fidx = lambda i, f, tv: f * tv[i] + (nf - 1) * (1 - tv[i])
    smem_tile = lambda imap: pl.BlockSpec((1, 1, tm), imap, memory_space=pltpu.SMEM)
    grid_spec = pltpu.PrefetchScalarGridSpec(
        num_scalar_prefetch=2,
        grid=(n_tiles, nf),
        in_specs=[
            smem_tile(lambda i, f, te, tv: (jnp.minimum(i + 1, n_tiles - 1), 0, 0)),
            smem_tile(lambda i, f, te, tv: (0, 0, 0)),
            pl.BlockSpec(memory_space=pl.ANY),
            pl.BlockSpec((1, d, tf), lambda i, f, te, tv: (te[i], 0, fidx(i, f, tv))),
            pl.BlockSpec((1, d, tf), lambda i, f, te, tv: (te[i], 0, fidx(i, f, tv))),
            pl.BlockSpec((1, tf, d), lambda i, f, te, tv: (te[i], fidx(i, f, tv), 0)),
        ],
        out_specs=pl.BlockSpec((tm * TOKEN_ROWS, LANES), lambda i, f, te, tv: (i, 0)),
        scratch_shapes=[
            pltpu.VMEM((2 * tm * TOKEN_ROWS, LANES), F32),
            pltpu.VMEM((tm, d), BF16),
            pltpu.VMEM((tm, d), F32),
            pltpu.SemaphoreType.DMA((2,)),
        ],
    )
    return pl.pallas_call(
        functools.partial(_expert_kernel, fchunk=1024, nf=nf),
        grid_spec=grid_spec,
        out_shape=jax.ShapeDtypeStruct((n_tiles * tm * TOKEN_ROWS, LANES), F32),
        compiler_params=_params(("arbitrary", "arbitrary")),
        name="moe_experts",
    )(tile_expert, tile_valid, src_tok, src_tok, t8, wg, wu, wd)


def _combine_kernel(nxt_ref, first_ref, ys_ref, h_ref, rt_ref, g2_ref, fg_ref, o_ref, ybuf_ref, sems):
    i = pl.program_id(0)
    ns = pl.num_programs(0)
    tc = h_ref.shape[0]
    npair = TOP_K * tc
    slot = lax.rem(i, 2)

    def gather(idx_ref, into):
        def body(j, c):
            for u in range(DMA_UNROLL):
                r = j * DMA_UNROLL + u
                for k in range(TOP_K):
                    _token_copy(ys_ref, idx_ref[0, 0, TOP_K * r + k], ybuf_ref, into * npair + k * tc + r,
                                sems.at[into]).start()
            return c
        lax.fori_loop(0, tc // DMA_UNROLL, body, 0)

    @pl.when(i == 0)
    def _():
        gather(first_ref, 0)

    @pl.when(i + 1 < ns)
    def _():
        gather(nxt_ref, 1 - slot)

    _tokens_wait(ys_ref, ybuf_ref, slot * npair, npair, sems.at[slot])
    rt = rt_ref[...]
    base = pl.multiple_of(slot * (npair * TOKEN_ROWS), TOKEN_ROWS)
    pieces = []
    for j in range(TOKEN_ROWS):
        yj = None
        for k in range(TOP_K):
            v = ybuf_ref[pl.ds(base + k * tc * TOKEN_ROWS + j, tc, stride=TOKEN_ROWS), :]
            yj = rt[:, k:k + 1] * v if yj is None else yj + rt[:, k:k + 1] * v
        pieces.append(yj)
    y = jnp.concatenate(pieces, axis=1)
    h4 = h_ref[...] + g2_ref[0] * y
    ms = jnp.mean(h4 * h4, axis=-1, keepdims=True)
    o_ref[...] = h4 * lax.rsqrt(ms + NORM_EPS) * fg_ref[...]


def _combine(ys8, dest, h2d, rt, g2, final_g, seq):
    n, d = h2d.shape
    tc = _divisor(seq, 256, DMA_UNROLL)
    tpb = seq // tc
    ns = n // tc
    dest3 = dest.reshape(ns, 1, TOP_K * tc)
    smem_tile = lambda imap: pl.BlockSpec((1, 1, TOP_K * tc), imap, memory_space=pltpu.SMEM)
    return pl.pallas_call(
        _combine_kernel,
        grid=(ns,),
        in_specs=[
            smem_tile(lambda i: (jnp.minimum(i + 1, ns - 1), 0, 0)),
            smem_tile(lambda i: (0, 0, 0)),
            pl.BlockSpec(memory_space=pl.ANY),
            pl.BlockSpec((tc, d), lambda i: (i, 0)),
            pl.BlockSpec((tc, LANES), lambda i: (i, 0)),
            pl.BlockSpec((1, 1, d), lambda i: (i // tpb, 0, 0)),
            _resident((1, d)),
        ],
        out_specs=pl.BlockSpec((tc, d), lambda i: (i, 0)),
        out_shape=jax.ShapeDtypeStruct((n, d), F32),
        scratch_shapes=[pltpu.VMEM((2 * TOP_K * tc * TOKEN_ROWS, LANES), F32), pltpu.SemaphoreType.DMA((2,))],
        compiler_params=_params(("arbitrary",)),
        name="moe_combine_norm",
    )(dest3, dest3, ys8, h2d, rt, g2, final_g.reshape(1, d))


def _route(rt, n_experts, tm):
    n = rt.shape[0]
    npairs = n * TOP_K
    pair_expert = rt[:, 2:2 + TOP_K].astype(jnp.int32).reshape(npairs)
    onehot = (pair_expert[:, None] == jnp.arange(n_experts, dtype=jnp.int32)[None, :]).astype(jnp.int32)
    csum = jnp.cumsum(onehot, axis=0)
    rank = jnp.sum(onehot * csum, axis=1) - 1
    counts = csum[-1]
    tiles = (counts + tm - 1) // tm
    tile_end = jnp.cumsum(tiles)
    tile_start = tile_end - tiles
    dest = jnp.sum(onehot * tile_start[None, :], axis=1) * tm + rank
    n_tiles = npairs // tm + n_experts
    ti = jnp.arange(n_tiles, dtype=jnp.int32)
    n_active = tile_end[-1]
    tile_valid = (ti < n_active).astype(jnp.int32)
    last_used = jnp.minimum(ti, n_active - 1)
    expert_of = jnp.sum((last_used[:, None] >= tile_end[None, :]).astype(jnp.int32), axis=1)
    tile_expert = jnp.minimum(expert_of, n_experts - 1)
    order_tok = (jnp.sort(pair_expert * npairs + jnp.arange(npairs, dtype=jnp.int32)) % npairs) // TOP_K
    order_tok = jnp.concatenate([order_tok, jnp.zeros((tm,), jnp.int32)])
    count_start = jnp.cumsum(counts) - counts
    tile_in_expert = ti - tile_start[tile_expert]
    run_start = jnp.clip(count_start[tile_expert] + tile_in_expert * tm, 0, npairs)
    run_len = jnp.clip(counts[tile_expert] - tile_in_expert * tm, 0, tm) * tile_valid
    runs = jax.vmap(lambda s: lax.dynamic_slice(order_tok, (s,), (tm,)))(run_start)
    src_tok = jnp.where(jnp.arange(tm, dtype=jnp.int32)[None, :] < run_len[:, None], runs, 0)
    return dest.astype(jnp.int32), src_tok.reshape(n_tiles, 1, tm), tile_expert, tile_valid


def kernel(x, c, mod_w, mod_b, norm1_g, norm2_g, conv_w_pw1, conv_b_pw1, conv_w_dw, conv_b_dw, conv_ln_g, conv_ln_b, conv_w_pw2, conv_b_pw2, ssm_a_re, ssm_a_im, ssm_log_dt, ssm_b_re, ssm_b_im, ssm_c_re, ssm_c_im, ssm_d, ssm_w_glu, ssm_b_glu, ffn_w_gate, ffn_w_up, ffn_w_down, moe_w_router, moe_b_router, moe_w_gate, moe_w_up, moe_w_down, final_norm_g):
    nb, seq, d = x.shape
    n = nb * seq
    assert mod_w.shape[0] == 2 and seq % S5_CHUNK == 0 and d == TOKEN_ROWS * LANES
    n_experts = moe_w_router.shape[-1]

    mod = _modulation(c, mod_w, mod_b)
    parts = [[mod[i, :, k * d:(k + 1) * d].reshape(nb, 1, d) for k in range(6)] for i in range(2)]
    sh1a, sc1a, g1a, sh2a, sc2a, g2a = parts[0]
    sh1b, sc1b, g1b, sh2b, sc2b, g2b = parts[1]

    u = _pw1(x.reshape(n, d), norm1_g[0], sc1a, sh1a, conv_w_pw1[0], conv_b_pw1[0], seq)
    h1 = _conv_block(u.reshape(nb, seq, d), x, conv_w_dw[0], conv_b_dw[0], conv_ln_g[0], conv_ln_b[0],
                     conv_w_pw2[0], conv_b_pw2[0], g1a)
    h2, u1 = _dense_ffn(h1.reshape(n, d), norm2_g[0], sc2a, sh2a, g2a,
                        ffn_w_gate[0], ffn_w_up[0], ffn_w_down[0], norm1_g[1], sc1b, sh1b, seq)

    tables = _s5_tables(ssm_a_re[0], ssm_a_im[0], ssm_log_dt[0], ssm_b_re[0], ssm_b_im[0],
                        ssm_c_re[0], ssm_c_im[0], ssm_d[0])
    y1 = _s5_mix(u1, tables, nb)
    h3, t8, rt = _glu_router(y1, ssm_w_glu[0], ssm_b_glu[0], h2, g1b, norm2_g[1], sc2b, sh2b,
                             moe_w_router[0], moe_b_router[0], seq)
    tm = 512
    dest, src_tok, tile_expert, tile_valid = _route(rt, n_experts, tm)
    ys8 = _experts(t8, src_tok, tile_expert, tile_valid, moe_w_gate[0].astype(BF16), moe_w_up[0].astype(BF16),
                   moe_w_down[0].astype(BF16), tm)
    out = _combine(ys8, dest, h3, rt, g2b, final_norm_g, seq)
    return out.reshape(nb, seq, d)
```

```python
import functools

import jax
import jax.numpy as jnp
from jax import lax
from jax.experimental import pallas as pl
from jax.experimental.pallas import tpu as pltpu

F32 = jnp.float32
BF16 = jnp.bfloat16
HIGHEST = lax.Precision.HIGHEST

NORM_EPS = 1e-6
TOP_K = 2
LANES = 128
S5_CHUNK = 16
CONV_HALO = 32
CONV_ROWS = 64
TOKEN_ROWS = 8
DMA_UNROLL = 8
VMEM_LIMIT = 56 * 2**20


def _params(sem):
    return pltpu.CompilerParams(dimension_semantics=sem, vmem_limit_bytes=VMEM_LIMIT)


def _divisor(n, cap, mult):
    best = None
    for d in range(mult, min(n, cap) + 1, mult):
        if n % d == 0:
            best = d
    assert best is not None, (n, cap, mult)
    return best


def _resident(shape):
    nd = len(shape)
    return pl.BlockSpec(shape, lambda *_: (0,) * nd, pipeline_mode=pl.Buffered(1))


def _rms_mod(x, g, sc, sh):
    ms = jnp.mean(x * x, axis=-1, keepdims=True)
    return (x * lax.rsqrt(ms + NORM_EPS) * g) * (1.0 + sc) + sh


def _silu(x):
    return x * jax.nn.sigmoid(x)


def _gelu_tanh(x):
    c = 0.7978845608028654
    return 0.5 * x * (1.0 + jnp.tanh(c * (x + 0.044715 * (x * x * x))))


def _split_bf16(x):
    hi = x.astype(BF16)
    return hi, (x - hi.astype(F32)).astype(BF16)


def _mod_kernel(c_ref, w_ref, b_ref, o_ref):
    cond = _silu(c_ref[...])
    o_ref[0] = jnp.dot(cond, w_ref[0], preferred_element_type=F32, precision=HIGHEST) + b_ref[0]


def _modulation(c, mod_w, mod_b):
    depth, d, d6 = mod_w.shape
    nb = c.shape[0]
    rows = -(-nb // 8) * 8
    c8 = jnp.pad(c, ((0, rows - nb), (0, 0)))
    tn = _divisor(d6, 1536, LANES)
    out = pl.pallas_call(
        _mod_kernel,
        grid=(depth, d6 // tn),
        in_specs=[
            pl.BlockSpec((rows, d), lambda i, j: (0, 0)),
            pl.BlockSpec((1, d, tn), lambda i, j: (i, 0, j)),
            pl.BlockSpec((1, 1, tn), lambda i, j: (i, 0, j)),
        ],
        out_specs=pl.BlockSpec((1, rows, tn), lambda i, j: (i, 0, j)),
        out_shape=jax.ShapeDtypeStruct((depth, rows, d6), F32),
        compiler_params=_params(("arbitrary", "arbitrary")),
        name="adaln_mod",
    )(c8, mod_w, mod_b.reshape(depth, 1, d6))
    return out[:, :nb, :]


def _pw1_kernel(x_ref, g_ref, sc_ref, sh_ref, w_ref, b_ref, o_ref):
    x = x_ref[...]
    d = x.shape[1]
    y = _rms_mod(x, g_ref[...], sc_ref[0], sh_ref[0])
    u = jnp.dot(y.astype(BF16), w_ref[...], preferred_element_type=F32) + b_ref[...]
    o_ref[...] = u[:, :d] * jax.nn.sigmoid(u[:, d:])


def _pw1(x2, norm_g, sc, sh, w, b, seq):
    n, d = x2.shape
    tm = _divisor(seq, 512, 8)
    tpb = seq // tm
    bmap = lambda i: (i // tpb, 0, 0)
    return pl.pallas_call(
        _pw1_kernel,
        grid=(n // tm,),
        in_specs=[
            pl.BlockSpec((tm, d), lambda i: (i, 0)),
            _resident((1, d)),
            pl.BlockSpec((1, 1, d), bmap),
            pl.BlockSpec((1, 1, d), bmap),
            _resident((d, 2 * d)),
            _resident((1, 2 * d)),
        ],
        out_specs=pl.BlockSpec((tm, d), lambda i: (i, 0)),
        out_shape=jax.ShapeDtypeStruct((n, d), F32),
        compiler_params=_params(("arbitrary",)),
        name="conv_pw1_glu",
    )(x2, norm_g.reshape(1, d), sc, sh, w.astype(BF16), b.reshape(1, 2 * d))


def _conv_kernel(cur_ref, prev_ref, wdw_ref, bdw_ref, lng_ref, lnb_ref, w2_ref, b2_ref, h_ref, g1_ref,
                 o_ref, buf_ref, cv_ref, *, taps):
    tm, d = cv_ref.shape
    i = pl.program_id(1)
    buf_ref[0:CONV_HALO, :] = jnp.where(i > 0, prev_ref[0], 0.0)
    buf_ref[CONV_HALO:, :] = cur_ref[0]
    off0 = CONV_HALO - (taps - 1)
    span = CONV_ROWS + CONV_HALO
    for c in range(d // LANES):
        lanes = slice(c * LANES, (c + 1) * LANES)

        def body(k, carry, lanes=lanes):
            r0 = pl.multiple_of(k * CONV_ROWS, CONV_ROWS)
            v = buf_ref[pl.ds(r0, span), lanes]
            acc = jnp.broadcast_to(bdw_ref[:, lanes], (CONV_ROWS, LANES))
            for s in range(8):
                xs = v if s == 0 else pltpu.roll(v, span - s, axis=0)
                for q in range(span // 8):
                    o = 8 * q + s
                    if o < off0 or o > off0 + taps - 1:
                        continue
                    wrow = wdw_ref[o - off0:o - off0 + 1, lanes]
                    acc = acc + wrow * xs[8 * q:8 * q + CONV_ROWS, :]
            cv_ref[pl.ds(r0, CONV_ROWS), lanes] = acc
            return carry

        lax.fori_loop(0, tm // CONV_ROWS, body, 0)
    v = cv_ref[...]
    mu = jnp.mean(v, axis=-1, keepdims=True)
    xc = v - mu
    var = jnp.mean(xc * xc, axis=-1, keepdims=True)
    y = _silu(xc * lax.rsqrt(var + NORM_EPS) * lng_ref[...] + lnb_ref[...])
    z = jnp.dot(y.astype(BF16), w2_ref[...], preferred_element_type=F32) + b2_ref[...]
    o_ref[0] = h_ref[0] + g1_ref[0] * z


def _conv_block(u3, x3, w_dw, b_dw, ln_g, ln_b, w2, b2, g1):
    nb, seq, d = x3.shape
    taps = w_dw.shape[0]
    assert taps - 1 <= CONV_HALO and d % LANES == 0
    tm = _divisor(seq, 512, CONV_ROWS)
    hb = tm // CONV_HALO
    wpad = jnp.pad(w_dw, ((0, -taps % 8), (0, 0)))
    row = lambda a: a.reshape(1, d)
    return pl.pallas_call(
        functools.partial(_conv_kernel, taps=taps),
        grid=(nb, seq // tm),
        in_specs=[
            pl.BlockSpec((1, tm, d), lambda b, i: (b, i, 0)),
            pl.BlockSpec((1, CONV_HALO, d), lambda b, i: (b, jnp.maximum(i * hb - 1, 0), 0)),
            _resident(wpad.shape),
            _resident((1, d)),
            _resident((1, d)),
            _resident((1, d)),
            _resident((d, d)),
            _resident((1, d)),
            pl.BlockSpec((1, tm, d), lambda b, i: (b, i, 0)),
            pl.BlockSpec((1, 1, d), lambda b, i: (b, 0, 0)),
        ],
        out_specs=pl.BlockSpec((1, tm, d), lambda b, i: (b, i, 0)),
        out_shape=jax.ShapeDtypeStruct((nb, seq, d), F32),
        scratch_shapes=[pltpu.VMEM((tm + CONV_HALO, d), F32), pltpu.VMEM((tm, d), F32)],
        compiler_params=_params(("arbitrary", "arbitrary")),
        name="conv_dw_ln_pw2",
    )(u3, u3, wpad, row(b_dw), row(ln_g), row(ln_b), w2.astype(BF16), row(b2), x3, g1)


def _ffn_kernel(h_ref, n2g_ref, sc_ref, sh_ref, g2_ref, wg_ref, wu_ref, wd_ref, n1g_ref, sc1_ref, sh1_ref,
                h2_ref, u_ref, *, fchunk):
    h = h_ref[...]
    t = _rms_mod(h, n2g_ref[...], sc_ref[0], sh_ref[0]).astype(BF16)
    dff = wg_ref.shape[1]
    acc = None
    for f0 in range(0, dff, fchunk):
        f1 = min(f0 + fchunk, dff)
        g = jnp.dot(t, wg_ref[:, f0:f1], preferred_element_type=F32)
        u = jnp.dot(t, wu_ref[:, f0:f1], preferred_element_type=F32)
        a = (_silu(g) * u).astype(BF16)
        y = jnp.dot(a, wd_ref[f0:f1, :], preferred_element_type=F32)
        acc = y if acc is None else acc + y
    h2 = h + g2_ref[0] * acc
    h2_ref[...] = h2
    u_ref[...] = _rms_mod(h2, n1g_ref[...], sc1_ref[0], sh1_ref[0])


def _dense_ffn(h, n2g, sc2, sh2, g2, wg, wu, wd, n1g, sc1, sh1, seq):
    n, d = h.shape
    dff = wg.shape[1]
    tm = _divisor(seq, 512, 8)
    tpb = seq // tm
    bmap = lambda i: (i // tpb, 0, 0)
    tile = pl.BlockSpec((tm, d), lambda i: (i, 0))
    return pl.pallas_call(
        functools.partial(_ffn_kernel, fchunk=512),
        grid=(n // tm,),
        in_specs=[
            tile,
            _resident((1, d)),
            pl.BlockSpec((1, 1, d), bmap),
            pl.BlockSpec((1, 1, d), bmap),
            pl.BlockSpec((1, 1, d), bmap),
            _resident((d, dff)),
            _resident((d, dff)),
            _resident((dff, d)),
            _resident((1, d)),
            pl.BlockSpec((1, 1, d), bmap),
            pl.BlockSpec((1, 1, d), bmap),
        ],
        out_specs=[tile, tile],
        out_shape=[jax.ShapeDtypeStruct((n, d), F32), jax.ShapeDtypeStruct((n, d), F32)],
        compiler_params=_params(("arbitrary",)),
        name="dense_swiglu",
    )(h, n2g.reshape(1, d), sc2, sh2, g2, wg.astype(BF16), wu.astype(BF16), wd.astype(BF16),
      n1g.reshape(1, d), sc1, sh1)


def _s5_tables(a_re, a_im, log_dt, b_re, b_im, c_re, c_im, d_skip):
    ng, p = a_re.shape
    c = b_re.shape[-1]
    gpb = LANES // c
    nblk = ng // gpb
    L = S5_CHUNK
    dt = jnp.exp(log_dt)[:, None]
    mag = jnp.exp(dt * a_re)
    ab_re = mag * jnp.cos(dt * a_im)
    ab_im = mag * jnp.sin(dt * a_im)
    den = a_re * a_re + a_im * a_im
    f_re = ((ab_re - 1.0) * a_re + ab_im * a_im) / den
    f_im = (ab_im * a_re - (ab_re - 1.0) * a_im) / den
    bb_re = f_re[..., None] * b_re - f_im[..., None] * b_im
    bb_im = f_re[..., None] * b_im + f_im[..., None] * b_re
    pr, pi = [jnp.ones_like(ab_re)], [jnp.zeros_like(ab_re)]
    for _ in range(L):
        pr, pi = pr + [pr[-1] * ab_re - pi[-1] * ab_im], pi + [pr[-1] * ab_im + pi[-1] * ab_re]
    eye = jnp.eye(gpb, dtype=F32)
    blk = lambda a: a.reshape((nblk, gpb) + a.shape[1:])
    bbx_re = jnp.einsum("bgpc,gh->bgchp", blk(bb_re), eye).reshape(nblk, LANES, gpb * p)
    bbx_im = jnp.einsum("bgpc,gh->bgchp", blk(bb_im), eye).reshape(nblk, LANES, gpb * p)
    ctx_re = jnp.einsum("bgop,gh->bgohp", blk(c_re), eye).reshape(nblk, LANES, gpb * p)
    ctx_im = jnp.einsum("bgop,gh->bgohp", blk(c_im), eye).reshape(nblk, LANES, gpb * p)
    ccm = jnp.einsum("rbgop,gh->brgpho", jnp.stack([blk(c_re), -blk(c_im)]), eye).reshape(nblk, 2 * gpb * p, LANES)
    pw_re = jnp.stack(pr).reshape(L + 1, nblk, gpb * p).transpose(1, 0, 2)
    pw_im = jnp.stack(pi).reshape(L + 1, nblk, gpb * p).transpose(1, 0, 2)
    d_blk = jnp.tile(d_skip.reshape(nblk, LANES), (1, L)).reshape(nblk, 1, L * LANES)
    return bbx_re, bbx_im, ctx_re, ctx_im, ccm, pw_re, pw_im, d_blk


def _s5_kernel(u_ref, bbr_ref, bbi_ref, ctr_ref, cti_ref, ccm_ref, pwr_ref, pwi_ref, d_ref, o_ref,
               t_ref, we_ref, vt_ref, e_ref, hin_ref):
    L = S5_CHUNK
    rows, ew = e_ref.shape
    ph = ew // 2

    @pl.when(pl.program_id(1) == 0)
    def _build_operands():
        t_ref[...] = jnp.zeros_like(t_ref)
        bbr, bbi = bbr_ref[0], bbi_ref[0]
        ctr, cti = ctr_ref[0], cti_ref[0]
        cc_hi, cc_lo = _split_bf16(ccm_ref[0])
        for lp in range(L):
            j = L - 1 - lp
            p_re, p_im = pwr_ref[0, j:j + 1, :], pwi_ref[0, j:j + 1, :]
            w = jnp.concatenate([bbr * p_re - bbi * p_im, bbr * p_im + bbi * p_re], axis=1)
            w_hi, w_lo = _split_bf16(w)
            we_ref[lp * LANES:(lp + 1) * LANES, :] = w_hi
            k = (jnp.dot(w_hi, cc_hi, preferred_element_type=F32)
                 + jnp.dot(w_lo, cc_hi, preferred_element_type=F32)
                 + jnp.dot(w_hi, cc_lo, preferred_element_type=F32)).astype(BF16)
            for l1 in range(L - j):
                t_ref[l1 * LANES:(l1 + 1) * LANES, (l1 + j) * LANES:(l1 + j + 1) * LANES] = k
        for l in range(L):
            p_re, p_im = pwr_ref[0, l + 1:l + 2, :], pwi_ref[0, l + 1:l + 2, :]
            v = jnp.concatenate([ctr * p_re - cti * p_im, -(ctr * p_im + cti * p_re)], axis=1)
            vt_ref[l * LANES:(l + 1) * LANES, :] = v.astype(BF16)

    xcat = jnp.concatenate([u_ref[pl.ds(l, rows, stride=L), :] for l in range(L)], axis=1)
    xb = xcat.astype(BF16)
    e_ref[...] = jnp.dot(xb, we_ref[...], preferred_element_type=F32)
    ar = pwr_ref[0, L:L + 1, :]
    ai = pwi_ref[0, L:L + 1, :]

    def body(r, carry):
        hr, hi = carry
        hin_ref[pl.ds(r, 1), 0:ph] = hr
        hin_ref[pl.ds(r, 1), ph:ew] = hi
        er = e_ref[pl.ds(r, 1), 0:ph]
        ei = e_ref[pl.ds(r, 1), ph:ew]
        return (ar * hr - ai * hi + er, ar * hi + ai * hr + ei)

    zero = jnp.zeros((1, ph), F32)
    lax.fori_loop(0, rows, body, (zero, zero))
    tw = 2 * LANES
    y = jnp.concatenate([jnp.dot(xb[:, :c0 + tw], t_ref[0:c0 + tw, c0:c0 + tw], preferred_element_type=F32)
                         for c0 in range(0, L * LANES, tw)], axis=1)
    y = y + lax.dot_general(hin_ref[...].astype(BF16), vt_ref[...], (((1,), (1,)), ((), ())),
                            preferred_element_type=F32)
    y = _gelu_tanh(y + d_ref[0] * xcat)
    for l in range(L):
        o_ref[pl.ds(l, rows, stride=L), :] = y[:, l * LANES:(l + 1) * LANES]


def _s5_mix(u, tables, nbatch):
    n, d = u.shape
    L = S5_CHUNK
    bbx_re, bbx_im, ctx_re, ctx_im, ccm, pw_re, pw_im, d_blk = tables
    nblk, _, sw = bbx_re.shape
    seq = n // nbatch
    rows = seq // L
    slab = lambda a: pl.BlockSpec((1,) + a.shape[1:], lambda g, b: (g, 0, 0))
    return pl.pallas_call(
        _s5_kernel,
        grid=(nblk, nbatch),
        in_specs=[pl.BlockSpec((seq, LANES), lambda g, b: (b, g))] + [slab(a) for a in tables],
        out_specs=pl.BlockSpec((seq, LANES), lambda g, b: (b, g)),
        out_shape=jax.ShapeDtypeStruct((n, d), F32),
        scratch_shapes=[
            pltpu.VMEM((L * LANES, L * LANES), BF16),
            pltpu.VMEM((L * LANES, 2 * sw), BF16),
            pltpu.VMEM((L * LANES, 2 * sw), BF16),
            pltpu.VMEM((rows, 2 * sw), F32),
            pltpu.VMEM((rows, 2 * sw), F32),
        ],
        compiler_params=_params(("arbitrary", "arbitrary")),
        name="s5_chunked_scan",
    )(u, *tables)


def _glu_kernel(y_ref, w_ref, b_ref, h_ref, g1_ref, n2g_ref, sc_ref, sh_ref, wr_ref, br_ref,
                h3_ref, t_ref, rt_ref):
    d = h_ref.shape[1]
    ne = wr_ref.shape[0]
    z = jnp.dot(y_ref[...].astype(BF16), w_ref[...], preferred_element_type=F32) + b_ref[...]
    h3 = h_ref[...] + g1_ref[0] * (z[:, :d] * jax.nn.sigmoid(z[:, d:]))
    h3_ref[...] = h3
    t = _rms_mod(h3, n2g_ref[...], sc_ref[0], sh_ref[0])
    rows = t.shape[0]
    for j in range(TOKEN_ROWS):
        t_ref[pl.ds(j, rows, stride=TOKEN_ROWS), :] = t[:, j * LANES:(j + 1) * LANES]
    logit = [jnp.sum(t * wr_ref[e:e + 1, :], axis=-1, keepdims=True) + br_ref[e:e + 1, :] for e in range(ne)]

    def top1(cols):
        m = cols[0]
        for col in cols[1:]:
            m = jnp.maximum(m, col)
        idx = jnp.full_like(m, float(ne))
        for e in reversed(range(ne)):
            idx = jnp.where(cols[e] == m, float(e), idx)
        return m, idx

    m1, i1 = top1(logit)
    m2, i2 = top1([jnp.where(i1 == float(e), -jnp.inf, logit[e]) for e in range(ne)])
    e2 = jnp.exp(m2 - m1)
    den = 1.0 + e2
    lane = lax.broadcasted_iota(jnp.int32, rt_ref.shape, 1)
    rt_ref[...] = jnp.where(lane == 0, 1.0 / den,
                            jnp.where(lane == 1, e2 / den,
                                      jnp.where(lane == 2, i1, jnp.where(lane == 3, i2, 0.0))))


def _glu_router(y, w_glu, b_glu, h, g1, n2g, sc2, sh2, w_router, b_router, seq):
    n, d = h.shape
    ne = w_router.shape[1]
    tm = _divisor(seq, 512, 8)
    tpb = seq // tm
    bmap = lambda i: (i // tpb, 0, 0)
    tile = pl.BlockSpec((tm, d), lambda i: (i, 0))
    return pl.pallas_call(
        _glu_kernel,
        grid=(n // tm,),
        in_specs=[
            tile,
            _resident((d, 2 * d)),
            _resident((1, 2 * d)),
            tile,
            pl.BlockSpec((1, 1, d), bmap),
            _resident((1, d)),
            pl.BlockSpec((1, 1, d), bmap),
            pl.BlockSpec((1, 1, d), bmap),
            _resident((ne, d)),
            _resident((ne, 1)),
        ],
        out_specs=[tile, pl.BlockSpec((tm * TOKEN_ROWS, LANES), lambda i: (i, 0)),
                   pl.BlockSpec((tm, LANES), lambda i: (i, 0))],
        out_shape=[
            jax.ShapeDtypeStruct((n, d), F32),
            jax.ShapeDtypeStruct((n * TOKEN_ROWS, LANES), F32),
            jax.ShapeDtypeStruct((n, LANES), F32),
        ],
        compiler_params=_params(("arbitrary",)),
        name="s5_glu_router",
    )(y, w_glu.astype(BF16), b_glu.reshape(1, 2 * d), h, g1, n2g.reshape(1, d), sc2, sh2,
      w_router.T, b_router.reshape(ne, 1))


def _token_copy(src_ref, src_tok, dst_ref, dst_tok, sem):
    s = pl.multiple_of(src_tok * TOKEN_ROWS, TOKEN_ROWS)
    t = pl.multiple_of(dst_tok * TOKEN_ROWS, TOKEN_ROWS)
    return pltpu.make_async_copy(src_ref.at[pl.ds(s, TOKEN_ROWS), :], dst_ref.at[pl.ds(t, TOKEN_ROWS), :], sem)


def _tokens_wait(src_ref, dst_ref, dst_tok, ntok, sem):
    t = pl.multiple_of(dst_tok * TOKEN_ROWS, TOKEN_ROWS)
    pltpu.make_async_copy(src_ref.at[pl.ds(0, ntok * TOKEN_ROWS), :],
                          dst_ref.at[pl.ds(t, ntok * TOKEN_ROWS), :], sem).wait()


def _expert_kernel(te_ref, tv_ref, nxt_ref, first_ref, t8_ref, wg_ref, wu_ref, wd_ref, o_ref,
                   xbuf_ref, xb_ref, acc_ref, sems, *, fchunk, nf):
    del te_ref
    i = pl.program_id(0)
    f = pl.program_id(1)
    nt = pl.num_programs(0)
    tm = xb_ref.shape[0]
    slot = lax.rem(i, 2)

    def gather(idx_ref, into):
        def body(j, c):
            for u in range(DMA_UNROLL):
                r = j * DMA_UNROLL + u
                _token_copy(t8_ref, idx_ref[0, 0, r], xbuf_ref, into * tm + r, sems.at[into]).start(priority=u % 2)
            return c
        lax.fori_loop(0, tm // DMA_UNROLL, body, 0)

    @pl.when(jnp.logical_and(i == 0, f == 0))
    def _():
        gather(first_ref, 0)

    @pl.when(jnp.logical_and(f == 0, jnp.logical_and(i + 1 < nt, tv_ref[jnp.minimum(i + 1, nt - 1)] == 1)))
    def _():
        gather(nxt_ref, 1 - slot)

    @pl.when(jnp.logical_and(tv_ref[i] == 0, f == nf - 1))
    def _():
        o_ref[...] = jnp.zeros_like(o_ref)

    @pl.when(tv_ref[i] == 1)
    def _():
        @pl.when(f == 0)
        def _():
            _tokens_wait(t8_ref, xbuf_ref, slot * tm, tm, sems.at[slot])
            base = pl.multiple_of(slot * (tm * TOKEN_ROWS), TOKEN_ROWS)
            for j in range(TOKEN_ROWS):
                xb_ref[:, j * LANES:(j + 1) * LANES] = (
                    xbuf_ref[pl.ds(base + j, tm, stride=TOKEN_ROWS), :].astype(BF16))

        x = xb_ref[...]
        tf = wg_ref.shape[2]
        y = None
        for f0 in range(0, tf, fchunk):
            f1 = min(f0 + fchunk, tf)
            g = jnp.dot(x, wg_ref[0, :, f0:f1], preferred_element_type=F32)
            u = jnp.dot(x, wu_ref[0, :, f0:f1], preferred_element_type=F32)
            a = (_silu(g) * u).astype(BF16)
            yy = jnp.dot(a, wd_ref[0, f0:f1, :], preferred_element_type=F32)
            y = yy if y is None else y + yy

        @pl.when(f == 0)
        def _():
            acc_ref[...] = y

        @pl.when(f > 0)
        def _():
            acc_ref[...] += y

        @pl.when(f == nf - 1)
        def _():
            acc = acc_ref[...]
            for j in range(TOKEN_ROWS):
                o_ref[pl.ds(j, tm, stride=TOKEN_ROWS), :] = acc[:, j * LANES:(j + 1) * LANES]


def _experts(t8, src_tok, tile_expert, tile_valid, wg, wu, wd, tm):
    n_tiles = src_tok.shape[0]
    d = TOKEN_ROWS * LANES
    dff = wg.shape[2]
    tf = _divisor(dff, 1792, 256)
    nf = dff // tf
    assert tm % (nf * DMA_UNROLL) == 0
    fidx = lambda i, f, tv: f * tv[i] + (nf - 1) * (1 - tv[i])
    smem_tile = lambda imap: pl.BlockSpec((1, 1, tm), imap, memory_space=pltpu.SMEM)
    grid_spec = pltpu.PrefetchScalarGridSpec(
        num_scalar_prefetch=2,
        grid=(n_tiles, nf),
        in_specs=[
            smem_tile(lambda i, f, te, tv: (jnp.minimum(i + 1, n_tiles - 1), 0, 0)),
            smem_tile(lambda i, f, te, tv: (0, 0, 0)),
            pl.BlockSpec(memory_space=pl.ANY),
            pl.BlockSpec((1, d, tf), lambda i, f, te, tv: (te[i], 0, fidx(i, f, tv))),
            pl.BlockSpec((1, d, tf), lambda i, f, te, tv: (te[i], 0, fidx(i, f, tv))),
            pl.BlockSpec((1, tf, d), lambda i, f, te, tv: (te[i], fidx(i, f, tv), 0)),
        ],
        out_specs=pl.BlockSpec((tm * TOKEN_ROWS, LANES), lambda i, f, te, tv: (i, 0)),
        scratch_shapes=[
            pltpu.VMEM((2 * tm * TOKEN_ROWS, LANES), F32),
            pltpu.VMEM((tm, d), BF16),
            pltpu.VMEM((tm, d), F32),
            pltpu.SemaphoreType.DMA((2,)),
        ],
    )
    return pl.pallas_call(
        functools.partial(_expert_kernel, fchunk=1024, nf=nf),
        grid_spec=grid_spec,
        out_shape=jax.ShapeDtypeStruct((n_tiles * tm * TOKEN_ROWS, LANES), F32),
        compiler_params=_params(("arbitrary", "arbitrary")),
        name="moe_experts",
    )(tile_expert, tile_valid, src_tok, src_tok, t8, wg, wu, wd)


def _combine_kernel(nxt_ref, first_ref, ys_ref, h_ref, rt_ref, g2_ref, fg_ref, o_ref, ybuf_ref, sems):
    i = pl.program_id(0)
    ns = pl.num_programs(0)
    tc = h_ref.shape[0]
    npair = TOP_K * tc
    slot = lax.rem(i, 2)

    def gather(idx_ref, into):
        def body(j, c):
            for u in range(DMA_UNROLL):
                r = j * DMA_UNROLL + u
                for k in range(TOP_K):
                    _token_copy(ys_ref, idx_ref[0, 0, TOP_K * r + k], ybuf_ref, into * npair + k * tc + r,
                                sems.at[into]).start(priority=k % 2)
            return c
        lax.fori_loop(0, tc // DMA_UNROLL, body, 0)

    @pl.when(i == 0)
    def _():
        gather(first_ref, 0)

    @pl.when(i + 1 < ns)
    def _():
        gather(nxt_ref, 1 - slot)

    _tokens_wait(ys_ref, ybuf_ref, slot * npair, npair, sems.at[slot])
    rt = rt_ref[...]
    base = pl.multiple_of(slot * (npair * TOKEN_ROWS), TOKEN_ROWS)
    pieces = []
    for j in range(TOKEN_ROWS):
        yj = None
        for k in range(TOP_K):
            v = ybuf_ref[pl.ds(base + k * tc * TOKEN_ROWS + j, tc, stride=TOKEN_ROWS), :]
            yj = rt[:, k:k + 1] * v if yj is None else yj + rt[:, k:k + 1] * v
        pieces.append(yj)
    y = jnp.concatenate(pieces, axis=1)
    h4 = h_ref[...] + g2_ref[0] * y
    ms = jnp.mean(h4 * h4, axis=-1, keepdims=True)
    o_ref[...] = h4 * lax.rsqrt(ms + NORM_EPS) * fg_ref[...]


def _combine(ys8, dest, h2d, rt, g2, final_g, seq):
    n, d = h2d.shape
    tc = _divisor(seq, 256, DMA_UNROLL)
    tpb = seq // tc
    ns = n // tc
    dest3 = dest.reshape(ns, 1, TOP_K * tc)
    smem_tile = lambda imap: pl.BlockSpec((1, 1, TOP_K * tc), imap, memory_space=pltpu.SMEM)
    return pl.pallas_call(
        _combine_kernel,
        grid=(ns,),
        in_specs=[
            smem_tile(lambda i: (jnp.minimum(i + 1, ns - 1), 0, 0)),
            smem_tile(lambda i: (0, 0, 0)),
            pl.BlockSpec(memory_space=pl.ANY),
            pl.BlockSpec((tc, d), lambda i: (i, 0)),
            pl.BlockSpec((tc, LANES), lambda i: (i, 0)),
            pl.BlockSpec((1, 1, d), lambda i: (i // tpb, 0, 0)),
            _resident((1, d)),
        ],
        out_specs=pl.BlockSpec((tc, d), lambda i: (i, 0)),
        out_shape=jax.ShapeDtypeStruct((n, d), F32),
        scratch_shapes=[pltpu.VMEM((2 * TOP_K * tc * TOKEN_ROWS, LANES), F32), pltpu.SemaphoreType.DMA((2,))],
        compiler_params=_params(("arbitrary",)),
        name="moe_combine_norm",
    )(dest3, dest3, ys8, h2d, rt, g2, final_g.reshape(1, d))


def _route(rt, n_experts, tm):
    n = rt.shape[0]
    npairs = n * TOP_K
    pair_expert = rt[:, 2:2 + TOP_K].astype(jnp.int32).reshape(npairs)
    onehot = (pair_expert[:, None] == jnp.arange(n_experts, dtype=jnp.int32)[None, :]).astype(jnp.int32)
    csum = jnp.cumsum(onehot, axis=0)
    rank = jnp.sum(onehot * csum, axis=1) - 1
    counts = csum[-1]
    tiles = (counts + tm - 1) // tm
    tile_end = jnp.cumsum(tiles)
    tile_start = tile_end - tiles
    dest = jnp.sum(onehot * tile_start[None, :], axis=1) * tm + rank
    n_tiles = npairs // tm + n_experts
    ti = jnp.arange(n_tiles, dtype=jnp.int32)
    n_active = tile_end[-1]
    tile_valid = (ti < n_active).astype(jnp.int32)
    last_used = jnp.minimum(ti, n_active - 1)
    expert_of = jnp.sum((last_used[:, None] >= tile_end[None, :]).astype(jnp.int32), axis=1)
    tile_expert = jnp.minimum(expert_of, n_experts - 1)
    order_tok = (jnp.sort(pair_expert * npairs + jnp.arange(npairs, dtype=jnp.int32)) % npairs) // TOP_K
    nrows = n_tiles * tm
    count_start = jnp.cumsum(counts) - counts
    padded = jnp.concatenate([jnp.zeros((nrows,), jnp.int32), order_tok, jnp.zeros((nrows,), jnp.int32)])
    row = jnp.arange(nrows, dtype=jnp.int32)
    src_tok = jnp.zeros((nrows,), jnp.int32)
    for e in range(n_experts):
        first_row = tile_start[e] * tm
        shifted = lax.dynamic_slice(padded, (nrows + count_start[e] - first_row,), (nrows,))
        mine = jnp.logical_and(row >= first_row, row < first_row + counts[e])
        src_tok = jnp.where(mine, shifted, src_tok)
    return dest.astype(jnp.int32), src_tok.reshape(n_tiles, 1, tm), tile_expert, tile_valid


def kernel(x, c, mod_w, mod_b, norm1_g, norm2_g, conv_w_pw1, conv_b_pw1, conv_w_dw, conv_b_dw, conv_ln_g, conv_ln_b, conv_w_pw2, conv_b_pw2, ssm_a_re, ssm_a_im, ssm_log_dt, ssm_b_re, ssm_b_im, ssm_c_re, ssm_c_im, ssm_d, ssm_w_glu, ssm_b_glu, ffn_w_gate, ffn_w_up, ffn_w_down, moe_w_router, moe_b_router, moe_w_gate, moe_w_up, moe_w_down, final_norm_g):
    nb, seq, d = x.shape
    n = nb * seq
    assert mod_w.shape[0] == 2 and seq % S5_CHUNK == 0 and d == TOKEN_ROWS * LANES
    n_experts = moe_w_router.shape[-1]

    mod = _modulation(c, mod_w, mod_b)
    parts = [[mod[i, :, k * d:(k + 1) * d].reshape(nb, 1, d) for k in range(6)] for i in range(2)]
    sh1a, sc1a, g1a, sh2a, sc2a, g2a = parts[0]
    sh1b, sc1b, g1b, sh2b, sc2b, g2b = parts[1]

    u = _pw1(x.reshape(n, d), norm1_g[0], sc1a, sh1a, conv_w_pw1[0], conv_b_pw1[0], seq)
    h1 = _conv_block(u.reshape(nb, seq, d), x, conv_w_dw[0], conv_b_dw[0], conv_ln_g[0], conv_ln_b[0],
                     conv_w_pw2[0], conv_b_pw2[0], g1a)
    h2, u1 = _dense_ffn(h1.reshape(n, d), norm2_g[0], sc2a, sh2a, g2a,
                        ffn_w_gate[0], ffn_w_up[0], ffn_w_down[0], norm1_g[1], sc1b, sh1b, seq)

    tables = _s5_tables(ssm_a_re[0], ssm_a_im[0], ssm_log_dt[0], ssm_b_re[0], ssm_b_im[0],
                        ssm_c_re[0], ssm_c_im[0], ssm_d[0])
    y1 = _s5_mix(u1, tables, nb)
    h3, t8, rt = _glu_router(y1, ssm_w_glu[0], ssm_b_glu[0], h2, g1b, norm2_g[1], sc2b, sh2b,
                             moe_w_router[0], moe_b_router[0], seq)
    tm = 512
    dest, src_tok, tile_expert, tile_valid = _route(rt, n_experts, tm)
    ys8 = _experts(t8, src_tok, tile_expert, tile_valid, moe_w_gate[0].astype(BF16), moe_w_up[0].astype(BF16),
                   moe_w_down[0].astype(BF16), tm)
    out = _combine(ys8, dest, h3, rt, g2b, final_norm_g, seq)
    return out.reshape(nb, seq, d)
```

```python
import functools

import jax
import jax.numpy as jnp
from jax import lax
from jax.experimental import pallas as pl
from jax.experimental.pallas import tpu as pltpu

F32 = jnp.float32
BF16 = jnp.bfloat16
HIGHEST = lax.Precision.HIGHEST

NORM_EPS = 1e-6
TOP_K = 2
LANES = 128
S5_CHUNK = 16
CONV_HALO = 32
CONV_ROWS = 64
TOKEN_ROWS = 8
DMA_UNROLL = 8
VMEM_LIMIT = 56 * 2**20


def _params(sem):
    return pltpu.CompilerParams(dimension_semantics=sem, vmem_limit_bytes=VMEM_LIMIT)


def _divisor(n, cap, mult):
    best = None
    for d in range(mult, min(n, cap) + 1, mult):
        if n % d == 0:
            best = d
    assert best is not None, (n, cap, mult)
    return best


def _resident(shape):
    nd = len(shape)
    return pl.BlockSpec(shape, lambda *_: (0,) * nd, pipeline_mode=pl.Buffered(1))


def _rms_mod(x, g, sc, sh):
    ms = jnp.mean(x * x, axis=-1, keepdims=True)
    return (x * lax.rsqrt(ms + NORM_EPS) * g) * (1.0 + sc) + sh


def _silu(x):
    return x * jax.nn.sigmoid(x)


def _gelu_tanh(x):
    c = 0.7978845608028654
    return 0.5 * x * (1.0 + jnp.tanh(c * (x + 0.044715 * (x * x * x))))


def _split_bf16(x):
    hi = x.astype(BF16)
    return hi, (x - hi.astype(F32)).astype(BF16)


def _mod_kernel(c_ref, w_ref, b_ref, o_ref):
    cond = _silu(c_ref[...])
    o_ref[0] = jnp.dot(cond, w_ref[0], preferred_element_type=F32, precision=HIGHEST) + b_ref[0]


def _modulation(c, mod_w, mod_b):
    depth, d, d6 = mod_w.shape
    nb = c.shape[0]
    rows = -(-nb // 8) * 8
    c8 = jnp.pad(c, ((0, rows - nb), (0, 0)))
    tn = _divisor(d6, 1536, LANES)
    out = pl.pallas_call(
        _mod_kernel,
        grid=(depth, d6 // tn),
        in_specs=[
            pl.BlockSpec((rows, d), lambda i, j: (0, 0)),
            pl.BlockSpec((1, d, tn), lambda i, j: (i, 0, j)),
            pl.BlockSpec((1, 1, tn), lambda i, j: (i, 0, j)),
        ],
        out_specs=pl.BlockSpec((1, rows, tn), lambda i, j: (i, 0, j)),
        out_shape=jax.ShapeDtypeStruct((depth, rows, d6), F32),
        compiler_params=_params(("arbitrary", "arbitrary")),
        name="adaln_mod",
    )(c8, mod_w, mod_b.reshape(depth, 1, d6))
    return out[:, :nb, :]


def _pw1_kernel(x_ref, g_ref, sc_ref, sh_ref, w_ref, b_ref, o_ref):
    x = x_ref[...]
    d = x.shape[1]
    y = _rms_mod(x, g_ref[...], sc_ref[0], sh_ref[0])
    u = jnp.dot(y.astype(BF16), w_ref[...], preferred_element_type=F32) + b_ref[...]
    o_ref[...] = u[:, :d] * jax.nn.sigmoid(u[:, d:])


def _pw1(x2, norm_g, sc, sh, w, b, seq):
    n, d = x2.shape
    tm = _divisor(seq, 512, 8)
    tpb = seq // tm
    bmap = lambda i: (i // tpb, 0, 0)
    return pl.pallas_call(
        _pw1_kernel,
        grid=(n // tm,),
        in_specs=[
            pl.BlockSpec((tm, d), lambda i: (i, 0)),
            _resident((1, d)),
            pl.BlockSpec((1, 1, d), bmap),
            pl.BlockSpec((1, 1, d), bmap),
            _resident((d, 2 * d)),
            _resident((1, 2 * d)),
        ],
        out_specs=pl.BlockSpec((tm, d), lambda i: (i, 0)),
        out_shape=jax.ShapeDtypeStruct((n, d), F32),
        compiler_params=_params(("arbitrary",)),
        name="conv_pw1_glu",
    )(x2, norm_g.reshape(1, d), sc, sh, w.astype(BF16), b.reshape(1, 2 * d))


def _conv_kernel(cur_ref, prev_ref, wdw_ref, bdw_ref, lng_ref, lnb_ref, w2_ref, b2_ref, h_ref, g1_ref,
                 o_ref, buf_ref, cv_ref, *, taps):
    tm, d = cv_ref.shape
    i = pl.program_id(1)
    buf_ref[0:CONV_HALO, :] = jnp.where(i > 0, prev_ref[0], 0.0)
    buf_ref[CONV_HALO:, :] = cur_ref[0]
    off0 = CONV_HALO - (taps - 1)
    span = CONV_ROWS + CONV_HALO
    for c in range(d // LANES):
        lanes = slice(c * LANES, (c + 1) * LANES)

        def body(k, carry, lanes=lanes):
            r0 = pl.multiple_of(k * CONV_ROWS, CONV_ROWS)
            v = buf_ref[pl.ds(r0, span), lanes]
            acc = jnp.broadcast_to(bdw_ref[:, lanes], (CONV_ROWS, LANES))
            for s in range(8):
                xs = v if s == 0 else pltpu.roll(v, span - s, axis=0)
                for q in range(span // 8):
                    o = 8 * q + s
                    if o < off0 or o > off0 + taps - 1:
                        continue
                    wrow = wdw_ref[o - off0:o - off0 + 1, lanes]
                    acc = acc + wrow * xs[8 * q:8 * q + CONV_ROWS, :]
            cv_ref[pl.ds(r0, CONV_ROWS), lanes] = acc
            return carry

        lax.fori_loop(0, tm // CONV_ROWS, body, 0)
    v = cv_ref[...]
    mu = jnp.mean(v, axis=-1, keepdims=True)
    xc = v - mu
    var = jnp.mean(xc * xc, axis=-1, keepdims=True)
    y = _silu(xc * lax.rsqrt(var + NORM_EPS) * lng_ref[...] + lnb_ref[...])
    z = jnp.dot(y.astype(BF16), w2_ref[...], preferred_element_type=F32) + b2_ref[...]
    o_ref[0] = h_ref[0] + g1_ref[0] * z


def _conv_block(u3, x3, w_dw, b_dw, ln_g, ln_b, w2, b2, g1):
    nb, seq, d = x3.shape
    taps = w_dw.shape[0]
    assert taps - 1 <= CONV_HALO and d % LANES == 0
    tm = _divisor(seq, 512, CONV_ROWS)
    hb = tm // CONV_HALO
    wpad = jnp.pad(w_dw, ((0, -taps % 8), (0, 0)))
    row = lambda a: a.reshape(1, d)
    return pl.pallas_call(
        functools.partial(_conv_kernel, taps=taps),
        grid=(nb, seq // tm),
        in_specs=[
            pl.BlockSpec((1, tm, d), lambda b, i: (b, i, 0)),
            pl.BlockSpec((1, CONV_HALO, d), lambda b, i: (b, jnp.maximum(i * hb - 1, 0), 0)),
            _resident(wpad.shape),
            _resident((1, d)),
            _resident((1, d)),
            _resident((1, d)),
            _resident((d, d)),
            _resident((1, d)),
            pl.BlockSpec((1, tm, d), lambda b, i: (b, i, 0)),
            pl.BlockSpec((1, 1, d), lambda b, i: (b, 0, 0)),
        ],
        out_specs=pl.BlockSpec((1, tm, d), lambda b, i: (b, i, 0)),
        out_shape=jax.ShapeDtypeStruct((nb, seq, d), F32),
        scratch_shapes=[pltpu.VMEM((tm + CONV_HALO, d), F32), pltpu.VMEM((tm, d), F32)],
        compiler_params=_params(("arbitrary", "arbitrary")),
        name="conv_dw_ln_pw2",
    )(u3, u3, wpad, row(b_dw), row(ln_g), row(ln_b), w2.astype(BF16), row(b2), x3, g1)


def _ffn_kernel(h_ref, n2g_ref, sc_ref, sh_ref, g2_ref, wg_ref, wu_ref, wd_ref, n1g_ref, sc1_ref, sh1_ref,
                h2_ref, u_ref, *, fchunk):
    h = h_ref[...]
    t = _rms_mod(h, n2g_ref[...], sc_ref[0], sh_ref[0]).astype(BF16)
    dff = wg_ref.shape[1]
    acc = None
    for f0 in range(0, dff, fchunk):
        f1 = min(f0 + fchunk, dff)
        g = jnp.dot(t, wg_ref[:, f0:f1], preferred_element_type=F32)
        u = jnp.dot(t, wu_ref[:, f0:f1], preferred_element_type=F32)
        a = (_silu(g) * u).astype(BF16)
        y = jnp.dot(a, wd_ref[f0:f1, :], preferred_element_type=F32)
        acc = y if acc is None else acc + y
    h2 = h + g2_ref[0] * acc
    h2_ref[...] = h2
    u = _rms_mod(h2, n1g_ref[...], sc1_ref[0], sh1_ref[0])
    for g in range(u_ref.shape[0]):
        u_ref[g] = u[:, g * LANES:(g + 1) * LANES]


def _dense_ffn(h, n2g, sc2, sh2, g2, wg, wu, wd, n1g, sc1, sh1, seq):
    n, d = h.shape
    nblk = d // LANES
    dff = wg.shape[1]
    tm = _divisor(seq, 512, 8)
    tpb = seq // tm
    bmap = lambda i: (i // tpb, 0, 0)
    tile = pl.BlockSpec((tm, d), lambda i: (i, 0))
    return pl.pallas_call(
        functools.partial(_ffn_kernel, fchunk=512),
        grid=(n // tm,),
        in_specs=[
            tile,
            _resident((1, d)),
            pl.BlockSpec((1, 1, d), bmap),
            pl.BlockSpec((1, 1, d), bmap),
            pl.BlockSpec((1, 1, d), bmap),
            _resident((d, dff)),
            _resident((d, dff)),
            _resident((dff, d)),
            _resident((1, d)),
            pl.BlockSpec((1, 1, d), bmap),
            pl.BlockSpec((1, 1, d), bmap),
        ],
        out_specs=[tile, pl.BlockSpec((nblk, tm, LANES), lambda i: (0, i, 0))],
        out_shape=[jax.ShapeDtypeStruct((n, d), F32), jax.ShapeDtypeStruct((nblk, n, LANES), F32)],
        compiler_params=_params(("arbitrary",)),
        name="dense_swiglu",
    )(h, n2g.reshape(1, d), sc2, sh2, g2, wg.astype(BF16), wu.astype(BF16), wd.astype(BF16),
      n1g.reshape(1, d), sc1, sh1)


def _s5_tables(a_re, a_im, log_dt, b_re, b_im, c_re, c_im, d_skip):
    ng, p = a_re.shape
    c = b_re.shape[-1]
    gpb = LANES // c
    nblk = ng // gpb
    L = S5_CHUNK
    eye = jnp.eye(gpb, dtype=F32)
    blk = lambda a: a.reshape((nblk, gpb) + a.shape[1:])
    dt = jnp.exp(log_dt)[:, None]
    mag = jnp.exp(dt * a_re)
    ab_re = mag * jnp.cos(dt * a_im)
    ab_im = mag * jnp.sin(dt * a_im)
    den = a_re * a_re + a_im * a_im
    f_re = ((ab_re - 1.0) * a_re + ab_im * a_im) / den
    f_im = (ab_im * a_re - (ab_re - 1.0) * a_im) / den
    bb_re = f_re[..., None] * b_re - f_im[..., None] * b_im
    bb_im = f_re[..., None] * b_im + f_im[..., None] * b_re
    jj = jnp.arange(L + 1, dtype=F32)[None, :, None]
    ang = blk(dt * a_im).reshape(nblk, 1, gpb * p) * jj
    pmag = jnp.exp(blk(dt * a_re).reshape(nblk, 1, gpb * p) * jj)
    pw_re = pmag * jnp.cos(ang)
    pw_im = pmag * jnp.sin(ang)
    bbx_re = jnp.einsum("bgpc,gh->bgchp", blk(bb_re), eye).reshape(nblk, LANES, gpb * p)
    bbx_im = jnp.einsum("bgpc,gh->bgchp", blk(bb_im), eye).reshape(nblk, LANES, gpb * p)
    ctx_re = jnp.einsum("bgop,gh->bgohp", blk(c_re), eye).reshape(nblk, LANES, gpb * p)
    ctx_im = jnp.einsum("bgop,gh->bgohp", blk(c_im), eye).reshape(nblk, LANES, gpb * p)
    ccm = jnp.einsum("rbgop,gh->brgpho", jnp.stack([blk(c_re), -blk(c_im)]), eye).reshape(nblk, 2 * gpb * p, LANES)
    d_blk = jnp.tile(d_skip.reshape(nblk, LANES), (1, L)).reshape(nblk, 1, L * LANES)
    return bbx_re, bbx_im, ctx_re, ctx_im, ccm, pw_re, pw_im, d_blk


def _s5_kernel(u_ref, bbr_ref, bbi_ref, ctr_ref, cti_ref, ccm_ref, pwr_ref, pwi_ref, d_ref, o_ref,
               t_ref, we_ref, vt_ref, e_ref, hin_ref):
    L = S5_CHUNK
    rows, ew = e_ref.shape
    ph = ew // 2

    @pl.when(pl.program_id(1) == 0)
    def _build_operands():
        t_ref[...] = jnp.zeros_like(t_ref)
        bbr, bbi = bbr_ref[0], bbi_ref[0]
        ctr, cti = ctr_ref[0], cti_ref[0]
        cc_hi, cc_lo = _split_bf16(ccm_ref[0])
        for lp in range(L):
            j = L - 1 - lp
            p_re, p_im = pwr_ref[0, j:j + 1, :], pwi_ref[0, j:j + 1, :]
            w = jnp.concatenate([bbr * p_re - bbi * p_im, bbr * p_im + bbi * p_re], axis=1)
            w_hi, w_lo = _split_bf16(w)
            we_ref[lp * LANES:(lp + 1) * LANES, :] = w_hi
            k = (jnp.dot(w_hi, cc_hi, preferred_element_type=F32)
                 + jnp.dot(w_lo, cc_hi, preferred_element_type=F32)
                 + jnp.dot(w_hi, cc_lo, preferred_element_type=F32)).astype(BF16)
            for l1 in range(L - j):
                t_ref[l1 * LANES:(l1 + 1) * LANES, (l1 + j) * LANES:(l1 + j + 1) * LANES] = k
        for l in range(L):
            p_re, p_im = pwr_ref[0, l + 1:l + 2, :], pwi_ref[0, l + 1:l + 2, :]
            v = jnp.concatenate([ctr * p_re - cti * p_im, -(ctr * p_im + cti * p_re)], axis=1)
            vt_ref[l * LANES:(l + 1) * LANES, :] = v.astype(BF16)

    xcat = jnp.concatenate([u_ref[0, pl.ds(l, rows, stride=L), :] for l in range(L)], axis=1)
    xb = xcat.astype(BF16)
    e_ref[...] = jnp.dot(xb, we_ref[...], preferred_element_type=F32)
    ar = pwr_ref[0, L:L + 1, :]
    ai = pwi_ref[0, L:L + 1, :]

    def body(r, carry):
        hr, hi = carry
        hin_ref[pl.ds(r, 1), 0:ph] = hr
        hin_ref[pl.ds(r, 1), ph:ew] = hi
        er = e_ref[pl.ds(r, 1), 0:ph]
        ei = e_ref[pl.ds(r, 1), ph:ew]
        return (ar * hr - ai * hi + er, ar * hi + ai * hr + ei)

    zero = jnp.zeros((1, ph), F32)
    lax.fori_loop(0, rows, body, (zero, zero))
    tw = 2 * LANES
    y = jnp.concatenate([jnp.dot(xb[:, :c0 + tw], t_ref[0:c0 + tw, c0:c0 + tw], preferred_element_type=F32)
                         for c0 in range(0, L * LANES, tw)], axis=1)
    y = y + lax.dot_general(hin_ref[...].astype(BF16), vt_ref[...], (((1,), (1,)), ((), ())),
                            preferred_element_type=F32)
    y = _gelu_tanh(y + d_ref[0] * xcat)
    for l in range(L):
        o_ref[0, pl.ds(l, rows, stride=L), :] = y[:, l * LANES:(l + 1) * LANES]


def _s5_mix(u, tables, nbatch):
    nblk, n, _ = u.shape
    L = S5_CHUNK
    bbx_re, bbx_im, ctx_re, ctx_im, ccm, pw_re, pw_im, d_blk = tables
    sw = bbx_re.shape[2]
    seq = n // nbatch
    rows = seq // L
    slab = lambda a: pl.BlockSpec((1,) + a.shape[1:], lambda g, b: (g, 0, 0))
    return pl.pallas_call(
        _s5_kernel,
        grid=(nblk, nbatch),
        in_specs=[pl.BlockSpec((1, seq, LANES), lambda g, b: (g, b, 0))] + [slab(a) for a in tables],
        out_specs=pl.BlockSpec((1, seq, LANES), lambda g, b: (g, b, 0)),
        out_shape=jax.ShapeDtypeStruct((nblk, n, LANES), F32),
        scratch_shapes=[
            pltpu.VMEM((L * LANES, L * LANES), BF16),
            pltpu.VMEM((L * LANES, 2 * sw), BF16),
            pltpu.VMEM((L * LANES, 2 * sw), BF16),
            pltpu.VMEM((rows, 2 * sw), F32),
            pltpu.VMEM((rows, 2 * sw), F32),
        ],
        compiler_params=_params(("arbitrary", "arbitrary")),
        name="s5_chunked_scan",
    )(u, *tables)


def _glu_kernel(y_ref, w_ref, b_ref, h_ref, g1_ref, n2g_ref, sc_ref, sh_ref, wr_ref, br_ref,
                h3_ref, t_ref, rt_ref):
    d = h_ref.shape[1]
    ne = wr_ref.shape[0]
    y = jnp.concatenate([y_ref[g] for g in range(y_ref.shape[0])], axis=1)
    z = jnp.dot(y.astype(BF16), w_ref[...], preferred_element_type=F32) + b_ref[...]
    h3 = h_ref[...] + g1_ref[0] * (z[:, :d] * jax.nn.sigmoid(z[:, d:]))
    h3_ref[...] = h3
    t = _rms_mod(h3, n2g_ref[...], sc_ref[0], sh_ref[0])
    rows = t.shape[0]
    for j in range(TOKEN_ROWS):
        t_ref[pl.ds(j, rows, stride=TOKEN_ROWS), :] = t[:, j * LANES:(j + 1) * LANES]
    logit = [jnp.sum(t * wr_ref[e:e + 1, :], axis=-1, keepdims=True) + br_ref[e:e + 1, :] for e in range(ne)]

    def top1(cols):
        m = cols[0]
        for col in cols[1:]:
            m = jnp.maximum(m, col)
        idx = jnp.full_like(m, float(ne))
        for e in reversed(range(ne)):
            idx = jnp.where(cols[e] == m, float(e), idx)
        return m, idx

    m1, i1 = top1(logit)
    m2, i2 = top1([jnp.where(i1 == float(e), -jnp.inf, logit[e]) for e in range(ne)])
    e2 = jnp.exp(m2 - m1)
    den = 1.0 + e2
    lane = lax.broadcasted_iota(jnp.int32, rt_ref.shape, 1)
    rt_ref[...] = jnp.where(lane == 0, 1.0 / den,
                            jnp.where(lane == 1, e2 / den,
                                      jnp.where(lane == 2, i1, jnp.where(lane == 3, i2, 0.0))))


def _glu_router(y, w_glu, b_glu, h, g1, n2g, sc2, sh2, w_router, b_router, seq):
    n, d = h.shape
    ne = w_router.shape[1]
    tm = _divisor(seq, 512, 8)
    tpb = seq // tm
    bmap = lambda i: (i // tpb, 0, 0)
    tile = pl.BlockSpec((tm, d), lambda i: (i, 0))
    return pl.pallas_call(
        _glu_kernel,
        grid=(n // tm,),
        in_specs=[
            pl.BlockSpec((d // LANES, tm, LANES), lambda i: (0, i, 0)),
            _resident((d, 2 * d)),
            _resident((1, 2 * d)),
            tile,
            pl.BlockSpec((1, 1, d), bmap),
            _resident((1, d)),
            pl.BlockSpec((1, 1, d), bmap),
            pl.BlockSpec((1, 1, d), bmap),
            _resident((ne, d)),
            _resident((ne, 1)),
        ],
        out_specs=[tile, pl.BlockSpec((tm * TOKEN_ROWS, LANES), lambda i: (i, 0)),
                   pl.BlockSpec((tm, LANES), lambda i: (i, 0))],
        out_shape=[
            jax.ShapeDtypeStruct((n, d), F32),
            jax.ShapeDtypeStruct((n * TOKEN_ROWS, LANES), F32),
            jax.ShapeDtypeStruct((n, LANES), F32),
        ],
        compiler_params=_params(("arbitrary",)),
        name="s5_glu_router",
    )(y, w_glu.astype(BF16), b_glu.reshape(1, 2 * d), h, g1, n2g.reshape(1, d), sc2, sh2,
      w_router.T, b_router.reshape(ne, 1))


def _token_copy(src_ref, src_tok, dst_ref, dst_tok, sem):
    s = pl.multiple_of(src_tok * TOKEN_ROWS, TOKEN_ROWS)
    t = pl.multiple_of(dst_tok * TOKEN_ROWS, TOKEN_ROWS)
    return pltpu.make_async_copy(src_ref.at[pl.ds(s, TOKEN_ROWS), :], dst_ref.at[pl.ds(t, TOKEN_ROWS), :], sem)


def _tokens_wait(src_ref, dst_ref, dst_tok, ntok, sem):
    t = pl.multiple_of(dst_tok * TOKEN_ROWS, TOKEN_ROWS)
    pltpu.make_async_copy(src_ref.at[pl.ds(0, ntok * TOKEN_ROWS), :],
                          dst_ref.at[pl.ds(t, ntok * TOKEN_ROWS), :], sem).wait()


def _expert_kernel(te_ref, tv_ref, nxt_ref, first_ref, t8_ref, wg_ref, wu_ref, wd_ref, o_ref,
                   xbuf_ref, xb_ref, acc_ref, sems, *, fchunk, nf):
    del te_ref
    i = pl.program_id(0)
    f = pl.program_id(1)
    nt = pl.num_programs(0)
    tm = xb_ref.shape[0]
    slot = lax.rem(i, 2)

    def gather(idx_ref, into):
        def body(j, c):
            for u in range(DMA_UNROLL):
                r = j * DMA_UNROLL + u
                _token_copy(t8_ref, idx_ref[0, 0, r], xbuf_ref, into * tm + r, sems.at[into]).start(priority=u % 2)
            return c
        lax.fori_loop(0, tm // DMA_UNROLL, body, 0)

    @pl.when(jnp.logical_and(i == 0, f == 0))
    def _():
        gather(first_ref, 0)

    @pl.when(jnp.logical_and(f == 0, jnp.logical_and(i + 1 < nt, tv_ref[jnp.minimum(i + 1, nt - 1)] == 1)))
    def _():
        gather(nxt_ref, 1 - slot)

    @pl.when(jnp.logical_and(tv_ref[i] == 0, f == nf - 1))
    def _():
        o_ref[...] = jnp.zeros_like(o_ref)

    @pl.when(tv_ref[i] == 1)
    def _():
        @pl.when(f == 0)
        def _():
            _tokens_wait(t8_ref, xbuf_ref, slot * tm, tm, sems.at[slot])
            base = pl.multiple_of(slot * (tm * TOKEN_ROWS), TOKEN_ROWS)
            for j in range(TOKEN_ROWS):
                xb_ref[:, j * LANES:(j + 1) * LANES] = (
                    xbuf_ref[pl.ds(base + j, tm, stride=TOKEN_ROWS), :].astype(BF16))

        x = xb_ref[...]
        tf = wg_ref.shape[2]
        y = None
        for f0 in range(0, tf, fchunk):
            f1 = min(f0 + fchunk, tf)
            g = jnp.dot(x, wg_ref[0, :, f0:f1], preferred_element_type=F32)
            u = jnp.dot(x, wu_ref[0, :, f0:f1], preferred_element_type=F32)
            a = (_silu(g) * u).astype(BF16)
            yy = jnp.dot(a, wd_ref[0, f0:f1, :], preferred_element_type=F32)
            y = yy if y is None else y + yy

        @pl.when(f == 0)
        def _():
            acc_ref[...] = y

        @pl.when(f > 0)
        def _():
            acc_ref[...] += y

        @pl.when(f == nf - 1)
        def _():
            acc = acc_ref[...]
            for j in range(TOKEN_ROWS):
                o_ref[pl.ds(j, tm, stride=TOKEN_ROWS), :] = acc[:, j * LANES:(j + 1) * LANES]


def _experts(t8, src_tok, tile_expert, tile_valid, wg, wu, wd, tm):
    n_tiles = src_tok.shape[0]
    d = TOKEN_ROWS * LANES
    dff = wg.shape[2]
    tf = _divisor(dff, 1792, 256)
    nf = dff // tf
    assert tm % (nf * DMA_UNROLL) == 0
    fidx = lambda i, f, tv: f * tv[i] + (nf - 1) * (1 - tv[i])
    smem_tile = lambda imap: pl.BlockSpec((1, 1, tm), imap, memory_space=pltpu.SMEM)
    grid_spec = pltpu.PrefetchScalarGridSpec(
        num_scalar_prefetch=2,
        grid=(n_tiles, nf),
        in_specs=[
            smem_tile(lambda i, f, te, tv: (jnp.minimum(i + 1, n_tiles - 1), 0, 0)),
            smem_tile(lambda i, f, te, tv: (0, 0, 0)),
            pl.BlockSpec(memory_space=pl.ANY),
            pl.BlockSpec((1, d, tf), lambda i, f, te, tv: (te[i], 0, fidx(i, f, tv))),
            pl.BlockSpec((1, d, tf), lambda i, f, te, tv: (te[i], 0, fidx(i, f, tv))),
            pl.BlockSpec((1, tf, d), lambda i, f, te, tv: (te[i], fidx(i, f, tv), 0)),
        ],
        out_specs=pl.BlockSpec((tm * TOKEN_ROWS, LANES), lambda i, f, te, tv: (i, 0)),
        scratch_shapes=[
            pltpu.VMEM((2 * tm * TOKEN_ROWS, LANES), F32),
            pltpu.VMEM((tm, d), BF16),
            pltpu.VMEM((tm, d), F32),
            pltpu.SemaphoreType.DMA((2,)),
        ],
    )
    return pl.pallas_call(
        functools.partial(_expert_kernel, fchunk=1024, nf=nf),
        grid_spec=grid_spec,
        out_shape=jax.ShapeDtypeStruct((n_tiles * tm * TOKEN_ROWS, LANES), F32),
        compiler_params=_params(("arbitrary", "arbitrary")),
        name="moe_experts",
    )(tile_expert, tile_valid, src_tok, src_tok, t8, wg, wu, wd)


def _combine_kernel(nxt_ref, first_ref, ys_ref, h_ref, rt_ref, g2_ref, fg_ref, o_ref, ybuf_ref, sems):
    i = pl.program_id(0)
    ns = pl.num_programs(0)
    tc = h_ref.shape[0]
    npair = TOP_K * tc
    slot = lax.rem(i, 2)

    def gather(idx_ref, into):
        def body(j, c):
            for u in range(DMA_UNROLL):
                r = j * DMA_UNROLL + u
                for k in range(TOP_K):
                    _token_copy(ys_ref, idx_ref[0, 0, TOP_K * r + k], ybuf_ref, into * npair + k * tc + r,
                                sems.at[into]).start(priority=k % 2)
            return c
        lax.fori_loop(0, tc // DMA_UNROLL, body, 0)

    @pl.when(i == 0)
    def _():
        gather(first_ref, 0)

    @pl.when(i + 1 < ns)
    def _():
        gather(nxt_ref, 1 - slot)

    _tokens_wait(ys_ref, ybuf_ref, slot * npair, npair, sems.at[slot])
    rt = rt_ref[...]
    base = pl.multiple_of(slot * (npair * TOKEN_ROWS), TOKEN_ROWS)
    pieces = []
    for j in range(TOKEN_ROWS):
        yj = None
        for k in range(TOP_K):
            v = ybuf_ref[pl.ds(base + k * tc * TOKEN_ROWS + j, tc, stride=TOKEN_ROWS), :]
            yj = rt[:, k:k + 1] * v if yj is None else yj + rt[:, k:k + 1] * v
        pieces.append(yj)
    y = jnp.concatenate(pieces, axis=1)
    h4 = h_ref[...] + g2_ref[0] * y
    ms = jnp.mean(h4 * h4, axis=-1, keepdims=True)
    o_ref[...] = h4 * lax.rsqrt(ms + NORM_EPS) * fg_ref[...]


def _combine(ys8, dest, h2d, rt, g2, final_g, seq):
    n, d = h2d.shape
    tc = _divisor(seq, 512, DMA_UNROLL)
    tpb = seq // tc
    ns = n // tc
    dest3 = dest.reshape(ns, 1, TOP_K * tc)
    smem_tile = lambda imap: pl.BlockSpec((1, 1, TOP_K * tc), imap, memory_space=pltpu.SMEM)
    return pl.pallas_call(
        _combine_kernel,
        grid=(ns,),
        in_specs=[
            smem_tile(lambda i: (jnp.minimum(i + 1, ns - 1), 0, 0)),
            smem_tile(lambda i: (0, 0, 0)),
            pl.BlockSpec(memory_space=pl.ANY),
            pl.BlockSpec((tc, d), lambda i: (i, 0)),
            pl.BlockSpec((tc, LANES), lambda i: (i, 0)),
            pl.BlockSpec((1, 1, d), lambda i: (i // tpb, 0, 0)),
            _resident((1, d)),
        ],
        out_specs=pl.BlockSpec((tc, d), lambda i: (i, 0)),
        out_shape=jax.ShapeDtypeStruct((n, d), F32),
        scratch_shapes=[pltpu.VMEM((2 * TOP_K * tc * TOKEN_ROWS, LANES), F32), pltpu.SemaphoreType.DMA((2,))],
        compiler_params=_params(("arbitrary",)),
        name="moe_combine_norm",
    )(dest3, dest3, ys8, h2d, rt, g2, final_g.reshape(1, d))


def _route(rt, n_experts, tm):
    n = rt.shape[0]
    npairs = n * TOP_K
    pair_expert = rt[:, 2:2 + TOP_K].astype(jnp.int32).reshape(npairs)
    onehot = (pair_expert[:, None] == jnp.arange(n_experts, dtype=jnp.int32)[None, :]).astype(jnp.int32)
    csum = jnp.cumsum(onehot, axis=0)
    rank = jnp.sum(onehot * csum, axis=1) - 1
    counts = csum[-1]
    tiles = (counts + tm - 1) // tm
    tile_end = jnp.cumsum(tiles)
    tile_start = tile_end - tiles
    dest = jnp.sum(onehot * tile_start[None, :], axis=1) * tm + rank
    n_tiles = npairs // tm + n_experts
    ti = jnp.arange(n_tiles, dtype=jnp.int32)
    n_active = tile_end[-1]
    tile_valid = (ti < n_active).astype(jnp.int32)
    last_used = jnp.minimum(ti, n_active - 1)
    expert_of = jnp.sum((last_used[:, None] >= tile_end[None, :]).astype(jnp.int32), axis=1)
    tile_expert = jnp.minimum(expert_of, n_experts - 1)
    order_tok = (jnp.sort(pair_expert * npairs + jnp.arange(npairs, dtype=jnp.int32)) % npairs) // TOP_K
    nrows = n_tiles * tm
    count_start = jnp.cumsum(counts) - counts
    padded = jnp.concatenate([jnp.zeros((nrows,), jnp.int32), order_tok, jnp.zeros((nrows,), jnp.int32)])
    row = jnp.arange(nrows, dtype=jnp.int32)
    src_tok = jnp.zeros((nrows,), jnp.int32)
    for e in range(n_experts):
        first_row = tile_start[e] * tm
        shifted = lax.dynamic_slice(padded, (nrows + count_start[e] - first_row,), (nrows,))
        mine = jnp.logical_and(row >= first_row, row < first_row + counts[e])
        src_tok = jnp.where(mine, shifted, src_tok)
    return dest.astype(jnp.int32), src_tok.reshape(n_tiles, 1, tm), tile_expert, tile_valid


def kernel(x, c, mod_w, mod_b, norm1_g, norm2_g, conv_w_pw1, conv_b_pw1, conv_w_dw, conv_b_dw, conv_ln_g, conv_ln_b, conv_w_pw2, conv_b_pw2, ssm_a_re, ssm_a_im, ssm_log_dt, ssm_b_re, ssm_b_im, ssm_c_re, ssm_c_im, ssm_d, ssm_w_glu, ssm_b_glu, ffn_w_gate, ffn_w_up, ffn_w_down, moe_w_router, moe_b_router, moe_w_gate, moe_w_up, moe_w_down, final_norm_g):
    nb, seq, d = x.shape
    n = nb * seq
    assert mod_w.shape[0] == 2 and seq % S5_CHUNK == 0 and d == TOKEN_ROWS * LANES
    n_experts = moe_w_router.shape[-1]

    mod = _modulation(c, mod_w, mod_b)
    parts = [[mod[i, :, k * d:(k + 1) * d].reshape(nb, 1, d) for k in range(6)] for i in range(2)]
    sh1a, sc1a, g1a, sh2a, sc2a, g2a = parts[0]
    sh1b, sc1b, g1b, sh2b, sc2b, g2b = parts[1]

    u = _pw1(x.reshape(n, d), norm1_g[0], sc1a, sh1a, conv_w_pw1[0], conv_b_pw1[0], seq)
    h1 = _conv_block(u.reshape(nb, seq, d), x, conv_w_dw[0], conv_b_dw[0], conv_ln_g[0], conv_ln_b[0],
                     conv_w_pw2[0], conv_b_pw2[0], g1a)
    h2, u1 = _dense_ffn(h1.reshape(n, d), norm2_g[0], sc2a, sh2a, g2a,
                        ffn_w_gate[0], ffn_w_up[0], ffn_w_down[0], norm1_g[1], sc1b, sh1b, seq)

    tables = _s5_tables(ssm_a_re[0], ssm_a_im[0], ssm_log_dt[0], ssm_b_re[0], ssm_b_im[0],
                        ssm_c_re[0], ssm_c_im[0], ssm_d[0])
    y1 = _s5_mix(u1, tables, nb)
    h3, t8, rt = _glu_router(y1, ssm_w_glu[0], ssm_b_glu[0], h2, g1b, norm2_g[1], sc2b, sh2b,
                             moe_w_router[0], moe_b_router[0], seq)
    tm = 512
    dest, src_tok, tile_expert, tile_valid = _route(rt, n_experts, tm)
    ys8 = _experts(t8, src_tok, tile_expert, tile_valid, moe_w_gate[0].astype(BF16), moe_w_up[0].astype(BF16),
                   moe_w_down[0].astype(BF16), tm)
    out = _combine(ys8, dest, h3, rt, g2b, final_norm_g, seq)
    return out.reshape(nb, seq, d)
```

```python
import functools

import jax
import jax.numpy as jnp
from jax import lax
from jax.experimental import pallas as pl
from jax.experimental.pallas import tpu as pltpu

F32 = jnp.float32
BF16 = jnp.bfloat16
HIGHEST = lax.Precision.HIGHEST

NORM_EPS = 1e-6
TOP_K = 2
LANES = 128
S5_CHUNK = 16
S5_SEQ_PER_STEP = 2
CONV_HALO = 32
CONV_ROWS = 64
TOKEN_ROWS = 8
DMA_UNROLL = 8
VMEM_LIMIT = 56 * 2**20


def _params(sem):
    return pltpu.CompilerParams(dimension_semantics=sem, vmem_limit_bytes=VMEM_LIMIT)


def _divisor(n, cap, mult):
    best = None
    for d in range(mult, min(n, cap) + 1, mult):
        if n % d == 0:
            best = d
    assert best is not None, (n, cap, mult)
    return best


def _resident(shape):
    nd = len(shape)
    return pl.BlockSpec(shape, lambda *_: (0,) * nd, pipeline_mode=pl.Buffered(1))


def _rms_mod(x, g, sc, sh):
    ms = jnp.mean(x * x, axis=-1, keepdims=True)
    return (x * lax.rsqrt(ms + NORM_EPS) * g) * (1.0 + sc) + sh


def _silu(x):
    return x * jax.nn.sigmoid(x)


def _gelu_tanh(x):
    c = 0.7978845608028654
    return 0.5 * x * (1.0 + jnp.tanh(c * (x + 0.044715 * (x * x * x))))


def _split_bf16(x):
    hi = x.astype(BF16)
    return hi, (x - hi.astype(F32)).astype(BF16)


def _mod_kernel(c_ref, w_ref, b_ref, o_ref):
    cond = _silu(c_ref[...])
    o_ref[0] = jnp.dot(cond, w_ref[0], preferred_element_type=F32, precision=HIGHEST) + b_ref[0]


def _modulation(c, mod_w, mod_b):
    depth, d, d6 = mod_w.shape
    nb = c.shape[0]
    rows = -(-nb // 8) * 8
    c8 = jnp.pad(c, ((0, rows - nb), (0, 0)))
    tn = _divisor(d6, 1536, LANES)
    out = pl.pallas_call(
        _mod_kernel,
        grid=(depth, d6 // tn),
        in_specs=[
            pl.BlockSpec((rows, d), lambda i, j: (0, 0)),
            pl.BlockSpec((1, d, tn), lambda i, j: (i, 0, j)),
            pl.BlockSpec((1, 1, tn), lambda i, j: (i, 0, j)),
        ],
        out_specs=pl.BlockSpec((1, rows, tn), lambda i, j: (i, 0, j)),
        out_shape=jax.ShapeDtypeStruct((depth, rows, d6), F32),
        compiler_params=_params(("arbitrary", "arbitrary")),
        name="adaln_mod",
    )(c8, mod_w, mod_b.reshape(depth, 1, d6))
    return out[:, :nb, :]


def _cast_rider(w, nsteps, imap):
    w2 = w.reshape(-1, w.shape[-1])
    rows, cols = w2.shape
    assert rows % (16 * nsteps) == 0, (rows, nsteps)
    return w2, pl.BlockSpec((rows // nsteps, cols), imap), jax.ShapeDtypeStruct((rows, cols), BF16)


def _pw1_kernel(x_ref, g_ref, sc_ref, sh_ref, w_ref, b_ref, cw_ref, o_ref, co_ref):
    x = x_ref[...]
    d = x.shape[1]
    y = _rms_mod(x, g_ref[...], sc_ref[0], sh_ref[0])
    u = jnp.dot(y.astype(BF16), w_ref[...], preferred_element_type=F32) + b_ref[...]
    o_ref[...] = u[:, :d] * jax.nn.sigmoid(u[:, d:])
    co_ref[...] = cw_ref[...].astype(BF16)


def _pw1(x2, norm_g, sc, sh, w, b, seq, rider):
    n, d = x2.shape
    tm = _divisor(seq, 512, 8)
    tpb = seq // tm
    bmap = lambda i: (i // tpb, 0, 0)
    rider2, rider_spec, rider_shape = _cast_rider(rider, n // tm, lambda i: (i, 0))
    out, casted = pl.pallas_call(
        _pw1_kernel,
        grid=(n // tm,),
        in_specs=[
            pl.BlockSpec((tm, d), lambda i: (i, 0)),
            _resident((1, d)),
            pl.BlockSpec((1, 1, d), bmap),
            pl.BlockSpec((1, 1, d), bmap),
            _resident((d, 2 * d)),
            _resident((1, 2 * d)),
            rider_spec,
        ],
        out_specs=[pl.BlockSpec((tm, d), lambda i: (i, 0)), rider_spec],
        out_shape=[jax.ShapeDtypeStruct((n, d), F32), rider_shape],
        compiler_params=_params(("arbitrary",)),
        name="conv_pw1_glu",
    )(x2, norm_g.reshape(1, d), sc, sh, w.astype(BF16), b.reshape(1, 2 * d), rider2)
    return out, casted.reshape(rider.shape)


def _conv_kernel(cur_ref, prev_ref, wdw_ref, bdw_ref, lng_ref, lnb_ref, w2_ref, b2_ref, h_ref, g1_ref, cw_ref,
                 o_ref, co_ref, buf_ref, cv_ref, *, taps):
    tm, d = cv_ref.shape
    i = pl.program_id(1)
    co_ref[...] = cw_ref[...].astype(BF16)
    buf_ref[0:CONV_HALO, :] = jnp.where(i > 0, prev_ref[0], 0.0)
    buf_ref[CONV_HALO:, :] = cur_ref[0]
    off0 = CONV_HALO - (taps - 1)
    span = CONV_ROWS + CONV_HALO
    for c in range(d // LANES):
        lanes = slice(c * LANES, (c + 1) * LANES)

        def body(k, carry, lanes=lanes):
            r0 = pl.multiple_of(k * CONV_ROWS, CONV_ROWS)
            v = buf_ref[pl.ds(r0, span), lanes]
            acc = jnp.broadcast_to(bdw_ref[:, lanes], (CONV_ROWS, LANES))
            for s in range(8):
                xs = v if s == 0 else pltpu.roll(v, span - s, axis=0)
                for q in range(span // 8):
                    o = 8 * q + s
                    if o < off0 or o > off0 + taps - 1:
                        continue
                    wrow = wdw_ref[o - off0:o - off0 + 1, lanes]
                    acc = acc + wrow * xs[8 * q:8 * q + CONV_ROWS, :]
            cv_ref[pl.ds(r0, CONV_ROWS), lanes] = acc
            return carry

        lax.fori_loop(0, tm // CONV_ROWS, body, 0)
    v = cv_ref[...]
    mu = jnp.mean(v, axis=-1, keepdims=True)
    xc = v - mu
    var = jnp.mean(xc * xc, axis=-1, keepdims=True)
    y = _silu(xc * lax.rsqrt(var + NORM_EPS) * lng_ref[...] + lnb_ref[...])
    z = jnp.dot(y.astype(BF16), w2_ref[...], preferred_element_type=F32) + b2_ref[...]
    o_ref[0] = h_ref[0] + g1_ref[0] * z


def _conv_block(u3, x3, w_dw, b_dw, ln_g, ln_b, w2, b2, g1, rider):
    nb, seq, d = x3.shape
    taps = w_dw.shape[0]
    assert taps - 1 <= CONV_HALO and d % LANES == 0
    tm = _divisor(seq, 512, CONV_ROWS)
    hb = tm // CONV_HALO
    tps = seq // tm
    wpad = jnp.pad(w_dw, ((0, -taps % 8), (0, 0)))
    row = lambda a: a.reshape(1, d)
    rider2, rider_spec, rider_shape = _cast_rider(rider, nb * tps, lambda b, i: (b * tps + i, 0))
    out, casted = pl.pallas_call(
        functools.partial(_conv_kernel, taps=taps),
        grid=(nb, seq // tm),
        in_specs=[
            pl.BlockSpec((1, tm, d), lambda b, i: (b, i, 0)),
            pl.BlockSpec((1, CONV_HALO, d), lambda b, i: (b, jnp.maximum(i * hb - 1, 0), 0)),
            _resident(wpad.shape),
            _resident((1, d)),
            _resident((1, d)),
            _resident((1, d)),
            _resident((d, d)),
            _resident((1, d)),
            pl.BlockSpec((1, tm, d), lambda b, i: (b, i, 0)),
            pl.BlockSpec((1, 1, d), lambda b, i: (b, 0, 0)),
            rider_spec,
        ],
        out_specs=[pl.BlockSpec((1, tm, d), lambda b, i: (b, i, 0)), rider_spec],
        out_shape=[jax.ShapeDtypeStruct((nb, seq, d), F32), rider_shape],
        scratch_shapes=[pltpu.VMEM((tm + CONV_HALO, d), F32), pltpu.VMEM((tm, d), F32)],
        compiler_params=_params(("arbitrary", "arbitrary")),
        name="conv_dw_ln_pw2",
    )(u3, u3, wpad, row(b_dw), row(ln_g), row(ln_b), w2.astype(BF16), row(b2), x3, g1, rider2)
    return out, casted.reshape(rider.shape)


def _ffn_kernel(h_ref, n2g_ref, sc_ref, sh_ref, g2_ref, wg_ref, wu_ref, wd_ref, n1g_ref, sc1_ref, sh1_ref, cw_ref,
                h2_ref, u_ref, co_ref, *, fchunk):
    co_ref[...] = cw_ref[...].astype(BF16)
    h = h_ref[...]
    t = _rms_mod(h, n2g_ref[...], sc_ref[0], sh_ref[0]).astype(BF16)
    dff = wg_ref.shape[1]
    acc = None
    for f0 in range(0, dff, fchunk):
        f1 = min(f0 + fchunk, dff)
        g = jnp.dot(t, wg_ref[:, f0:f1], preferred_element_type=F32)
        u = jnp.dot(t, wu_ref[:, f0:f1], preferred_element_type=F32)
        a = (_silu(g) * u).astype(BF16)
        y = jnp.dot(a, wd_ref[f0:f1, :], preferred_element_type=F32)
        acc = y if acc is None else acc + y
    h2 = h + g2_ref[0] * acc
    h2_ref[...] = h2
    u = _rms_mod(h2, n1g_ref[...], sc1_ref[0], sh1_ref[0])
    for g in range(u_ref.shape[0]):
        u_ref[g] = u[:, g * LANES:(g + 1) * LANES]


def _dense_ffn(h, n2g, sc2, sh2, g2, wg, wu, wd, n1g, sc1, sh1, seq, rider):
    n, d = h.shape
    nblk = d // LANES
    dff = wg.shape[1]
    tm = _divisor(seq, 512, 8)
    tpb = seq // tm
    bmap = lambda i: (i // tpb, 0, 0)
    tile = pl.BlockSpec((tm, d), lambda i: (i, 0))
    rider2, rider_spec, rider_shape = _cast_rider(rider, n // tm, lambda i: (i, 0))
    h2, u, casted = pl.pallas_call(
        functools.partial(_ffn_kernel, fchunk=512),
        grid=(n // tm,),
        in_specs=[
            tile,
            _resident((1, d)),
            pl.BlockSpec((1, 1, d), bmap),
            pl.BlockSpec((1, 1, d), bmap),
            pl.BlockSpec((1, 1, d), bmap),
            _resident((d, dff)),
            _resident((d, dff)),
            _resident((dff, d)),
            _resident((1, d)),
            pl.BlockSpec((1, 1, d), bmap),
            pl.BlockSpec((1, 1, d), bmap),
            rider_spec,
        ],
        out_specs=[tile, pl.BlockSpec((nblk, tm, LANES), lambda i: (0, i, 0)), rider_spec],
        out_shape=[jax.ShapeDtypeStruct((n, d), F32), jax.ShapeDtypeStruct((nblk, n, LANES), F32), rider_shape],
        compiler_params=_params(("arbitrary",)),
        name="dense_swiglu",
    )(h, n2g.reshape(1, d), sc2, sh2, g2, wg.astype(BF16), wu.astype(BF16), wd.astype(BF16),
      n1g.reshape(1, d), sc1, sh1, rider2)
    return h2, u, casted.reshape(rider.shape)


def _s5_tables(a_re, a_im, log_dt, b_re, b_im, c_re, c_im, d_skip):
    ng, p = a_re.shape
    c = b_re.shape[-1]
    gpb = LANES // c
    nblk = ng // gpb
    L = S5_CHUNK
    eye = jnp.eye(gpb, dtype=F32)
    blk = lambda a: a.reshape((nblk, gpb) + a.shape[1:])
    dt = jnp.exp(log_dt)[:, None]
    mag = jnp.exp(dt * a_re)
    ab_re = mag * jnp.cos(dt * a_im)
    ab_im = mag * jnp.sin(dt * a_im)
    den = a_re * a_re + a_im * a_im
    f_re = ((ab_re - 1.0) * a_re + ab_im * a_im) / den
    f_im = (ab_im * a_re - (ab_re - 1.0) * a_im) / den
    bb_re = f_re[..., None] * b_re - f_im[..., None] * b_im
    bb_im = f_re[..., None] * b_im + f_im[..., None] * b_re
    jj = jnp.arange(L + 1, dtype=F32)[None, :, None]
    ang = blk(dt * a_im).reshape(nblk, 1, gpb * p) * jj
    pmag = jnp.exp(blk(dt * a_re).reshape(nblk, 1, gpb * p) * jj)
    pw_re = pmag * jnp.cos(ang)
    pw_im = pmag * jnp.sin(ang)
    bbx_re = jnp.einsum("bgpc,gh->bgchp", blk(bb_re), eye).reshape(nblk, LANES, gpb * p)
    bbx_im = jnp.einsum("bgpc,gh->bgchp", blk(bb_im), eye).reshape(nblk, LANES, gpb * p)
    ctx_re = jnp.einsum("bgop,gh->bgohp", blk(c_re), eye).reshape(nblk, LANES, gpb * p)
    ctx_im = jnp.einsum("bgop,gh->bgohp", blk(c_im), eye).reshape(nblk, LANES, gpb * p)
    ccm = jnp.einsum("rbgop,gh->brgpho", jnp.stack([blk(c_re), -blk(c_im)]), eye).reshape(nblk, 2 * gpb * p, LANES)
    d_blk = jnp.tile(d_skip.reshape(nblk, LANES), (1, L)).reshape(nblk, 1, L * LANES)
    return bbx_re, bbx_im, ctx_re, ctx_im, ccm, pw_re, pw_im, d_blk


def _s5_kernel(u_ref, bbr_ref, bbi_ref, ctr_ref, cti_ref, ccm_ref, pwr_ref, pwi_ref, d_ref, o_ref,
               t_ref, we_ref, vt_ref, e_ref, hin_ref, *, nseq):
    L = S5_CHUNK
    ew = e_ref.shape[1]
    rows = e_ref.shape[0] // nseq
    seq = rows * L
    ph = ew // 2

    @pl.when(pl.program_id(1) == 0)
    def _build_operands():
        t_ref[...] = jnp.zeros_like(t_ref)
        bbr, bbi = bbr_ref[0], bbi_ref[0]
        ctr, cti = ctr_ref[0], cti_ref[0]
        cc_hi, cc_lo = _split_bf16(ccm_ref[0])
        for lp in range(L):
            j = L - 1 - lp
            p_re, p_im = pwr_ref[0, j:j + 1, :], pwi_ref[0, j:j + 1, :]
            w = jnp.concatenate([bbr * p_re - bbi * p_im, bbr * p_im + bbi * p_re], axis=1)
            w_hi, w_lo = _split_bf16(w)
            we_ref[lp * LANES:(lp + 1) * LANES, :] = w_hi
            k = (jnp.dot(w_hi, cc_hi, preferred_element_type=F32)
                 + jnp.dot(w_lo, cc_hi, preferred_element_type=F32)
                 + jnp.dot(w_hi, cc_lo, preferred_element_type=F32)).astype(BF16)
            for l1 in range(L - j):
                t_ref[l1 * LANES:(l1 + 1) * LANES, (l1 + j) * LANES:(l1 + j + 1) * LANES] = k
        for l in range(L):
            p_re, p_im = pwr_ref[0, l + 1:l + 2, :], pwi_ref[0, l + 1:l + 2, :]
            v = jnp.concatenate([ctr * p_re - cti * p_im, -(ctr * p_im + cti * p_re)], axis=1)
            vt_ref[l * LANES:(l + 1) * LANES, :] = v.astype(BF16)

    xcat = jnp.concatenate(
        [jnp.concatenate([u_ref[0, pl.ds(q * seq + l, rows, stride=L), :] for l in range(L)], axis=1)
         for q in range(nseq)], axis=0)
    xb = xcat.astype(BF16)
    e_ref[...] = jnp.dot(xb, we_ref[...], preferred_element_type=F32)
    ar = pwr_ref[0, L:L + 1, :]
    ai = pwi_ref[0, L:L + 1, :]

    def body(r, carry):
        out = []
        for q in range(nseq):
            hr, hi = carry[2 * q], carry[2 * q + 1]
            hin_ref[pl.ds(q * rows + r, 1), 0:ph] = hr
            hin_ref[pl.ds(q * rows + r, 1), ph:ew] = hi
            er = e_ref[pl.ds(q * rows + r, 1), 0:ph]
            ei = e_ref[pl.ds(q * rows + r, 1), ph:ew]
            out += [ar * hr - ai * hi + er, ar * hi + ai * hr + ei]
        return tuple(out)

    zero = jnp.zeros((1, ph), F32)
    lax.fori_loop(0, rows, body, (zero,) * (2 * nseq))
    tw = 2 * LANES
    y = jnp.concatenate([jnp.dot(xb[:, :c0 + tw], t_ref[0:c0 + tw, c0:c0 + tw], preferred_element_type=F32)
                         for c0 in range(0, L * LANES, tw)], axis=1)
    y = y + lax.dot_general(hin_ref[...].astype(BF16), vt_ref[...], (((1,), (1,)), ((), ())),
                            preferred_element_type=F32)
    y = _gelu_tanh(y + d_ref[0] * xcat)
    for q in range(nseq):
        for l in range(L):
            o_ref[0, pl.ds(q * seq + l, rows, stride=L), :] = y[q * rows:(q + 1) * rows, l * LANES:(l + 1) * LANES]


def _s5_mix(u, tables, nbatch):
    nblk, n, _ = u.shape
    L = S5_CHUNK
    bbx_re, bbx_im, ctx_re, ctx_im, ccm, pw_re, pw_im, d_blk = tables
    sw = bbx_re.shape[2]
    seq = n // nbatch
    nseq = S5_SEQ_PER_STEP if nbatch % S5_SEQ_PER_STEP == 0 else 1
    rows = nseq * (seq // L)
    slab = lambda a: pl.BlockSpec((1,) + a.shape[1:], lambda g, b: (g, 0, 0))
    return pl.pallas_call(
        functools.partial(_s5_kernel, nseq=nseq),
        grid=(nblk, nbatch // nseq),
        in_specs=[pl.BlockSpec((1, nseq * seq, LANES), lambda g, b: (g, b, 0))] + [slab(a) for a in tables],
        out_specs=pl.BlockSpec((1, nseq * seq, LANES), lambda g, b: (g, b, 0)),
        out_shape=jax.ShapeDtypeStruct((nblk, n, LANES), F32),
        scratch_shapes=[
            pltpu.VMEM((L * LANES, L * LANES), BF16),
            pltpu.VMEM((L * LANES, 2 * sw), BF16),
            pltpu.VMEM((L * LANES, 2 * sw), BF16),
            pltpu.VMEM((rows, 2 * sw), F32),
            pltpu.VMEM((rows, 2 * sw), F32),
        ],
        compiler_params=_params(("arbitrary", "arbitrary")),
        name="s5_chunked_scan",
    )(u, *tables)


def _glu_kernel(y_ref, w_ref, b_ref, h_ref, g1_ref, n2g_ref, sc_ref, sh_ref, wr_ref, br_ref,
                h3_ref, t_ref, rt_ref):
    d = h_ref.shape[1]
    ne = wr_ref.shape[0]
    y = jnp.concatenate([y_ref[g] for g in range(y_ref.shape[0])], axis=1)
    z = jnp.dot(y.astype(BF16), w_ref[...], preferred_element_type=F32) + b_ref[...]
    h3 = h_ref[...] + g1_ref[0] * (z[:, :d] * jax.nn.sigmoid(z[:, d:]))
    h3_ref[...] = h3
    t = _rms_mod(h3, n2g_ref[...], sc_ref[0], sh_ref[0])
    rows = t.shape[0]
    for j in range(TOKEN_ROWS):
        t_ref[pl.ds(j, rows, stride=TOKEN_ROWS), :] = t[:, j * LANES:(j + 1) * LANES]
    logit = [jnp.sum(t * wr_ref[e:e + 1, :], axis=-1, keepdims=True) + br_ref[e:e + 1, :] for e in range(ne)]

    def top1(cols):
        m = cols[0]
        for col in cols[1:]:
            m = jnp.maximum(m, col)
        idx = jnp.full_like(m, float(ne))
        for e in reversed(range(ne)):
            idx = jnp.where(cols[e] == m, float(e), idx)
        return m, idx

    m1, i1 = top1(logit)
    m2, i2 = top1([jnp.where(i1 == float(e), -jnp.inf, logit[e]) for e in range(ne)])
    e2 = jnp.exp(m2 - m1)
    den = 1.0 + e2
    lane = lax.broadcasted_iota(jnp.int32, rt_ref.shape, 1)
    rt_ref[...] = jnp.where(lane == 0, 1.0 / den,
                            jnp.where(lane == 1, e2 / den,
                                      jnp.where(lane == 2, i1, jnp.where(lane == 3, i2, 0.0))))


def _glu_router(y, w_glu, b_glu, h, g1, n2g, sc2, sh2, w_router, b_router, seq):
    n, d = h.shape
    ne = w_router.shape[1]
    tm = _divisor(seq, 512, 8)
    tpb = seq // tm
    bmap = lambda i: (i // tpb, 0, 0)
    tile = pl.BlockSpec((tm, d), lambda i: (i, 0))
    return pl.pallas_call(
        _glu_kernel,
        grid=(n // tm,),
        in_specs=[
            pl.BlockSpec((d // LANES, tm, LANES), lambda i: (0, i, 0)),
            _resident((d, 2 * d)),
            _resident((1, 2 * d)),
            tile,
            pl.BlockSpec((1, 1, d), bmap),
            _resident((1, d)),
            pl.BlockSpec((1, 1, d), bmap),
            pl.BlockSpec((1, 1, d), bmap),
            _resident((ne, d)),
            _resident((ne, 1)),
        ],
        out_specs=[tile, pl.BlockSpec((tm * TOKEN_ROWS, LANES), lambda i: (i, 0)),
                   pl.BlockSpec((tm, LANES), lambda i: (i, 0))],
        out_shape=[
            jax.ShapeDtypeStruct((n, d), F32),
            jax.ShapeDtypeStruct((n * TOKEN_ROWS, LANES), F32),
            jax.ShapeDtypeStruct((n, LANES), F32),
        ],
        compiler_params=_params(("arbitrary",)),
        name="s5_glu_router",
    )(y, w_glu.astype(BF16), b_glu.reshape(1, 2 * d), h, g1, n2g.reshape(1, d), sc2, sh2,
      w_router.T, b_router.reshape(ne, 1))


def _token_copy(src_ref, src_tok, dst_ref, dst_tok, sem):
    s = pl.multiple_of(src_tok * TOKEN_ROWS, TOKEN_ROWS)
    t = pl.multiple_of(dst_tok * TOKEN_ROWS, TOKEN_ROWS)
    return pltpu.make_async_copy(src_ref.at[pl.ds(s, TOKEN_ROWS), :], dst_ref.at[pl.ds(t, TOKEN_ROWS), :], sem)


def _tokens_wait(src_ref, dst_ref, dst_tok, ntok, sem):
    t = pl.multiple_of(dst_tok * TOKEN_ROWS, TOKEN_ROWS)
    pltpu.make_async_copy(src_ref.at[pl.ds(0, ntok * TOKEN_ROWS), :],
                          dst_ref.at[pl.ds(t, ntok * TOKEN_ROWS), :], sem).wait()


def _expert_kernel(te_ref, tv_ref, nxt_ref, first_ref, t8_ref, wg_ref, wu_ref, wd_ref, o_ref,
                   xbuf_ref, xb_ref, acc_ref, sems, *, fchunk, nf):
    del te_ref
    i = pl.program_id(0)
    f = pl.program_id(1)
    nt = pl.num_programs(0)
    tm = xb_ref.shape[0]
    slot = lax.rem(i, 2)

    def gather(idx_ref, into):
        def body(j, c):
            for u in range(DMA_UNROLL):
                r = j * DMA_UNROLL + u
                _token_copy(t8_ref, idx_ref[0, 0, r], xbuf_ref, into * tm + r, sems.at[into]).start(priority=u % 2)
            return c
        lax.fori_loop(0, tm // DMA_UNROLL, body, 0)

    @pl.when(jnp.logical_and(i == 0, f == 0))
    def _():
        gather(first_ref, 0)

    @pl.when(jnp.logical_and(f == 0, jnp.logical_and(i + 1 < nt, tv_ref[jnp.minimum(i + 1, nt - 1)] == 1)))
    def _():
        gather(nxt_ref, 1 - slot)

    @pl.when(jnp.logical_and(tv_ref[i] == 0, f == nf - 1))
    def _():
        o_ref[...] = jnp.zeros_like(o_ref)

    @pl.when(tv_ref[i] == 1)
    def _():
        @pl.when(f == 0)
        def _():
            _tokens_wait(t8_ref, xbuf_ref, slot * tm, tm, sems.at[slot])
            base = pl.multiple_of(slot * (tm * TOKEN_ROWS), TOKEN_ROWS)
            for j in range(TOKEN_ROWS):
                xb_ref[:, j * LANES:(j + 1) * LANES] = (
                    xbuf_ref[pl.ds(base + j, tm, stride=TOKEN_ROWS), :].astype(BF16))

        x = xb_ref[...]
        tf = wg_ref.shape[2]
        y = None
        for f0 in range(0, tf, fchunk):
            f1 = min(f0 + fchunk, tf)
            g = jnp.dot(x, wg_ref[0, :, f0:f1], preferred_element_type=F32)
            u = jnp.dot(x, wu_ref[0, :, f0:f1], preferred_element_type=F32)
            a = (_silu(g) * u).astype(BF16)
            yy = jnp.dot(a, wd_ref[0, f0:f1, :], preferred_element_type=F32)
            y = yy if y is None else y + yy

        @pl.when(f == 0)
        def _():
            acc_ref[...] = y

        @pl.when(f > 0)
        def _():
            acc_ref[...] += y

        @pl.when(f == nf - 1)
        def _():
            acc = acc_ref[...]
            for j in range(TOKEN_ROWS):
                o_ref[pl.ds(j, tm, stride=TOKEN_ROWS), :] = acc[:, j * LANES:(j + 1) * LANES]


def _experts(t8, src_tok, tile_expert, tile_valid, wg, wu, wd, tm):
    n_tiles = src_tok.shape[0]
    d = TOKEN_ROWS * LANES
    dff = wg.shape[2]
    tf = _divisor(dff, 1792, 256)
    nf = dff // tf
    assert tm % (nf * DMA_UNROLL) == 0
    fidx = lambda i, f, tv: f * tv[i] + (nf - 1) * (1 - tv[i])
    smem_tile = lambda imap: pl.BlockSpec((1, 1, tm), imap, memory_space=pltpu.SMEM)
    grid_spec = pltpu.PrefetchScalarGridSpec(
        num_scalar_prefetch=2,
        grid=(n_tiles, nf),
        in_specs=[
            smem_tile(lambda i, f, te, tv: (jnp.minimum(i + 1, n_tiles - 1), 0, 0)),
            smem_tile(lambda i, f, te, tv: (0, 0, 0)),
            pl.BlockSpec(memory_space=pl.ANY),
            pl.BlockSpec((1, d, tf), lambda i, f, te, tv: (te[i], 0, fidx(i, f, tv))),
            pl.BlockSpec((1, d, tf), lambda i, f, te, tv: (te[i], 0, fidx(i, f, tv))),
            pl.BlockSpec((1, tf, d), lambda i, f, te, tv: (te[i], fidx(i, f, tv), 0)),
        ],
        out_specs=pl.BlockSpec((tm * TOKEN_ROWS, LANES), lambda i, f, te, tv: (i, 0)),
        scratch_shapes=[
            pltpu.VMEM((2 * tm * TOKEN_ROWS, LANES), F32),
            pltpu.VMEM((tm, d), BF16),
            pltpu.VMEM((tm, d), F32),
            pltpu.SemaphoreType.DMA((2,)),
        ],
    )
    return pl.pallas_call(
        functools.partial(_expert_kernel, fchunk=1024, nf=nf),
        grid_spec=grid_spec,
        out_shape=jax.ShapeDtypeStruct((n_tiles * tm * TOKEN_ROWS, LANES), F32),
        compiler_params=_params(("arbitrary", "arbitrary")),
        name="moe_experts",
    )(tile_expert, tile_valid, src_tok, src_tok, t8, wg, wu, wd)


def _combine_kernel(nxt_ref, first_ref, ys_ref, h_ref, rt_ref, g2_ref, fg_ref, o_ref, ybuf_ref, sems):
    i = pl.program_id(0)
    ns = pl.num_programs(0)
    tc = h_ref.shape[0]
    npair = TOP_K * tc
    slot = lax.rem(i, 2)

    def gather(idx_ref, into):
        def body(j, c):
            for u in range(DMA_UNROLL):
                r = j * DMA_UNROLL + u
                for k in range(TOP_K):
                    _token_copy(ys_ref, idx_ref[0, 0, TOP_K * r + k], ybuf_ref, into * npair + k * tc + r,
                                sems.at[into]).start(priority=k % 2)
            return c
        lax.fori_loop(0, tc // DMA_UNROLL, body, 0)

    @pl.when(i == 0)
    def _():
        gather(first_ref, 0)

    @pl.when(i + 1 < ns)
    def _():
        gather(nxt_ref, 1 - slot)

    _tokens_wait(ys_ref, ybuf_ref, slot * npair, npair, sems.at[slot])
    rt = rt_ref[...]
    base = pl.multiple_of(slot * (npair * TOKEN_ROWS), TOKEN_ROWS)
    pieces = []
    for j in range(TOKEN_ROWS):
        yj = None
        for k in range(TOP_K):
            v = ybuf_ref[pl.ds(base + k * tc * TOKEN_ROWS + j, tc, stride=TOKEN_ROWS), :]
            yj = rt[:, k:k + 1] * v if yj is None else yj + rt[:, k:k + 1] * v
        pieces.append(yj)
    y = jnp.concatenate(pieces, axis=1)
    h4 = h_ref[...] + g2_ref[0] * y
    ms = jnp.mean(h4 * h4, axis=-1, keepdims=True)
    o_ref[...] = h4 * lax.rsqrt(ms + NORM_EPS) * fg_ref[...]


def _combine(ys8, dest, h2d, rt, g2, final_g, seq):
    n, d = h2d.shape
    tc = _divisor(seq, 512, DMA_UNROLL)
    tpb = seq // tc
    ns = n // tc
    dest3 = dest.reshape(ns, 1, TOP_K * tc)
    smem_tile = lambda imap: pl.BlockSpec((1, 1, TOP_K * tc), imap, memory_space=pltpu.SMEM)
    return pl.pallas_call(
        _combine_kernel,
        grid=(ns,),
        in_specs=[
            smem_tile(lambda i: (jnp.minimum(i + 1, ns - 1), 0, 0)),
            smem_tile(lambda i: (0, 0, 0)),
            pl.BlockSpec(memory_space=pl.ANY),
            pl.BlockSpec((tc, d), lambda i: (i, 0)),
            pl.BlockSpec((tc, LANES), lambda i: (i, 0)),
            pl.BlockSpec((1, 1, d), lambda i: (i // tpb, 0, 0)),
            _resident((1, d)),
        ],
        out_specs=pl.BlockSpec((tc, d), lambda i: (i, 0)),
        out_shape=jax.ShapeDtypeStruct((n, d), F32),
        scratch_shapes=[pltpu.VMEM((2 * TOP_K * tc * TOKEN_ROWS, LANES), F32), pltpu.SemaphoreType.DMA((2,))],
        compiler_params=_params(("arbitrary",)),
        name="moe_combine_norm",
    )(dest3, dest3, ys8, h2d, rt, g2, final_g.reshape(1, d))


def _route(rt, n_experts, tm):
    n = rt.shape[0]
    npairs = n * TOP_K
    pair_expert = rt[:, 2:2 + TOP_K].astype(jnp.int32).reshape(npairs)
    onehot = (pair_expert[:, None] == jnp.arange(n_experts, dtype=jnp.int32)[None, :]).astype(jnp.int32)
    csum = jnp.cumsum(onehot, axis=0)
    rank = jnp.sum(onehot * csum, axis=1) - 1
    counts = csum[-1]
    tiles = (counts + tm - 1) // tm
    tile_end = jnp.cumsum(tiles)
    tile_start = tile_end - tiles
    dest = jnp.sum(onehot * tile_start[None, :], axis=1) * tm + rank
    n_tiles = npairs // tm + n_experts
    ti = jnp.arange(n_tiles, dtype=jnp.int32)
    n_active = tile_end[-1]
    tile_valid = (ti < n_active).astype(jnp.int32)
    last_used = jnp.minimum(ti, n_active - 1)
    expert_of = jnp.sum((last_used[:, None] >= tile_end[None, :]).astype(jnp.int32), axis=1)
    tile_expert = jnp.minimum(expert_of, n_experts - 1)
    order_tok = (jnp.sort(pair_expert * npairs + jnp.arange(npairs, dtype=jnp.int32)) % npairs) // TOP_K
    nrows = n_tiles * tm
    count_start = jnp.cumsum(counts) - counts
    padded = jnp.concatenate([jnp.zeros((nrows,), jnp.int32), order_tok, jnp.zeros((nrows,), jnp.int32)])
    row = jnp.arange(nrows, dtype=jnp.int32)
    src_tok = jnp.zeros((nrows,), jnp.int32)
    for e in range(n_experts):
        first_row = tile_start[e] * tm
        shifted = lax.dynamic_slice(padded, (nrows + count_start[e] - first_row,), (nrows,))
        mine = jnp.logical_and(row >= first_row, row < first_row + counts[e])
        src_tok = jnp.where(mine, shifted, src_tok)
    return dest.astype(jnp.int32), src_tok.reshape(n_tiles, 1, tm), tile_expert, tile_valid


def kernel(x, c, mod_w, mod_b, norm1_g, norm2_g, conv_w_pw1, conv_b_pw1, conv_w_dw, conv_b_dw, conv_ln_g, conv_ln_b, conv_w_pw2, conv_b_pw2, ssm_a_re, ssm_a_im, ssm_log_dt, ssm_b_re, ssm_b_im, ssm_c_re, ssm_c_im, ssm_d, ssm_w_glu, ssm_b_glu, ffn_w_gate, ffn_w_up, ffn_w_down, moe_w_router, moe_b_router, moe_w_gate, moe_w_up, moe_w_down, final_norm_g):
    nb, seq, d = x.shape
    n = nb * seq
    assert mod_w.shape[0] == 2 and seq % S5_CHUNK == 0 and d == TOKEN_ROWS * LANES
    n_experts = moe_w_router.shape[-1]

    mod = _modulation(c, mod_w, mod_b)
    parts = [[mod[i, :, k * d:(k + 1) * d].reshape(nb, 1, d) for k in range(6)] for i in range(2)]
    sh1a, sc1a, g1a, sh2a, sc2a, g2a = parts[0]
    sh1b, sc1b, g1b, sh2b, sc2b, g2b = parts[1]

    u, moe_wg = _pw1(x.reshape(n, d), norm1_g[0], sc1a, sh1a, conv_w_pw1[0], conv_b_pw1[0], seq, moe_w_gate[0])
    h1, moe_wu = _conv_block(u.reshape(nb, seq, d), x, conv_w_dw[0], conv_b_dw[0], conv_ln_g[0], conv_ln_b[0],
                             conv_w_pw2[0], conv_b_pw2[0], g1a, moe_w_up[0])
    h2, u1, moe_wd = _dense_ffn(h1.reshape(n, d), norm2_g[0], sc2a, sh2a, g2a, ffn_w_gate[0], ffn_w_up[0],
                                ffn_w_down[0], norm1_g[1], sc1b, sh1b, seq, moe_w_down[0])

    tables = _s5_tables(ssm_a_re[0], ssm_a_im[0], ssm_log_dt[0], ssm_b_re[0], ssm_b_im[0],
                        ssm_c_re[0], ssm_c_im[0], ssm_d[0])
    y1 = _s5_mix(u1, tables, nb)
    h3, t8, rt = _glu_router(y1, ssm_w_glu[0], ssm_b_glu[0], h2, g1b, norm2_g[1], sc2b, sh2b,
                             moe_w_router[0], moe_b_router[0], seq)
    tm = 512
    dest, src_tok, tile_expert, tile_valid = _route(rt, n_experts, tm)
    ys8 = _experts(t8, src_tok, tile_expert, tile_valid, moe_wg, moe_wu, moe_wd, tm)
    out = _combine(ys8, dest, h3, rt, g2b, final_norm_g, seq)
    return out.reshape(nb, seq, d)
```

```python
import functools

import jax
import jax.numpy as jnp
from jax import lax
from jax.experimental import pallas as pl
from jax.experimental.pallas import tpu as pltpu

F32 = jnp.float32
BF16 = jnp.bfloat16
HIGHEST = lax.Precision.HIGHEST

NORM_EPS = 1e-6
TOP_K = 2
LANES = 128
S5_CHUNK = 16
S5_SEQ_PER_STEP = 2
CONV_HALO = 32
CONV_ROWS = 64
TOKEN_ROWS = 8
DMA_UNROLL = 8
VMEM_LIMIT = 56 * 2**20


def _params(sem):
    return pltpu.CompilerParams(dimension_semantics=sem, vmem_limit_bytes=VMEM_LIMIT)


def _divisor(n, cap, mult):
    best = None
    for d in range(mult, min(n, cap) + 1, mult):
        if n % d == 0:
            best = d
    assert best is not None, (n, cap, mult)
    return best


def _resident(shape):
    nd = len(shape)
    return pl.BlockSpec(shape, lambda *_: (0,) * nd, pipeline_mode=pl.Buffered(1))


def _rms_mod(x, g, sc, sh):
    ms = jnp.mean(x * x, axis=-1, keepdims=True)
    return (x * lax.rsqrt(ms + NORM_EPS) * g) * (1.0 + sc) + sh


def _silu(x):
    return x * jax.nn.sigmoid(x)


def _gelu_tanh(x):
    c = 0.7978845608028654
    return 0.5 * x * (1.0 + jnp.tanh(c * (x + 0.044715 * (x * x * x))))


def _split_bf16(x):
    hi = x.astype(BF16)
    return hi, (x - hi.astype(F32)).astype(BF16)


def _mod_kernel(c_ref, w_ref, b_ref, o_ref):
    cond = _silu(c_ref[...])
    o_ref[0] = jnp.dot(cond, w_ref[0], preferred_element_type=F32, precision=HIGHEST) + b_ref[0]


def _modulation(c, mod_w, mod_b):
    depth, d, d6 = mod_w.shape
    nb = c.shape[0]
    rows = -(-nb // 8) * 8
    c8 = jnp.pad(c, ((0, rows - nb), (0, 0)))
    tn = _divisor(d6, 1536, LANES)
    out = pl.pallas_call(
        _mod_kernel,
        grid=(depth, d6 // tn),
        in_specs=[
            pl.BlockSpec((rows, d), lambda i, j: (0, 0)),
            pl.BlockSpec((1, d, tn), lambda i, j: (i, 0, j)),
            pl.BlockSpec((1, 1, tn), lambda i, j: (i, 0, j)),
        ],
        out_specs=pl.BlockSpec((1, rows, tn), lambda i, j: (i, 0, j)),
        out_shape=jax.ShapeDtypeStruct((depth, rows, d6), F32),
        compiler_params=_params(("arbitrary", "arbitrary")),
        name="adaln_mod",
    )(c8, mod_w, mod_b.reshape(depth, 1, d6))
    return out[:, :nb, :]


def _cast_rider(w, nsteps, imap):
    w2 = w.reshape(-1, w.shape[-1])
    rows, cols = w2.shape
    assert rows % (16 * nsteps) == 0, (rows, nsteps)
    return w2, pl.BlockSpec((rows // nsteps, cols), imap), jax.ShapeDtypeStruct((rows, cols), BF16)


def _pw1_kernel(x_ref, g_ref, sc_ref, sh_ref, w_ref, b_ref, cw_ref, o_ref, co_ref):
    x = x_ref[...]
    d = x.shape[1]
    y = _rms_mod(x, g_ref[...], sc_ref[0], sh_ref[0])
    u = jnp.dot(y.astype(BF16), w_ref[...], preferred_element_type=F32) + b_ref[...]
    o_ref[...] = u[:, :d] * jax.nn.sigmoid(u[:, d:])
    co_ref[...] = cw_ref[...].astype(BF16)


def _pw1(x2, norm_g, sc, sh, w, b, seq, rider):
    n, d = x2.shape
    tm = _divisor(seq, 512, 8)
    tpb = seq // tm
    bmap = lambda i: (i // tpb, 0, 0)
    rider2, rider_spec, rider_shape = _cast_rider(rider, n // tm, lambda i: (i, 0))
    out, casted = pl.pallas_call(
        _pw1_kernel,
        grid=(n // tm,),
        in_specs=[
            pl.BlockSpec((tm, d), lambda i: (i, 0)),
            _resident((1, d)),
            pl.BlockSpec((1, 1, d), bmap),
            pl.BlockSpec((1, 1, d), bmap),
            _resident((d, 2 * d)),
            _resident((1, 2 * d)),
            rider_spec,
        ],
        out_specs=[pl.BlockSpec((tm, d), lambda i: (i, 0)), rider_spec],
        out_shape=[jax.ShapeDtypeStruct((n, d), F32), rider_shape],
        compiler_params=_params(("arbitrary",)),
        name="conv_pw1_glu",
    )(x2, norm_g.reshape(1, d), sc, sh, w.astype(BF16), b.reshape(1, 2 * d), rider2)
    return out, casted.reshape(rider.shape)


def _conv_kernel(cur_ref, prev_ref, wdw_ref, bdw_ref, lng_ref, lnb_ref, w2_ref, b2_ref, h_ref, g1_ref, cw_ref,
                 o_ref, co_ref, buf_ref, cv_ref, *, taps):
    tm, d = cv_ref.shape
    i = pl.program_id(1)
    co_ref[...] = cw_ref[...].astype(BF16)
    buf_ref[0:CONV_HALO, :] = jnp.where(i > 0, prev_ref[0], 0.0)
    buf_ref[CONV_HALO:, :] = cur_ref[0]
    off0 = CONV_HALO - (taps - 1)
    span = CONV_ROWS + CONV_HALO
    for c in range(d // LANES):
        lanes = slice(c * LANES, (c + 1) * LANES)

        def body(k, carry, lanes=lanes):
            r0 = pl.multiple_of(k * CONV_ROWS, CONV_ROWS)
            v = buf_ref[pl.ds(r0, span), lanes]
            acc = jnp.broadcast_to(bdw_ref[:, lanes], (CONV_ROWS, LANES))
            for s in range(8):
                xs = v if s == 0 else pltpu.roll(v, span - s, axis=0)
                for q in range(span // 8):
                    o = 8 * q + s
                    if o < off0 or o > off0 + taps - 1:
                        continue
                    wrow = wdw_ref[o - off0:o - off0 + 1, lanes]
                    acc = acc + wrow * xs[8 * q:8 * q + CONV_ROWS, :]
            cv_ref[pl.ds(r0, CONV_ROWS), lanes] = acc
            return carry

        lax.fori_loop(0, tm // CONV_ROWS, body, 0)
    v = cv_ref[...]
    mu = jnp.mean(v, axis=-1, keepdims=True)
    xc = v - mu
    var = jnp.mean(xc * xc, axis=-1, keepdims=True)
    y = _silu(xc * lax.rsqrt(var + NORM_EPS) * lng_ref[...] + lnb_ref[...])
    z = jnp.dot(y.astype(BF16), w2_ref[...], preferred_element_type=F32) + b2_ref[...]
    o_ref[0] = h_ref[0] + g1_ref[0] * z


def _conv_block(u3, x3, w_dw, b_dw, ln_g, ln_b, w2, b2, g1, rider):
    nb, seq, d = x3.shape
    taps = w_dw.shape[0]
    assert taps - 1 <= CONV_HALO and d % LANES == 0
    tm = _divisor(seq, 512, CONV_ROWS)
    hb = tm // CONV_HALO
    tps = seq // tm
    wpad = jnp.pad(w_dw, ((0, -taps % 8), (0, 0)))
    row = lambda a: a.reshape(1, d)
    rider2, rider_spec, rider_shape = _cast_rider(rider, nb * tps, lambda b, i: (b * tps + i, 0))
    out, casted = pl.pallas_call(
        functools.partial(_conv_kernel, taps=taps),
        grid=(nb, seq // tm),
        in_specs=[
            pl.BlockSpec((1, tm, d), lambda b, i: (b, i, 0)),
            pl.BlockSpec((1, CONV_HALO, d), lambda b, i: (b, jnp.maximum(i * hb - 1, 0), 0)),
            _resident(wpad.shape),
            _resident((1, d)),
            _resident((1, d)),
            _resident((1, d)),
            _resident((d, d)),
            _resident((1, d)),
            pl.BlockSpec((1, tm, d), lambda b, i: (b, i, 0)),
            pl.BlockSpec((1, 1, d), lambda b, i: (b, 0, 0)),
            rider_spec,
        ],
        out_specs=[pl.BlockSpec((1, tm, d), lambda b, i: (b, i, 0)), rider_spec],
        out_shape=[jax.ShapeDtypeStruct((nb, seq, d), F32), rider_shape],
        scratch_shapes=[pltpu.VMEM((tm + CONV_HALO, d), F32), pltpu.VMEM((tm, d), F32)],
        compiler_params=_params(("arbitrary", "arbitrary")),
        name="conv_dw_ln_pw2",
    )(u3, u3, wpad, row(b_dw), row(ln_g), row(ln_b), w2.astype(BF16), row(b2), x3, g1, rider2)
    return out, casted.reshape(rider.shape)


def _ffn_kernel(h_ref, n2g_ref, sc_ref, sh_ref, g2_ref, wg_ref, wu_ref, wd_ref, n1g_ref, sc1_ref, sh1_ref, cw_ref,
                h2_ref, u_ref, co_ref, *, fchunk):
    co_ref[...] = cw_ref[...].astype(BF16)
    h = h_ref[...]
    t = _rms_mod(h, n2g_ref[...], sc_ref[0], sh_ref[0]).astype(BF16)
    dff = wg_ref.shape[1]
    acc = None
    for f0 in range(0, dff, fchunk):
        f1 = min(f0 + fchunk, dff)
        g = jnp.dot(t, wg_ref[:, f0:f1], preferred_element_type=F32)
        u = jnp.dot(t, wu_ref[:, f0:f1], preferred_element_type=F32)
        a = (_silu(g) * u).astype(BF16)
        y = jnp.dot(a, wd_ref[f0:f1, :], preferred_element_type=F32)
        acc = y if acc is None else acc + y
    h2 = h + g2_ref[0] * acc
    h2_ref[...] = h2
    u = _rms_mod(h2, n1g_ref[...], sc1_ref[0], sh1_ref[0])
    for g in range(u_ref.shape[0]):
        u_ref[g] = u[:, g * LANES:(g + 1) * LANES]


def _dense_ffn(h, n2g, sc2, sh2, g2, wg, wu, wd, n1g, sc1, sh1, seq, rider):
    n, d = h.shape
    nblk = d // LANES
    dff = wg.shape[1]
    tm = _divisor(seq, 512, 8)
    tpb = seq // tm
    bmap = lambda i: (i // tpb, 0, 0)
    tile = pl.BlockSpec((tm, d), lambda i: (i, 0))
    rider2, rider_spec, rider_shape = _cast_rider(rider, n // tm, lambda i: (i, 0))
    h2, u, casted = pl.pallas_call(
        functools.partial(_ffn_kernel, fchunk=512),
        grid=(n // tm,),
        in_specs=[
            tile,
            _resident((1, d)),
            pl.BlockSpec((1, 1, d), bmap),
            pl.BlockSpec((1, 1, d), bmap),
            pl.BlockSpec((1, 1, d), bmap),
            _resident((d, dff)),
            _resident((d, dff)),
            _resident((dff, d)),
            _resident((1, d)),
            pl.BlockSpec((1, 1, d), bmap),
            pl.BlockSpec((1, 1, d), bmap),
            rider_spec,
        ],
        out_specs=[tile, pl.BlockSpec((nblk, tm, LANES), lambda i: (0, i, 0)), rider_spec],
        out_shape=[jax.ShapeDtypeStruct((n, d), F32), jax.ShapeDtypeStruct((nblk, n, LANES), F32), rider_shape],
        compiler_params=_params(("arbitrary",)),
        name="dense_swiglu",
    )(h, n2g.reshape(1, d), sc2, sh2, g2, wg.astype(BF16), wu.astype(BF16), wd.astype(BF16),
      n1g.reshape(1, d), sc1, sh1, rider2)
    return h2, u, casted.reshape(rider.shape)


def _s5_tables(a_re, a_im, log_dt, b_re, b_im, c_re, c_im, d_skip):
    ng, p = a_re.shape
    c = b_re.shape[-1]
    gpb = LANES // c
    nblk = ng // gpb
    L = S5_CHUNK
    eye = jnp.eye(gpb, dtype=F32)
    blk = lambda a: a.reshape((nblk, gpb) + a.shape[1:])
    dt = jnp.exp(log_dt)[:, None]
    mag = jnp.exp(dt * a_re)
    ab_re = mag * jnp.cos(dt * a_im)
    ab_im = mag * jnp.sin(dt * a_im)
    den = a_re * a_re + a_im * a_im
    f_re = ((ab_re - 1.0) * a_re + ab_im * a_im) / den
    f_im = (ab_im * a_re - (ab_re - 1.0) * a_im) / den
    bb_re = f_re[..., None] * b_re - f_im[..., None] * b_im
    bb_im = f_re[..., None] * b_im + f_im[..., None] * b_re
    jj = jnp.arange(L + 1, dtype=F32)[None, :, None]
    ang = blk(dt * a_im).reshape(nblk, 1, gpb * p) * jj
    pmag = jnp.exp(blk(dt * a_re).reshape(nblk, 1, gpb * p) * jj)
    pw_re = pmag * jnp.cos(ang)
    pw_im = pmag * jnp.sin(ang)
    bbx_re = jnp.einsum("bgpc,gh->bgchp", blk(bb_re), eye).reshape(nblk, LANES, gpb * p)
    bbx_im = jnp.einsum("bgpc,gh->bgchp", blk(bb_im), eye).reshape(nblk, LANES, gpb * p)
    ctx_re = jnp.einsum("bgop,gh->bgohp", blk(c_re), eye).reshape(nblk, LANES, gpb * p)
    ctx_im = jnp.einsum("bgop,gh->bgohp", blk(c_im), eye).reshape(nblk, LANES, gpb * p)
    ccm = jnp.einsum("rbgop,gh->brgpho", jnp.stack([blk(c_re), -blk(c_im)]), eye).reshape(nblk, 2 * gpb * p, LANES)
    d_blk = jnp.tile(d_skip.reshape(nblk, LANES), (1, L)).reshape(nblk, 1, L * LANES)
    return bbx_re, bbx_im, ctx_re, ctx_im, ccm, pw_re, pw_im, d_blk


def _s5_kernel(u_ref, bbr_ref, bbi_ref, ctr_ref, cti_ref, ccm_ref, pwr_ref, pwi_ref, d_ref, o_ref,
               t_ref, we_ref, vt_ref, e_ref, hin_ref, *, nseq):
    L = S5_CHUNK
    ew = e_ref.shape[1]
    rows = e_ref.shape[0] // nseq
    seq = rows * L
    ph = ew // 2

    @pl.when(pl.program_id(1) == 0)
    def _build_operands():
        t_ref[...] = jnp.zeros_like(t_ref)
        bbr, bbi = bbr_ref[0], bbi_ref[0]
        ctr, cti = ctr_ref[0], cti_ref[0]
        cc_hi, cc_lo = _split_bf16(ccm_ref[0])
        for lp in range(L):
            j = L - 1 - lp
            p_re, p_im = pwr_ref[0, j:j + 1, :], pwi_ref[0, j:j + 1, :]
            w = jnp.concatenate([bbr * p_re - bbi * p_im, bbr * p_im + bbi * p_re], axis=1)
            w_hi, w_lo = _split_bf16(w)
            we_ref[lp * LANES:(lp + 1) * LANES, :] = w_hi
            k = (jnp.dot(w_hi, cc_hi, preferred_element_type=F32)
                 + jnp.dot(w_lo, cc_hi, preferred_element_type=F32)
                 + jnp.dot(w_hi, cc_lo, preferred_element_type=F32)).astype(BF16)
            for l1 in range(L - j):
                t_ref[l1 * LANES:(l1 + 1) * LANES, (l1 + j) * LANES:(l1 + j + 1) * LANES] = k
        for l in range(L):
            p_re, p_im = pwr_ref[0, l + 1:l + 2, :], pwi_ref[0, l + 1:l + 2, :]
            v = jnp.concatenate([ctr * p_re - cti * p_im, -(ctr * p_im + cti * p_re)], axis=1)
            vt_ref[l * LANES:(l + 1) * LANES, :] = v.astype(BF16)

    xcat = jnp.concatenate(
        [jnp.concatenate([u_ref[0, pl.ds(q * seq + l, rows, stride=L), :] for l in range(L)], axis=1)
         for q in range(nseq)], axis=0)
    xb = xcat.astype(BF16)
    e_ref[...] = jnp.dot(xb, we_ref[...], preferred_element_type=F32)
    ar = pwr_ref[0, L:L + 1, :]
    ai = pwi_ref[0, L:L + 1, :]

    def body(r, carry):
        out = []
        for q in range(nseq):
            hr, hi = carry[2 * q], carry[2 * q + 1]
            hin_ref[pl.ds(q * rows + r, 1), 0:ph] = hr
            hin_ref[pl.ds(q * rows + r, 1), ph:ew] = hi
            er = e_ref[pl.ds(q * rows + r, 1), 0:ph]
            ei = e_ref[pl.ds(q * rows + r, 1), ph:ew]
            out += [ar * hr - ai * hi + er, ar * hi + ai * hr + ei]
        return tuple(out)

    zero = jnp.zeros((1, ph), F32)
    lax.fori_loop(0, rows, body, (zero,) * (2 * nseq))
    tw = 2 * LANES
    y = jnp.concatenate([jnp.dot(xb[:, :c0 + tw], t_ref[0:c0 + tw, c0:c0 + tw], preferred_element_type=F32)
                         for c0 in range(0, L * LANES, tw)], axis=1)
    y = y + lax.dot_general(hin_ref[...].astype(BF16), vt_ref[...], (((1,), (1,)), ((), ())),
                            preferred_element_type=F32)
    y = _gelu_tanh(y + d_ref[0] * xcat)
    for q in range(nseq):
        for l in range(L):
            o_ref[0, pl.ds(q * seq + l, rows, stride=L), :] = y[q * rows:(q + 1) * rows, l * LANES:(l + 1) * LANES]


def _s5_mix(u, tables, nbatch):
    nblk, n, _ = u.shape
    L = S5_CHUNK
    bbx_re, bbx_im, ctx_re, ctx_im, ccm, pw_re, pw_im, d_blk = tables
    sw = bbx_re.shape[2]
    seq = n // nbatch
    nseq = S5_SEQ_PER_STEP if nbatch % S5_SEQ_PER_STEP == 0 else 1
    rows = nseq * (seq // L)
    slab = lambda a: pl.BlockSpec((1,) + a.shape[1:], lambda g, b: (g, 0, 0))
    return pl.pallas_call(
        functools.partial(_s5_kernel, nseq=nseq),
        grid=(nblk, nbatch // nseq),
        in_specs=[pl.BlockSpec((1, nseq * seq, LANES), lambda g, b: (g, b, 0))] + [slab(a) for a in tables],
        out_specs=pl.BlockSpec((1, nseq * seq, LANES), lambda g, b: (g, b, 0)),
        out_shape=jax.ShapeDtypeStruct((nblk, n, LANES), F32),
        scratch_shapes=[
            pltpu.VMEM((L * LANES, L * LANES), BF16),
            pltpu.VMEM((L * LANES, 2 * sw), BF16),
            pltpu.VMEM((L * LANES, 2 * sw), BF16),
            pltpu.VMEM((rows, 2 * sw), F32),
            pltpu.VMEM((rows, 2 * sw), F32),
        ],
        compiler_params=_params(("arbitrary", "arbitrary")),
        name="s5_chunked_scan",
    )(u, *tables)


def _glu_kernel(y_ref, w_ref, b_ref, h_ref, g1_ref, n2g_ref, sc_ref, sh_ref, wrh_ref, wrl_ref, br_ref,
                h3_ref, t_ref, rt_ref):
    d = h_ref.shape[1]
    y = jnp.concatenate([y_ref[g] for g in range(y_ref.shape[0])], axis=1)
    z = jnp.dot(y.astype(BF16), w_ref[...], preferred_element_type=F32) + b_ref[...]
    h3 = h_ref[...] + g1_ref[0] * (z[:, :d] * jax.nn.sigmoid(z[:, d:]))
    h3_ref[...] = h3
    t = _rms_mod(h3, n2g_ref[...], sc_ref[0], sh_ref[0])
    rows = t.shape[0]
    for j in range(TOKEN_ROWS):
        t_ref[pl.ds(j, rows, stride=TOKEN_ROWS), :] = t[:, j * LANES:(j + 1) * LANES]
    t_hi, t_lo = _split_bf16(t)
    logits = (jnp.dot(t_hi, wrh_ref[...], preferred_element_type=F32)
              + jnp.dot(t_lo, wrh_ref[...], preferred_element_type=F32)
              + jnp.dot(t_hi, wrl_ref[...], preferred_element_type=F32)) + br_ref[...]
    lane = lax.broadcasted_iota(jnp.int32, logits.shape, 1).astype(F32)

    def top1(v):
        m = jnp.max(v, axis=-1, keepdims=True)
        return m, jnp.min(jnp.where(v == m, lane, float(LANES)), axis=-1, keepdims=True)

    m1, i1 = top1(logits)
    m2, i2 = top1(jnp.where(lane == i1, -jnp.inf, logits))
    e2 = jnp.exp(m2 - m1)
    den = 1.0 + e2
    rt_ref[...] = jnp.where(lane == 0.0, 1.0 / den,
                            jnp.where(lane == 1.0, e2 / den,
                                      jnp.where(lane == 2.0, i1, jnp.where(lane == 3.0, i2, 0.0))))


def _glu_router(y, w_glu, b_glu, h, g1, n2g, sc2, sh2, w_router, b_router, seq):
    n, d = h.shape
    ne = w_router.shape[1]
    assert ne <= LANES
    tm = _divisor(seq, 512, 8)
    tpb = seq // tm
    bmap = lambda i: (i // tpb, 0, 0)
    tile = pl.BlockSpec((tm, d), lambda i: (i, 0))
    wr_hi, wr_lo = _split_bf16(jnp.pad(w_router, ((0, 0), (0, LANES - ne))))
    br = jnp.pad(b_router, (0, LANES - ne), constant_values=-1e30).reshape(1, LANES)
    return pl.pallas_call(
        _glu_kernel,
        grid=(n // tm,),
        in_specs=[
            pl.BlockSpec((d // LANES, tm, LANES), lambda i: (0, i, 0)),
            _resident((d, 2 * d)),
            _resident((1, 2 * d)),
            tile,
            pl.BlockSpec((1, 1, d), bmap),
            _resident((1, d)),
            pl.BlockSpec((1, 1, d), bmap),
            pl.BlockSpec((1, 1, d), bmap),
            _resident((d, LANES)),
            _resident((d, LANES)),
            _resident((1, LANES)),
        ],
        out_specs=[tile, pl.BlockSpec((tm * TOKEN_ROWS, LANES), lambda i: (i, 0)),
                   pl.BlockSpec((tm, LANES), lambda i: (i, 0))],
        out_shape=[
            jax.ShapeDtypeStruct((n, d), F32),
            jax.ShapeDtypeStruct((n * TOKEN_ROWS, LANES), F32),
            jax.ShapeDtypeStruct((n, LANES), F32),
        ],
        compiler_params=_params(("arbitrary",)),
        name="s5_glu_router",
    )(y, w_glu.astype(BF16), b_glu.reshape(1, 2 * d), h, g1, n2g.reshape(1, d), sc2, sh2,
      wr_hi, wr_lo, br)


def _token_copy(src_ref, src_tok, dst_ref, dst_tok, sem):
    s = pl.multiple_of(src_tok * TOKEN_ROWS, TOKEN_ROWS)
    t = pl.multiple_of(dst_tok * TOKEN_ROWS, TOKEN_ROWS)
    return pltpu.make_async_copy(src_ref.at[pl.ds(s, TOKEN_ROWS), :], dst_ref.at[pl.ds(t, TOKEN_ROWS), :], sem)


def _tokens_wait(src_ref, dst_ref, dst_tok, ntok, sem):
    t = pl.multiple_of(dst_tok * TOKEN_ROWS, TOKEN_ROWS)
    pltpu.make_async_copy(src_ref.at[pl.ds(0, ntok * TOKEN_ROWS), :],
                          dst_ref.at[pl.ds(t, ntok * TOKEN_ROWS), :], sem).wait()


def _expert_kernel(te_ref, tv_ref, nxt_ref, first_ref, t8_ref, wg_ref, wu_ref, wd_ref, o_ref,
                   xbuf_ref, xb_ref, acc_ref, sems, *, fchunk, nf):
    del te_ref
    i = pl.program_id(0)
    f = pl.program_id(1)
    nt = pl.num_programs(0)
    tm = xb_ref.shape[0]
    slot = lax.rem(i, 2)

    def gather(idx_ref, into):
        def body(j, c):
            for u in range(DMA_UNROLL):
                r = j * DMA_UNROLL + u
                _token_copy(t8_ref, idx_ref[0, 0, r], xbuf_ref, into * tm + r, sems.at[into]).start(priority=u % 2)
            return c
        lax.fori_loop(0, tm // DMA_UNROLL, body, 0)

    @pl.when(jnp.logical_and(i == 0, f == 0))
    def _():
        gather(first_ref, 0)

    @pl.when(jnp.logical_and(f == 0, jnp.logical_and(i + 1 < nt, tv_ref[jnp.minimum(i + 1, nt - 1)] == 1)))
    def _():
        gather(nxt_ref, 1 - slot)

    @pl.when(jnp.logical_and(tv_ref[i] == 0, f == nf - 1))
    def _():
        o_ref[...] = jnp.zeros_like(o_ref)

    @pl.when(tv_ref[i] == 1)
    def _():
        @pl.when(f == 0)
        def _():
            _tokens_wait(t8_ref, xbuf_ref, slot * tm, tm, sems.at[slot])
            base = pl.multiple_of(slot * (tm * TOKEN_ROWS), TOKEN_ROWS)
            for j in range(TOKEN_ROWS):
                xb_ref[:, j * LANES:(j + 1) * LANES] = (
                    xbuf_ref[pl.ds(base + j, tm, stride=TOKEN_ROWS), :].astype(BF16))

        x = xb_ref[...]
        tf = wg_ref.shape[2]
        y = None
        for f0 in range(0, tf, fchunk):
            f1 = min(f0 + fchunk, tf)
            g = jnp.dot(x, wg_ref[0, :, f0:f1], preferred_element_type=F32)
            u = jnp.dot(x, wu_ref[0, :, f0:f1], preferred_element_type=F32)
            a = (_silu(g) * u).astype(BF16)
            yy = jnp.dot(a, wd_ref[0, f0:f1, :], preferred_element_type=F32)
            y = yy if y is None else y + yy

        def emit(total):
            for j in range(TOKEN_ROWS):
                o_ref[pl.ds(j, tm, stride=TOKEN_ROWS), :] = total[:, j * LANES:(j + 1) * LANES]

        if nf == 1:
            emit(y)
        else:
            @pl.when(f == 0)
            def _():
                acc_ref[...] = y

            @pl.when(jnp.logical_and(f > 0, f < nf - 1))
            def _():
                acc_ref[...] += y

            @pl.when(f == nf - 1)
            def _():
                emit(acc_ref[...] + y)


def _experts(t8, src_tok, tile_expert, tile_valid, wg, wu, wd, tm):
    n_tiles = src_tok.shape[0]
    d = TOKEN_ROWS * LANES
    dff = wg.shape[2]
    tf = _divisor(dff, 1792, 256)
    nf = dff // tf
    assert tm % (nf * DMA_UNROLL) == 0
    fidx = lambda i, f, tv: f * tv[i] + (nf - 1) * (1 - tv[i])
    smem_tile = lambda imap: pl.BlockSpec((1, 1, tm), imap, memory_space=pltpu.SMEM)
    grid_spec = pltpu.PrefetchScalarGridSpec(
        num_scalar_prefetch=2,
        grid=(n_tiles, nf),
        in_specs=[
            smem_tile(lambda i, f, te, tv: (jnp.minimum(i + 1, n_tiles - 1), 0, 0)),
            smem_tile(lambda i, f, te, tv: (0, 0, 0)),
            pl.BlockSpec(memory_space=pl.ANY),
            pl.BlockSpec((1, d, tf), lambda i, f, te, tv: (te[i], 0, fidx(i, f, tv))),
            pl.BlockSpec((1, d, tf), lambda i, f, te, tv: (te[i], 0, fidx(i, f, tv))),
            pl.BlockSpec((1, tf, d), lambda i, f, te, tv: (te[i], fidx(i, f, tv), 0)),
        ],
        out_specs=pl.BlockSpec((tm * TOKEN_ROWS, LANES), lambda i, f, te, tv: (i, 0)),
        scratch_shapes=[
            pltpu.VMEM((2 * tm * TOKEN_ROWS, LANES), F32),
            pltpu.VMEM((tm, d), BF16),
            pltpu.VMEM((tm, d), F32),
            pltpu.SemaphoreType.DMA((2,)),
        ],
    )
    return pl.pallas_call(
        functools.partial(_expert_kernel, fchunk=1024, nf=nf),
        grid_spec=grid_spec,
        out_shape=jax.ShapeDtypeStruct((n_tiles * tm * TOKEN_ROWS, LANES), F32),
        compiler_params=_params(("arbitrary", "arbitrary")),
        name="moe_experts",
    )(tile_expert, tile_valid, src_tok, src_tok, t8, wg, wu, wd)


def _combine_kernel(nxt_ref, first_ref, ys_ref, h_ref, rt_ref, g2_ref, fg_ref, o_ref, ybuf_ref, sems):
    i = pl.program_id(0)
    ns = pl.num_programs(0)
    tc = h_ref.shape[0]
    npair = TOP_K * tc
    slot = lax.rem(i, 2)

    def gather(idx_ref, into):
        def body(j, c):
            for u in range(DMA_UNROLL):
                r = j * DMA_UNROLL + u
                for k in range(TOP_K):
                    _token_copy(ys_ref, idx_ref[0, 0, TOP_K * r + k], ybuf_ref, into * npair + k * tc + r,
                                sems.at[into]).start(priority=k % 2)
            return c
        lax.fori_loop(0, tc // DMA_UNROLL, body, 0)

    @pl.when(i == 0)
    def _():
        gather(first_ref, 0)

    @pl.when(i + 1 < ns)
    def _():
        gather(nxt_ref, 1 - slot)

    _tokens_wait(ys_ref, ybuf_ref, slot * npair, npair, sems.at[slot])
    rt = rt_ref[...]
    base = pl.multiple_of(slot * (npair * TOKEN_ROWS), TOKEN_ROWS)
    pieces = []
    for j in range(TOKEN_ROWS):
        yj = None
        for k in range(TOP_K):
            v = ybuf_ref[pl.ds(base + k * tc * TOKEN_ROWS + j, tc, stride=TOKEN_ROWS), :]
            yj = rt[:, k:k + 1] * v if yj is None else yj + rt[:, k:k + 1] * v
        pieces.append(yj)
    y = jnp.concatenate(pieces, axis=1)
    h4 = h_ref[...] + g2_ref[0] * y
    ms = jnp.mean(h4 * h4, axis=-1, keepdims=True)
    o_ref[...] = h4 * lax.rsqrt(ms + NORM_EPS) * fg_ref[...]


def _combine(ys8, dest, h2d, rt, g2, final_g, seq):
    n, d = h2d.shape
    tc = _divisor(seq, 512, DMA_UNROLL)
    tpb = seq // tc
    ns = n // tc
    dest3 = dest.reshape(ns, 1, TOP_K * tc)
    smem_tile = lambda imap: pl.BlockSpec((1, 1, TOP_K * tc), imap, memory_space=pltpu.SMEM)
    return pl.pallas_call(
        _combine_kernel,
        grid=(ns,),
        in_specs=[
            smem_tile(lambda i: (jnp.minimum(i + 1, ns - 1), 0, 0)),
            smem_tile(lambda i: (0, 0, 0)),
            pl.BlockSpec(memory_space=pl.ANY),
            pl.BlockSpec((tc, d), lambda i: (i, 0)),
            pl.BlockSpec((tc, LANES), lambda i: (i, 0)),
            pl.BlockSpec((1, 1, d), lambda i: (i // tpb, 0, 0)),
            _resident((1, d)),
        ],
        out_specs=pl.BlockSpec((tc, d), lambda i: (i, 0)),
        out_shape=jax.ShapeDtypeStruct((n, d), F32),
        scratch_shapes=[pltpu.VMEM((2 * TOP_K * tc * TOKEN_ROWS, LANES), F32), pltpu.SemaphoreType.DMA((2,))],
        compiler_params=_params(("arbitrary",)),
        name="moe_combine_norm",
    )(dest3, dest3, ys8, h2d, rt, g2, final_g.reshape(1, d))


def _route(rt, n_experts, tm):
    n = rt.shape[0]
    npairs = n * TOP_K
    pair_expert = rt[:, 2:2 + TOP_K].astype(jnp.int32).reshape(npairs)
    onehot = (pair_expert[:, None] == jnp.arange(n_experts, dtype=jnp.int32)[None, :]).astype(jnp.int32)
    csum = jnp.cumsum(onehot, axis=0)
    rank = jnp.sum(onehot * csum, axis=1) - 1
    counts = csum[-1]
    tiles = (counts + tm - 1) // tm
    tile_end = jnp.cumsum(tiles)
    tile_start = tile_end - tiles
    dest = jnp.sum(onehot * tile_start[None, :], axis=1) * tm + rank
    n_tiles = npairs // tm + n_experts
    ti = jnp.arange(n_tiles, dtype=jnp.int32)
    n_active = tile_end[-1]
    tile_valid = (ti < n_active).astype(jnp.int32)
    last_used = jnp.minimum(ti, n_active - 1)
    expert_of = jnp.sum((last_used[:, None] >= tile_end[None, :]).astype(jnp.int32), axis=1)
    tile_expert = jnp.minimum(expert_of, n_experts - 1)
    order_tok = (jnp.sort(pair_expert * npairs + jnp.arange(npairs, dtype=jnp.int32)) % npairs) // TOP_K
    nrows = n_tiles * tm
    count_start = jnp.cumsum(counts) - counts
    padded = jnp.concatenate([jnp.zeros((nrows,), jnp.int32), order_tok, jnp.zeros((nrows,), jnp.int32)])
    row = jnp.arange(nrows, dtype=jnp.int32)
    src_tok = jnp.zeros((nrows,), jnp.int32)
    for e in range(n_experts):
        first_row = tile_start[e] * tm
        shifted = lax.dynamic_slice(padded, (nrows + count_start[e] - first_row,), (nrows,))
        mine = jnp.logical_and(row >= first_row, row < first_row + counts[e])
        src_tok = jnp.where(mine, shifted, src_tok)
    return dest.astype(jnp.int32), src_tok.reshape(n_tiles, 1, tm), tile_expert, tile_valid


def kernel(x, c, mod_w, mod_b, norm1_g, norm2_g, conv_w_pw1, conv_b_pw1, conv_w_dw, conv_b_dw, conv_ln_g, conv_ln_b, conv_w_pw2, conv_b_pw2, ssm_a_re, ssm_a_im, ssm_log_dt, ssm_b_re, ssm_b_im, ssm_c_re, ssm_c_im, ssm_d, ssm_w_glu, ssm_b_glu, ffn_w_gate, ffn_w_up, ffn_w_down, moe_w_router, moe_b_router, moe_w_gate, moe_w_up, moe_w_down, final_norm_g):
    nb, seq, d = x.shape
    n = nb * seq
    assert mod_w.shape[0] == 2 and seq % S5_CHUNK == 0 and d == TOKEN_ROWS * LANES
    n_experts = moe_w_router.shape[-1]

    mod = _modulation(c, mod_w, mod_b)
    parts = [[mod[i, :, k * d:(k + 1) * d].reshape(nb, 1, d) for k in range(6)] for i in range(2)]
    sh1a, sc1a, g1a, sh2a, sc2a, g2a = parts[0]
    sh1b, sc1b, g1b, sh2b, sc2b, g2b = parts[1]

    u, moe_wg = _pw1(x.reshape(n, d), norm1_g[0], sc1a, sh1a, conv_w_pw1[0], conv_b_pw1[0], seq, moe_w_gate[0])
    h1, moe_wu = _conv_block(u.reshape(nb, seq, d), x, conv_w_dw[0], conv_b_dw[0], conv_ln_g[0], conv_ln_b[0],
                             conv_w_pw2[0], conv_b_pw2[0], g1a, moe_w_up[0])
    h2, u1, moe_wd = _dense_ffn(h1.reshape(n, d), norm2_g[0], sc2a, sh2a, g2a, ffn_w_gate[0], ffn_w_up[0],
                                ffn_w_down[0], norm1_g[1], sc1b, sh1b, seq, moe_w_down[0])

    tables = _s5_tables(ssm_a_re[0], ssm_a_im[0], ssm_log_dt[0], ssm_b_re[0], ssm_b_im[0],
                        ssm_c_re[0], ssm_c_im[0], ssm_d[0])
    y1 = _s5_mix(u1, tables, nb)
    h3, t8, rt = _glu_router(y1, ssm_w_glu[0], ssm_b_glu[0], h2, g1b, norm2_g[1], sc2b, sh2b,
                             moe_w_router[0], moe_b_router[0], seq)
    tm = 512
    dest, src_tok, tile_expert, tile_valid = _route(rt, n_experts, tm)
    ys8 = _experts(t8, src_tok, tile_expert, tile_valid, moe_wg, moe_wu, moe_wd, tm)
    out = _combine(ys8, dest, h3, rt, g2b, final_norm_g, seq)
    return out.reshape(nb, seq, d)
```

```python
import functools

import jax
import jax.numpy as jnp
from jax import lax
from jax.experimental import pallas as pl
from jax.experimental.pallas import tpu as pltpu

F32 = jnp.float32
BF16 = jnp.bfloat16
HIGHEST = lax.Precision.HIGHEST

NORM_EPS = 1e-6
TOP_K = 2
LANES = 128
S5_CHUNK = 16
S5_SEQ_PER_STEP = 2
CONV_HALO = 32
CONV_ROWS = 64
TOKEN_ROWS = 8
DMA_UNROLL = 8
VMEM_LIMIT = 56 * 2**20


def _params(sem):
    return pltpu.CompilerParams(dimension_semantics=sem, vmem_limit_bytes=VMEM_LIMIT)


def _divisor(n, cap, mult):
    best = None
    for d in range(mult, min(n, cap) + 1, mult):
        if n % d == 0:
            best = d
    assert best is not None, (n, cap, mult)
    return best


def _resident(shape):
    nd = len(shape)
    return pl.BlockSpec(shape, lambda *_: (0,) * nd, pipeline_mode=pl.Buffered(1))


def _rms_mod(x, g, sc, sh):
    ms = jnp.mean(x * x, axis=-1, keepdims=True)
    return (x * lax.rsqrt(ms + NORM_EPS) * g) * (1.0 + sc) + sh


def _silu(x):
    return x * jax.nn.sigmoid(x)


def _gelu_tanh(x):
    c = 0.7978845608028654
    return 0.5 * x * (1.0 + jnp.tanh(c * (x + 0.044715 * (x * x * x))))


def _split_bf16(x):
    hi = x.astype(BF16)
    return hi, (x - hi.astype(F32)).astype(BF16)


def _mod_kernel(c_ref, w_ref, b_ref, o_ref):
    cond = _silu(c_ref[...])
    o_ref[0] = jnp.dot(cond, w_ref[0], preferred_element_type=F32, precision=HIGHEST) + b_ref[0]


def _modulation(c, mod_w, mod_b):
    depth, d, d6 = mod_w.shape
    nb = c.shape[0]
    rows = -(-nb // 8) * 8
    c8 = jnp.pad(c, ((0, rows - nb), (0, 0)))
    tn = _divisor(d6, 1536, LANES)
    out = pl.pallas_call(
        _mod_kernel,
        grid=(depth, d6 // tn),
        in_specs=[
            pl.BlockSpec((rows, d), lambda i, j: (0, 0)),
            pl.BlockSpec((1, d, tn), lambda i, j: (i, 0, j)),
            pl.BlockSpec((1, 1, tn), lambda i, j: (i, 0, j)),
        ],
        out_specs=pl.BlockSpec((1, rows, tn), lambda i, j: (i, 0, j)),
        out_shape=jax.ShapeDtypeStruct((depth, rows, d6), F32),
        compiler_params=_params(("arbitrary", "arbitrary")),
        name="adaln_mod",
    )(c8, mod_w, mod_b.reshape(depth, 1, d6))
    return out[:, :nb, :]


def _cast_rider(w, nsteps, imap):
    w2 = w.reshape(-1, w.shape[-1])
    rows, cols = w2.shape
    assert rows % (16 * nsteps) == 0, (rows, nsteps)
    return w2, pl.BlockSpec((rows // nsteps, cols), imap), jax.ShapeDtypeStruct((rows, cols), BF16)


def _pw1_kernel(x_ref, g_ref, sc_ref, sh_ref, w_ref, b_ref, cw_ref, o_ref, co_ref):
    x = x_ref[...]
    d = x.shape[1]
    y = _rms_mod(x, g_ref[...], sc_ref[0], sh_ref[0])
    u = jnp.dot(y.astype(BF16), w_ref[...], preferred_element_type=F32) + b_ref[...]
    o_ref[...] = u[:, :d] * jax.nn.sigmoid(u[:, d:])
    co_ref[...] = cw_ref[...].astype(BF16)


def _pw1(x2, norm_g, sc, sh, w, b, seq, rider):
    n, d = x2.shape
    tm = _divisor(seq, 512, 8)
    tpb = seq // tm
    bmap = lambda i: (i // tpb, 0, 0)
    rider2, rider_spec, rider_shape = _cast_rider(rider, n // tm, lambda i: (i, 0))
    out, casted = pl.pallas_call(
        _pw1_kernel,
        grid=(n // tm,),
        in_specs=[
            pl.BlockSpec((tm, d), lambda i: (i, 0)),
            _resident((1, d)),
            pl.BlockSpec((1, 1, d), bmap),
            pl.BlockSpec((1, 1, d), bmap),
            _resident((d, 2 * d)),
            _resident((1, 2 * d)),
            rider_spec,
        ],
        out_specs=[pl.BlockSpec((tm, d), lambda i: (i, 0)), rider_spec],
        out_shape=[jax.ShapeDtypeStruct((n, d), F32), rider_shape],
        compiler_params=_params(("arbitrary",)),
        name="conv_pw1_glu",
    )(x2, norm_g.reshape(1, d), sc, sh, w.astype(BF16), b.reshape(1, 2 * d), rider2)
    return out, casted.reshape(rider.shape)


def _conv_kernel(cur_ref, prev_ref, wdw_ref, bdw_ref, lng_ref, lnb_ref, w2_ref, b2_ref, h_ref, g1_ref, cw_ref,
                 o_ref, co_ref, buf_ref, cv_ref, *, taps):
    tm, d = cv_ref.shape
    i = pl.program_id(1)
    co_ref[...] = cw_ref[...].astype(BF16)
    buf_ref[0:CONV_HALO, :] = jnp.where(i > 0, prev_ref[0], 0.0)
    buf_ref[CONV_HALO:, :] = cur_ref[0]
    off0 = CONV_HALO - (taps - 1)
    span = CONV_ROWS + CONV_HALO
    for c in range(d // LANES):
        lanes = slice(c * LANES, (c + 1) * LANES)

        def body(k, carry, lanes=lanes):
            r0 = pl.multiple_of(k * CONV_ROWS, CONV_ROWS)
            v = buf_ref[pl.ds(r0, span), lanes]
            acc = jnp.broadcast_to(bdw_ref[:, lanes], (CONV_ROWS, LANES))
            for s in range(8):
                xs = v if s == 0 else pltpu.roll(v, span - s, axis=0)
                for q in range(span // 8):
                    o = 8 * q + s
                    if o < off0 or o > off0 + taps - 1:
                        continue
                    wrow = wdw_ref[o - off0:o - off0 + 1, lanes]
                    acc = acc + wrow * xs[8 * q:8 * q + CONV_ROWS, :]
            cv_ref[pl.ds(r0, CONV_ROWS), lanes] = acc
            return carry

        lax.fori_loop(0, tm // CONV_ROWS, body, 0)
    v = cv_ref[...]
    mu = jnp.mean(v, axis=-1, keepdims=True)
    xc = v - mu
    var = jnp.mean(xc * xc, axis=-1, keepdims=True)
    y = _silu(xc * lax.rsqrt(var + NORM_EPS) * lng_ref[...] + lnb_ref[...])
    z = jnp.dot(y.astype(BF16), w2_ref[...], preferred_element_type=F32) + b2_ref[...]
    o_ref[0] = h_ref[0] + g1_ref[0] * z


def _conv_block(u3, x3, w_dw, b_dw, ln_g, ln_b, w2, b2, g1, rider):
    nb, seq, d = x3.shape
    taps = w_dw.shape[0]
    assert taps - 1 <= CONV_HALO and d % LANES == 0
    tm = _divisor(seq, 512, CONV_ROWS)
    hb = tm // CONV_HALO
    tps = seq // tm
    wpad = jnp.pad(w_dw, ((0, -taps % 8), (0, 0)))
    row = lambda a: a.reshape(1, d)
    rider2, rider_spec, rider_shape = _cast_rider(rider, nb * tps, lambda b, i: (b * tps + i, 0))
    out, casted = pl.pallas_call(
        functools.partial(_conv_kernel, taps=taps),
        grid=(nb, seq // tm),
        in_specs=[
            pl.BlockSpec((1, tm, d), lambda b, i: (b, i, 0)),
            pl.BlockSpec((1, CONV_HALO, d), lambda b, i: (b, jnp.maximum(i * hb - 1, 0), 0)),
            _resident(wpad.shape),
            _resident((1, d)),
            _resident((1, d)),
            _resident((1, d)),
            _resident((d, d)),
            _resident((1, d)),
            pl.BlockSpec((1, tm, d), lambda b, i: (b, i, 0)),
            pl.BlockSpec((1, 1, d), lambda b, i: (b, 0, 0)),
            rider_spec,
        ],
        out_specs=[pl.BlockSpec((1, tm, d), lambda b, i: (b, i, 0)), rider_spec],
        out_shape=[jax.ShapeDtypeStruct((nb, seq, d), F32), rider_shape],
        scratch_shapes=[pltpu.VMEM((tm + CONV_HALO, d), F32), pltpu.VMEM((tm, d), F32)],
        compiler_params=_params(("arbitrary", "arbitrary")),
        name="conv_dw_ln_pw2",
    )(u3, u3, wpad, row(b_dw), row(ln_g), row(ln_b), w2.astype(BF16), row(b2), x3, g1, rider2)
    return out, casted.reshape(rider.shape)


def _ffn_kernel(h_ref, n2g_ref, sc_ref, sh_ref, g2_ref, wg_ref, wu_ref, wd_ref, n1g_ref, sc1_ref, sh1_ref, cw_ref,
                h2_ref, u_ref, co_ref, *, fchunk):
    co_ref[...] = cw_ref[...].astype(BF16)
    h = h_ref[...]
    t = _rms_mod(h, n2g_ref[...], sc_ref[0], sh_ref[0]).astype(BF16)
    dff = wg_ref.shape[1]
    acc = None
    for f0 in range(0, dff, fchunk):
        f1 = min(f0 + fchunk, dff)
        g = jnp.dot(t, wg_ref[:, f0:f1], preferred_element_type=F32)
        u = jnp.dot(t, wu_ref[:, f0:f1], preferred_element_type=F32)
        a = (_silu(g) * u).astype(BF16)
        y = jnp.dot(a, wd_ref[f0:f1, :], preferred_element_type=F32)
        acc = y if acc is None else acc + y
    h2 = h + g2_ref[0] * acc
    h2_ref[...] = h2
    u = _rms_mod(h2, n1g_ref[...], sc1_ref[0], sh1_ref[0])
    for g in range(u_ref.shape[0]):
        u_ref[g] = u[:, g * LANES:(g + 1) * LANES]


def _dense_ffn(h, n2g, sc2, sh2, g2, wg, wu, wd, n1g, sc1, sh1, seq, rider):
    n, d = h.shape
    nblk = d // LANES
    dff = wg.shape[1]
    tm = _divisor(seq, 512, 8)
    tpb = seq // tm
    bmap = lambda i: (i // tpb, 0, 0)
    tile = pl.BlockSpec((tm, d), lambda i: (i, 0))
    rider2, rider_spec, rider_shape = _cast_rider(rider, n // tm, lambda i: (i, 0))
    h2, u, casted = pl.pallas_call(
        functools.partial(_ffn_kernel, fchunk=512),
        grid=(n // tm,),
        in_specs=[
            tile,
            _resident((1, d)),
            pl.BlockSpec((1, 1, d), bmap),
            pl.BlockSpec((1, 1, d), bmap),
            pl.BlockSpec((1, 1, d), bmap),
            _resident((d, dff)),
            _resident((d, dff)),
            _resident((dff, d)),
            _resident((1, d)),
            pl.BlockSpec((1, 1, d), bmap),
            pl.BlockSpec((1, 1, d), bmap),
            rider_spec,
        ],
        out_specs=[tile, pl.BlockSpec((nblk, tm, LANES), lambda i: (0, i, 0)), rider_spec],
        out_shape=[jax.ShapeDtypeStruct((n, d), F32), jax.ShapeDtypeStruct((nblk, n, LANES), F32), rider_shape],
        compiler_params=_params(("arbitrary",)),
        name="dense_swiglu",
    )(h, n2g.reshape(1, d), sc2, sh2, g2, wg.astype(BF16), wu.astype(BF16), wd.astype(BF16),
      n1g.reshape(1, d), sc1, sh1, rider2)
    return h2, u, casted.reshape(rider.shape)


def _s5_tables(a_re, a_im, log_dt, b_re, b_im, c_re, c_im, d_skip):
    ng, p = a_re.shape
    c = b_re.shape[-1]
    gpb = LANES // c
    nblk = ng // gpb
    L = S5_CHUNK
    eye = jnp.eye(gpb, dtype=F32)
    blk = lambda a: a.reshape((nblk, gpb) + a.shape[1:])
    dt = jnp.exp(log_dt)[:, None]
    mag = jnp.exp(dt * a_re)
    ab_re = mag * jnp.cos(dt * a_im)
    ab_im = mag * jnp.sin(dt * a_im)
    den = a_re * a_re + a_im * a_im
    f_re = ((ab_re - 1.0) * a_re + ab_im * a_im) / den
    f_im = (ab_im * a_re - (ab_re - 1.0) * a_im) / den
    bb_re = f_re[..., None] * b_re - f_im[..., None] * b_im
    bb_im = f_re[..., None] * b_im + f_im[..., None] * b_re
    jj = jnp.arange(L + 1, dtype=F32)[None, :, None]
    ang = blk(dt * a_im).reshape(nblk, 1, gpb * p) * jj
    pmag = jnp.exp(blk(dt * a_re).reshape(nblk, 1, gpb * p) * jj)
    pw_re = pmag * jnp.cos(ang)
    pw_im = pmag * jnp.sin(ang)
    bbx_re = jnp.einsum("bgpc,gh->bgchp", blk(bb_re), eye).reshape(nblk, LANES, gpb * p)
    bbx_im = jnp.einsum("bgpc,gh->bgchp", blk(bb_im), eye).reshape(nblk, LANES, gpb * p)
    ctx_re = jnp.einsum("bgop,gh->bgohp", blk(c_re), eye).reshape(nblk, LANES, gpb * p)
    ctx_im = jnp.einsum("bgop,gh->bgohp", blk(c_im), eye).reshape(nblk, LANES, gpb * p)
    ccm = jnp.einsum("rbgop,gh->brgpho", jnp.stack([blk(c_re), -blk(c_im)]), eye).reshape(nblk, 2 * gpb * p, LANES)
    d_blk = jnp.tile(d_skip.reshape(nblk, LANES), (1, L)).reshape(nblk, 1, L * LANES)
    return bbx_re, bbx_im, ctx_re, ctx_im, ccm, pw_re, pw_im, d_blk


def _s5_kernel(u_ref, bbr_ref, bbi_ref, ctr_ref, cti_ref, ccm_ref, pwr_ref, pwi_ref, d_ref, o_ref,
               t_ref, we_ref, vt_ref, e_ref, hin_ref, *, nseq):
    L = S5_CHUNK
    ew = e_ref.shape[1]
    rows = e_ref.shape[0] // nseq
    seq = rows * L
    ph = ew // 2

    @pl.when(pl.program_id(1) == 0)
    def _build_operands():
        t_ref[...] = jnp.zeros_like(t_ref)
        bbr, bbi = bbr_ref[0], bbi_ref[0]
        ctr, cti = ctr_ref[0], cti_ref[0]
        cc_hi, cc_lo = _split_bf16(ccm_ref[0])
        for lp in range(L):
            j = L - 1 - lp
            p_re, p_im = pwr_ref[0, j:j + 1, :], pwi_ref[0, j:j + 1, :]
            w = jnp.concatenate([bbr * p_re - bbi * p_im, bbr * p_im + bbi * p_re], axis=1)
            w_hi, w_lo = _split_bf16(w)
            we_ref[lp * LANES:(lp + 1) * LANES, :] = w_hi
            k = (jnp.dot(w_hi, cc_hi, preferred_element_type=F32)
                 + jnp.dot(w_lo, cc_hi, preferred_element_type=F32)
                 + jnp.dot(w_hi, cc_lo, preferred_element_type=F32)).astype(BF16)
            for l1 in range(L - j):
                t_ref[l1 * LANES:(l1 + 1) * LANES, (l1 + j) * LANES:(l1 + j + 1) * LANES] = k
        for l in range(L):
            p_re, p_im = pwr_ref[0, l + 1:l + 2, :], pwi_ref[0, l + 1:l + 2, :]
            v = jnp.concatenate([ctr * p_re - cti * p_im, -(ctr * p_im + cti * p_re)], axis=1)
            vt_ref[l * LANES:(l + 1) * LANES, :] = v.astype(BF16)

    xcat = jnp.concatenate(
        [jnp.concatenate([u_ref[0, pl.ds(q * seq + l, rows, stride=L), :] for l in range(L)], axis=1)
         for q in range(nseq)], axis=0)
    xb = xcat.astype(BF16)
    e_ref[...] = jnp.dot(xb, we_ref[...], preferred_element_type=F32)
    ar = pwr_ref[0, L:L + 1, :]
    ai = pwi_ref[0, L:L + 1, :]

    def body(r, carry):
        out = []
        for q in range(nseq):
            hr, hi = carry[2 * q], carry[2 * q + 1]
            hin_ref[pl.ds(q * rows + r, 1), 0:ph] = hr
            hin_ref[pl.ds(q * rows + r, 1), ph:ew] = hi
            er = e_ref[pl.ds(q * rows + r, 1), 0:ph]
            ei = e_ref[pl.ds(q * rows + r, 1), ph:ew]
            out += [ar * hr - ai * hi + er, ar * hi + ai * hr + ei]
        return tuple(out)

    zero = jnp.zeros((1, ph), F32)
    lax.fori_loop(0, rows, body, (zero,) * (2 * nseq))
    tw = 2 * LANES
    y = jnp.concatenate([jnp.dot(xb[:, :c0 + tw], t_ref[0:c0 + tw, c0:c0 + tw], preferred_element_type=F32)
                         for c0 in range(0, L * LANES, tw)], axis=1)
    y = y + lax.dot_general(hin_ref[...].astype(BF16), vt_ref[...], (((1,), (1,)), ((), ())),
                            preferred_element_type=F32)
    y = _gelu_tanh(y + d_ref[0] * xcat)
    for q in range(nseq):
        for l in range(L):
            o_ref[0, pl.ds(q * seq + l, rows, stride=L), :] = y[q * rows:(q + 1) * rows, l * LANES:(l + 1) * LANES]


def _s5_mix(u, tables, nbatch):
    nblk, n, _ = u.shape
    L = S5_CHUNK
    bbx_re, bbx_im, ctx_re, ctx_im, ccm, pw_re, pw_im, d_blk = tables
    sw = bbx_re.shape[2]
    seq = n // nbatch
    nseq = S5_SEQ_PER_STEP if nbatch % S5_SEQ_PER_STEP == 0 else 1
    rows = nseq * (seq // L)
    slab = lambda a: pl.BlockSpec((1,) + a.shape[1:], lambda g, b: (g, 0, 0))
    return pl.pallas_call(
        functools.partial(_s5_kernel, nseq=nseq),
        grid=(nblk, nbatch // nseq),
        in_specs=[pl.BlockSpec((1, nseq * seq, LANES), lambda g, b: (g, b, 0))] + [slab(a) for a in tables],
        out_specs=pl.BlockSpec((1, nseq * seq, LANES), lambda g, b: (g, b, 0)),
        out_shape=jax.ShapeDtypeStruct((nblk, n, LANES), F32),
        scratch_shapes=[
            pltpu.VMEM((L * LANES, L * LANES), BF16),
            pltpu.VMEM((L * LANES, 2 * sw), BF16),
            pltpu.VMEM((L * LANES, 2 * sw), BF16),
            pltpu.VMEM((rows, 2 * sw), F32),
            pltpu.VMEM((rows, 2 * sw), F32),
        ],
        compiler_params=_params(("arbitrary", "arbitrary")),
        name="s5_chunked_scan",
    )(u, *tables)


def _glu_kernel(y_ref, w_ref, b_ref, h_ref, g1_ref, n2g_ref, sc_ref, sh_ref, wrh_ref, wrl_ref, br_ref,
                h3_ref, t_ref, rt_ref):
    d = h_ref.shape[1]
    y = jnp.concatenate([y_ref[g] for g in range(y_ref.shape[0])], axis=1)
    z = jnp.dot(y.astype(BF16), w_ref[...], preferred_element_type=F32) + b_ref[...]
    h3 = h_ref[...] + g1_ref[0] * (z[:, :d] * jax.nn.sigmoid(z[:, d:]))
    h3_ref[...] = h3
    t = _rms_mod(h3, n2g_ref[...], sc_ref[0], sh_ref[0])
    rows = t.shape[0]
    for j in range(TOKEN_ROWS):
        t_ref[pl.ds(j, rows, stride=TOKEN_ROWS), :] = t[:, j * LANES:(j + 1) * LANES]
    t_hi, t_lo = _split_bf16(t)
    logits = (jnp.dot(t_hi, wrh_ref[...], preferred_element_type=F32)
              + jnp.dot(t_lo, wrh_ref[...], preferred_element_type=F32)
              + jnp.dot(t_hi, wrl_ref[...], preferred_element_type=F32)) + br_ref[...]
    lane = lax.broadcasted_iota(jnp.int32, logits.shape, 1).astype(F32)

    def top1(v):
        m = jnp.max(v, axis=-1, keepdims=True)
        return m, jnp.min(jnp.where(v == m, lane, float(LANES)), axis=-1, keepdims=True)

    m1, i1 = top1(logits)
    m2, i2 = top1(jnp.where(lane == i1, -jnp.inf, logits))
    e2 = jnp.exp(m2 - m1)
    den = 1.0 + e2
    rt_ref[...] = jnp.where(lane == 0.0, 1.0 / den,
                            jnp.where(lane == 1.0, e2 / den,
                                      jnp.where(lane == 2.0, i1, jnp.where(lane == 3.0, i2, 0.0))))


def _glu_router(y, w_glu, b_glu, h, g1, n2g, sc2, sh2, w_router, b_router, seq):
    n, d = h.shape
    ne = w_router.shape[1]
    assert ne <= LANES
    tm = _divisor(seq, 512, 8)
    tpb = seq // tm
    bmap = lambda i: (i // tpb, 0, 0)
    tile = pl.BlockSpec((tm, d), lambda i: (i, 0))
    wr_hi, wr_lo = _split_bf16(jnp.pad(w_router, ((0, 0), (0, LANES - ne))))
    br = jnp.pad(b_router, (0, LANES - ne), constant_values=-1e30).reshape(1, LANES)
    return pl.pallas_call(
        _glu_kernel,
        grid=(n // tm,),
        in_specs=[
            pl.BlockSpec((d // LANES, tm, LANES), lambda i: (0, i, 0)),
            _resident((d, 2 * d)),
            _resident((1, 2 * d)),
            tile,
            pl.BlockSpec((1, 1, d), bmap),
            _resident((1, d)),
            pl.BlockSpec((1, 1, d), bmap),
            pl.BlockSpec((1, 1, d), bmap),
            _resident((d, LANES)),
            _resident((d, LANES)),
            _resident((1, LANES)),
        ],
        out_specs=[tile, pl.BlockSpec((tm * TOKEN_ROWS, LANES), lambda i: (i, 0)),
                   pl.BlockSpec((tm, LANES), lambda i: (i, 0))],
        out_shape=[
            jax.ShapeDtypeStruct((n, d), F32),
            jax.ShapeDtypeStruct((n * TOKEN_ROWS, LANES), F32),
            jax.ShapeDtypeStruct((n, LANES), F32),
        ],
        compiler_params=_params(("arbitrary",)),
        name="s5_glu_router",
    )(y, w_glu.astype(BF16), b_glu.reshape(1, 2 * d), h, g1, n2g.reshape(1, d), sc2, sh2,
      wr_hi, wr_lo, br)


def _token_copy(src_ref, src_tok, dst_ref, dst_tok, sem):
    s = pl.multiple_of(src_tok * TOKEN_ROWS, TOKEN_ROWS)
    t = pl.multiple_of(dst_tok * TOKEN_ROWS, TOKEN_ROWS)
    return pltpu.make_async_copy(src_ref.at[pl.ds(s, TOKEN_ROWS), :], dst_ref.at[pl.ds(t, TOKEN_ROWS), :], sem)


def _tokens_wait(src_ref, dst_ref, dst_tok, ntok, sem):
    t = pl.multiple_of(dst_tok * TOKEN_ROWS, TOKEN_ROWS)
    pltpu.make_async_copy(src_ref.at[pl.ds(0, ntok * TOKEN_ROWS), :],
                          dst_ref.at[pl.ds(t, ntok * TOKEN_ROWS), :], sem).wait()


def _expert_kernel(te_ref, tv_ref, nxt_ref, first_ref, t8_ref, wg_ref, wu_ref, wd_ref, o_ref,
                   xbuf_ref, xb_ref, acc_ref, sems, *, fchunk, nf):
    del te_ref
    i = pl.program_id(0)
    f = pl.program_id(1)
    nt = pl.num_programs(0)
    tm = xb_ref.shape[0]
    slot = lax.rem(i, 2)

    def start_row(idx_ref, into, r, prio):
        _token_copy(t8_ref, idx_ref[0, 0, r], xbuf_ref, into * tm + r, sems.at[into]).start(priority=prio)

    @pl.when(jnp.logical_and(i == 0, f == 0))
    def _():
        def body(j, c):
            for u in range(DMA_UNROLL):
                start_row(first_ref, 0, j * DMA_UNROLL + u, u % 2)
            return c
        lax.fori_loop(0, tm // DMA_UNROLL, body, 0)

    @pl.when(jnp.logical_and(f == 0, jnp.logical_and(tv_ref[i] == 0, tv_ref[jnp.maximum(i - 1, 0)] == 1)))
    def _():
        _tokens_wait(t8_ref, xbuf_ref, slot * tm, tm, sems.at[slot])

    @pl.when(jnp.logical_and(tv_ref[i] == 0, f == nf - 1))
    def _():
        o_ref[...] = jnp.zeros_like(o_ref)

    @pl.when(tv_ref[i] == 1)
    def _():
        @pl.when(f == 0)
        def _():
            _tokens_wait(t8_ref, xbuf_ref, slot * tm, tm, sems.at[slot])
            base = pl.multiple_of(slot * (tm * TOKEN_ROWS), TOKEN_ROWS)
            for j in range(TOKEN_ROWS):
                xb_ref[:, j * LANES:(j + 1) * LANES] = (
                    xbuf_ref[pl.ds(base + j, tm, stride=TOKEN_ROWS), :].astype(BF16))

        x = xb_ref[...]
        tf = wg_ref.shape[2]
        chunks = [(f0, min(f0 + fchunk, tf)) for f0 in range(0, tf, fchunk)]
        per_step = tm // nf
        ngroups = 3 * len(chunks)
        bounds = [per_step * k // ngroups for k in range(ngroups + 1)]
        group = iter(range(ngroups))

        def issue_next_group():
            k = next(group)
            for r in range(bounds[k], bounds[k + 1]):
                start_row(nxt_ref, 1 - slot, f * per_step + r, r % 2)

        y = None
        for f0, f1 in chunks:
            issue_next_group()
            g = jnp.dot(x, wg_ref[0, :, f0:f1], preferred_element_type=F32)
            issue_next_group()
            u = jnp.dot(x, wu_ref[0, :, f0:f1], preferred_element_type=F32)
            a = (_silu(g) * u).astype(BF16)
            issue_next_group()
            yy = jnp.dot(a, wd_ref[0, f0:f1, :], preferred_element_type=F32)
            y = yy if y is None else y + yy

        def emit(total):
            for j in range(TOKEN_ROWS):
                o_ref[pl.ds(j, tm, stride=TOKEN_ROWS), :] = total[:, j * LANES:(j + 1) * LANES]

        if nf == 1:
            emit(y)
        else:
            @pl.when(f == 0)
            def _():
                acc_ref[...] = y

            @pl.when(jnp.logical_and(f > 0, f < nf - 1))
            def _():
                acc_ref[...] += y

            @pl.when(f == nf - 1)
            def _():
                emit(acc_ref[...] + y)

        @pl.when(jnp.logical_and(i == nt - 1, f == nf - 1))
        def _():
            _tokens_wait(t8_ref, xbuf_ref, (1 - slot) * tm, tm, sems.at[1 - slot])


def _experts(t8, src_tok, tile_expert, tile_valid, wg, wu, wd, tm):
    n_tiles = src_tok.shape[0]
    d = TOKEN_ROWS * LANES
    dff = wg.shape[2]
    tf = _divisor(dff, 1792, 256)
    nf = dff // tf
    assert tm % (nf * DMA_UNROLL) == 0
    fidx = lambda i, f, tv: f * tv[i] + (nf - 1) * (1 - tv[i])
    smem_tile = lambda imap: pl.BlockSpec((1, 1, tm), imap, memory_space=pltpu.SMEM)
    grid_spec = pltpu.PrefetchScalarGridSpec(
        num_scalar_prefetch=2,
        grid=(n_tiles, nf),
        in_specs=[
            smem_tile(lambda i, f, te, tv: (jnp.minimum(i + 1, n_tiles - 1), 0, 0)),
            smem_tile(lambda i, f, te, tv: (0, 0, 0)),
            pl.BlockSpec(memory_space=pl.ANY),
            pl.BlockSpec((1, d, tf), lambda i, f, te, tv: (te[i], 0, fidx(i, f, tv))),
            pl.BlockSpec((1, d, tf), lambda i, f, te, tv: (te[i], 0, fidx(i, f, tv))),
            pl.BlockSpec((1, tf, d), lambda i, f, te, tv: (te[i], fidx(i, f, tv), 0)),
        ],
        out_specs=pl.BlockSpec((tm * TOKEN_ROWS, LANES), lambda i, f, te, tv: (i, 0)),
        scratch_shapes=[
            pltpu.VMEM((2 * tm * TOKEN_ROWS, LANES), F32),
            pltpu.VMEM((tm, d), BF16),
            pltpu.VMEM((tm, d), F32),
            pltpu.SemaphoreType.DMA((2,)),
        ],
    )
    return pl.pallas_call(
        functools.partial(_expert_kernel, fchunk=1024, nf=nf),
        grid_spec=grid_spec,
        out_shape=jax.ShapeDtypeStruct((n_tiles * tm * TOKEN_ROWS, LANES), F32),
        compiler_params=_params(("arbitrary", "arbitrary")),
        name="moe_experts",
    )(tile_expert, tile_valid, src_tok, src_tok, t8, wg, wu, wd)


def _combine_kernel(nxt_ref, first_ref, ys_ref, h_ref, rt_ref, g2_ref, fg_ref, o_ref, ybuf_ref, sems):
    i = pl.program_id(0)
    ns = pl.num_programs(0)
    tc = h_ref.shape[0]
    npair = TOP_K * tc
    slot = lax.rem(i, 2)

    def gather(idx_ref, into):
        def body(j, c):
            for u in range(DMA_UNROLL):
                r = j * DMA_UNROLL + u
                for k in range(TOP_K):
                    _token_copy(ys_ref, idx_ref[0, 0, TOP_K * r + k], ybuf_ref, into * npair + k * tc + r,
                                sems.at[into]).start(priority=k % 2)
            return c
        lax.fori_loop(0, tc // DMA_UNROLL, body, 0)

    @pl.when(i == 0)
    def _():
        gather(first_ref, 0)

    @pl.when(i + 1 < ns)
    def _():
        gather(nxt_ref, 1 - slot)

    _tokens_wait(ys_ref, ybuf_ref, slot * npair, npair, sems.at[slot])
    rt = rt_ref[...]
    base = pl.multiple_of(slot * (npair * TOKEN_ROWS), TOKEN_ROWS)
    pieces = []
    for j in range(TOKEN_ROWS):
        yj = None
        for k in range(TOP_K):
            v = ybuf_ref[pl.ds(base + k * tc * TOKEN_ROWS + j, tc, stride=TOKEN_ROWS), :]
            yj = rt[:, k:k + 1] * v if yj is None else yj + rt[:, k:k + 1] * v
        pieces.append(yj)
    y = jnp.concatenate(pieces, axis=1)
    h4 = h_ref[...] + g2_ref[0] * y
    ms = jnp.mean(h4 * h4, axis=-1, keepdims=True)
    o_ref[...] = h4 * lax.rsqrt(ms + NORM_EPS) * fg_ref[...]


def _combine(ys8, dest, h2d, rt, g2, final_g, seq):
    n, d = h2d.shape
    tc = _divisor(seq, 512, DMA_UNROLL)
    tpb = seq // tc
    ns = n // tc
    dest3 = dest.reshape(ns, 1, TOP_K * tc)
    smem_tile = lambda imap: pl.BlockSpec((1, 1, TOP_K * tc), imap, memory_space=pltpu.SMEM)
    return pl.pallas_call(
        _combine_kernel,
        grid=(ns,),
        in_specs=[
            smem_tile(lambda i: (jnp.minimum(i + 1, ns - 1), 0, 0)),
            smem_tile(lambda i: (0, 0, 0)),
            pl.BlockSpec(memory_space=pl.ANY),
            pl.BlockSpec((tc, d), lambda i: (i, 0)),
            pl.BlockSpec((tc, LANES), lambda i: (i, 0)),
            pl.BlockSpec((1, 1, d), lambda i: (i // tpb, 0, 0)),
            _resident((1, d)),
        ],
        out_specs=pl.BlockSpec((tc, d), lambda i: (i, 0)),
        out_shape=jax.ShapeDtypeStruct((n, d), F32),
        scratch_shapes=[pltpu.VMEM((2 * TOP_K * tc * TOKEN_ROWS, LANES), F32), pltpu.SemaphoreType.DMA((2,))],
        compiler_params=_params(("arbitrary",)),
        name="moe_combine_norm",
    )(dest3, dest3, ys8, h2d, rt, g2, final_g.reshape(1, d))


def _route(rt, n_experts, tm):
    n = rt.shape[0]
    npairs = n * TOP_K
    pair_expert = rt[:, 2:2 + TOP_K].astype(jnp.int32).reshape(npairs)
    onehot = (pair_expert[:, None] == jnp.arange(n_experts, dtype=jnp.int32)[None, :]).astype(jnp.int32)
    csum = jnp.cumsum(onehot, axis=0)
    rank = jnp.sum(onehot * csum, axis=1) - 1
    counts = csum[-1]
    tiles = (counts + tm - 1) // tm
    tile_end = jnp.cumsum(tiles)
    tile_start = tile_end - tiles
    dest = jnp.sum(onehot * tile_start[None, :], axis=1) * tm + rank
    n_tiles = npairs // tm + n_experts
    ti = jnp.arange(n_tiles, dtype=jnp.int32)
    n_active = tile_end[-1]
    tile_valid = (ti < n_active).astype(jnp.int32)
    last_used = jnp.minimum(ti, n_active - 1)
    expert_of = jnp.sum((last_used[:, None] >= tile_end[None, :]).astype(jnp.int32), axis=1)
    tile_expert = jnp.minimum(expert_of, n_experts - 1)
    order_tok = (jnp.sort(pair_expert * npairs + jnp.arange(npairs, dtype=jnp.int32)) % npairs) // TOP_K
    nrows = n_tiles * tm
    count_start = jnp.cumsum(counts) - counts
    padded = jnp.concatenate([jnp.zeros((nrows,), jnp.int32), order_tok, jnp.zeros((nrows,), jnp.int32)])
    row = jnp.arange(nrows, dtype=jnp.int32)
    src_tok = jnp.zeros((nrows,), jnp.int32)
    for e in range(n_experts):
        first_row = tile_start[e] * tm
        shifted = lax.dynamic_slice(padded, (nrows + count_start[e] - first_row,), (nrows,))
        mine = jnp.logical_and(row >= first_row, row < first_row + counts[e])
        src_tok = jnp.where(mine, shifted, src_tok)
    return dest.astype(jnp.int32), src_tok.reshape(n_tiles, 1, tm), tile_expert, tile_valid


def kernel(x, c, mod_w, mod_b, norm1_g, norm2_g, conv_w_pw1, conv_b_pw1, conv_w_dw, conv_b_dw, conv_ln_g, conv_ln_b, conv_w_pw2, conv_b_pw2, ssm_a_re, ssm_a_im, ssm_log_dt, ssm_b_re, ssm_b_im, ssm_c_re, ssm_c_im, ssm_d, ssm_w_glu, ssm_b_glu, ffn_w_gate, ffn_w_up, ffn_w_down, moe_w_router, moe_b_router, moe_w_gate, moe_w_up, moe_w_down, final_norm_g):
    nb, seq, d = x.shape
    n = nb * seq
    assert mod_w.shape[0] == 2 and seq % S5_CHUNK == 0 and d == TOKEN_ROWS * LANES
    n_experts = moe_w_router.shape[-1]

    mod = _modulation(c, mod_w, mod_b)
    parts = [[mod[i, :, k * d:(k + 1) * d].reshape(nb, 1, d) for k in range(6)] for i in range(2)]
    sh1a, sc1a, g1a, sh2a, sc2a, g2a = parts[0]
    sh1b, sc1b, g1b, sh2b, sc2b, g2b = parts[1]

    u, moe_wg = _pw1(x.reshape(n, d), norm1_g[0], sc1a, sh1a, conv_w_pw1[0], conv_b_pw1[0], seq, moe_w_gate[0])
    h1, moe_wu = _conv_block(u.reshape(nb, seq, d), x, conv_w_dw[0], conv_b_dw[0], conv_ln_g[0], conv_ln_b[0],
                             conv_w_pw2[0], conv_b_pw2[0], g1a, moe_w_up[0])
    h2, u1, moe_wd = _dense_ffn(h1.reshape(n, d), norm2_g[0], sc2a, sh2a, g2a, ffn_w_gate[0], ffn_w_up[0],
                                ffn_w_down[0], norm1_g[1], sc1b, sh1b, seq, moe_w_down[0])

    tables = _s5_tables(ssm_a_re[0], ssm_a_im[0], ssm_log_dt[0], ssm_b_re[0], ssm_b_im[0],
                        ssm_c_re[0], ssm_c_im[0], ssm_d[0])
    y1 = _s5_mix(u1, tables, nb)
    h3, t8, rt = _glu_router(y1, ssm_w_glu[0], ssm_b_glu[0], h2, g1b, norm2_g[1], sc2b, sh2b,
                             moe_w_router[0], moe_b_router[0], seq)
    tm = 512
    dest, src_tok, tile_expert, tile_valid = _route(rt, n_experts, tm)
    ys8 = _experts(t8, src_tok, tile_expert, tile_valid, moe_wg, moe_wu, moe_wd, tm)
    out = _combine(ys8, dest, h3, rt, g2b, final_norm_g, seq)
    return out.reshape(nb, seq, d)
```

```python
import functools

import jax
import jax.numpy as jnp
from jax import lax
from jax.experimental import pallas as pl
from jax.experimental.pallas import tpu as pltpu

F32 = jnp.float32
BF16 = jnp.bfloat16
HIGHEST = lax.Precision.HIGHEST

NORM_EPS = 1e-6
TOP_K = 2
LANES = 128
S5_CHUNK = 16
S5_SEQ_PER_STEP = 2
CONV_HALO = 32
CONV_ROWS = 64
TOKEN_ROWS = 8
DMA_UNROLL = 8
VMEM_LIMIT = 56 * 2**20


def _params(sem):
    return pltpu.CompilerParams(dimension_semantics=sem, vmem_limit_bytes=VMEM_LIMIT)


def _divisor(n, cap, mult):
    best = None
    for d in range(mult, min(n, cap) + 1, mult):
        if n % d == 0:
            best = d
    assert best is not None, (n, cap, mult)
    return best


def _resident(shape):
    nd = len(shape)
    return pl.BlockSpec(shape, lambda *_: (0,) * nd, pipeline_mode=pl.Buffered(1))


def _rms_mod(x, g, sc, sh):
    ms = jnp.mean(x * x, axis=-1, keepdims=True)
    return (x * lax.rsqrt(ms + NORM_EPS) * g) * (1.0 + sc) + sh


def _silu(x):
    return x * jax.nn.sigmoid(x)


def _gelu_tanh(x):
    c = 0.7978845608028654
    return 0.5 * x * (1.0 + jnp.tanh(c * (x + 0.044715 * (x * x * x))))


def _split_bf16(x):
    hi = x.astype(BF16)
    return hi, (x - hi.astype(F32)).astype(BF16)


def _mod_kernel(c_ref, w_ref, b_ref, o_ref):
    cond = _silu(c_ref[...])
    o_ref[0] = jnp.dot(cond, w_ref[0], preferred_element_type=F32, precision=HIGHEST) + b_ref[0]


def _modulation(c, mod_w, mod_b):
    depth, d, d6 = mod_w.shape
    nb = c.shape[0]
    rows = -(-nb // 8) * 8
    c8 = jnp.pad(c, ((0, rows - nb), (0, 0)))
    tn = _divisor(d6, 1536, LANES)
    out = pl.pallas_call(
        _mod_kernel,
        grid=(depth, d6 // tn),
        in_specs=[
            pl.BlockSpec((rows, d), lambda i, j: (0, 0)),
            pl.BlockSpec((1, d, tn), lambda i, j: (i, 0, j)),
            pl.BlockSpec((1, 1, tn), lambda i, j: (i, 0, j)),
        ],
        out_specs=pl.BlockSpec((1, rows, tn), lambda i, j: (i, 0, j)),
        out_shape=jax.ShapeDtypeStruct((depth, rows, d6), F32),
        compiler_params=_params(("arbitrary", "arbitrary")),
        name="adaln_mod",
    )(c8, mod_w, mod_b.reshape(depth, 1, d6))
    return out[:, :nb, :]


def _cast_rider(w, nsteps, imap):
    w2 = w.reshape(-1, w.shape[-1])
    rows, cols = w2.shape
    assert rows % (16 * nsteps) == 0, (rows, nsteps)
    return w2, pl.BlockSpec((rows // nsteps, cols), imap), jax.ShapeDtypeStruct((rows, cols), BF16)


def _pw1_kernel(x_ref, g_ref, sc_ref, sh_ref, w_ref, b_ref, o_ref):
    x = x_ref[...]
    d = x.shape[1]
    y = _rms_mod(x, g_ref[...], sc_ref[0], sh_ref[0])
    u = jnp.dot(y.astype(BF16), w_ref[...], preferred_element_type=F32) + b_ref[...]
    o_ref[...] = u[:, :d] * jax.nn.sigmoid(u[:, d:])


def _pw1(x2, norm_g, sc, sh, w, b, seq):
    n, d = x2.shape
    tm = _divisor(seq, 512, 8)
    tpb = seq // tm
    bmap = lambda i: (i // tpb, 0, 0)
    return pl.pallas_call(
        _pw1_kernel,
        grid=(n // tm,),
        in_specs=[
            pl.BlockSpec((tm, d), lambda i: (i, 0)),
            _resident((1, d)),
            pl.BlockSpec((1, 1, d), bmap),
            pl.BlockSpec((1, 1, d), bmap),
            _resident((d, 2 * d)),
            _resident((1, 2 * d)),
        ],
        out_specs=pl.BlockSpec((tm, d), lambda i: (i, 0)),
        out_shape=jax.ShapeDtypeStruct((n, d), F32),
        compiler_params=_params(("arbitrary",)),
        name="conv_pw1_glu",
    )(x2, norm_g.reshape(1, d), sc, sh, w.astype(BF16), b.reshape(1, 2 * d))


def _conv_kernel(cur_ref, prev_ref, wdw_ref, bdw_ref, lng_ref, lnb_ref, w2_ref, b2_ref, h_ref, g1_ref,
                 cwa_ref, cwb_ref, o_ref, coa_ref, cob_ref, buf_ref, cv_ref, *, taps):
    tm, d = cv_ref.shape
    i = pl.program_id(1)
    coa_ref[...] = cwa_ref[...].astype(BF16)
    cob_ref[...] = cwb_ref[...].astype(BF16)
    buf_ref[0:CONV_HALO, :] = jnp.where(i > 0, prev_ref[0], 0.0)
    buf_ref[CONV_HALO:, :] = cur_ref[0]
    off0 = CONV_HALO - (taps - 1)
    span = CONV_ROWS + CONV_HALO
    for c in range(d // LANES):
        lanes = slice(c * LANES, (c + 1) * LANES)

        def body(k, carry, lanes=lanes):
            r0 = pl.multiple_of(k * CONV_ROWS, CONV_ROWS)
            v = buf_ref[pl.ds(r0, span), lanes]
            acc = jnp.broadcast_to(bdw_ref[:, lanes], (CONV_ROWS, LANES))
            for s in range(8):
                xs = v if s == 0 else pltpu.roll(v, span - s, axis=0)
                for q in range(span // 8):
                    o = 8 * q + s
                    if o < off0 or o > off0 + taps - 1:
                        continue
                    wrow = wdw_ref[o - off0:o - off0 + 1, lanes]
                    acc = acc + wrow * xs[8 * q:8 * q + CONV_ROWS, :]
            cv_ref[pl.ds(r0, CONV_ROWS), lanes] = acc
            return carry

        lax.fori_loop(0, tm // CONV_ROWS, body, 0)
    v = cv_ref[...]
    mu = jnp.mean(v, axis=-1, keepdims=True)
    xc = v - mu
    var = jnp.mean(xc * xc, axis=-1, keepdims=True)
    y = _silu(xc * lax.rsqrt(var + NORM_EPS) * lng_ref[...] + lnb_ref[...])
    z = jnp.dot(y.astype(BF16), w2_ref[...], preferred_element_type=F32) + b2_ref[...]
    o_ref[0] = h_ref[0] + g1_ref[0] * z


def _conv_block(u3, x3, w_dw, b_dw, ln_g, ln_b, w2, b2, g1, rider_a, rider_b):
    nb, seq, d = x3.shape
    taps = w_dw.shape[0]
    assert taps - 1 <= CONV_HALO and d % LANES == 0
    tm = _divisor(seq, 512, CONV_ROWS)
    hb = tm // CONV_HALO
    tps = seq // tm
    wpad = jnp.pad(w_dw, ((0, -taps % 8), (0, 0)))
    row = lambda a: a.reshape(1, d)
    step = lambda b, i: (b * tps + i, 0)
    ra2, ra_spec, ra_shape = _cast_rider(rider_a, nb * tps, step)
    rb2, rb_spec, rb_shape = _cast_rider(rider_b, nb * tps, step)
    out, cast_a, cast_b = pl.pallas_call(
        functools.partial(_conv_kernel, taps=taps),
        grid=(nb, seq // tm),
        in_specs=[
            pl.BlockSpec((1, tm, d), lambda b, i: (b, i, 0)),
            pl.BlockSpec((1, CONV_HALO, d), lambda b, i: (b, jnp.maximum(i * hb - 1, 0), 0)),
            _resident(wpad.shape),
            _resident((1, d)),
            _resident((1, d)),
            _resident((1, d)),
            _resident((d, d)),
            _resident((1, d)),
            pl.BlockSpec((1, tm, d), lambda b, i: (b, i, 0)),
            pl.BlockSpec((1, 1, d), lambda b, i: (b, 0, 0)),
            ra_spec,
            rb_spec,
        ],
        out_specs=[pl.BlockSpec((1, tm, d), lambda b, i: (b, i, 0)), ra_spec, rb_spec],
        out_shape=[jax.ShapeDtypeStruct((nb, seq, d), F32), ra_shape, rb_shape],
        scratch_shapes=[pltpu.VMEM((tm + CONV_HALO, d), F32), pltpu.VMEM((tm, d), F32)],
        compiler_params=_params(("arbitrary", "arbitrary")),
        name="conv_dw_ln_pw2",
    )(u3, u3, wpad, row(b_dw), row(ln_g), row(ln_b), w2.astype(BF16), row(b2), x3, g1, ra2, rb2)
    return out, cast_a.reshape(rider_a.shape), cast_b.reshape(rider_b.shape)


def _ffn_kernel(h_ref, n2g_ref, sc_ref, sh_ref, g2_ref, wg_ref, wu_ref, wd_ref, n1g_ref, sc1_ref, sh1_ref, cw_ref,
                h2_ref, u_ref, co_ref, *, fchunk):
    co_ref[...] = cw_ref[...].astype(BF16)
    h = h_ref[...]
    t = _rms_mod(h, n2g_ref[...], sc_ref[0], sh_ref[0]).astype(BF16)
    dff = wg_ref.shape[1]
    acc = None
    for f0 in range(0, dff, fchunk):
        f1 = min(f0 + fchunk, dff)
        g = jnp.dot(t, wg_ref[:, f0:f1], preferred_element_type=F32)
        u = jnp.dot(t, wu_ref[:, f0:f1], preferred_element_type=F32)
        a = (_silu(g) * u).astype(BF16)
        y = jnp.dot(a, wd_ref[f0:f1, :], preferred_element_type=F32)
        acc = y if acc is None else acc + y
    h2 = h + g2_ref[0] * acc
    h2_ref[...] = h2
    u = _rms_mod(h2, n1g_ref[...], sc1_ref[0], sh1_ref[0])
    for g in range(u_ref.shape[0]):
        u_ref[g] = u[:, g * LANES:(g + 1) * LANES]


def _dense_ffn(h, n2g, sc2, sh2, g2, wg, wu, wd, n1g, sc1, sh1, seq, rider):
    n, d = h.shape
    nblk = d // LANES
    dff = wg.shape[1]
    tm = _divisor(seq, 512, 8)
    tpb = seq // tm
    bmap = lambda i: (i // tpb, 0, 0)
    tile = pl.BlockSpec((tm, d), lambda i: (i, 0))
    rider2, rider_spec, rider_shape = _cast_rider(rider, n // tm, lambda i: (i, 0))
    h2, u, casted = pl.pallas_call(
        functools.partial(_ffn_kernel, fchunk=512),
        grid=(n // tm,),
        in_specs=[
            tile,
            _resident((1, d)),
            pl.BlockSpec((1, 1, d), bmap),
            pl.BlockSpec((1, 1, d), bmap),
            pl.BlockSpec((1, 1, d), bmap),
            _resident((d, dff)),
            _resident((d, dff)),
            _resident((dff, d)),
            _resident((1, d)),
            pl.BlockSpec((1, 1, d), bmap),
            pl.BlockSpec((1, 1, d), bmap),
            rider_spec,
        ],
        out_specs=[tile, pl.BlockSpec((nblk, tm, LANES), lambda i: (0, i, 0)), rider_spec],
        out_shape=[jax.ShapeDtypeStruct((n, d), F32), jax.ShapeDtypeStruct((nblk, n, LANES), F32), rider_shape],
        compiler_params=_params(("arbitrary",)),
        name="dense_swiglu",
    )(h, n2g.reshape(1, d), sc2, sh2, g2, wg.astype(BF16), wu.astype(BF16), wd.astype(BF16),
      n1g.reshape(1, d), sc1, sh1, rider2)
    return h2, u, casted.reshape(rider.shape)


def _s5_tables(a_re, a_im, log_dt, b_re, b_im, c_re, c_im, d_skip):
    ng, p = a_re.shape
    c = b_re.shape[-1]
    gpb = LANES // c
    nblk = ng // gpb
    L = S5_CHUNK
    eye = jnp.eye(gpb, dtype=F32)
    blk = lambda a: a.reshape((nblk, gpb) + a.shape[1:])
    dt = jnp.exp(log_dt)[:, None]
    mag = jnp.exp(dt * a_re)
    ab_re = mag * jnp.cos(dt * a_im)
    ab_im = mag * jnp.sin(dt * a_im)
    den = a_re * a_re + a_im * a_im
    f_re = ((ab_re - 1.0) * a_re + ab_im * a_im) / den
    f_im = (ab_im * a_re - (ab_re - 1.0) * a_im) / den
    bb_re = f_re[..., None] * b_re - f_im[..., None] * b_im
    bb_im = f_re[..., None] * b_im + f_im[..., None] * b_re
    jj = jnp.arange(L + 1, dtype=F32)[None, :, None]
    ang = blk(dt * a_im).reshape(nblk, 1, gpb * p) * jj
    pmag = jnp.exp(blk(dt * a_re).reshape(nblk, 1, gpb * p) * jj)
    pw_re = pmag * jnp.cos(ang)
    pw_im = pmag * jnp.sin(ang)
    bbx_re = jnp.einsum("bgpc,gh->bgchp", blk(bb_re), eye).reshape(nblk, LANES, gpb * p)
    bbx_im = jnp.einsum("bgpc,gh->bgchp", blk(bb_im), eye).reshape(nblk, LANES, gpb * p)
    ctx_re = jnp.einsum("bgop,gh->bgohp", blk(c_re), eye).reshape(nblk, LANES, gpb * p)
    ctx_im = jnp.einsum("bgop,gh->bgohp", blk(c_im), eye).reshape(nblk, LANES, gpb * p)
    ccm = jnp.einsum("rbgop,gh->brgpho", jnp.stack([blk(c_re), -blk(c_im)]), eye).reshape(nblk, 2 * gpb * p, LANES)
    d_blk = jnp.tile(d_skip.reshape(nblk, LANES), (1, L)).reshape(nblk, 1, L * LANES)
    return bbx_re, bbx_im, ctx_re, ctx_im, ccm, pw_re, pw_im, d_blk


def _s5_kernel(u_ref, bbr_ref, bbi_ref, ctr_ref, cti_ref, ccm_ref, pwr_ref, pwi_ref, d_ref, o_ref,
               t_ref, we_ref, vt_ref, e_ref, hin_ref, *, nseq):
    L = S5_CHUNK
    ew = e_ref.shape[1]
    rows = e_ref.shape[0] // nseq
    seq = rows * L
    ph = ew // 2

    @pl.when(pl.program_id(1) == 0)
    def _build_operands():
        zblock = jnp.zeros((LANES, LANES), BF16)
        for k in range(L // 2):
            t_ref[(2 * k + 1) * LANES:(2 * k + 2) * LANES, 2 * k * LANES:(2 * k + 1) * LANES] = zblock
        bbr, bbi = bbr_ref[0], bbi_ref[0]
        ctr, cti = ctr_ref[0], cti_ref[0]
        cc_hi, cc_lo = _split_bf16(ccm_ref[0])
        for lp in range(L):
            j = L - 1 - lp
            p_re, p_im = pwr_ref[0, j:j + 1, :], pwi_ref[0, j:j + 1, :]
            w = jnp.concatenate([bbr * p_re - bbi * p_im, bbr * p_im + bbi * p_re], axis=1)
            w_hi, w_lo = _split_bf16(w)
            we_ref[lp * LANES:(lp + 1) * LANES, :] = w_hi
            k = (jnp.dot(w_hi, cc_hi, preferred_element_type=F32)
                 + jnp.dot(w_lo, cc_hi, preferred_element_type=F32)
                 + jnp.dot(w_hi, cc_lo, preferred_element_type=F32)).astype(BF16)
            for l1 in range(L - j):
                t_ref[l1 * LANES:(l1 + 1) * LANES, (l1 + j) * LANES:(l1 + j + 1) * LANES] = k
        for l in range(L):
            p_re, p_im = pwr_ref[0, l + 1:l + 2, :], pwi_ref[0, l + 1:l + 2, :]
            v = jnp.concatenate([ctr * p_re - cti * p_im, -(ctr * p_im + cti * p_re)], axis=1)
            vt_ref[l * LANES:(l + 1) * LANES, :] = v.astype(BF16)

    xcat = jnp.concatenate(
        [jnp.concatenate([u_ref[0, pl.ds(q * seq + l, rows, stride=L), :] for l in range(L)], axis=1)
         for q in range(nseq)], axis=0)
    xb = xcat.astype(BF16)
    e_ref[...] = jnp.dot(xb, we_ref[...], preferred_element_type=F32)
    ar = pwr_ref[0, L:L + 1, :]
    ai = pwi_ref[0, L:L + 1, :]

    def body(r, carry):
        out = []
        for q in range(nseq):
            hr, hi = carry[2 * q], carry[2 * q + 1]
            hin_ref[pl.ds(q * rows + r, 1), 0:ph] = hr
            hin_ref[pl.ds(q * rows + r, 1), ph:ew] = hi
            er = e_ref[pl.ds(q * rows + r, 1), 0:ph]
            ei = e_ref[pl.ds(q * rows + r, 1), ph:ew]
            out += [ar * hr - ai * hi + er, ar * hi + ai * hr + ei]
        return tuple(out)

    zero = jnp.zeros((1, ph), F32)
    lax.fori_loop(0, rows, body, (zero,) * (2 * nseq))
    tw = 2 * LANES
    y = jnp.concatenate([jnp.dot(xb[:, :c0 + tw], t_ref[0:c0 + tw, c0:c0 + tw], preferred_element_type=F32)
                         for c0 in range(0, L * LANES, tw)], axis=1)
    y = y + lax.dot_general(hin_ref[...].astype(BF16), vt_ref[...], (((1,), (1,)), ((), ())),
                            preferred_element_type=F32)
    y = _gelu_tanh(y + d_ref[0] * xcat)
    for q in range(nseq):
        for l in range(L):
            o_ref[0, pl.ds(q * seq + l, rows, stride=L), :] = y[q * rows:(q + 1) * rows, l * LANES:(l + 1) * LANES]


def _s5_mix(u, tables, nbatch):
    nblk, n, _ = u.shape
    L = S5_CHUNK
    bbx_re, bbx_im, ctx_re, ctx_im, ccm, pw_re, pw_im, d_blk = tables
    sw = bbx_re.shape[2]
    seq = n // nbatch
    nseq = S5_SEQ_PER_STEP if nbatch % S5_SEQ_PER_STEP == 0 else 1
    rows = nseq * (seq // L)
    slab = lambda a: pl.BlockSpec((1,) + a.shape[1:], lambda g, b: (g, 0, 0))
    return pl.pallas_call(
        functools.partial(_s5_kernel, nseq=nseq),
        grid=(nblk, nbatch // nseq),
        in_specs=[pl.BlockSpec((1, nseq * seq, LANES), lambda g, b: (g, b, 0))] + [slab(a) for a in tables],
        out_specs=pl.BlockSpec((1, nseq * seq, LANES), lambda g, b: (g, b, 0)),
        out_shape=jax.ShapeDtypeStruct((nblk, n, LANES), F32),
        scratch_shapes=[
            pltpu.VMEM((L * LANES, L * LANES), BF16),
            pltpu.VMEM((L * LANES, 2 * sw), BF16),
            pltpu.VMEM((L * LANES, 2 * sw), BF16),
            pltpu.VMEM((rows, 2 * sw), F32),
            pltpu.VMEM((rows, 2 * sw), F32),
        ],
        compiler_params=_params(("arbitrary", "arbitrary")),
        name="s5_chunked_scan",
    )(u, *tables)


def _glu_kernel(y_ref, w_ref, b_ref, h_ref, g1_ref, n2g_ref, sc_ref, sh_ref, wrh_ref, wrl_ref, br_ref,
                h3_ref, t_ref, rt_ref):
    d = h_ref.shape[1]
    y = jnp.concatenate([y_ref[g] for g in range(y_ref.shape[0])], axis=1)
    z = jnp.dot(y.astype(BF16), w_ref[...], preferred_element_type=F32) + b_ref[...]
    h3 = h_ref[...] + g1_ref[0] * (z[:, :d] * jax.nn.sigmoid(z[:, d:]))
    h3_ref[...] = h3
    t = _rms_mod(h3, n2g_ref[...], sc_ref[0], sh_ref[0])
    rows = t.shape[0]
    for j in range(TOKEN_ROWS):
        t_ref[pl.ds(j, rows, stride=TOKEN_ROWS), :] = t[:, j * LANES:(j + 1) * LANES]
    t_hi, t_lo = _split_bf16(t)
    logits = (jnp.dot(t_hi, wrh_ref[...], preferred_element_type=F32)
              + jnp.dot(t_lo, wrh_ref[...], preferred_element_type=F32)
              + jnp.dot(t_hi, wrl_ref[...], preferred_element_type=F32)) + br_ref[...]
    lane = lax.broadcasted_iota(jnp.int32, logits.shape, 1).astype(F32)

    def top1(v):
        m = jnp.max(v, axis=-1, keepdims=True)
        return m, jnp.min(jnp.where(v == m, lane, float(LANES)), axis=-1, keepdims=True)

    m1, i1 = top1(logits)
    m2, i2 = top1(jnp.where(lane == i1, -jnp.inf, logits))
    e2 = jnp.exp(m2 - m1)
    den = 1.0 + e2
    rt_ref[...] = jnp.where(lane == 0.0, 1.0 / den,
                            jnp.where(lane == 1.0, e2 / den,
                                      jnp.where(lane == 2.0, i1, jnp.where(lane == 3.0, i2, 0.0))))


def _glu_router(y, w_glu, b_glu, h, g1, n2g, sc2, sh2, w_router, b_router, seq):
    n, d = h.shape
    ne = w_router.shape[1]
    assert ne <= LANES
    tm = _divisor(seq, 512, 8)
    tpb = seq // tm
    bmap = lambda i: (i // tpb, 0, 0)
    tile = pl.BlockSpec((tm, d), lambda i: (i, 0))
    wr_hi, wr_lo = _split_bf16(jnp.pad(w_router, ((0, 0), (0, LANES - ne))))
    br = jnp.pad(b_router, (0, LANES - ne), constant_values=-1e30).reshape(1, LANES)
    return pl.pallas_call(
        _glu_kernel,
        grid=(n // tm,),
        in_specs=[
            pl.BlockSpec((d // LANES, tm, LANES), lambda i: (0, i, 0)),
            _resident((d, 2 * d)),
            _resident((1, 2 * d)),
            tile,
            pl.BlockSpec((1, 1, d), bmap),
            _resident((1, d)),
            pl.BlockSpec((1, 1, d), bmap),
            pl.BlockSpec((1, 1, d), bmap),
            _resident((d, LANES)),
            _resident((d, LANES)),
            _resident((1, LANES)),
        ],
        out_specs=[tile, pl.BlockSpec((tm * TOKEN_ROWS, LANES), lambda i: (i, 0)),
                   pl.BlockSpec((tm, LANES), lambda i: (i, 0))],
        out_shape=[
            jax.ShapeDtypeStruct((n, d), F32),
            jax.ShapeDtypeStruct((n * TOKEN_ROWS, LANES), F32),
            jax.ShapeDtypeStruct((n, LANES), F32),
        ],
        compiler_params=_params(("arbitrary",)),
        name="s5_glu_router",
    )(y, w_glu.astype(BF16), b_glu.reshape(1, 2 * d), h, g1, n2g.reshape(1, d), sc2, sh2,
      wr_hi, wr_lo, br)


def _token_copy(src_ref, src_tok, dst_ref, dst_tok, sem):
    s = pl.multiple_of(src_tok * TOKEN_ROWS, TOKEN_ROWS)
    t = pl.multiple_of(dst_tok * TOKEN_ROWS, TOKEN_ROWS)
    return pltpu.make_async_copy(src_ref.at[pl.ds(s, TOKEN_ROWS), :], dst_ref.at[pl.ds(t, TOKEN_ROWS), :], sem)


def _tokens_wait(src_ref, dst_ref, dst_tok, ntok, sem):
    t = pl.multiple_of(dst_tok * TOKEN_ROWS, TOKEN_ROWS)
    pltpu.make_async_copy(src_ref.at[pl.ds(0, ntok * TOKEN_ROWS), :],
                          dst_ref.at[pl.ds(t, ntok * TOKEN_ROWS), :], sem).wait()


def _expert_kernel(te_ref, tv_ref, nxt_ref, first_ref, t8_ref, wg_ref, wu_ref, wd_ref, o_ref,
                   xbuf_ref, xb_ref, acc_ref, sems, *, fchunk, nf):
    del te_ref
    i = pl.program_id(0)
    f = pl.program_id(1)
    nt = pl.num_programs(0)
    tm = xb_ref.shape[0]
    slot = lax.rem(i, 2)

    def gather(idx_ref, into):
        def body(j, c):
            for u in range(DMA_UNROLL):
                r = j * DMA_UNROLL + u
                _token_copy(t8_ref, idx_ref[0, 0, r], xbuf_ref, into * tm + r, sems.at[into]).start(priority=u % 2)
            return c
        lax.fori_loop(0, tm // DMA_UNROLL, body, 0)

    @pl.when(jnp.logical_and(i == 0, f == 0))
    def _():
        gather(first_ref, 0)

    @pl.when(jnp.logical_and(f == 0, jnp.logical_and(i + 1 < nt, tv_ref[jnp.minimum(i + 1, nt - 1)] == 1)))
    def _():
        gather(nxt_ref, 1 - slot)

    @pl.when(jnp.logical_and(tv_ref[i] == 0, f == nf - 1))
    def _():
        o_ref[...] = jnp.zeros_like(o_ref)

    @pl.when(tv_ref[i] == 1)
    def _():
        @pl.when(f == 0)
        def _():
            _tokens_wait(t8_ref, xbuf_ref, slot * tm, tm, sems.at[slot])
            base = pl.multiple_of(slot * (tm * TOKEN_ROWS), TOKEN_ROWS)
            for j in range(TOKEN_ROWS):
                xb_ref[:, j * LANES:(j + 1) * LANES] = (
                    xbuf_ref[pl.ds(base + j, tm, stride=TOKEN_ROWS), :].astype(BF16))

        x = xb_ref[...]
        tf = wg_ref.shape[2]
        y = None
        for f0 in range(0, tf, fchunk):
            f1 = min(f0 + fchunk, tf)
            g = jnp.dot(x, wg_ref[0, :, f0:f1], preferred_element_type=F32)
            u = jnp.dot(x, wu_ref[0, :, f0:f1], preferred_element_type=F32)
            a = (_silu(g) * u).astype(BF16)
            yy = jnp.dot(a, wd_ref[0, f0:f1, :], preferred_element_type=F32)
            y = yy if y is None else y + yy

        def emit(total):
            for j in range(TOKEN_ROWS):
                o_ref[pl.ds(j, tm, stride=TOKEN_ROWS), :] = total[:, j * LANES:(j + 1) * LANES]

        if nf == 1:
            emit(y)
        else:
            @pl.when(f == 0)
            def _():
                acc_ref[...] = y

            @pl.when(jnp.logical_and(f > 0, f < nf - 1))
            def _():
                acc_ref[...] += y

            @pl.when(f == nf - 1)
            def _():
                emit(acc_ref[...] + y)


def _experts(t8, src_tok, tile_expert, tile_valid, wg, wu, wd, tm):
    n_tiles = src_tok.shape[0]
    d = TOKEN_ROWS * LANES
    dff = wg.shape[2]
    tf = _divisor(dff, 1792, 256)
    nf = dff // tf
    assert tm % (nf * DMA_UNROLL) == 0
    fidx = lambda i, f, tv: f * tv[i] + (nf - 1) * (1 - tv[i])
    smem_tile = lambda imap: pl.BlockSpec((1, 1, tm), imap, memory_space=pltpu.SMEM)
    grid_spec = pltpu.PrefetchScalarGridSpec(
        num_scalar_prefetch=2,
        grid=(n_tiles, nf),
        in_specs=[
            smem_tile(lambda i, f, te, tv: (jnp.minimum(i + 1, n_tiles - 1), 0, 0)),
            smem_tile(lambda i, f, te, tv: (0, 0, 0)),
            pl.BlockSpec(memory_space=pl.ANY),
            pl.BlockSpec((1, d, tf), lambda i, f, te, tv: (te[i], 0, fidx(i, f, tv))),
            pl.BlockSpec((1, d, tf), lambda i, f, te, tv: (te[i], 0, fidx(i, f, tv))),
            pl.BlockSpec((1, tf, d), lambda i, f, te, tv: (te[i], fidx(i, f, tv), 0)),
        ],
        out_specs=pl.BlockSpec((tm * TOKEN_ROWS, LANES), lambda i, f, te, tv: (i, 0)),
        scratch_shapes=[
            pltpu.VMEM((2 * tm * TOKEN_ROWS, LANES), F32),
            pltpu.VMEM((tm, d), BF16),
            pltpu.VMEM((tm, d), F32),
            pltpu.SemaphoreType.DMA((2,)),
        ],
    )
    return pl.pallas_call(
        functools.partial(_expert_kernel, fchunk=1024, nf=nf),
        grid_spec=grid_spec,
        out_shape=jax.ShapeDtypeStruct((n_tiles * tm * TOKEN_ROWS, LANES), F32),
        compiler_params=_params(("arbitrary", "arbitrary")),
        name="moe_experts",
    )(tile_expert, tile_valid, src_tok, src_tok, t8, wg, wu, wd)


def _combine_kernel(nxt_ref, first_ref, ys_ref, h_ref, rt_ref, g2_ref, fg_ref, o_ref, ybuf_ref, sems):
    i = pl.program_id(0)
    ns = pl.num_programs(0)
    tc = h_ref.shape[0]
    npair = TOP_K * tc
    slot = lax.rem(i, 2)

    def gather(idx_ref, into):
        def body(j, c):
            for u in range(DMA_UNROLL):
                r = j * DMA_UNROLL + u
                for k in range(TOP_K):
                    _token_copy(ys_ref, idx_ref[0, 0, TOP_K * r + k], ybuf_ref, into * npair + k * tc + r,
                                sems.at[into]).start(priority=k % 2)
            return c
        lax.fori_loop(0, tc // DMA_UNROLL, body, 0)

    @pl.when(i == 0)
    def _():
        gather(first_ref, 0)

    @pl.when(i + 1 < ns)
    def _():
        gather(nxt_ref, 1 - slot)

    _tokens_wait(ys_ref, ybuf_ref, slot * npair, npair, sems.at[slot])
    rt = rt_ref[...]
    base = pl.multiple_of(slot * (npair * TOKEN_ROWS), TOKEN_ROWS)
    pieces = []
    for j in range(TOKEN_ROWS):
        yj = None
        for k in range(TOP_K):
            v = ybuf_ref[pl.ds(base + k * tc * TOKEN_ROWS + j, tc, stride=TOKEN_ROWS), :]
            yj = rt[:, k:k + 1] * v if yj is None else yj + rt[:, k:k + 1] * v
        pieces.append(yj)
    y = jnp.concatenate(pieces, axis=1)
    h4 = h_ref[...] + g2_ref[0] * y
    ms = jnp.mean(h4 * h4, axis=-1, keepdims=True)
    o_ref[...] = h4 * lax.rsqrt(ms + NORM_EPS) * fg_ref[...]


def _combine(ys8, dest, h2d, rt, g2, final_g, seq):
    n, d = h2d.shape
    tc = _divisor(seq, 512, DMA_UNROLL)
    tpb = seq // tc
    ns = n // tc
    dest3 = dest.reshape(ns, 1, TOP_K * tc)
    smem_tile = lambda imap: pl.BlockSpec((1, 1, TOP_K * tc), imap, memory_space=pltpu.SMEM)
    return pl.pallas_call(
        _combine_kernel,
        grid=(ns,),
        in_specs=[
            smem_tile(lambda i: (jnp.minimum(i + 1, ns - 1), 0, 0)),
            smem_tile(lambda i: (0, 0, 0)),
            pl.BlockSpec(memory_space=pl.ANY),
            pl.BlockSpec((tc, d), lambda i: (i, 0)),
            pl.BlockSpec((tc, LANES), lambda i: (i, 0)),
            pl.BlockSpec((1, 1, d), lambda i: (i // tpb, 0, 0)),
            _resident((1, d)),
        ],
        out_specs=pl.BlockSpec((tc, d), lambda i: (i, 0)),
        out_shape=jax.ShapeDtypeStruct((n, d), F32),
        scratch_shapes=[pltpu.VMEM((2 * TOP_K * tc * TOKEN_ROWS, LANES), F32), pltpu.SemaphoreType.DMA((2,))],
        compiler_params=_params(("arbitrary",)),
        name="moe_combine_norm",
    )(dest3, dest3, ys8, h2d, rt, g2, final_g.reshape(1, d))


def _route(rt, n_experts, tm):
    n = rt.shape[0]
    npairs = n * TOP_K
    pair_expert = rt[:, 2:2 + TOP_K].astype(jnp.int32).reshape(npairs)
    onehot = (pair_expert[:, None] == jnp.arange(n_experts, dtype=jnp.int32)[None, :]).astype(jnp.int32)
    csum = jnp.cumsum(onehot, axis=0)
    rank = jnp.sum(onehot * csum, axis=1) - 1
    counts = csum[-1]
    tiles = (counts + tm - 1) // tm
    tile_end = jnp.cumsum(tiles)
    tile_start = tile_end - tiles
    dest = jnp.sum(onehot * tile_start[None, :], axis=1) * tm + rank
    n_tiles = npairs // tm + n_experts
    ti = jnp.arange(n_tiles, dtype=jnp.int32)
    n_active = tile_end[-1]
    tile_valid = (ti < n_active).astype(jnp.int32)
    last_used = jnp.minimum(ti, n_active - 1)
    expert_of = jnp.sum((last_used[:, None] >= tile_end[None, :]).astype(jnp.int32), axis=1)
    tile_expert = jnp.minimum(expert_of, n_experts - 1)
    order_tok = (jnp.sort(pair_expert * npairs + jnp.arange(npairs, dtype=jnp.int32)) % npairs) // TOP_K
    nrows = n_tiles * tm
    count_start = jnp.cumsum(counts) - counts
    padded = jnp.concatenate([jnp.zeros((nrows,), jnp.int32), order_tok, jnp.zeros((nrows,), jnp.int32)])
    row = jnp.arange(nrows, dtype=jnp.int32)
    src_tok = jnp.zeros((nrows,), jnp.int32)
    for e in range(n_experts):
        first_row = tile_start[e] * tm
        shifted = lax.dynamic_slice(padded, (nrows + count_start[e] - first_row,), (nrows,))
        mine = jnp.logical_and(row >= first_row, row < first_row + counts[e])
        src_tok = jnp.where(mine, shifted, src_tok)
    return dest.astype(jnp.int32), src_tok.reshape(n_tiles, 1, tm), tile_expert, tile_valid


def kernel(x, c, mod_w, mod_b, norm1_g, norm2_g, conv_w_pw1, conv_b_pw1, conv_w_dw, conv_b_dw, conv_ln_g, conv_ln_b, conv_w_pw2, conv_b_pw2, ssm_a_re, ssm_a_im, ssm_log_dt, ssm_b_re, ssm_b_im, ssm_c_re, ssm_c_im, ssm_d, ssm_w_glu, ssm_b_glu, ffn_w_gate, ffn_w_up, ffn_w_down, moe_w_router, moe_b_router, moe_w_gate, moe_w_up, moe_w_down, final_norm_g):
    nb, seq, d = x.shape
    n = nb * seq
    assert mod_w.shape[0] == 2 and seq % S5_CHUNK == 0 and d == TOKEN_ROWS * LANES
    n_experts = moe_w_router.shape[-1]

    mod = _modulation(c, mod_w, mod_b)
    parts = [[mod[i, :, k * d:(k + 1) * d].reshape(nb, 1, d) for k in range(6)] for i in range(2)]
    sh1a, sc1a, g1a, sh2a, sc2a, g2a = parts[0]
    sh1b, sc1b, g1b, sh2b, sc2b, g2b = parts[1]

    u = _pw1(x.reshape(n, d), norm1_g[0], sc1a, sh1a, conv_w_pw1[0], conv_b_pw1[0], seq)
    h1, moe_wg, moe_wu = _conv_block(u.reshape(nb, seq, d), x, conv_w_dw[0], conv_b_dw[0], conv_ln_g[0],
                                     conv_ln_b[0], conv_w_pw2[0], conv_b_pw2[0], g1a, moe_w_gate[0], moe_w_up[0])
    h2, u1, moe_wd = _dense_ffn(h1.reshape(n, d), norm2_g[0], sc2a, sh2a, g2a, ffn_w_gate[0], ffn_w_up[0],
                                ffn_w_down[0], norm1_g[1], sc1b, sh1b, seq, moe_w_down[0])

    tables = _s5_tables(ssm_a_re[0], ssm_a_im[0], ssm_log_dt[0], ssm_b_re[0], ssm_b_im[0],
                        ssm_c_re[0], ssm_c_im[0], ssm_d[0])
    y1 = _s5_mix(u1, tables, nb)
    h3, t8, rt = _glu_router(y1, ssm_w_glu[0], ssm_b_glu[0], h2, g1b, norm2_g[1], sc2b, sh2b,
                             moe_w_router[0], moe_b_router[0], seq)
    tm = 512
    dest, src_tok, tile_expert, tile_valid = _route(rt, n_experts, tm)
    ys8 = _experts(t8, src_tok, tile_expert, tile_valid, moe_wg, moe_wu, moe_wd, tm)
    out = _combine(ys8, dest, h3, rt, g2b, final_norm_g, seq)
    return out.reshape(nb, seq, d)
```

```python
import functools

import jax
import jax.numpy as jnp
from jax import lax
from jax.experimental import pallas as pl
from jax.experimental.pallas import tpu as pltpu

F32 = jnp.float32
BF16 = jnp.bfloat16
HIGHEST = lax.Precision.HIGHEST

NORM_EPS = 1e-6
TOP_K = 2
LANES = 128
S5_CHUNK = 16
S5_SEQ_PER_STEP = 2
CONV_HALO = 32
CONV_ROWS = 64
TOKEN_ROWS = 8
DMA_UNROLL = 8
VMEM_LIMIT = 56 * 2**20


def _params(sem):
    return pltpu.CompilerParams(dimension_semantics=sem, vmem_limit_bytes=VMEM_LIMIT)


def _divisor(n, cap, mult):
    best = None
    for d in range(mult, min(n, cap) + 1, mult):
        if n % d == 0:
            best = d
    assert best is not None, (n, cap, mult)
    return best


def _resident(shape):
    nd = len(shape)
    return pl.BlockSpec(shape, lambda *_: (0,) * nd, pipeline_mode=pl.Buffered(1))


def _rms_mod(x, g, sc, sh):
    ms = jnp.mean(x * x, axis=-1, keepdims=True)
    return (x * lax.rsqrt(ms + NORM_EPS) * g) * (1.0 + sc) + sh


def _silu(x):
    return x * jax.nn.sigmoid(x)


def _gelu_tanh(x):
    c = 0.7978845608028654
    return 0.5 * x * (1.0 + jnp.tanh(c * (x + 0.044715 * (x * x * x))))


def _split_bf16(x):
    hi = x.astype(BF16)
    return hi, (x - hi.astype(F32)).astype(BF16)


def _mod_kernel(c_ref, w_ref, b_ref, o_ref):
    cond = _silu(c_ref[...])
    o_ref[0] = jnp.dot(cond, w_ref[0], preferred_element_type=F32, precision=HIGHEST) + b_ref[0]


def _modulation(c, mod_w, mod_b):
    depth, d, d6 = mod_w.shape
    nb = c.shape[0]
    rows = -(-nb // 8) * 8
    c8 = jnp.pad(c, ((0, rows - nb), (0, 0)))
    tn = _divisor(d6, 1536, LANES)
    out = pl.pallas_call(
        _mod_kernel,
        grid=(depth, d6 // tn),
        in_specs=[
            pl.BlockSpec((rows, d), lambda i, j: (0, 0)),
            pl.BlockSpec((1, d, tn), lambda i, j: (i, 0, j)),
            pl.BlockSpec((1, 1, tn), lambda i, j: (i, 0, j)),
        ],
        out_specs=pl.BlockSpec((1, rows, tn), lambda i, j: (i, 0, j)),
        out_shape=jax.ShapeDtypeStruct((depth, rows, d6), F32),
        compiler_params=_params(("arbitrary", "arbitrary")),
        name="adaln_mod",
    )(c8, mod_w, mod_b.reshape(depth, 1, d6))
    return out[:, :nb, :]


def _cast_rider(w, nsteps, step):
    w2 = w.reshape(-1, w.shape[-1])
    rows, cols = w2.shape
    slabs = max(s for s in range(1, nsteps + 1) if nsteps % s == 0 and rows % (16 * s) == 0)
    rep = nsteps // slabs
    spec = pl.BlockSpec((rows // slabs, cols), lambda *g: (step(*g) // rep, 0))
    return w2, spec, jax.ShapeDtypeStruct((rows, cols), BF16)


def _pw1_kernel(x_ref, g_ref, sc_ref, sh_ref, w_ref, b_ref, *refs):
    nr = (len(refs) - 1) // 2
    o_ref = refs[nr]
    for cw_ref, co_ref in zip(refs[:nr], refs[nr + 1:]):
        co_ref[...] = cw_ref[...].astype(BF16)
    x = x_ref[...]
    d = x.shape[1]
    y = _rms_mod(x, g_ref[...], sc_ref[0], sh_ref[0])
    u = jnp.dot(y.astype(BF16), w_ref[...], preferred_element_type=F32) + b_ref[...]
    o_ref[...] = u[:, :d] * jax.nn.sigmoid(u[:, d:])


def _pw1(x2, norm_g, sc, sh, w, b, seq, riders):
    n, d = x2.shape
    tm = _divisor(seq, 512, 8)
    tpb = seq // tm
    bmap = lambda i: (i // tpb, 0, 0)
    cast = [_cast_rider(r, n // tm, lambda i: i) for r in riders]
    out = pl.pallas_call(
        _pw1_kernel,
        grid=(n // tm,),
        in_specs=[
            pl.BlockSpec((tm, d), lambda i: (i, 0)),
            _resident((1, d)),
            pl.BlockSpec((1, 1, d), bmap),
            pl.BlockSpec((1, 1, d), bmap),
            _resident((d, 2 * d)),
            _resident((1, 2 * d)),
        ] + [c[1] for c in cast],
        out_specs=[pl.BlockSpec((tm, d), lambda i: (i, 0))] + [c[1] for c in cast],
        out_shape=[jax.ShapeDtypeStruct((n, d), F32)] + [c[2] for c in cast],
        compiler_params=_params(("arbitrary",)),
        name="conv_pw1_glu",
    )(x2, norm_g.reshape(1, d), sc, sh, w.astype(BF16), b.reshape(1, 2 * d), *[c[0] for c in cast])
    return [out[0]] + [o.reshape(r.shape) for o, r in zip(out[1:], riders)]


def _conv_kernel(cur_ref, prev_ref, wdw_ref, bdw_ref, lng_ref, lnb_ref, w2_ref, b2_ref, h_ref, g1_ref,
                 cwa_ref, cwb_ref, o_ref, coa_ref, cob_ref, buf_ref, cv_ref, *, taps):
    tm, d = cv_ref.shape
    i = pl.program_id(1)
    coa_ref[...] = cwa_ref[...].astype(BF16)
    cob_ref[...] = cwb_ref[...].astype(BF16)
    buf_ref[0:CONV_HALO, :] = jnp.where(i > 0, prev_ref[0], 0.0)
    buf_ref[CONV_HALO:, :] = cur_ref[0]
    off0 = CONV_HALO - (taps - 1)
    span = CONV_ROWS + CONV_HALO
    for c in range(d // LANES):
        lanes = slice(c * LANES, (c + 1) * LANES)

        def body(k, carry, lanes=lanes):
            r0 = pl.multiple_of(k * CONV_ROWS, CONV_ROWS)
            v = buf_ref[pl.ds(r0, span), lanes]
            acc = jnp.broadcast_to(bdw_ref[:, lanes], (CONV_ROWS, LANES))
            for s in range(8):
                xs = v if s == 0 else pltpu.roll(v, span - s, axis=0)
                for q in range(span // 8):
                    o = 8 * q + s
                    if o < off0 or o > off0 + taps - 1:
                        continue
                    wrow = wdw_ref[o - off0:o - off0 + 1, lanes]
                    acc = acc + wrow * xs[8 * q:8 * q + CONV_ROWS, :]
            cv_ref[pl.ds(r0, CONV_ROWS), lanes] = acc
            return carry

        lax.fori_loop(0, tm // CONV_ROWS, body, 0)
    v = cv_ref[...]
    mu = jnp.mean(v, axis=-1, keepdims=True)
    xc = v - mu
    var = jnp.mean(xc * xc, axis=-1, keepdims=True)
    y = _silu(xc * lax.rsqrt(var + NORM_EPS) * lng_ref[...] + lnb_ref[...])
    z = jnp.dot(y.astype(BF16), w2_ref[...], preferred_element_type=F32) + b2_ref[...]
    o_ref[0] = h_ref[0] + g1_ref[0] * z


def _conv_block(u3, x3, w_dw, b_dw, ln_g, ln_b, w2, b2, g1, rider_a, rider_b):
    nb, seq, d = x3.shape
    taps = w_dw.shape[0]
    assert taps - 1 <= CONV_HALO and d % LANES == 0
    tm = _divisor(seq, 512, CONV_ROWS)
    hb = tm // CONV_HALO
    tps = seq // tm
    wpad = jnp.pad(w_dw, ((0, -taps % 8), (0, 0)))
    row = lambda a: a.reshape(1, d)
    step = lambda b, i: b * tps + i
    ra2, ra_spec, ra_shape = _cast_rider(rider_a, nb * tps, step)
    rb2, rb_spec, rb_shape = _cast_rider(rider_b, nb * tps, step)
    out, cast_a, cast_b = pl.pallas_call(
        functools.partial(_conv_kernel, taps=taps),
        grid=(nb, seq // tm),
        in_specs=[
            pl.BlockSpec((1, tm, d), lambda b, i: (b, i, 0)),
            pl.BlockSpec((1, CONV_HALO, d), lambda b, i: (b, jnp.maximum(i * hb - 1, 0), 0)),
            _resident(wpad.shape),
            _resident((1, d)),
            _resident((1, d)),
            _resident((1, d)),
            _resident((d, d)),
            _resident((1, d)),
            pl.BlockSpec((1, tm, d), lambda b, i: (b, i, 0)),
            pl.BlockSpec((1, 1, d), lambda b, i: (b, 0, 0)),
            ra_spec,
            rb_spec,
        ],
        out_specs=[pl.BlockSpec((1, tm, d), lambda b, i: (b, i, 0)), ra_spec, rb_spec],
        out_shape=[jax.ShapeDtypeStruct((nb, seq, d), F32), ra_shape, rb_shape],
        scratch_shapes=[pltpu.VMEM((tm + CONV_HALO, d), F32), pltpu.VMEM((tm, d), F32)],
        compiler_params=_params(("arbitrary", "arbitrary")),
        name="conv_dw_ln_pw2",
    )(u3, u3, wpad, row(b_dw), row(ln_g), row(ln_b), w2.astype(BF16), row(b2), x3, g1, ra2, rb2)
    return out, cast_a.reshape(rider_a.shape), cast_b.reshape(rider_b.shape)


def _ffn_kernel(h_ref, n2g_ref, sc_ref, sh_ref, g2_ref, wg_ref, wu_ref, wd_ref, n1g_ref, sc1_ref, sh1_ref, cw_ref,
                h2_ref, u_ref, co_ref, *, fchunk):
    co_ref[...] = cw_ref[...].astype(BF16)
    h = h_ref[...]
    t = _rms_mod(h, n2g_ref[...], sc_ref[0], sh_ref[0]).astype(BF16)
    dff = wg_ref.shape[1]
    acc = None
    for f0 in range(0, dff, fchunk):
        f1 = min(f0 + fchunk, dff)
        g = jnp.dot(t, wg_ref[:, f0:f1], preferred_element_type=F32)
        u = jnp.dot(t, wu_ref[:, f0:f1], preferred_element_type=F32)
        a = (_silu(g) * u).astype(BF16)
        y = jnp.dot(a, wd_ref[f0:f1, :], preferred_element_type=F32)
        acc = y if acc is None else acc + y
    h2 = h + g2_ref[0] * acc
    h2_ref[...] = h2
    u = _rms_mod(h2, n1g_ref[...], sc1_ref[0], sh1_ref[0])
    for g in range(u_ref.shape[0]):
        u_ref[g] = u[:, g * LANES:(g + 1) * LANES]


def _dense_ffn(h, n2g, sc2, sh2, g2, wg, wu, wd, n1g, sc1, sh1, seq, rider):
    n, d = h.shape
    nblk = d // LANES
    dff = wg.shape[1]
    tm = _divisor(seq, 512, 8)
    tpb = seq // tm
    bmap = lambda i: (i // tpb, 0, 0)
    tile = pl.BlockSpec((tm, d), lambda i: (i, 0))
    rider2, rider_spec, rider_shape = _cast_rider(rider, n // tm, lambda i: i)
    h2, u, casted = pl.pallas_call(
        functools.partial(_ffn_kernel, fchunk=512),
        grid=(n // tm,),
        in_specs=[
            tile,
            _resident((1, d)),
            pl.BlockSpec((1, 1, d), bmap),
            pl.BlockSpec((1, 1, d), bmap),
            pl.BlockSpec((1, 1, d), bmap),
            _resident((d, dff)),
            _resident((d, dff)),
            _resident((dff, d)),
            _resident((1, d)),
            pl.BlockSpec((1, 1, d), bmap),
            pl.BlockSpec((1, 1, d), bmap),
            rider_spec,
        ],
        out_specs=[tile, pl.BlockSpec((nblk, tm, LANES), lambda i: (0, i, 0)), rider_spec],
        out_shape=[jax.ShapeDtypeStruct((n, d), F32), jax.ShapeDtypeStruct((nblk, n, LANES), F32), rider_shape],
        compiler_params=_params(("arbitrary",)),
        name="dense_swiglu",
    )(h, n2g.reshape(1, d), sc2, sh2, g2, wg.astype(BF16), wu.astype(BF16), wd.astype(BF16),
      n1g.reshape(1, d), sc1, sh1, rider2)
    return h2, u, casted.reshape(rider.shape)


def _s5_tables(a_re, a_im, log_dt, b_re, b_im, c_re, c_im, d_skip):
    ng, p = a_re.shape
    c = b_re.shape[-1]
    gpb = LANES // c
    nblk = ng // gpb
    L = S5_CHUNK
    eye = jnp.eye(gpb, dtype=F32)
    blk = lambda a: a.reshape((nblk, gpb) + a.shape[1:])
    dt = jnp.exp(log_dt)[:, None]
    mag = jnp.exp(dt * a_re)
    ab_re = mag * jnp.cos(dt * a_im)
    ab_im = mag * jnp.sin(dt * a_im)
    den = a_re * a_re + a_im * a_im
    f_re = ((ab_re - 1.0) * a_re + ab_im * a_im) / den
    f_im = (ab_im * a_re - (ab_re - 1.0) * a_im) / den
    bb_re = f_re[..., None] * b_re - f_im[..., None] * b_im
    bb_im = f_re[..., None] * b_im + f_im[..., None] * b_re
    jj = jnp.arange(L + 1, dtype=F32)[None, :, None]
    ang = blk(dt * a_im).reshape(nblk, 1, gpb * p) * jj
    pmag = jnp.exp(blk(dt * a_re).reshape(nblk, 1, gpb * p) * jj)
    pw_re = pmag * jnp.cos(ang)
    pw_im = pmag * jnp.sin(ang)
    bbx_re = jnp.einsum("bgpc,gh->bgchp", blk(bb_re), eye).reshape(nblk, LANES, gpb * p)
    bbx_im = jnp.einsum("bgpc,gh->bgchp", blk(bb_im), eye).reshape(nblk, LANES, gpb * p)
    ctx_re = jnp.einsum("bgop,gh->bgohp", blk(c_re), eye).reshape(nblk, LANES, gpb * p)
    ctx_im = jnp.einsum("bgop,gh->bgohp", blk(c_im), eye).reshape(nblk, LANES, gpb * p)
    ccm = jnp.einsum("rbgop,gh->brgpho", jnp.stack([blk(c_re), -blk(c_im)]), eye).reshape(nblk, 2 * gpb * p, LANES)
    d_blk = jnp.tile(d_skip.reshape(nblk, LANES), (1, L)).reshape(nblk, 1, L * LANES)
    return bbx_re, bbx_im, ctx_re, ctx_im, ccm, pw_re, pw_im, d_blk


def _s5_kernel(u_ref, bbr_ref, bbi_ref, ctr_ref, cti_ref, ccm_ref, pwr_ref, pwi_ref, d_ref, o_ref,
               t_ref, we_ref, vt_ref, e_ref, hin_ref, *, nseq):
    L = S5_CHUNK
    ew = e_ref.shape[1]
    rows = e_ref.shape[0] // nseq
    seq = rows * L
    ph = ew // 2

    @pl.when(pl.program_id(1) == 0)
    def _build_operands():
        zblock = jnp.zeros((LANES, LANES), BF16)
        for k in range(L // 2):
            t_ref[(2 * k + 1) * LANES:(2 * k + 2) * LANES, 2 * k * LANES:(2 * k + 1) * LANES] = zblock
        bbr, bbi = bbr_ref[0], bbi_ref[0]
        ctr, cti = ctr_ref[0], cti_ref[0]
        cc_hi, cc_lo = _split_bf16(ccm_ref[0])
        for lp in range(L):
            j = L - 1 - lp
            p_re, p_im = pwr_ref[0, j:j + 1, :], pwi_ref[0, j:j + 1, :]
            w = jnp.concatenate([bbr * p_re - bbi * p_im, bbr * p_im + bbi * p_re], axis=1)
            w_hi, w_lo = _split_bf16(w)
            we_ref[lp * LANES:(lp + 1) * LANES, :] = w_hi
            k = (jnp.dot(w_hi, cc_hi, preferred_element_type=F32)
                 + jnp.dot(w_lo, cc_hi, preferred_element_type=F32)
                 + jnp.dot(w_hi, cc_lo, preferred_element_type=F32)).astype(BF16)
            for l1 in range(L - j):
                t_ref[l1 * LANES:(l1 + 1) * LANES, (l1 + j) * LANES:(l1 + j + 1) * LANES] = k
        for l in range(L):
            p_re, p_im = pwr_ref[0, l + 1:l + 2, :], pwi_ref[0, l + 1:l + 2, :]
            v = jnp.concatenate([ctr * p_re - cti * p_im, -(ctr * p_im + cti * p_re)], axis=1)
            vt_ref[l * LANES:(l + 1) * LANES, :] = v.astype(BF16)

    xcat = jnp.concatenate(
        [jnp.concatenate([u_ref[0, pl.ds(q * seq + l, rows, stride=L), :] for l in range(L)], axis=1)
         for q in range(nseq)], axis=0)
    xb = xcat.astype(BF16)
    e_ref[...] = jnp.dot(xb, we_ref[...], preferred_element_type=F32)
    ar = pwr_ref[0, L:L + 1, :]
    ai = pwi_ref[0, L:L + 1, :]

    def body(r, carry):
        out = []
        for q in range(nseq):
            hr, hi = carry[2 * q], carry[2 * q + 1]
            hin_ref[pl.ds(q * rows + r, 1), 0:ph] = hr
            hin_ref[pl.ds(q * rows + r, 1), ph:ew] = hi
            er = e_ref[pl.ds(q * rows + r, 1), 0:ph]
            ei = e_ref[pl.ds(q * rows + r, 1), ph:ew]
            out += [ar * hr - ai * hi + er, ar * hi + ai * hr + ei]
        return tuple(out)

    zero = jnp.zeros((1, ph), F32)
    lax.fori_loop(0, rows, body, (zero,) * (2 * nseq))
    tw = 2 * LANES
    y = jnp.concatenate([jnp.dot(xb[:, :c0 + tw], t_ref[0:c0 + tw, c0:c0 + tw], preferred_element_type=F32)
                         for c0 in range(0, L * LANES, tw)], axis=1)
    y = y + lax.dot_general(hin_ref[...].astype(BF16), vt_ref[...], (((1,), (1,)), ((), ())),
                            preferred_element_type=F32)
    y = _gelu_tanh(y + d_ref[0] * xcat)
    for q in range(nseq):
        for l in range(L):
            o_ref[0, pl.ds(q * seq + l, rows, stride=L), :] = y[q * rows:(q + 1) * rows, l * LANES:(l + 1) * LANES]


def _s5_mix(u, tables, nbatch):
    nblk, n, _ = u.shape
    L = S5_CHUNK
    bbx_re, bbx_im, ctx_re, ctx_im, ccm, pw_re, pw_im, d_blk = tables
    sw = bbx_re.shape[2]
    seq = n // nbatch
    nseq = S5_SEQ_PER_STEP if nbatch % S5_SEQ_PER_STEP == 0 else 1
    rows = nseq * (seq // L)
    slab = lambda a: pl.BlockSpec((1,) + a.shape[1:], lambda g, b: (g, 0, 0))
    return pl.pallas_call(
        functools.partial(_s5_kernel, nseq=nseq),
        grid=(nblk, nbatch // nseq),
        in_specs=[pl.BlockSpec((1, nseq * seq, LANES), lambda g, b: (g, b, 0))] + [slab(a) for a in tables],
        out_specs=pl.BlockSpec((1, nseq * seq, LANES), lambda g, b: (g, b, 0)),
        out_shape=jax.ShapeDtypeStruct((nblk, n, LANES), F32),
        scratch_shapes=[
            pltpu.VMEM((L * LANES, L * LANES), BF16),
            pltpu.VMEM((L * LANES, 2 * sw), BF16),
            pltpu.VMEM((L * LANES, 2 * sw), BF16),
            pltpu.VMEM((rows, 2 * sw), F32),
            pltpu.VMEM((rows, 2 * sw), F32),
        ],
        compiler_params=_params(("arbitrary", "arbitrary")),
        name="s5_chunked_scan",
    )(u, *tables)


def _glu_kernel(y_ref, w_ref, b_ref, h_ref, g1_ref, n2g_ref, sc_ref, sh_ref, wrh_ref, wrl_ref, br_ref,
                h3_ref, t_ref, rt_ref):
    d = h_ref.shape[1]
    y = jnp.concatenate([y_ref[g] for g in range(y_ref.shape[0])], axis=1)
    z = jnp.dot(y.astype(BF16), w_ref[...], preferred_element_type=F32) + b_ref[...]
    h3 = h_ref[...] + g1_ref[0] * (z[:, :d] * jax.nn.sigmoid(z[:, d:]))
    h3_ref[...] = h3
    t = _rms_mod(h3, n2g_ref[...], sc_ref[0], sh_ref[0])
    rows = t.shape[0]
    for j in range(TOKEN_ROWS):
        t_ref[pl.ds(j, rows, stride=TOKEN_ROWS), :] = t[:, j * LANES:(j + 1) * LANES]
    t_hi, t_lo = _split_bf16(t)
    logits = (jnp.dot(t_hi, wrh_ref[...], preferred_element_type=F32)
              + jnp.dot(t_lo, wrh_ref[...], preferred_element_type=F32)
              + jnp.dot(t_hi, wrl_ref[...], preferred_element_type=F32)) + br_ref[...]
    lane = lax.broadcasted_iota(jnp.int32, logits.shape, 1).astype(F32)

    def top1(v):
        m = jnp.max(v, axis=-1, keepdims=True)
        return m, jnp.min(jnp.where(v == m, lane, float(LANES)), axis=-1, keepdims=True)

    m1, i1 = top1(logits)
    m2, i2 = top1(jnp.where(lane == i1, -jnp.inf, logits))
    e2 = jnp.exp(m2 - m1)
    den = 1.0 + e2
    rt_ref[...] = jnp.where(lane == 0.0, 1.0 / den,
                            jnp.where(lane == 1.0, e2 / den,
                                      jnp.where(lane == 2.0, i1, jnp.where(lane == 3.0, i2, 0.0))))


def _glu_router(y, w_glu, b_glu, h, g1, n2g, sc2, sh2, w_router, b_router, seq):
    n, d = h.shape
    ne = w_router.shape[1]
    assert ne <= LANES
    tm = _divisor(seq, 512, 8)
    tpb = seq // tm
    bmap = lambda i: (i // tpb, 0, 0)
    tile = pl.BlockSpec((tm, d), lambda i: (i, 0))
    wr_hi, wr_lo = _split_bf16(jnp.pad(w_router, ((0, 0), (0, LANES - ne))))
    br = jnp.pad(b_router, (0, LANES - ne), constant_values=-1e30).reshape(1, LANES)
    return pl.pallas_call(
        _glu_kernel,
        grid=(n // tm,),
        in_specs=[
            pl.BlockSpec((d // LANES, tm, LANES), lambda i: (0, i, 0)),
            _resident((d, 2 * d)),
            _resident((1, 2 * d)),
            tile,
            pl.BlockSpec((1, 1, d), bmap),
            _resident((1, d)),
            pl.BlockSpec((1, 1, d), bmap),
            pl.BlockSpec((1, 1, d), bmap),
            _resident((d, LANES)),
            _resident((d, LANES)),
            _resident((1, LANES)),
        ],
        out_specs=[tile, pl.BlockSpec((tm * TOKEN_ROWS, LANES), lambda i: (i, 0)),
                   pl.BlockSpec((tm, LANES), lambda i: (i, 0))],
        out_shape=[
            jax.ShapeDtypeStruct((n, d), F32),
            jax.ShapeDtypeStruct((n * TOKEN_ROWS, LANES), F32),
            jax.ShapeDtypeStruct((n, LANES), F32),
        ],
        compiler_params=_params(("arbitrary",)),
        name="s5_glu_router",
    )(y, w_glu.astype(BF16), b_glu.reshape(1, 2 * d), h, g1, n2g.reshape(1, d), sc2, sh2,
      wr_hi, wr_lo, br)


def _token_copy(src_ref, src_tok, dst_ref, dst_tok, sem):
    s = pl.multiple_of(src_tok * TOKEN_ROWS, TOKEN_ROWS)
    t = pl.multiple_of(dst_tok * TOKEN_ROWS, TOKEN_ROWS)
    return pltpu.make_async_copy(src_ref.at[pl.ds(s, TOKEN_ROWS), :], dst_ref.at[pl.ds(t, TOKEN_ROWS), :], sem)


def _tokens_wait(src_ref, dst_ref, dst_tok, ntok, sem):
    t = pl.multiple_of(dst_tok * TOKEN_ROWS, TOKEN_ROWS)
    pltpu.make_async_copy(src_ref.at[pl.ds(0, ntok * TOKEN_ROWS), :],
                          dst_ref.at[pl.ds(t, ntok * TOKEN_ROWS), :], sem).wait()


def _expert_kernel(te_ref, tv_ref, nxt_ref, first_ref, t8_ref, wg_ref, wu_ref, wd_ref, o_ref,
                   xbuf_ref, xb_ref, acc_ref, sems, *, fchunk, nf):
    del te_ref
    i = pl.program_id(0)
    f = pl.program_id(1)
    nt = pl.num_programs(0)
    tm = xb_ref.shape[0]
    slot = lax.rem(i, 2)

    def gather(idx_ref, into):
        def body(j, c):
            for u in range(DMA_UNROLL):
                r = j * DMA_UNROLL + u
                _token_copy(t8_ref, idx_ref[0, 0, r], xbuf_ref, into * tm + r, sems.at[into]).start(priority=u % 2)
            return c
        lax.fori_loop(0, tm // DMA_UNROLL, body, 0)

    @pl.when(jnp.logical_and(i == 0, f == 0))
    def _():
        gather(first_ref, 0)

    @pl.when(jnp.logical_and(f == 0, jnp.logical_and(i + 1 < nt, tv_ref[jnp.minimum(i + 1, nt - 1)] == 1)))
    def _():
        gather(nxt_ref, 1 - slot)

    @pl.when(jnp.logical_and(tv_ref[i] == 0, f == nf - 1))
    def _():
        o_ref[...] = jnp.zeros_like(o_ref)

    @pl.when(tv_ref[i] == 1)
    def _():
        @pl.when(f == 0)
        def _():
            _tokens_wait(t8_ref, xbuf_ref, slot * tm, tm, sems.at[slot])
            base = pl.multiple_of(slot * (tm * TOKEN_ROWS), TOKEN_ROWS)
            for j in range(TOKEN_ROWS):
                xb_ref[:, j * LANES:(j + 1) * LANES] = (
                    xbuf_ref[pl.ds(base + j, tm, stride=TOKEN_ROWS), :].astype(BF16))

        x = xb_ref[...]
        tf = wg_ref.shape[2]
        y = None
        for f0 in range(0, tf, fchunk):
            f1 = min(f0 + fchunk, tf)
            g = jnp.dot(x, wg_ref[0, :, f0:f1], preferred_element_type=F32)
            u = jnp.dot(x, wu_ref[0, :, f0:f1], preferred_element_type=F32)
            a = (_silu(g) * u).astype(BF16)
            yy = jnp.dot(a, wd_ref[0, f0:f1, :], preferred_element_type=F32)
            y = yy if y is None else y + yy

        def emit(total):
            for j in range(TOKEN_ROWS):
                o_ref[pl.ds(j, tm, stride=TOKEN_ROWS), :] = total[:, j * LANES:(j + 1) * LANES]

        if nf == 1:
            emit(y)
        else:
            @pl.when(f == 0)
            def _():
                acc_ref[...] = y

            @pl.when(jnp.logical_and(f > 0, f < nf - 1))
            def _():
                acc_ref[...] += y

            @pl.when(f == nf - 1)
            def _():
                emit(acc_ref[...] + y)


def _experts(t8, src_tok, tile_expert, tile_valid, wg, wu, wd, tm):
    n_tiles = src_tok.shape[0]
    d = TOKEN_ROWS * LANES
    dff = wg.shape[2]
    tf = _divisor(dff, 1792, 256)
    nf = dff // tf
    assert tm % (nf * DMA_UNROLL) == 0
    fidx = lambda i, f, tv: f * tv[i] + (nf - 1) * (1 - tv[i])
    smem_tile = lambda imap: pl.BlockSpec((1, 1, tm), imap, memory_space=pltpu.SMEM)
    grid_spec = pltpu.PrefetchScalarGridSpec(
        num_scalar_prefetch=2,
        grid=(n_tiles, nf),
        in_specs=[
            smem_tile(lambda i, f, te, tv: (jnp.minimum(i + 1, n_tiles - 1), 0, 0)),
            smem_tile(lambda i, f, te, tv: (0, 0, 0)),
            pl.BlockSpec(memory_space=pl.ANY),
            pl.BlockSpec((1, d, tf), lambda i, f, te, tv: (te[i], 0, fidx(i, f, tv))),
            pl.BlockSpec((1, d, tf), lambda i, f, te, tv: (te[i], 0, fidx(i, f, tv))),
            pl.BlockSpec((1, tf, d), lambda i, f, te, tv: (te[i], fidx(i, f, tv), 0)),
        ],
        out_specs=pl.BlockSpec((tm * TOKEN_ROWS, LANES), lambda i, f, te, tv: (i, 0)),
        scratch_shapes=[
            pltpu.VMEM((2 * tm * TOKEN_ROWS, LANES), F32),
            pltpu.VMEM((tm, d), BF16),
            pltpu.VMEM((tm, d), F32),
            pltpu.SemaphoreType.DMA((2,)),
        ],
    )
    return pl.pallas_call(
        functools.partial(_expert_kernel, fchunk=1024, nf=nf),
        grid_spec=grid_spec,
        out_shape=jax.ShapeDtypeStruct((n_tiles * tm * TOKEN_ROWS, LANES), F32),
        compiler_params=_params(("arbitrary", "arbitrary")),
        name="moe_experts",
    )(tile_expert, tile_valid, src_tok, src_tok, t8, wg, wu, wd)


def _combine_kernel(nxt_ref, first_ref, ys_ref, h_ref, rt_ref, g2_ref, fg_ref, o_ref, ybuf_ref, sems):
    i = pl.program_id(0)
    ns = pl.num_programs(0)
    tc = h_ref.shape[0]
    npair = TOP_K * tc
    slot = lax.rem(i, 2)

    def gather(idx_ref, into):
        def body(j, c):
            for u in range(DMA_UNROLL):
                r = j * DMA_UNROLL + u
                for k in range(TOP_K):
                    _token_copy(ys_ref, idx_ref[0, 0, TOP_K * r + k], ybuf_ref, into * npair + k * tc + r,
                                sems.at[into]).start(priority=k % 2)
            return c
        lax.fori_loop(0, tc // DMA_UNROLL, body, 0)

    @pl.when(i == 0)
    def _():
        gather(first_ref, 0)

    @pl.when(i + 1 < ns)
    def _():
        gather(nxt_ref, 1 - slot)

    _tokens_wait(ys_ref, ybuf_ref, slot * npair, npair, sems.at[slot])
    rt = rt_ref[...]
    base = pl.multiple_of(slot * (npair * TOKEN_ROWS), TOKEN_ROWS)
    pieces = []
    for j in range(TOKEN_ROWS):
        yj = None
        for k in range(TOP_K):
            v = ybuf_ref[pl.ds(base + k * tc * TOKEN_ROWS + j, tc, stride=TOKEN_ROWS), :]
            yj = rt[:, k:k + 1] * v if yj is None else yj + rt[:, k:k + 1] * v
        pieces.append(yj)
    y = jnp.concatenate(pieces, axis=1)
    h4 = h_ref[...] + g2_ref[0] * y
    ms = jnp.mean(h4 * h4, axis=-1, keepdims=True)
    o_ref[...] = h4 * lax.rsqrt(ms + NORM_EPS) * fg_ref[...]


def _combine(ys8, dest, h2d, rt, g2, final_g, seq):
    n, d = h2d.shape
    tc = _divisor(seq, 512, DMA_UNROLL)
    tpb = seq // tc
    ns = n // tc
    dest3 = dest.reshape(ns, 1, TOP_K * tc)
    smem_tile = lambda imap: pl.BlockSpec((1, 1, TOP_K * tc), imap, memory_space=pltpu.SMEM)
    return pl.pallas_call(
        _combine_kernel,
        grid=(ns,),
        in_specs=[
            smem_tile(lambda i: (jnp.minimum(i + 1, ns - 1), 0, 0)),
            smem_tile(lambda i: (0, 0, 0)),
            pl.BlockSpec(memory_space=pl.ANY),
            pl.BlockSpec((tc, d), lambda i: (i, 0)),
            pl.BlockSpec((tc, LANES), lambda i: (i, 0)),
            pl.BlockSpec((1, 1, d), lambda i: (i // tpb, 0, 0)),
            _resident((1, d)),
        ],
        out_specs=pl.BlockSpec((tc, d), lambda i: (i, 0)),
        out_shape=jax.ShapeDtypeStruct((n, d), F32),
        scratch_shapes=[pltpu.VMEM((2 * TOP_K * tc * TOKEN_ROWS, LANES), F32), pltpu.SemaphoreType.DMA((2,))],
        compiler_params=_params(("arbitrary",)),
        name="moe_combine_norm",
    )(dest3, dest3, ys8, h2d, rt, g2, final_g.reshape(1, d))


def _route(rt, n_experts, tm):
    n = rt.shape[0]
    npairs = n * TOP_K
    pair_expert = rt[:, 2:2 + TOP_K].astype(jnp.int32).reshape(npairs)
    onehot = (pair_expert[:, None] == jnp.arange(n_experts, dtype=jnp.int32)[None, :]).astype(jnp.int32)
    csum = jnp.cumsum(onehot, axis=0)
    rank = jnp.sum(onehot * csum, axis=1) - 1
    counts = csum[-1]
    tiles = (counts + tm - 1) // tm
    tile_end = jnp.cumsum(tiles)
    tile_start = tile_end - tiles
    dest = jnp.sum(onehot * tile_start[None, :], axis=1) * tm + rank
    n_tiles = npairs // tm + n_experts
    ti = jnp.arange(n_tiles, dtype=jnp.int32)
    n_active = tile_end[-1]
    tile_valid = (ti < n_active).astype(jnp.int32)
    last_used = jnp.minimum(ti, n_active - 1)
    expert_of = jnp.sum((last_used[:, None] >= tile_end[None, :]).astype(jnp.int32), axis=1)
    tile_expert = jnp.minimum(expert_of, n_experts - 1)
    order_tok = (jnp.sort(pair_expert * npairs + jnp.arange(npairs, dtype=jnp.int32)) % npairs) // TOP_K
    nrows = n_tiles * tm
    count_start = jnp.cumsum(counts) - counts
    padded = jnp.concatenate([jnp.zeros((nrows,), jnp.int32), order_tok, jnp.zeros((nrows,), jnp.int32)])
    row = jnp.arange(nrows, dtype=jnp.int32)
    src_tok = jnp.zeros((nrows,), jnp.int32)
    for e in range(n_experts):
        first_row = tile_start[e] * tm
        shifted = lax.dynamic_slice(padded, (nrows + count_start[e] - first_row,), (nrows,))
        mine = jnp.logical_and(row >= first_row, row < first_row + counts[e])
        src_tok = jnp.where(mine, shifted, src_tok)
    return dest.astype(jnp.int32), src_tok.reshape(n_tiles, 1, tm), tile_expert, tile_valid


def kernel(x, c, mod_w, mod_b, norm1_g, norm2_g, conv_w_pw1, conv_b_pw1, conv_w_dw, conv_b_dw, conv_ln_g, conv_ln_b, conv_w_pw2, conv_b_pw2, ssm_a_re, ssm_a_im, ssm_log_dt, ssm_b_re, ssm_b_im, ssm_c_re, ssm_c_im, ssm_d, ssm_w_glu, ssm_b_glu, ffn_w_gate, ffn_w_up, ffn_w_down, moe_w_router, moe_b_router, moe_w_gate, moe_w_up, moe_w_down, final_norm_g):
    nb, seq, d = x.shape
    n = nb * seq
    assert mod_w.shape[0] == 2 and seq % S5_CHUNK == 0 and d == TOKEN_ROWS * LANES
    n_experts = moe_w_router.shape[-1]

    mod = _modulation(c, mod_w, mod_b)
    parts = [[mod[i, :, k * d:(k + 1) * d].reshape(nb, 1, d) for k in range(6)] for i in range(2)]
    sh1a, sc1a, g1a, sh2a, sc2a, g2a = parts[0]
    sh1b, sc1b, g1b, sh2b, sc2b, g2b = parts[1]

    u, w_pw2, ffn_wg, ffn_wu, ffn_wd, w_glu = _pw1(
        x.reshape(n, d), norm1_g[0], sc1a, sh1a, conv_w_pw1[0], conv_b_pw1[0], seq,
        [conv_w_pw2[0], ffn_w_gate[0], ffn_w_up[0], ffn_w_down[0], ssm_w_glu[0]])
    h1, moe_wg, moe_wu = _conv_block(u.reshape(nb, seq, d), x, conv_w_dw[0], conv_b_dw[0], conv_ln_g[0],
                                     conv_ln_b[0], w_pw2, conv_b_pw2[0], g1a, moe_w_gate[0], moe_w_up[0])
    h2, u1, moe_wd = _dense_ffn(h1.reshape(n, d), norm2_g[0], sc2a, sh2a, g2a, ffn_wg, ffn_wu, ffn_wd,
                                norm1_g[1], sc1b, sh1b, seq, moe_w_down[0])

    tables = _s5_tables(ssm_a_re[0], ssm_a_im[0], ssm_log_dt[0], ssm_b_re[0], ssm_b_im[0],
                        ssm_c_re[0], ssm_c_im[0], ssm_d[0])
    y1 = _s5_mix(u1, tables, nb)
    h3, t8, rt = _glu_router(y1, w_glu, ssm_b_glu[0], h2, g1b, norm2_g[1], sc2b, sh2b,
                             moe_w_router[0], moe_b_router[0], seq)
    tm = 512
    dest, src_tok, tile_expert, tile_valid = _route(rt, n_experts, tm)
    ys8 = _experts(t8, src_tok, tile_expert, tile_valid, moe_wg, moe_wu, moe_wd, tm)
    out = _combine(ys8, dest, h3, rt, g2b, final_norm_g, seq)
    return out.reshape(nb, seq, d)
```

```python
import functools

import jax
import jax.numpy as jnp
from jax import lax
from jax.experimental import pallas as pl
from jax.experimental.pallas import tpu as pltpu

F32 = jnp.float32
BF16 = jnp.bfloat16
HIGHEST = lax.Precision.HIGHEST

NORM_EPS = 1e-6
TOP_K = 2
LANES = 128
S5_CHUNK = 16
S5_SEQ_PER_STEP = 2
CONV_HALO = 32
CONV_ROWS = 64
TOKEN_ROWS = 8
DMA_UNROLL = 8
VMEM_LIMIT = 56 * 2**20


def _params(sem):
    return pltpu.CompilerParams(dimension_semantics=sem, vmem_limit_bytes=VMEM_LIMIT)


def _divisor(n, cap, mult):
    best = None
    for d in range(mult, min(n, cap) + 1, mult):
        if n % d == 0:
            best = d
    assert best is not None, (n, cap, mult)
    return best


def _resident(shape):
    nd = len(shape)
    return pl.BlockSpec(shape, lambda *_: (0,) * nd, pipeline_mode=pl.Buffered(1))


def _rms_mod(x, g, sc, sh):
    ms = jnp.mean(x * x, axis=-1, keepdims=True)
    return (x * lax.rsqrt(ms + NORM_EPS) * g) * (1.0 + sc) + sh


def _silu(x):
    return x * jax.nn.sigmoid(x)


def _gelu_tanh(x):
    c = 0.7978845608028654
    return 0.5 * x * (1.0 + jnp.tanh(c * (x + 0.044715 * (x * x * x))))


def _split_bf16(x):
    hi = x.astype(BF16)
    return hi, (x - hi.astype(F32)).astype(BF16)


def _mod_kernel(c_ref, w_ref, b_ref, o_ref):
    cond = _silu(c_ref[...])
    o_ref[0] = jnp.dot(cond, w_ref[0], preferred_element_type=F32, precision=HIGHEST) + b_ref[0]


def _modulation(c, mod_w, mod_b):
    depth, d, d6 = mod_w.shape
    nb = c.shape[0]
    rows = -(-nb // 8) * 8
    c8 = jnp.pad(c, ((0, rows - nb), (0, 0)))
    tn = _divisor(d6, 1536, LANES)
    out = pl.pallas_call(
        _mod_kernel,
        grid=(depth, d6 // tn),
        in_specs=[
            pl.BlockSpec((rows, d), lambda i, j: (0, 0)),
            pl.BlockSpec((1, d, tn), lambda i, j: (i, 0, j)),
            pl.BlockSpec((1, 1, tn), lambda i, j: (i, 0, j)),
        ],
        out_specs=pl.BlockSpec((1, rows, tn), lambda i, j: (i, 0, j)),
        out_shape=jax.ShapeDtypeStruct((depth, rows, d6), F32),
        compiler_params=_params(("arbitrary", "arbitrary")),
        name="adaln_mod",
    )(c8, mod_w, mod_b.reshape(depth, 1, d6))
    return out[:, :nb, :]


def _cast_rider(w, nsteps, step):
    w2 = w.reshape(-1, w.shape[-1])
    rows, cols = w2.shape
    slabs = max(s for s in range(1, nsteps + 1) if nsteps % s == 0 and rows % (16 * s) == 0)
    rep = nsteps // slabs
    spec = pl.BlockSpec((rows // slabs, cols), lambda *g: (step(*g) // rep, 0))
    return w2, spec, jax.ShapeDtypeStruct((rows, cols), BF16)


def _pw1_kernel(x_ref, g_ref, sc_ref, sh_ref, w_ref, b_ref, *refs):
    nr = (len(refs) - 1) // 2
    o_ref = refs[nr]
    for cw_ref, co_ref in zip(refs[:nr], refs[nr + 1:]):
        co_ref[...] = cw_ref[...].astype(BF16)
    x = x_ref[...]
    d = x.shape[1]
    y = _rms_mod(x, g_ref[...], sc_ref[0], sh_ref[0])
    u = jnp.dot(y.astype(BF16), w_ref[...], preferred_element_type=F32) + b_ref[...]
    o_ref[...] = u[:, :d] * jax.nn.sigmoid(u[:, d:])


def _pw1(x2, norm_g, sc, sh, w, b, seq, riders):
    n, d = x2.shape
    tm = _divisor(seq, 512, 8)
    tpb = seq // tm
    bmap = lambda i: (i // tpb, 0, 0)
    cast = [_cast_rider(r, n // tm, lambda i: i) for r in riders]
    out = pl.pallas_call(
        _pw1_kernel,
        grid=(n // tm,),
        in_specs=[
            pl.BlockSpec((tm, d), lambda i: (i, 0)),
            _resident((1, d)),
            pl.BlockSpec((1, 1, d), bmap),
            pl.BlockSpec((1, 1, d), bmap),
            _resident((d, 2 * d)),
            _resident((1, 2 * d)),
        ] + [c[1] for c in cast],
        out_specs=[pl.BlockSpec((tm, d), lambda i: (i, 0))] + [c[1] for c in cast],
        out_shape=[jax.ShapeDtypeStruct((n, d), F32)] + [c[2] for c in cast],
        compiler_params=_params(("arbitrary",)),
        name="conv_pw1_glu",
    )(x2, norm_g.reshape(1, d), sc, sh, w.astype(BF16), b.reshape(1, 2 * d), *[c[0] for c in cast])
    return [out[0]] + [o.reshape(r.shape) for o, r in zip(out[1:], riders)]


def _conv_kernel(cur_ref, prev_ref, wdw_ref, bdw_ref, lng_ref, lnb_ref, w2_ref, b2_ref, h_ref, g1_ref,
                 cwa_ref, cwb_ref, o_ref, coa_ref, cob_ref, buf_ref, cv_ref, *, taps):
    tm, d = cv_ref.shape
    i = pl.program_id(1)
    coa_ref[...] = cwa_ref[...].astype(BF16)
    cob_ref[...] = cwb_ref[...].astype(BF16)
    buf_ref[0:CONV_HALO, :] = jnp.where(i > 0, prev_ref[0], 0.0)
    buf_ref[CONV_HALO:, :] = cur_ref[0]
    off0 = CONV_HALO - (taps - 1)
    span = CONV_ROWS + CONV_HALO
    for c in range(d // LANES):
        lanes = slice(c * LANES, (c + 1) * LANES)

        def body(k, carry, lanes=lanes):
            r0 = pl.multiple_of(k * CONV_ROWS, CONV_ROWS)
            v = buf_ref[pl.ds(r0, span), lanes]
            acc = jnp.broadcast_to(bdw_ref[:, lanes], (CONV_ROWS, LANES))
            for s in range(8):
                xs = v if s == 0 else pltpu.roll(v, span - s, axis=0)
                for q in range(span // 8):
                    o = 8 * q + s
                    if o < off0 or o > off0 + taps - 1:
                        continue
                    wrow = wdw_ref[o - off0:o - off0 + 1, lanes]
                    acc = acc + wrow * xs[8 * q:8 * q + CONV_ROWS, :]
            cv_ref[pl.ds(r0, CONV_ROWS), lanes] = acc
            return carry

        lax.fori_loop(0, tm // CONV_ROWS, body, 0)
    v = cv_ref[...]
    mu = jnp.mean(v, axis=-1, keepdims=True)
    xc = v - mu
    var = jnp.mean(xc * xc, axis=-1, keepdims=True)
    y = _silu(xc * lax.rsqrt(var + NORM_EPS) * lng_ref[...] + lnb_ref[...])
    z = jnp.dot(y.astype(BF16), w2_ref[...], preferred_element_type=F32) + b2_ref[...]
    o_ref[0] = h_ref[0] + g1_ref[0] * z


def _conv_block(u3, x3, w_dw, b_dw, ln_g, ln_b, w2, b2, g1, rider_a, rider_b):
    nb, seq, d = x3.shape
    taps = w_dw.shape[0]
    assert taps - 1 <= CONV_HALO and d % LANES == 0
    tm = _divisor(seq, 512, CONV_ROWS)
    hb = tm // CONV_HALO
    tps = seq // tm
    wpad = jnp.pad(w_dw, ((0, -taps % 8), (0, 0)))
    row = lambda a: a.reshape(1, d)
    step = lambda b, i: b * tps + i
    ra2, ra_spec, ra_shape = _cast_rider(rider_a, nb * tps, step)
    rb2, rb_spec, rb_shape = _cast_rider(rider_b, nb * tps, step)
    out, cast_a, cast_b = pl.pallas_call(
        functools.partial(_conv_kernel, taps=taps),
        grid=(nb, seq // tm),
        in_specs=[
            pl.BlockSpec((1, tm, d), lambda b, i: (b, i, 0)),
            pl.BlockSpec((1, CONV_HALO, d), lambda b, i: (b, jnp.maximum(i * hb - 1, 0), 0)),
            _resident(wpad.shape),
            _resident((1, d)),
            _resident((1, d)),
            _resident((1, d)),
            _resident((d, d)),
            _resident((1, d)),
            pl.BlockSpec((1, tm, d), lambda b, i: (b, i, 0)),
            pl.BlockSpec((1, 1, d), lambda b, i: (b, 0, 0)),
            ra_spec,
            rb_spec,
        ],
        out_specs=[pl.BlockSpec((1, tm, d), lambda b, i: (b, i, 0)), ra_spec, rb_spec],
        out_shape=[jax.ShapeDtypeStruct((nb, seq, d), F32), ra_shape, rb_shape],
        scratch_shapes=[pltpu.VMEM((tm + CONV_HALO, d), F32), pltpu.VMEM((tm, d), F32)],
        compiler_params=_params(("arbitrary", "arbitrary")),
        name="conv_dw_ln_pw2",
    )(u3, u3, wpad, row(b_dw), row(ln_g), row(ln_b), w2.astype(BF16), row(b2), x3, g1, ra2, rb2)
    return out, cast_a.reshape(rider_a.shape), cast_b.reshape(rider_b.shape)


def _ffn_kernel(h_ref, n2g_ref, sc_ref, sh_ref, g2_ref, wg_ref, wu_ref, wd_ref, n1g_ref, sc1_ref, sh1_ref, cw_ref,
                h2_ref, u_ref, co_ref, *, fchunk):
    co_ref[...] = cw_ref[...].astype(BF16)
    h = h_ref[...]
    t = _rms_mod(h, n2g_ref[...], sc_ref[0], sh_ref[0]).astype(BF16)
    dff = wg_ref.shape[1]
    acc = None
    for f0 in range(0, dff, fchunk):
        f1 = min(f0 + fchunk, dff)
        g = jnp.dot(t, wg_ref[:, f0:f1], preferred_element_type=F32)
        u = jnp.dot(t, wu_ref[:, f0:f1], preferred_element_type=F32)
        a = (_silu(g) * u).astype(BF16)
        y = jnp.dot(a, wd_ref[f0:f1, :], preferred_element_type=F32)
        acc = y if acc is None else acc + y
    h2 = h + g2_ref[0] * acc
    h2_ref[...] = h2
    u = _rms_mod(h2, n1g_ref[...], sc1_ref[0], sh1_ref[0])
    for g in range(u_ref.shape[0]):
        u_ref[g] = u[:, g * LANES:(g + 1) * LANES]


def _dense_ffn(h, n2g, sc2, sh2, g2, wg, wu, wd, n1g, sc1, sh1, seq, rider):
    n, d = h.shape
    nblk = d // LANES
    dff = wg.shape[1]
    tm = _divisor(seq, 512, 8)
    tpb = seq // tm
    bmap = lambda i: (i // tpb, 0, 0)
    tile = pl.BlockSpec((tm, d), lambda i: (i, 0))
    rider2, rider_spec, rider_shape = _cast_rider(rider, n // tm, lambda i: i)
    h2, u, casted = pl.pallas_call(
        functools.partial(_ffn_kernel, fchunk=512),
        grid=(n // tm,),
        in_specs=[
            tile,
            _resident((1, d)),
            pl.BlockSpec((1, 1, d), bmap),
            pl.BlockSpec((1, 1, d), bmap),
            pl.BlockSpec((1, 1, d), bmap),
            _resident((d, dff)),
            _resident((d, dff)),
            _resident((dff, d)),
            _resident((1, d)),
            pl.BlockSpec((1, 1, d), bmap),
            pl.BlockSpec((1, 1, d), bmap),
            rider_spec,
        ],
        out_specs=[tile, pl.BlockSpec((nblk, tm, LANES), lambda i: (0, i, 0)), rider_spec],
        out_shape=[jax.ShapeDtypeStruct((n, d), F32), jax.ShapeDtypeStruct((nblk, n, LANES), F32), rider_shape],
        compiler_params=_params(("arbitrary",)),
        name="dense_swiglu",
    )(h, n2g.reshape(1, d), sc2, sh2, g2, wg.astype(BF16), wu.astype(BF16), wd.astype(BF16),
      n1g.reshape(1, d), sc1, sh1, rider2)
    return h2, u, casted.reshape(rider.shape)


def _s5_tables(a_re, a_im, log_dt, b_re, b_im, c_re, c_im, d_skip):
    ng, p = a_re.shape
    c = b_re.shape[-1]
    gpb = LANES // c
    nblk = ng // gpb
    L = S5_CHUNK
    blk = lambda a: a.reshape((nblk, gpb) + a.shape[1:])
    dt = jnp.exp(log_dt)[:, None]
    mag = jnp.exp(dt * a_re)
    ab_re = mag * jnp.cos(dt * a_im)
    ab_im = mag * jnp.sin(dt * a_im)
    den = a_re * a_re + a_im * a_im
    f_re = ((ab_re - 1.0) * a_re + ab_im * a_im) / den
    f_im = (ab_im * a_re - (ab_re - 1.0) * a_im) / den
    bb_re = f_re[..., None] * b_re - f_im[..., None] * b_im
    bb_im = f_re[..., None] * b_im + f_im[..., None] * b_re
    jj = jnp.arange(L + 1, dtype=F32)[None, :, None]
    ang = blk(dt * a_im).reshape(nblk, 1, gpb * p) * jj
    pmag = jnp.exp(blk(dt * a_re).reshape(nblk, 1, gpb * p) * jj)
    pw_re = pmag * jnp.cos(ang)
    pw_im = pmag * jnp.sin(ang)
    bbc_re = jnp.swapaxes(blk(bb_re), 2, 3).reshape(nblk, LANES, p)
    bbc_im = jnp.swapaxes(blk(bb_im), 2, 3).reshape(nblk, LANES, p)
    cc_re = c_re.reshape(nblk, LANES, p)
    cc_im = c_im.reshape(nblk, LANES, p)
    d_blk = jnp.tile(d_skip.reshape(nblk, LANES), (1, L)).reshape(nblk, 1, L * LANES)
    return bbc_re, bbc_im, cc_re, cc_im, pw_re, pw_im, d_blk


def _s5_kernel(u_ref, bbr_ref, bbi_ref, ctr_ref, cti_ref, pwr_ref, pwi_ref, d_ref, o_ref,
               t_ref, we_ref, vt_ref, e_ref, hin_ref, *, nseq):
    L = S5_CHUNK
    ew = e_ref.shape[1]
    rows = e_ref.shape[0] // nseq
    seq = rows * L
    ph = ew // 2

    @pl.when(pl.program_id(1) == 0)
    def _build_operands():
        zblock = jnp.zeros((LANES, LANES), BF16)
        for k in range(L // 2):
            t_ref[(2 * k + 1) * LANES:(2 * k + 2) * LANES, 2 * k * LANES:(2 * k + 1) * LANES] = zblock
        p = bbr_ref.shape[2]
        gpb = ph // p
        shift_c = (LANES // gpb).bit_length() - 1
        shift_p = p.bit_length() - 1
        assert (1 << shift_c) * gpb == LANES and (1 << shift_p) == p

        def expand(a):
            wide = jnp.concatenate([a] * gpb, axis=1)
            rg = lax.shift_right_logical(lax.broadcasted_iota(jnp.int32, wide.shape, 0), shift_c)
            cg = lax.shift_right_logical(lax.broadcasted_iota(jnp.int32, wide.shape, 1), shift_p)
            return jnp.where(rg == cg, wide, 0.0)

        bbr, bbi = expand(bbr_ref[0]), expand(bbi_ref[0])
        ctr, cti = expand(ctr_ref[0]), expand(cti_ref[0])
        cc_hi, cc_lo = _split_bf16(jnp.concatenate([ctr.T, -cti.T], axis=0))
        for lp in range(L):
            j = L - 1 - lp
            p_re, p_im = pwr_ref[0, j:j + 1, :], pwi_ref[0, j:j + 1, :]
            w = jnp.concatenate([bbr * p_re - bbi * p_im, bbr * p_im + bbi * p_re], axis=1)
            w_hi, w_lo = _split_bf16(w)
            we_ref[lp * LANES:(lp + 1) * LANES, :] = w_hi
            k = (jnp.dot(w_hi, cc_hi, preferred_element_type=F32)
                 + jnp.dot(w_lo, cc_hi, preferred_element_type=F32)
                 + jnp.dot(w_hi, cc_lo, preferred_element_type=F32)).astype(BF16)
            for l1 in range(L - j):
                t_ref[l1 * LANES:(l1 + 1) * LANES, (l1 + j) * LANES:(l1 + j + 1) * LANES] = k
        for l in range(L):
            p_re, p_im = pwr_ref[0, l + 1:l + 2, :], pwi_ref[0, l + 1:l + 2, :]
            v = jnp.concatenate([ctr * p_re - cti * p_im, -(ctr * p_im + cti * p_re)], axis=1)
            vt_ref[l * LANES:(l + 1) * LANES, :] = v.astype(BF16)

    xcat = jnp.concatenate(
        [jnp.concatenate([u_ref[0, pl.ds(q * seq + l, rows, stride=L), :] for l in range(L)], axis=1)
         for q in range(nseq)], axis=0)
    xb = xcat.astype(BF16)
    e_ref[...] = jnp.dot(xb, we_ref[...], preferred_element_type=F32)
    ar = pwr_ref[0, L:L + 1, :]
    ai = pwi_ref[0, L:L + 1, :]

    def body(r, carry):
        out = []
        for q in range(nseq):
            hr, hi = carry[2 * q], carry[2 * q + 1]
            hin_ref[pl.ds(q * rows + r, 1), 0:ph] = hr
            hin_ref[pl.ds(q * rows + r, 1), ph:ew] = hi
            er = e_ref[pl.ds(q * rows + r, 1), 0:ph]
            ei = e_ref[pl.ds(q * rows + r, 1), ph:ew]
            out += [ar * hr - ai * hi + er, ar * hi + ai * hr + ei]
        return tuple(out)

    zero = jnp.zeros((1, ph), F32)
    lax.fori_loop(0, rows, body, (zero,) * (2 * nseq))
    tw = 2 * LANES
    y = jnp.concatenate([jnp.dot(xb[:, :c0 + tw], t_ref[0:c0 + tw, c0:c0 + tw], preferred_element_type=F32)
                         for c0 in range(0, L * LANES, tw)], axis=1)
    y = y + lax.dot_general(hin_ref[...].astype(BF16), vt_ref[...], (((1,), (1,)), ((), ())),
                            preferred_element_type=F32)
    y = _gelu_tanh(y + d_ref[0] * xcat)
    for q in range(nseq):
        for l in range(L):
            o_ref[0, pl.ds(q * seq + l, rows, stride=L), :] = y[q * rows:(q + 1) * rows, l * LANES:(l + 1) * LANES]


def _s5_mix(u, tables, nbatch):
    nblk, n, _ = u.shape
    L = S5_CHUNK
    sw = tables[4].shape[2]
    seq = n // nbatch
    nseq = S5_SEQ_PER_STEP if nbatch % S5_SEQ_PER_STEP == 0 else 1
    rows = nseq * (seq // L)
    slab = lambda a: pl.BlockSpec((1,) + a.shape[1:], lambda g, b: (g, 0, 0))
    return pl.pallas_call(
        functools.partial(_s5_kernel, nseq=nseq),
        grid=(nblk, nbatch // nseq),
        in_specs=[pl.BlockSpec((1, nseq * seq, LANES), lambda g, b: (g, b, 0))] + [slab(a) for a in tables],
        out_specs=pl.BlockSpec((1, nseq * seq, LANES), lambda g, b: (g, b, 0)),
        out_shape=jax.ShapeDtypeStruct((nblk, n, LANES), F32),
        scratch_shapes=[
            pltpu.VMEM((L * LANES, L * LANES), BF16),
            pltpu.VMEM((L * LANES, 2 * sw), BF16),
            pltpu.VMEM((L * LANES, 2 * sw), BF16),
            pltpu.VMEM((rows, 2 * sw), F32),
            pltpu.VMEM((rows, 2 * sw), F32),
        ],
        compiler_params=_params(("arbitrary", "arbitrary")),
        name="s5_chunked_scan",
    )(u, *tables)


def _glu_kernel(y_ref, w_ref, b_ref, h_ref, g1_ref, n2g_ref, sc_ref, sh_ref, wrh_ref, wrl_ref, br_ref,
                h3_ref, t_ref, rt_ref):
    d = h_ref.shape[1]
    y = jnp.concatenate([y_ref[g] for g in range(y_ref.shape[0])], axis=1)
    z = jnp.dot(y.astype(BF16), w_ref[...], preferred_element_type=F32) + b_ref[...]
    h3 = h_ref[...] + g1_ref[0] * (z[:, :d] * jax.nn.sigmoid(z[:, d:]))
    h3_ref[...] = h3
    t = _rms_mod(h3, n2g_ref[...], sc_ref[0], sh_ref[0])
    rows = t.shape[0]
    for j in range(TOKEN_ROWS):
        t_ref[pl.ds(j, rows, stride=TOKEN_ROWS), :] = t[:, j * LANES:(j + 1) * LANES]
    t_hi, t_lo = _split_bf16(t)
    logits = (jnp.dot(t_hi, wrh_ref[...], preferred_element_type=F32)
              + jnp.dot(t_lo, wrh_ref[...], preferred_element_type=F32)
              + jnp.dot(t_hi, wrl_ref[...], preferred_element_type=F32)) + br_ref[...]
    lane = lax.broadcasted_iota(jnp.int32, logits.shape, 1).astype(F32)

    def top1(v):
        m = jnp.max(v, axis=-1, keepdims=True)
        return m, jnp.min(jnp.where(v == m, lane, float(LANES)), axis=-1, keepdims=True)

    m1, i1 = top1(logits)
    m2, i2 = top1(jnp.where(lane == i1, -jnp.inf, logits))
    e2 = jnp.exp(m2 - m1)
    den = 1.0 + e2
    rt_ref[...] = jnp.where(lane == 0.0, 1.0 / den,
                            jnp.where(lane == 1.0, e2 / den,
                                      jnp.where(lane == 2.0, i1, jnp.where(lane == 3.0, i2, 0.0))))


def _glu_router(y, w_glu, b_glu, h, g1, n2g, sc2, sh2, w_router, b_router, seq):
    n, d = h.shape
    ne = w_router.shape[1]
    assert ne <= LANES
    tm = _divisor(seq, 512, 8)
    tpb = seq // tm
    bmap = lambda i: (i // tpb, 0, 0)
    tile = pl.BlockSpec((tm, d), lambda i: (i, 0))
    wr_hi, wr_lo = _split_bf16(jnp.pad(w_router, ((0, 0), (0, LANES - ne))))
    br = jnp.pad(b_router, (0, LANES - ne), constant_values=-1e30).reshape(1, LANES)
    return pl.pallas_call(
        _glu_kernel,
        grid=(n // tm,),
        in_specs=[
            pl.BlockSpec((d // LANES, tm, LANES), lambda i: (0, i, 0)),
            _resident((d, 2 * d)),
            _resident((1, 2 * d)),
            tile,
            pl.BlockSpec((1, 1, d), bmap),
            _resident((1, d)),
            pl.BlockSpec((1, 1, d), bmap),
            pl.BlockSpec((1, 1, d), bmap),
            _resident((d, LANES)),
            _resident((d, LANES)),
            _resident((1, LANES)),
        ],
        out_specs=[tile, pl.BlockSpec((tm * TOKEN_ROWS, LANES), lambda i: (i, 0)),
                   pl.BlockSpec((tm, LANES), lambda i: (i, 0))],
        out_shape=[
            jax.ShapeDtypeStruct((n, d), F32),
            jax.ShapeDtypeStruct((n * TOKEN_ROWS, LANES), F32),
            jax.ShapeDtypeStruct((n, LANES), F32),
        ],
        compiler_params=_params(("arbitrary",)),
        name="s5_glu_router",
    )(y, w_glu.astype(BF16), b_glu.reshape(1, 2 * d), h, g1, n2g.reshape(1, d), sc2, sh2,
      wr_hi, wr_lo, br)


def _token_copy(src_ref, src_tok, dst_ref, dst_tok, sem):
    s = pl.multiple_of(src_tok * TOKEN_ROWS, TOKEN_ROWS)
    t = pl.multiple_of(dst_tok * TOKEN_ROWS, TOKEN_ROWS)
    return pltpu.make_async_copy(src_ref.at[pl.ds(s, TOKEN_ROWS), :], dst_ref.at[pl.ds(t, TOKEN_ROWS), :], sem)


def _tokens_wait(src_ref, dst_ref, dst_tok, ntok, sem):
    t = pl.multiple_of(dst_tok * TOKEN_ROWS, TOKEN_ROWS)
    pltpu.make_async_copy(src_ref.at[pl.ds(0, ntok * TOKEN_ROWS), :],
                          dst_ref.at[pl.ds(t, ntok * TOKEN_ROWS), :], sem).wait()


def _expert_kernel(te_ref, tv_ref, nxt_ref, first_ref, t8_ref, wg_ref, wu_ref, wd_ref, o_ref,
                   xbuf_ref, xb_ref, acc_ref, sems, *, fchunk, nf):
    del te_ref
    i = pl.program_id(0)
    f = pl.program_id(1)
    nt = pl.num_programs(0)
    tm = xb_ref.shape[0]
    slot = lax.rem(i, 2)

    def gather(idx_ref, into):
        def body(j, c):
            for u in range(DMA_UNROLL):
                r = j * DMA_UNROLL + u
                _token_copy(t8_ref, idx_ref[0, 0, r], xbuf_ref, into * tm + r, sems.at[into]).start(priority=u % 2)
            return c
        lax.fori_loop(0, tm // DMA_UNROLL, body, 0)

    @pl.when(jnp.logical_and(i == 0, f == 0))
    def _():
        gather(first_ref, 0)

    @pl.when(jnp.logical_and(f == 0, jnp.logical_and(i + 1 < nt, tv_ref[jnp.minimum(i + 1, nt - 1)] == 1)))
    def _():
        gather(nxt_ref, 1 - slot)

    @pl.when(jnp.logical_and(tv_ref[i] == 0, f == nf - 1))
    def _():
        o_ref[...] = jnp.zeros_like(o_ref)

    @pl.when(tv_ref[i] == 1)
    def _():
        @pl.when(f == 0)
        def _():
            _tokens_wait(t8_ref, xbuf_ref, slot * tm, tm, sems.at[slot])
            base = pl.multiple_of(slot * (tm * TOKEN_ROWS), TOKEN_ROWS)
            for j in range(TOKEN_ROWS):
                xb_ref[:, j * LANES:(j + 1) * LANES] = (
                    xbuf_ref[pl.ds(base + j, tm, stride=TOKEN_ROWS), :].astype(BF16))

        x = xb_ref[...]
        tf = wg_ref.shape[2]
        y = None
        for f0 in range(0, tf, fchunk):
            f1 = min(f0 + fchunk, tf)
            g = jnp.dot(x, wg_ref[0, :, f0:f1], preferred_element_type=F32)
            u = jnp.dot(x, wu_ref[0, :, f0:f1], preferred_element_type=F32)
            a = (_silu(g) * u).astype(BF16)
            yy = jnp.dot(a, wd_ref[0, f0:f1, :], preferred_element_type=F32)
            y = yy if y is None else y + yy

        def emit(total):
            for j in range(TOKEN_ROWS):
                o_ref[pl.ds(j, tm, stride=TOKEN_ROWS), :] = total[:, j * LANES:(j + 1) * LANES]

        if nf == 1:
            emit(y)
        else:
            @pl.when(f == 0)
            def _():
                acc_ref[...] = y

            @pl.when(jnp.logical_and(f > 0, f < nf - 1))
            def _():
                acc_ref[...] += y

            @pl.when(f == nf - 1)
            def _():
                emit(acc_ref[...] + y)


def _experts(t8, src_tok, tile_expert, tile_valid, wg, wu, wd, tm):
    n_tiles = src_tok.shape[0]
    d = TOKEN_ROWS * LANES
    dff = wg.shape[2]
    tf = _divisor(dff, 1792, 256)
    nf = dff // tf
    assert tm % (nf * DMA_UNROLL) == 0
    fidx = lambda i, f, tv: f * tv[i] + (nf - 1) * (1 - tv[i])
    smem_tile = lambda imap: pl.BlockSpec((1, 1, tm), imap, memory_space=pltpu.SMEM)
    grid_spec = pltpu.PrefetchScalarGridSpec(
        num_scalar_prefetch=2,
        grid=(n_tiles, nf),
        in_specs=[
            smem_tile(lambda i, f, te, tv: (jnp.minimum(i + 1, n_tiles - 1), 0, 0)),
            smem_tile(lambda i, f, te, tv: (0, 0, 0)),
            pl.BlockSpec(memory_space=pl.ANY),
            pl.BlockSpec((1, d, tf), lambda i, f, te, tv: (te[i], 0, fidx(i, f, tv))),
            pl.BlockSpec((1, d, tf), lambda i, f, te, tv: (te[i], 0, fidx(i, f, tv))),
            pl.BlockSpec((1, tf, d), lambda i, f, te, tv: (te[i], fidx(i, f, tv), 0)),
        ],
        out_specs=pl.BlockSpec((tm * TOKEN_ROWS, LANES), lambda i, f, te, tv: (i, 0)),
        scratch_shapes=[
            pltpu.VMEM((2 * tm * TOKEN_ROWS, LANES), F32),
            pltpu.VMEM((tm, d), BF16),
            pltpu.VMEM((tm, d), F32),
            pltpu.SemaphoreType.DMA((2,)),
        ],
    )
    return pl.pallas_call(
        functools.partial(_expert_kernel, fchunk=1024, nf=nf),
        grid_spec=grid_spec,
        out_shape=jax.ShapeDtypeStruct((n_tiles * tm * TOKEN_ROWS, LANES), F32),
        compiler_params=_params(("arbitrary", "arbitrary")),
        name="moe_experts",
    )(tile_expert, tile_valid, src_tok, src_tok, t8, wg, wu, wd)


def _combine_kernel(nxt_ref, first_ref, ys_ref, h_ref, rt_ref, g2_ref, fg_ref, o_ref, ybuf_ref, sems):
    i = pl.program_id(0)
    ns = pl.num_programs(0)
    tc = h_ref.shape[0]
    npair = TOP_K * tc
    slot = lax.rem(i, 2)

    def gather(idx_ref, into):
        def body(j, c):
            for u in range(DMA_UNROLL):
                r = j * DMA_UNROLL + u
                for k in range(TOP_K):
                    _token_copy(ys_ref, idx_ref[0, 0, TOP_K * r + k], ybuf_ref, into * npair + k * tc + r,
                                sems.at[into]).start(priority=k % 2)
            return c
        lax.fori_loop(0, tc // DMA_UNROLL, body, 0)

    @pl.when(i == 0)
    def _():
        gather(first_ref, 0)

    @pl.when(i + 1 < ns)
    def _():
        gather(nxt_ref, 1 - slot)

    _tokens_wait(ys_ref, ybuf_ref, slot * npair, npair, sems.at[slot])
    rt = rt_ref[...]
    base = pl.multiple_of(slot * (npair * TOKEN_ROWS), TOKEN_ROWS)
    pieces = []
    for j in range(TOKEN_ROWS):
        yj = None
        for k in range(TOP_K):
            v = ybuf_ref[pl.ds(base + k * tc * TOKEN_ROWS + j, tc, stride=TOKEN_ROWS), :]
            yj = rt[:, k:k + 1] * v if yj is None else yj + rt[:, k:k + 1] * v
        pieces.append(yj)
    y = jnp.concatenate(pieces, axis=1)
    h4 = h_ref[...] + g2_ref[0] * y
    ms = jnp.mean(h4 * h4, axis=-1, keepdims=True)
    o_ref[...] = h4 * lax.rsqrt(ms + NORM_EPS) * fg_ref[...]


def _combine(ys8, dest, h2d, rt, g2, final_g, seq):
    n, d = h2d.shape
    tc = _divisor(seq, 512, DMA_UNROLL)
    tpb = seq // tc
    ns = n // tc
    dest3 = dest.reshape(ns, 1, TOP_K * tc)
    smem_tile = lambda imap: pl.BlockSpec((1, 1, TOP_K * tc), imap, memory_space=pltpu.SMEM)
    return pl.pallas_call(
        _combine_kernel,
        grid=(ns,),
        in_specs=[
            smem_tile(lambda i: (jnp.minimum(i + 1, ns - 1), 0, 0)),
            smem_tile(lambda i: (0, 0, 0)),
            pl.BlockSpec(memory_space=pl.ANY),
            pl.BlockSpec((tc, d), lambda i: (i, 0)),
            pl.BlockSpec((tc, LANES), lambda i: (i, 0)),
            pl.BlockSpec((1, 1, d), lambda i: (i // tpb, 0, 0)),
            _resident((1, d)),
        ],
        out_specs=pl.BlockSpec((tc, d), lambda i: (i, 0)),
        out_shape=jax.ShapeDtypeStruct((n, d), F32),
        scratch_shapes=[pltpu.VMEM((2 * TOP_K * tc * TOKEN_ROWS, LANES), F32), pltpu.SemaphoreType.DMA((2,))],
        compiler_params=_params(("arbitrary",)),
        name="moe_combine_norm",
    )(dest3, dest3, ys8, h2d, rt, g2, final_g.reshape(1, d))


def _route(rt, n_experts, tm):
    n = rt.shape[0]
    npairs = n * TOP_K
    pair_expert = rt[:, 2:2 + TOP_K].astype(jnp.int32).reshape(npairs)
    onehot = (pair_expert[:, None] == jnp.arange(n_experts, dtype=jnp.int32)[None, :]).astype(jnp.int32)
    csum = jnp.cumsum(onehot, axis=0)
    rank = jnp.sum(onehot * csum, axis=1) - 1
    counts = csum[-1]
    tiles = (counts + tm - 1) // tm
    tile_end = jnp.cumsum(tiles)
    tile_start = tile_end - tiles
    dest = jnp.sum(onehot * tile_start[None, :], axis=1) * tm + rank
    n_tiles = npairs // tm + n_experts
    ti = jnp.arange(n_tiles, dtype=jnp.int32)
    n_active = tile_end[-1]
    tile_valid = (ti < n_active).astype(jnp.int32)
    last_used = jnp.minimum(ti, n_active - 1)
    expert_of = jnp.sum((last_used[:, None] >= tile_end[None, :]).astype(jnp.int32), axis=1)
    tile_expert = jnp.minimum(expert_of, n_experts - 1)
    order_tok = (jnp.sort(pair_expert * npairs + jnp.arange(npairs, dtype=jnp.int32)) % npairs) // TOP_K
    nrows = n_tiles * tm
    count_start = jnp.cumsum(counts) - counts
    padded = jnp.concatenate([jnp.zeros((nrows,), jnp.int32), order_tok, jnp.zeros((nrows,), jnp.int32)])
    row = jnp.arange(nrows, dtype=jnp.int32)
    src_tok = jnp.zeros((nrows,), jnp.int32)
    for e in range(n_experts):
        first_row = tile_start[e] * tm
        shifted = lax.dynamic_slice(padded, (nrows + count_start[e] - first_row,), (nrows,))
        mine = jnp.logical_and(row >= first_row, row < first_row + counts[e])
        src_tok = jnp.where(mine, shifted, src_tok)
    return dest.astype(jnp.int32), src_tok.reshape(n_tiles, 1, tm), tile_expert, tile_valid


def kernel(x, c, mod_w, mod_b, norm1_g, norm2_g, conv_w_pw1, conv_b_pw1, conv_w_dw, conv_b_dw, conv_ln_g, conv_ln_b, conv_w_pw2, conv_b_pw2, ssm_a_re, ssm_a_im, ssm_log_dt, ssm_b_re, ssm_b_im, ssm_c_re, ssm_c_im, ssm_d, ssm_w_glu, ssm_b_glu, ffn_w_gate, ffn_w_up, ffn_w_down, moe_w_router, moe_b_router, moe_w_gate, moe_w_up, moe_w_down, final_norm_g):
    nb, seq, d = x.shape
    n = nb * seq
    assert mod_w.shape[0] == 2 and seq % S5_CHUNK == 0 and d == TOKEN_ROWS * LANES
    n_experts = moe_w_router.shape[-1]

    mod = _modulation(c, mod_w, mod_b)
    parts = [[mod[i, :, k * d:(k + 1) * d].reshape(nb, 1, d) for k in range(6)] for i in range(2)]
    sh1a, sc1a, g1a, sh2a, sc2a, g2a = parts[0]
    sh1b, sc1b, g1b, sh2b, sc2b, g2b = parts[1]

    u, w_pw2, ffn_wg, ffn_wu, ffn_wd, w_glu = _pw1(
        x.reshape(n, d), norm1_g[0], sc1a, sh1a, conv_w_pw1[0], conv_b_pw1[0], seq,
        [conv_w_pw2[0], ffn_w_gate[0], ffn_w_up[0], ffn_w_down[0], ssm_w_glu[0]])
    h1, moe_wg, moe_wu = _conv_block(u.reshape(nb, seq, d), x, conv_w_dw[0], conv_b_dw[0], conv_ln_g[0],
                                     conv_ln_b[0], w_pw2, conv_b_pw2[0], g1a, moe_w_gate[0], moe_w_up[0])
    h2, u1, moe_wd = _dense_ffn(h1.reshape(n, d), norm2_g[0], sc2a, sh2a, g2a, ffn_wg, ffn_wu, ffn_wd,
                                norm1_g[1], sc1b, sh1b, seq, moe_w_down[0])

    tables = _s5_tables(ssm_a_re[0], ssm_a_im[0], ssm_log_dt[0], ssm_b_re[0], ssm_b_im[0],
                        ssm_c_re[0], ssm_c_im[0], ssm_d[0])
    y1 = _s5_mix(u1, tables, nb)
    h3, t8, rt = _glu_router(y1, w_glu, ssm_b_glu[0], h2, g1b, norm2_g[1], sc2b, sh2b,
                             moe_w_router[0], moe_b_router[0], seq)
    tm = 512
    dest, src_tok, tile_expert, tile_valid = _route(rt, n_experts, tm)
    ys8 = _experts(t8, src_tok, tile_expert, tile_valid, moe_wg, moe_wu, moe_wd, tm)
    out = _combine(ys8, dest, h3, rt, g2b, final_norm_g, seq)
    return out.reshape(nb, seq, d)
```

```python
import functools

import jax
import jax.numpy as jnp
from jax import lax
from jax.experimental import pallas as pl
from jax.experimental.pallas import tpu as pltpu

F32 = jnp.float32
BF16 = jnp.bfloat16
HIGHEST = lax.Precision.HIGHEST

NORM_EPS = 1e-6
TOP_K = 2
LANES = 128
S5_CHUNK = 16
S5_SEQ_PER_STEP = 2
CONV_HALO = 32
CONV_ROWS = 64
TOKEN_ROWS = 8
DMA_UNROLL = 8
VMEM_LIMIT = 56 * 2**20


def _params(sem):
    return pltpu.CompilerParams(dimension_semantics=sem, vmem_limit_bytes=VMEM_LIMIT)


def _divisor(n, cap, mult):
    best = None
    for d in range(mult, min(n, cap) + 1, mult):
        if n % d == 0:
            best = d
    assert best is not None, (n, cap, mult)
    return best


def _resident(shape):
    nd = len(shape)
    return pl.BlockSpec(shape, lambda *_: (0,) * nd, pipeline_mode=pl.Buffered(1))


def _rms_mod(x, g, sc, sh):
    ms = jnp.mean(x * x, axis=-1, keepdims=True)
    return (x * lax.rsqrt(ms + NORM_EPS) * g) * (1.0 + sc) + sh


def _silu(x):
    return x * jax.nn.sigmoid(x)


def _gelu_tanh(x):
    c = 0.7978845608028654
    return 0.5 * x * (1.0 + jnp.tanh(c * (x + 0.044715 * (x * x * x))))


def _split_bf16(x):
    hi = x.astype(BF16)
    return hi, (x - hi.astype(F32)).astype(BF16)


def _mod_kernel(c_ref, w_ref, b_ref, o_ref):
    cond = _silu(c_ref[...])
    o_ref[0] = jnp.dot(cond, w_ref[0], preferred_element_type=F32, precision=HIGHEST) + b_ref[0]


def _modulation(c, mod_w, mod_b):
    depth, d, d6 = mod_w.shape
    nb = c.shape[0]
    rows = -(-nb // 8) * 8
    c8 = jnp.pad(c, ((0, rows - nb), (0, 0)))
    tn = _divisor(d6, 1536, LANES)
    out = pl.pallas_call(
        _mod_kernel,
        grid=(depth, d6 // tn),
        in_specs=[
            pl.BlockSpec((rows, d), lambda i, j: (0, 0)),
            pl.BlockSpec((1, d, tn), lambda i, j: (i, 0, j)),
            pl.BlockSpec((1, 1, tn), lambda i, j: (i, 0, j)),
        ],
        out_specs=pl.BlockSpec((1, rows, tn), lambda i, j: (i, 0, j)),
        out_shape=jax.ShapeDtypeStruct((depth, rows, d6), F32),
        compiler_params=_params(("arbitrary", "arbitrary")),
        name="adaln_mod",
    )(c8, mod_w, mod_b.reshape(depth, 1, d6))
    return out[:, :nb, :]


def _cast_rider(w, nsteps, step):
    w2 = w.reshape(-1, w.shape[-1])
    rows, cols = w2.shape
    slabs = max(s for s in range(1, nsteps + 1) if nsteps % s == 0 and rows % (16 * s) == 0)
    rep = nsteps // slabs
    spec = pl.BlockSpec((rows // slabs, cols), lambda *g: (step(*g) // rep, 0))
    return w2, spec, jax.ShapeDtypeStruct((rows, cols), BF16)


def _pw1_kernel(x_ref, g_ref, sc_ref, sh_ref, w_ref, b_ref, *refs):
    nr = (len(refs) - 1) // 2
    o_ref = refs[nr]
    for cw_ref, co_ref in zip(refs[:nr], refs[nr + 1:]):
        co_ref[...] = cw_ref[...].astype(BF16)
    x = x_ref[...]
    d = x.shape[1]
    y = _rms_mod(x, g_ref[...], sc_ref[0], sh_ref[0])
    u = jnp.dot(y.astype(BF16), w_ref[...], preferred_element_type=F32) + b_ref[...]
    o_ref[...] = u[:, :d] * jax.nn.sigmoid(u[:, d:])


def _pw1(x2, norm_g, sc, sh, w, b, seq, riders):
    n, d = x2.shape
    tm = _divisor(seq, 512, 8)
    tpb = seq // tm
    bmap = lambda i: (i // tpb, 0, 0)
    cast = [_cast_rider(r, n // tm, lambda i: i) for r in riders]
    out = pl.pallas_call(
        _pw1_kernel,
        grid=(n // tm,),
        in_specs=[
            pl.BlockSpec((tm, d), lambda i: (i, 0)),
            _resident((1, d)),
            pl.BlockSpec((1, 1, d), bmap),
            pl.BlockSpec((1, 1, d), bmap),
            _resident((d, 2 * d)),
            _resident((1, 2 * d)),
        ] + [c[1] for c in cast],
        out_specs=[pl.BlockSpec((tm, d), lambda i: (i, 0))] + [c[1] for c in cast],
        out_shape=[jax.ShapeDtypeStruct((n, d), F32)] + [c[2] for c in cast],
        compiler_params=_params(("arbitrary",)),
        name="conv_pw1_glu",
    )(x2, norm_g.reshape(1, d), sc, sh, w.astype(BF16), b.reshape(1, 2 * d), *[c[0] for c in cast])
    return [out[0]] + [o.reshape(r.shape) for o, r in zip(out[1:], riders)]


def _conv_kernel(cur_ref, prev_ref, wdw_ref, bdw_ref, lng_ref, lnb_ref, w2_ref, b2_ref, h_ref, g1_ref,
                 cwa_ref, cwb_ref, o_ref, coa_ref, cob_ref, buf_ref, cv_ref, *, taps):
    tm, d = cv_ref.shape
    i = pl.program_id(1)
    coa_ref[...] = cwa_ref[...].astype(BF16)
    cob_ref[...] = cwb_ref[...].astype(BF16)
    buf_ref[0:CONV_HALO, :] = jnp.where(i > 0, prev_ref[0], 0.0)
    buf_ref[CONV_HALO:, :] = cur_ref[0]
    off0 = CONV_HALO - (taps - 1)
    span = CONV_ROWS + CONV_HALO
    for c in range(d // LANES):
        lanes = slice(c * LANES, (c + 1) * LANES)

        def body(k, carry, lanes=lanes):
            r0 = pl.multiple_of(k * CONV_ROWS, CONV_ROWS)
            v = buf_ref[pl.ds(r0, span), lanes]
            acc = jnp.broadcast_to(bdw_ref[:, lanes], (CONV_ROWS, LANES))
            for s in range(8):
                xs = v if s == 0 else pltpu.roll(v, span - s, axis=0)
                for q in range(span // 8):
                    o = 8 * q + s
                    if o < off0 or o > off0 + taps - 1:
                        continue
                    wrow = wdw_ref[o - off0:o - off0 + 1, lanes]
                    acc = acc + wrow * xs[8 * q:8 * q + CONV_ROWS, :]
            cv_ref[pl.ds(r0, CONV_ROWS), lanes] = acc
            return carry

        lax.fori_loop(0, tm // CONV_ROWS, body, 0)
    v = cv_ref[...]
    mu = jnp.mean(v, axis=-1, keepdims=True)
    xc = v - mu
    var = jnp.mean(xc * xc, axis=-1, keepdims=True)
    y = _silu(xc * lax.rsqrt(var + NORM_EPS) * lng_ref[...] + lnb_ref[...])
    z = jnp.dot(y.astype(BF16), w2_ref[...], preferred_element_type=F32) + b2_ref[...]
    o_ref[0] = h_ref[0] + g1_ref[0] * z


def _conv_block(u3, x3, w_dw, b_dw, ln_g, ln_b, w2, b2, g1, rider_a, rider_b):
    nb, seq, d = x3.shape
    taps = w_dw.shape[0]
    assert taps - 1 <= CONV_HALO and d % LANES == 0
    tm = _divisor(seq, 512, CONV_ROWS)
    hb = tm // CONV_HALO
    tps = seq // tm
    wpad = jnp.pad(w_dw, ((0, -taps % 8), (0, 0)))
    row = lambda a: a.reshape(1, d)
    step = lambda b, i: b * tps + i
    ra2, ra_spec, ra_shape = _cast_rider(rider_a, nb * tps, step)
    rb2, rb_spec, rb_shape = _cast_rider(rider_b, nb * tps, step)
    out, cast_a, cast_b = pl.pallas_call(
        functools.partial(_conv_kernel, taps=taps),
        grid=(nb, seq // tm),
        in_specs=[
            pl.BlockSpec((1, tm, d), lambda b, i: (b, i, 0)),
            pl.BlockSpec((1, CONV_HALO, d), lambda b, i: (b, jnp.maximum(i * hb - 1, 0), 0)),
            _resident(wpad.shape),
            _resident((1, d)),
            _resident((1, d)),
            _resident((1, d)),
            _resident((d, d)),
            _resident((1, d)),
            pl.BlockSpec((1, tm, d), lambda b, i: (b, i, 0)),
            pl.BlockSpec((1, 1, d), lambda b, i: (b, 0, 0)),
            ra_spec,
            rb_spec,
        ],
        out_specs=[pl.BlockSpec((1, tm, d), lambda b, i: (b, i, 0)), ra_spec, rb_spec],
        out_shape=[jax.ShapeDtypeStruct((nb, seq, d), F32), ra_shape, rb_shape],
        scratch_shapes=[pltpu.VMEM((tm + CONV_HALO, d), F32), pltpu.VMEM((tm, d), F32)],
        compiler_params=_params(("arbitrary", "arbitrary")),
        name="conv_dw_ln_pw2",
    )(u3, u3, wpad, row(b_dw), row(ln_g), row(ln_b), w2.astype(BF16), row(b2), x3, g1, ra2, rb2)
    return out, cast_a.reshape(rider_a.shape), cast_b.reshape(rider_b.shape)


def _ffn_kernel(h_ref, n2g_ref, sc_ref, sh_ref, g2_ref, wg_ref, wu_ref, wd_ref, n1g_ref, sc1_ref, sh1_ref, cw_ref,
                h2_ref, u_ref, co_ref, *, fchunk):
    co_ref[...] = cw_ref[...].astype(BF16)
    h = h_ref[...]
    t = _rms_mod(h, n2g_ref[...], sc_ref[0], sh_ref[0]).astype(BF16)
    dff = wg_ref.shape[1]
    acc = None
    for f0 in range(0, dff, fchunk):
        f1 = min(f0 + fchunk, dff)
        g = jnp.dot(t, wg_ref[:, f0:f1], preferred_element_type=F32)
        u = jnp.dot(t, wu_ref[:, f0:f1], preferred_element_type=F32)
        a = (_silu(g) * u).astype(BF16)
        y = jnp.dot(a, wd_ref[f0:f1, :], preferred_element_type=F32)
        acc = y if acc is None else acc + y
    h2 = h + g2_ref[0] * acc
    h2_ref[...] = h2
    u = _rms_mod(h2, n1g_ref[...], sc1_ref[0], sh1_ref[0])
    for g in range(u_ref.shape[0]):
        u_ref[g] = u[:, g * LANES:(g + 1) * LANES]


def _dense_ffn(h, n2g, sc2, sh2, g2, wg, wu, wd, n1g, sc1, sh1, seq, rider):
    n, d = h.shape
    nblk = d // LANES
    dff = wg.shape[1]
    tm = _divisor(seq, 512, 8)
    tpb = seq // tm
    bmap = lambda i: (i // tpb, 0, 0)
    tile = pl.BlockSpec((tm, d), lambda i: (i, 0))
    rider2, rider_spec, rider_shape = _cast_rider(rider, n // tm, lambda i: i)
    h2, u, casted = pl.pallas_call(
        functools.partial(_ffn_kernel, fchunk=512),
        grid=(n // tm,),
        in_specs=[
            tile,
            _resident((1, d)),
            pl.BlockSpec((1, 1, d), bmap),
            pl.BlockSpec((1, 1, d), bmap),
            pl.BlockSpec((1, 1, d), bmap),
            _resident((d, dff)),
            _resident((d, dff)),
            _resident((dff, d)),
            _resident((1, d)),
            pl.BlockSpec((1, 1, d), bmap),
            pl.BlockSpec((1, 1, d), bmap),
            rider_spec,
        ],
        out_specs=[tile, pl.BlockSpec((nblk, tm, LANES), lambda i: (0, i, 0)), rider_spec],
        out_shape=[jax.ShapeDtypeStruct((n, d), F32), jax.ShapeDtypeStruct((nblk, n, LANES), F32), rider_shape],
        compiler_params=_params(("arbitrary",)),
        name="dense_swiglu",
    )(h, n2g.reshape(1, d), sc2, sh2, g2, wg.astype(BF16), wu.astype(BF16), wd.astype(BF16),
      n1g.reshape(1, d), sc1, sh1, rider2)
    return h2, u, casted.reshape(rider.shape)


def _s5_tables(a_re, a_im, log_dt, b_re, b_im, c_re, c_im, d_skip):
    ng, p = a_re.shape
    c = b_re.shape[-1]
    gpb = LANES // c
    nblk = ng // gpb
    L = S5_CHUNK
    blk = lambda a: a.reshape((nblk, gpb) + a.shape[1:])
    dt = jnp.exp(log_dt)[:, None]
    mag = jnp.exp(dt * a_re)
    ab_re = mag * jnp.cos(dt * a_im)
    ab_im = mag * jnp.sin(dt * a_im)
    den = a_re * a_re + a_im * a_im
    f_re = ((ab_re - 1.0) * a_re + ab_im * a_im) / den
    f_im = (ab_im * a_re - (ab_re - 1.0) * a_im) / den
    bb_re = f_re[..., None] * b_re - f_im[..., None] * b_im
    bb_im = f_re[..., None] * b_im + f_im[..., None] * b_re
    jj = jnp.arange(L + 1, dtype=F32)[None, :, None]
    ang = blk(dt * a_im).reshape(nblk, 1, gpb * p) * jj
    pmag = jnp.exp(blk(dt * a_re).reshape(nblk, 1, gpb * p) * jj)
    pw_re = pmag * jnp.cos(ang)
    pw_im = pmag * jnp.sin(ang)
    bbc_re = jnp.swapaxes(blk(bb_re), 2, 3).reshape(nblk, LANES, p)
    bbc_im = jnp.swapaxes(blk(bb_im), 2, 3).reshape(nblk, LANES, p)
    cc_re = c_re.reshape(nblk, LANES, p)
    cc_im = c_im.reshape(nblk, LANES, p)
    d_blk = jnp.tile(d_skip.reshape(nblk, LANES), (1, L)).reshape(nblk, 1, L * LANES)
    return bbc_re, bbc_im, cc_re, cc_im, pw_re, pw_im, d_blk


def _s5_kernel(u_ref, bbr_ref, bbi_ref, ctr_ref, cti_ref, pwr_ref, pwi_ref, d_ref, o_ref,
               t_ref, we_ref, vt_ref, e_ref, hin_ref, *, nseq):
    L = S5_CHUNK
    ew = e_ref.shape[1]
    rows = e_ref.shape[0] // nseq
    seq = rows * L
    ph = ew // 2

    @pl.when(pl.program_id(1) == 0)
    def _build_operands():
        zblock = jnp.zeros((LANES, LANES), BF16)
        for k in range(L // 2):
            t_ref[(2 * k + 1) * LANES:(2 * k + 2) * LANES, 2 * k * LANES:(2 * k + 1) * LANES] = zblock
        p = bbr_ref.shape[2]
        gpb = ph // p
        shift_c = (LANES // gpb).bit_length() - 1
        shift_p = p.bit_length() - 1
        assert (1 << shift_c) * gpb == LANES and (1 << shift_p) == p

        def expand(a):
            wide = jnp.concatenate([a] * gpb, axis=1)
            rg = lax.shift_right_logical(lax.broadcasted_iota(jnp.int32, wide.shape, 0), shift_c)
            cg = lax.shift_right_logical(lax.broadcasted_iota(jnp.int32, wide.shape, 1), shift_p)
            return jnp.where(rg == cg, wide, 0.0)

        bbr, bbi = expand(bbr_ref[0]), expand(bbi_ref[0])
        ctr, cti = expand(ctr_ref[0]), expand(cti_ref[0])
        cc_hi, cc_lo = _split_bf16(jnp.concatenate([ctr.T, -cti.T], axis=0))
        for lp in range(L):
            j = L - 1 - lp
            p_re, p_im = pwr_ref[0, j:j + 1, :], pwi_ref[0, j:j + 1, :]
            w = jnp.concatenate([bbr * p_re - bbi * p_im, bbr * p_im + bbi * p_re], axis=1)
            w_hi, w_lo = _split_bf16(w)
            we_ref[lp * LANES:(lp + 1) * LANES, :] = w_hi
            k = (jnp.dot(w_hi, cc_hi, preferred_element_type=F32)
                 + jnp.dot(w_lo, cc_hi, preferred_element_type=F32)
                 + jnp.dot(w_hi, cc_lo, preferred_element_type=F32)).astype(BF16)
            for l1 in range(L - j):
                t_ref[l1 * LANES:(l1 + 1) * LANES, (l1 + j) * LANES:(l1 + j + 1) * LANES] = k
        for l in range(L):
            p_re, p_im = pwr_ref[0, l + 1:l + 2, :], pwi_ref[0, l + 1:l + 2, :]
            v = jnp.concatenate([ctr * p_re - cti * p_im, -(ctr * p_im + cti * p_re)], axis=1)
            vt_ref[:, l * LANES:(l + 1) * LANES] = v.T.astype(BF16)

    xcat = jnp.concatenate(
        [jnp.concatenate([u_ref[0, pl.ds(q * seq + l, rows, stride=L), :] for l in range(L)], axis=1)
         for q in range(nseq)], axis=0)
    xb = xcat.astype(BF16)
    e_ref[...] = jnp.dot(xb, we_ref[...], preferred_element_type=F32)
    ar = pwr_ref[0, L:L + 1, :]
    ai = pwi_ref[0, L:L + 1, :]

    def body(r, carry):
        out = []
        for q in range(nseq):
            hr, hi = carry[2 * q], carry[2 * q + 1]
            hin_ref[pl.ds(q * rows + r, 1), 0:ph] = hr
            hin_ref[pl.ds(q * rows + r, 1), ph:ew] = hi
            er = e_ref[pl.ds(q * rows + r, 1), 0:ph]
            ei = e_ref[pl.ds(q * rows + r, 1), ph:ew]
            out += [ar * hr - ai * hi + er, ar * hi + ai * hr + ei]
        return tuple(out)

    zero = jnp.zeros((1, ph), F32)
    lax.fori_loop(0, rows, body, (zero,) * (2 * nseq))
    tw = 2 * LANES
    y = jnp.concatenate([jnp.dot(xb[:, :c0 + tw], t_ref[0:c0 + tw, c0:c0 + tw], preferred_element_type=F32)
                         for c0 in range(0, L * LANES, tw)], axis=1)
    y = y + jnp.dot(hin_ref[...].astype(BF16), vt_ref[...], preferred_element_type=F32)
    y = _gelu_tanh(y + d_ref[0] * xcat)
    for q in range(nseq):
        for l in range(L):
            o_ref[0, pl.ds(q * seq + l, rows, stride=L), :] = y[q * rows:(q + 1) * rows, l * LANES:(l + 1) * LANES]


def _s5_mix(u, tables, nbatch):
    nblk, n, _ = u.shape
    L = S5_CHUNK
    sw = tables[4].shape[2]
    seq = n // nbatch
    nseq = S5_SEQ_PER_STEP if nbatch % S5_SEQ_PER_STEP == 0 else 1
    rows = nseq * (seq // L)
    slab = lambda a: pl.BlockSpec((1,) + a.shape[1:], lambda g, b: (g, 0, 0))
    return pl.pallas_call(
        functools.partial(_s5_kernel, nseq=nseq),
        grid=(nblk, nbatch // nseq),
        in_specs=[pl.BlockSpec((1, nseq * seq, LANES), lambda g, b: (g, b, 0))] + [slab(a) for a in tables],
        out_specs=pl.BlockSpec((1, nseq * seq, LANES), lambda g, b: (g, b, 0)),
        out_shape=jax.ShapeDtypeStruct((nblk, n, LANES), F32),
        scratch_shapes=[
            pltpu.VMEM((L * LANES, L * LANES), BF16),
            pltpu.VMEM((L * LANES, 2 * sw), BF16),
            pltpu.VMEM((2 * sw, L * LANES), BF16),
            pltpu.VMEM((rows, 2 * sw), F32),
            pltpu.VMEM((rows, 2 * sw), F32),
        ],
        compiler_params=_params(("arbitrary", "arbitrary")),
        name="s5_chunked_scan",
    )(u, *tables)


def _glu_kernel(y_ref, w_ref, b_ref, h_ref, g1_ref, n2g_ref, sc_ref, sh_ref, wrh_ref, wrl_ref, br_ref,
                h3_ref, t_ref, rt_ref):
    d = h_ref.shape[1]
    y = jnp.concatenate([y_ref[g] for g in range(y_ref.shape[0])], axis=1)
    z = jnp.dot(y.astype(BF16), w_ref[...], preferred_element_type=F32) + b_ref[...]
    h3 = h_ref[...] + g1_ref[0] * (z[:, :d] * jax.nn.sigmoid(z[:, d:]))
    h3_ref[...] = h3
    t = _rms_mod(h3, n2g_ref[...], sc_ref[0], sh_ref[0])
    rows = t.shape[0]
    for j in range(TOKEN_ROWS):
        t_ref[pl.ds(j, rows, stride=TOKEN_ROWS), :] = t[:, j * LANES:(j + 1) * LANES]
    t_hi, t_lo = _split_bf16(t)
    logits = (jnp.dot(t_hi, wrh_ref[...], preferred_element_type=F32)
              + jnp.dot(t_lo, wrh_ref[...], preferred_element_type=F32)
              + jnp.dot(t_hi, wrl_ref[...], preferred_element_type=F32)) + br_ref[...]
    lane = lax.broadcasted_iota(jnp.int32, logits.shape, 1).astype(F32)

    def top1(v):
        m = jnp.max(v, axis=-1, keepdims=True)
        return m, jnp.min(jnp.where(v == m, lane, float(LANES)), axis=-1, keepdims=True)

    m1, i1 = top1(logits)
    m2, i2 = top1(jnp.where(lane == i1, -jnp.inf, logits))
    e2 = jnp.exp(m2 - m1)
    den = 1.0 + e2
    rt_ref[...] = jnp.where(lane == 0.0, 1.0 / den,
                            jnp.where(lane == 1.0, e2 / den,
                                      jnp.where(lane == 2.0, i1, jnp.where(lane == 3.0, i2, 0.0))))


def _glu_router(y, w_glu, b_glu, h, g1, n2g, sc2, sh2, w_router, b_router, seq):
    n, d = h.shape
    ne = w_router.shape[1]
    assert ne <= LANES
    tm = _divisor(seq, 512, 8)
    tpb = seq // tm
    bmap = lambda i: (i // tpb, 0, 0)
    tile = pl.BlockSpec((tm, d), lambda i: (i, 0))
    wr_hi, wr_lo = _split_bf16(jnp.pad(w_router, ((0, 0), (0, LANES - ne))))
    br = jnp.pad(b_router, (0, LANES - ne), constant_values=-1e30).reshape(1, LANES)
    return pl.pallas_call(
        _glu_kernel,
        grid=(n // tm,),
        in_specs=[
            pl.BlockSpec((d // LANES, tm, LANES), lambda i: (0, i, 0)),
            _resident((d, 2 * d)),
            _resident((1, 2 * d)),
            tile,
            pl.BlockSpec((1, 1, d), bmap),
            _resident((1, d)),
            pl.BlockSpec((1, 1, d), bmap),
            pl.BlockSpec((1, 1, d), bmap),
            _resident((d, LANES)),
            _resident((d, LANES)),
            _resident((1, LANES)),
        ],
        out_specs=[tile, pl.BlockSpec((tm * TOKEN_ROWS, LANES), lambda i: (i, 0)),
                   pl.BlockSpec((tm, LANES), lambda i: (i, 0))],
        out_shape=[
            jax.ShapeDtypeStruct((n, d), F32),
            jax.ShapeDtypeStruct((n * TOKEN_ROWS, LANES), F32),
            jax.ShapeDtypeStruct((n, LANES), F32),
        ],
        compiler_params=_params(("arbitrary",)),
        name="s5_glu_router",
    )(y, w_glu.astype(BF16), b_glu.reshape(1, 2 * d), h, g1, n2g.reshape(1, d), sc2, sh2,
      wr_hi, wr_lo, br)


def _token_copy(src_ref, src_tok, dst_ref, dst_tok, sem):
    s = pl.multiple_of(src_tok * TOKEN_ROWS, TOKEN_ROWS)
    t = pl.multiple_of(dst_tok * TOKEN_ROWS, TOKEN_ROWS)
    return pltpu.make_async_copy(src_ref.at[pl.ds(s, TOKEN_ROWS), :], dst_ref.at[pl.ds(t, TOKEN_ROWS), :], sem)


def _tokens_wait(src_ref, dst_ref, dst_tok, ntok, sem):
    t = pl.multiple_of(dst_tok * TOKEN_ROWS, TOKEN_ROWS)
    pltpu.make_async_copy(src_ref.at[pl.ds(0, ntok * TOKEN_ROWS), :],
                          dst_ref.at[pl.ds(t, ntok * TOKEN_ROWS), :], sem).wait()


def _expert_kernel(te_ref, tv_ref, nxt_ref, first_ref, t8_ref, wg_ref, wu_ref, wd_ref, o_ref,
                   xbuf_ref, xb_ref, acc_ref, sems, *, fchunk, nf):
    del te_ref
    i = pl.program_id(0)
    f = pl.program_id(1)
    nt = pl.num_programs(0)
    tm = xb_ref.shape[0]
    slot = lax.rem(i, 2)

    def gather(idx_ref, into):
        def body(j, c):
            for u in range(DMA_UNROLL):
                r = j * DMA_UNROLL + u
                _token_copy(t8_ref, idx_ref[0, 0, r], xbuf_ref, into * tm + r, sems.at[into]).start(priority=u % 2)
            return c
        lax.fori_loop(0, tm // DMA_UNROLL, body, 0)

    @pl.when(jnp.logical_and(i == 0, f == 0))
    def _():
        gather(first_ref, 0)

    @pl.when(jnp.logical_and(f == 0, jnp.logical_and(i + 1 < nt, tv_ref[jnp.minimum(i + 1, nt - 1)] == 1)))
    def _():
        gather(nxt_ref, 1 - slot)

    @pl.when(jnp.logical_and(tv_ref[i] == 0, f == nf - 1))
    def _():
        o_ref[...] = jnp.zeros_like(o_ref)

    @pl.when(tv_ref[i] == 1)
    def _():
        @pl.when(f == 0)
        def _():
            _tokens_wait(t8_ref, xbuf_ref, slot * tm, tm, sems.at[slot])
            base = pl.multiple_of(slot * (tm * TOKEN_ROWS), TOKEN_ROWS)
            for j in range(TOKEN_ROWS):
                xb_ref[:, j * LANES:(j + 1) * LANES] = (
                    xbuf_ref[pl.ds(base + j, tm, stride=TOKEN_ROWS), :].astype(BF16))

        x = xb_ref[...]
        tf = wg_ref.shape[2]
        y = None
        for f0 in range(0, tf, fchunk):
            f1 = min(f0 + fchunk, tf)
            g = jnp.dot(x, wg_ref[0, :, f0:f1], preferred_element_type=F32)
            u = jnp.dot(x, wu_ref[0, :, f0:f1], preferred_element_type=F32)
            a = (_silu(g) * u).astype(BF16)
            yy = jnp.dot(a, wd_ref[0, f0:f1, :], preferred_element_type=F32)
            y = yy if y is None else y + yy

        def emit(total):
            for j in range(TOKEN_ROWS):
                o_ref[pl.ds(j, tm, stride=TOKEN_ROWS), :] = total[:, j * LANES:(j + 1) * LANES]

        if nf == 1:
            emit(y)
        else:
            @pl.when(f == 0)
            def _():
                acc_ref[...] = y

            @pl.when(jnp.logical_and(f > 0, f < nf - 1))
            def _():
                acc_ref[...] += y

            @pl.when(f == nf - 1)
            def _():
                emit(acc_ref[...] + y)


def _experts(t8, src_tok, tile_expert, tile_valid, wg, wu, wd, tm):
    n_tiles = src_tok.shape[0]
    d = TOKEN_ROWS * LANES
    dff = wg.shape[2]
    tf = _divisor(dff, 1792, 256)
    nf = dff // tf
    assert tm % (nf * DMA_UNROLL) == 0
    fidx = lambda i, f, tv: f * tv[i] + (nf - 1) * (1 - tv[i])
    smem_tile = lambda imap: pl.BlockSpec((1, 1, tm), imap, memory_space=pltpu.SMEM)
    grid_spec = pltpu.PrefetchScalarGridSpec(
        num_scalar_prefetch=2,
        grid=(n_tiles, nf),
        in_specs=[
            smem_tile(lambda i, f, te, tv: (jnp.minimum(i + 1, n_tiles - 1), 0, 0)),
            smem_tile(lambda i, f, te, tv: (0, 0, 0)),
            pl.BlockSpec(memory_space=pl.ANY),
            pl.BlockSpec((1, d, tf), lambda i, f, te, tv: (te[i], 0, fidx(i, f, tv))),
            pl.BlockSpec((1, d, tf), lambda i, f, te, tv: (te[i], 0, fidx(i, f, tv))),
            pl.BlockSpec((1, tf, d), lambda i, f, te, tv: (te[i], fidx(i, f, tv), 0)),
        ],
        out_specs=pl.BlockSpec((tm * TOKEN_ROWS, LANES), lambda i, f, te, tv: (i, 0)),
        scratch_shapes=[
            pltpu.VMEM((2 * tm * TOKEN_ROWS, LANES), F32),
            pltpu.VMEM((tm, d), BF16),
            pltpu.VMEM((tm, d), F32),
            pltpu.SemaphoreType.DMA((2,)),
        ],
    )
    return pl.pallas_call(
        functools.partial(_expert_kernel, fchunk=1024, nf=nf),
        grid_spec=grid_spec,
        out_shape=jax.ShapeDtypeStruct((n_tiles * tm * TOKEN_ROWS, LANES), F32),
        compiler_params=_params(("arbitrary", "arbitrary")),
        name="moe_experts",
    )(tile_expert, tile_valid, src_tok, src_tok, t8, wg, wu, wd)


def _combine_kernel(nxt_ref, first_ref, ys_ref, h_ref, rt_ref, g2_ref, fg_ref, o_ref, ybuf_ref, sems):
    i = pl.program_id(0)
    ns = pl.num_programs(0)
    tc = h_ref.shape[0]
    npair = TOP_K * tc
    slot = lax.rem(i, 2)

    def gather(idx_ref, into):
        def body(j, c):
            for u in range(DMA_UNROLL):
                r = j * DMA_UNROLL + u
                for k in range(TOP_K):
                    _token_copy(ys_ref, idx_ref[0, 0, TOP_K * r + k], ybuf_ref, into * npair + k * tc + r,
                                sems.at[into]).start(priority=k % 2)
            return c
        lax.fori_loop(0, tc // DMA_UNROLL, body, 0)

    @pl.when(i == 0)
    def _():
        gather(first_ref, 0)

    @pl.when(i + 1 < ns)
    def _():
        gather(nxt_ref, 1 - slot)

    _tokens_wait(ys_ref, ybuf_ref, slot * npair, npair, sems.at[slot])
    rt = rt_ref[...]
    base = pl.multiple_of(slot * (npair * TOKEN_ROWS), TOKEN_ROWS)
    pieces = []
    for j in range(TOKEN_ROWS):
        yj = None
        for k in range(TOP_K):
            v = ybuf_ref[pl.ds(base + k * tc * TOKEN_ROWS + j, tc, stride=TOKEN_ROWS), :]
            yj = rt[:, k:k + 1] * v if yj is None else yj + rt[:, k:k + 1] * v
        pieces.append(yj)
    y = jnp.concatenate(pieces, axis=1)
    h4 = h_ref[...] + g2_ref[0] * y
    ms = jnp.mean(h4 * h4, axis=-1, keepdims=True)
    o_ref[...] = h4 * lax.rsqrt(ms + NORM_EPS) * fg_ref[...]


def _combine(ys8, dest, h2d, rt, g2, final_g, seq):
    n, d = h2d.shape
    tc = _divisor(seq, 512, DMA_UNROLL)
    tpb = seq // tc
    ns = n // tc
    dest3 = dest.reshape(ns, 1, TOP_K * tc)
    smem_tile = lambda imap: pl.BlockSpec((1, 1, TOP_K * tc), imap, memory_space=pltpu.SMEM)
    return pl.pallas_call(
        _combine_kernel,
        grid=(ns,),
        in_specs=[
            smem_tile(lambda i: (jnp.minimum(i + 1, ns - 1), 0, 0)),
            smem_tile(lambda i: (0, 0, 0)),
            pl.BlockSpec(memory_space=pl.ANY),
            pl.BlockSpec((tc, d), lambda i: (i, 0)),
            pl.BlockSpec((tc, LANES), lambda i: (i, 0)),
            pl.BlockSpec((1, 1, d), lambda i: (i // tpb, 0, 0)),
            _resident((1, d)),
        ],
        out_specs=pl.BlockSpec((tc, d), lambda i: (i, 0)),
        out_shape=jax.ShapeDtypeStruct((n, d), F32),
        scratch_shapes=[pltpu.VMEM((2 * TOP_K * tc * TOKEN_ROWS, LANES), F32), pltpu.SemaphoreType.DMA((2,))],
        compiler_params=_params(("arbitrary",)),
        name="moe_combine_norm",
    )(dest3, dest3, ys8, h2d, rt, g2, final_g.reshape(1, d))


def _route(rt, n_experts, tm):
    n = rt.shape[0]
    npairs = n * TOP_K
    pair_expert = rt[:, 2:2 + TOP_K].astype(jnp.int32).reshape(npairs)
    onehot = (pair_expert[:, None] == jnp.arange(n_experts, dtype=jnp.int32)[None, :]).astype(jnp.int32)
    csum = jnp.cumsum(onehot, axis=0)
    rank = jnp.sum(onehot * csum, axis=1) - 1
    counts = csum[-1]
    tiles = (counts + tm - 1) // tm
    tile_end = jnp.cumsum(tiles)
    tile_start = tile_end - tiles
    dest = jnp.sum(onehot * tile_start[None, :], axis=1) * tm + rank
    n_tiles = npairs // tm + n_experts
    ti = jnp.arange(n_tiles, dtype=jnp.int32)
    n_active = tile_end[-1]
    tile_valid = (ti < n_active).astype(jnp.int32)
    last_used = jnp.minimum(ti, n_active - 1)
    expert_of = jnp.sum((last_used[:, None] >= tile_end[None, :]).astype(jnp.int32), axis=1)
    tile_expert = jnp.minimum(expert_of, n_experts - 1)
    order_tok = (jnp.sort(pair_expert * npairs + jnp.arange(npairs, dtype=jnp.int32)) % npairs) // TOP_K
    nrows = n_tiles * tm
    count_start = jnp.cumsum(counts) - counts
    padded = jnp.concatenate([jnp.zeros((nrows,), jnp.int32), order_tok, jnp.zeros((nrows,), jnp.int32)])
    row = jnp.arange(nrows, dtype=jnp.int32)
    src_tok = jnp.zeros((nrows,), jnp.int32)
    for e in range(n_experts):
        first_row = tile_start[e] * tm
        shifted = lax.dynamic_slice(padded, (nrows + count_start[e] - first_row,), (nrows,))
        mine = jnp.logical_and(row >= first_row, row < first_row + counts[e])
        src_tok = jnp.where(mine, shifted, src_tok)
    return dest.astype(jnp.int32), src_tok.reshape(n_tiles, 1, tm), tile_expert, tile_valid


def kernel(x, c, mod_w, mod_b, norm1_g, norm2_g, conv_w_pw1, conv_b_pw1, conv_w_dw, conv_b_dw, conv_ln_g, conv_ln_b, conv_w_pw2, conv_b_pw2, ssm_a_re, ssm_a_im, ssm_log_dt, ssm_b_re, ssm_b_im, ssm_c_re, ssm_c_im, ssm_d, ssm_w_glu, ssm_b_glu, ffn_w_gate, ffn_w_up, ffn_w_down, moe_w_router, moe_b_router, moe_w_gate, moe_w_up, moe_w_down, final_norm_g):
    nb, seq, d = x.shape
    n = nb * seq
    assert mod_w.shape[0] == 2 and seq % S5_CHUNK == 0 and d == TOKEN_ROWS * LANES
    n_experts = moe_w_router.shape[-1]

    mod = _modulation(c, mod_w, mod_b)
    parts = [[mod[i, :, k * d:(k + 1) * d].reshape(nb, 1, d) for k in range(6)] for i in range(2)]
    sh1a, sc1a, g1a, sh2a, sc2a, g2a = parts[0]
    sh1b, sc1b, g1b, sh2b, sc2b, g2b = parts[1]

    u, w_pw2, ffn_wg, ffn_wu, ffn_wd, w_glu = _pw1(
        x.reshape(n, d), norm1_g[0], sc1a, sh1a, conv_w_pw1[0], conv_b_pw1[0], seq,
        [conv_w_pw2[0], ffn_w_gate[0], ffn_w_up[0], ffn_w_down[0], ssm_w_glu[0]])
    h1, moe_wg, moe_wu = _conv_block(u.reshape(nb, seq, d), x, conv_w_dw[0], conv_b_dw[0], conv_ln_g[0],
                                     conv_ln_b[0], w_pw2, conv_b_pw2[0], g1a, moe_w_gate[0], moe_w_up[0])
    h2, u1, moe_wd = _dense_ffn(h1.reshape(n, d), norm2_g[0], sc2a, sh2a, g2a, ffn_wg, ffn_wu, ffn_wd,
                                norm1_g[1], sc1b, sh1b, seq, moe_w_down[0])

    tables = _s5_tables(ssm_a_re[0], ssm_a_im[0], ssm_log_dt[0], ssm_b_re[0], ssm_b_im[0],
                        ssm_c_re[0], ssm_c_im[0], ssm_d[0])
    y1 = _s5_mix(u1, tables, nb)
    h3, t8, rt = _glu_router(y1, w_glu, ssm_b_glu[0], h2, g1b, norm2_g[1], sc2b, sh2b,
                             moe_w_router[0], moe_b_router[0], seq)
    tm = 512
    dest, src_tok, tile_expert, tile_valid = _route(rt, n_experts, tm)
    ys8 = _experts(t8, src_tok, tile_expert, tile_valid, moe_wg, moe_wu, moe_wd, tm)
    out = _combine(ys8, dest, h3, rt, g2b, final_norm_g, seq)
    return out.reshape(nb, seq, d)
```

```python
import functools

import jax
import jax.numpy as jnp
from jax import lax
from jax.experimental import pallas as pl
from jax.experimental.pallas import tpu as pltpu

F32 = jnp.float32
BF16 = jnp.bfloat16
HIGHEST = lax.Precision.HIGHEST

NORM_EPS = 1e-6
TOP_K = 2
LANES = 128
S5_CHUNK = 16
S5_SEQ_PER_STEP = 2
CONV_HALO = 32
CONV_ROWS = 64
TOKEN_ROWS = 8
DMA_UNROLL = 8
VMEM_LIMIT = 56 * 2**20


def _params(sem):
    return pltpu.CompilerParams(dimension_semantics=sem, vmem_limit_bytes=VMEM_LIMIT)


def _divisor(n, cap, mult):
    best = None
    for d in range(mult, min(n, cap) + 1, mult):
        if n % d == 0:
            best = d
    assert best is not None, (n, cap, mult)
    return best


def _resident(shape):
    nd = len(shape)
    return pl.BlockSpec(shape, lambda *_: (0,) * nd, pipeline_mode=pl.Buffered(1))


def _rms_mod(x, g, sc, sh):
    ms = jnp.mean(x * x, axis=-1, keepdims=True)
    return (x * lax.rsqrt(ms + NORM_EPS) * g) * (1.0 + sc) + sh


def _silu(x):
    return x * jax.nn.sigmoid(x)


def _gelu_tanh(x):
    c = 0.7978845608028654
    return 0.5 * x * (1.0 + jnp.tanh(c * (x + 0.044715 * (x * x * x))))


def _split_bf16(x):
    hi = x.astype(BF16)
    return hi, (x - hi.astype(F32)).astype(BF16)


def _mod_kernel(c_ref, w_ref, b_ref, o_ref):
    cond = _silu(c_ref[...])
    o_ref[0] = jnp.dot(cond, w_ref[0], preferred_element_type=F32, precision=HIGHEST) + b_ref[0]


def _modulation(c, mod_w, mod_b):
    depth, d, d6 = mod_w.shape
    nb = c.shape[0]
    rows = -(-nb // 8) * 8
    c8 = jnp.pad(c, ((0, rows - nb), (0, 0)))
    tn = _divisor(d6, 1536, LANES)
    out = pl.pallas_call(
        _mod_kernel,
        grid=(depth, d6 // tn),
        in_specs=[
            pl.BlockSpec((rows, d), lambda i, j: (0, 0)),
            pl.BlockSpec((1, d, tn), lambda i, j: (i, 0, j)),
            pl.BlockSpec((1, 1, tn), lambda i, j: (i, 0, j)),
        ],
        out_specs=pl.BlockSpec((1, rows, tn), lambda i, j: (i, 0, j)),
        out_shape=jax.ShapeDtypeStruct((depth, rows, d6), F32),
        compiler_params=_params(("arbitrary", "arbitrary")),
        name="adaln_mod",
    )(c8, mod_w, mod_b.reshape(depth, 1, d6))
    return out[:, :nb, :]


def _cast_rider(w, nsteps, step):
    w2 = w.reshape(-1, w.shape[-1])
    rows, cols = w2.shape
    slabs = max(s for s in range(1, nsteps + 1) if nsteps % s == 0 and rows % (16 * s) == 0)
    rep = nsteps // slabs
    spec = pl.BlockSpec((rows // slabs, cols), lambda *g: (step(*g) // rep, 0))
    return w2, spec, jax.ShapeDtypeStruct((rows, cols), BF16)


def _pw1_kernel(x_ref, g_ref, sc_ref, sh_ref, w_ref, b_ref, *refs):
    nr = (len(refs) - 1) // 2
    o_ref = refs[nr]
    for cw_ref, co_ref in zip(refs[:nr], refs[nr + 1:]):
        co_ref[...] = cw_ref[...].astype(BF16)
    x = x_ref[...]
    d = x.shape[1]
    y = _rms_mod(x, g_ref[...], sc_ref[0], sh_ref[0])
    u = jnp.dot(y.astype(BF16), w_ref[...], preferred_element_type=F32) + b_ref[...]
    o_ref[...] = u[:, :d] * jax.nn.sigmoid(u[:, d:])


def _pw1(x2, norm_g, sc, sh, w, b, seq, riders):
    n, d = x2.shape
    tm = _divisor(seq, 512, 8)
    tpb = seq // tm
    bmap = lambda i: (i // tpb, 0, 0)
    cast = [_cast_rider(r, n // tm, lambda i: i) for r in riders]
    out = pl.pallas_call(
        _pw1_kernel,
        grid=(n // tm,),
        in_specs=[
            pl.BlockSpec((tm, d), lambda i: (i, 0)),
            _resident((1, d)),
            pl.BlockSpec((1, 1, d), bmap),
            pl.BlockSpec((1, 1, d), bmap),
            _resident((d, 2 * d)),
            _resident((1, 2 * d)),
        ] + [c[1] for c in cast],
        out_specs=[pl.BlockSpec((tm, d), lambda i: (i, 0))] + [c[1] for c in cast],
        out_shape=[jax.ShapeDtypeStruct((n, d), F32)] + [c[2] for c in cast],
        compiler_params=_params(("arbitrary",)),
        name="conv_pw1_glu",
    )(x2, norm_g.reshape(1, d), sc, sh, w.astype(BF16), b.reshape(1, 2 * d), *[c[0] for c in cast])
    return [out[0]] + [o.reshape(r.shape) for o, r in zip(out[1:], riders)]


def _conv_kernel(cur_ref, prev_ref, wdw_ref, bdw_ref, lng_ref, lnb_ref, w2_ref, b2_ref, h_ref, g1_ref,
                 cwa_ref, cwb_ref, o_ref, coa_ref, cob_ref, buf_ref, cv_ref, *, taps):
    tm, d = cv_ref.shape
    i = pl.program_id(1)
    coa_ref[...] = cwa_ref[...].astype(BF16)
    cob_ref[...] = cwb_ref[...].astype(BF16)
    buf_ref[0:CONV_HALO, :] = jnp.where(i > 0, prev_ref[0], 0.0)
    buf_ref[CONV_HALO:, :] = cur_ref[0]
    off0 = CONV_HALO - (taps - 1)
    span = CONV_ROWS + CONV_HALO
    for c in range(d // LANES):
        lanes = slice(c * LANES, (c + 1) * LANES)

        def body(k, carry, lanes=lanes):
            r0 = pl.multiple_of(k * CONV_ROWS, CONV_ROWS)
            v = buf_ref[pl.ds(r0, span), lanes]
            acc = jnp.broadcast_to(bdw_ref[:, lanes], (CONV_ROWS, LANES))
            for s in range(8):
                xs = v if s == 0 else pltpu.roll(v, span - s, axis=0)
                for q in range(span // 8):
                    o = 8 * q + s
                    if o < off0 or o > off0 + taps - 1:
                        continue
                    wrow = wdw_ref[o - off0:o - off0 + 1, lanes]
                    acc = acc + wrow * xs[8 * q:8 * q + CONV_ROWS, :]
            cv_ref[pl.ds(r0, CONV_ROWS), lanes] = acc
            return carry

        lax.fori_loop(0, tm // CONV_ROWS, body, 0)
    v = cv_ref[...]
    mu = jnp.mean(v, axis=-1, keepdims=True)
    xc = v - mu
    var = jnp.mean(xc * xc, axis=-1, keepdims=True)
    y = _silu(xc * lax.rsqrt(var + NORM_EPS) * lng_ref[...] + lnb_ref[...])
    z = jnp.dot(y.astype(BF16), w2_ref[...], preferred_element_type=F32) + b2_ref[...]
    o_ref[0] = h_ref[0] + g1_ref[0] * z


def _conv_block(u3, x3, w_dw, b_dw, ln_g, ln_b, w2, b2, g1, rider_a, rider_b):
    nb, seq, d = x3.shape
    taps = w_dw.shape[0]
    assert taps - 1 <= CONV_HALO and d % LANES == 0
    tm = _divisor(seq, 512, CONV_ROWS)
    hb = tm // CONV_HALO
    tps = seq // tm
    wpad = jnp.pad(w_dw, ((0, -taps % 8), (0, 0)))
    row = lambda a: a.reshape(1, d)
    step = lambda b, i: b * tps + i
    ra2, ra_spec, ra_shape = _cast_rider(rider_a, nb * tps, step)
    rb2, rb_spec, rb_shape = _cast_rider(rider_b, nb * tps, step)
    out, cast_a, cast_b = pl.pallas_call(
        functools.partial(_conv_kernel, taps=taps),
        grid=(nb, seq // tm),
        in_specs=[
            pl.BlockSpec((1, tm, d), lambda b, i: (b, i, 0)),
            pl.BlockSpec((1, CONV_HALO, d), lambda b, i: (b, jnp.maximum(i * hb - 1, 0), 0)),
            _resident(wpad.shape),
            _resident((1, d)),
            _resident((1, d)),
            _resident((1, d)),
            _resident((d, d)),
            _resident((1, d)),
            pl.BlockSpec((1, tm, d), lambda b, i: (b, i, 0)),
            pl.BlockSpec((1, 1, d), lambda b, i: (b, 0, 0)),
            ra_spec,
            rb_spec,
        ],
        out_specs=[pl.BlockSpec((1, tm, d), lambda b, i: (b, i, 0)), ra_spec, rb_spec],
        out_shape=[jax.ShapeDtypeStruct((nb, seq, d), F32), ra_shape, rb_shape],
        scratch_shapes=[pltpu.VMEM((tm + CONV_HALO, d), F32), pltpu.VMEM((tm, d), F32)],
        compiler_params=_params(("arbitrary", "arbitrary")),
        name="conv_dw_ln_pw2",
    )(u3, u3, wpad, row(b_dw), row(ln_g), row(ln_b), w2.astype(BF16), row(b2), x3, g1, ra2, rb2)
    return out, cast_a.reshape(rider_a.shape), cast_b.reshape(rider_b.shape)


def _ffn_kernel(h_ref, n2g_ref, sc_ref, sh_ref, g2_ref, wg_ref, wu_ref, wd_ref, n1g_ref, sc1_ref, sh1_ref, cw_ref,
                h2_ref, u_ref, co_ref, *, fchunk):
    co_ref[...] = cw_ref[...].astype(BF16)
    h = h_ref[...]
    t = _rms_mod(h, n2g_ref[...], sc_ref[0], sh_ref[0]).astype(BF16)
    dff = wg_ref.shape[1]
    acc = None
    for f0 in range(0, dff, fchunk):
        f1 = min(f0 + fchunk, dff)
        g = jnp.dot(t, wg_ref[:, f0:f1], preferred_element_type=F32)
        u = jnp.dot(t, wu_ref[:, f0:f1], preferred_element_type=F32)
        a = (_silu(g) * u).astype(BF16)
        y = jnp.dot(a, wd_ref[f0:f1, :], preferred_element_type=F32)
        acc = y if acc is None else acc + y
    h2 = h + g2_ref[0] * acc
    h2_ref[...] = h2
    u = _rms_mod(h2, n1g_ref[...], sc1_ref[0], sh1_ref[0])
    for g in range(u_ref.shape[0]):
        u_ref[g] = u[:, g * LANES:(g + 1) * LANES]


def _dense_ffn(h, n2g, sc2, sh2, g2, wg, wu, wd, n1g, sc1, sh1, seq, rider):
    n, d = h.shape
    nblk = d // LANES
    dff = wg.shape[1]
    tm = _divisor(seq, 512, 8)
    tpb = seq // tm
    bmap = lambda i: (i // tpb, 0, 0)
    tile = pl.BlockSpec((tm, d), lambda i: (i, 0))
    rider2, rider_spec, rider_shape = _cast_rider(rider, n // tm, lambda i: i)
    h2, u, casted = pl.pallas_call(
        functools.partial(_ffn_kernel, fchunk=512),
        grid=(n // tm,),
        in_specs=[
            tile,
            _resident((1, d)),
            pl.BlockSpec((1, 1, d), bmap),
            pl.BlockSpec((1, 1, d), bmap),
            pl.BlockSpec((1, 1, d), bmap),
            _resident((d, dff)),
            _resident((d, dff)),
            _resident((dff, d)),
            _resident((1, d)),
            pl.BlockSpec((1, 1, d), bmap),
            pl.BlockSpec((1, 1, d), bmap),
            rider_spec,
        ],
        out_specs=[tile, pl.BlockSpec((nblk, tm, LANES), lambda i: (0, i, 0)), rider_spec],
        out_shape=[jax.ShapeDtypeStruct((n, d), F32), jax.ShapeDtypeStruct((nblk, n, LANES), F32), rider_shape],
        compiler_params=_params(("arbitrary",)),
        name="dense_swiglu",
    )(h, n2g.reshape(1, d), sc2, sh2, g2, wg.astype(BF16), wu.astype(BF16), wd.astype(BF16),
      n1g.reshape(1, d), sc1, sh1, rider2)
    return h2, u, casted.reshape(rider.shape)


def _s5_tables(a_re, a_im, log_dt, b_re, b_im, c_re, c_im, d_skip):
    ng, p = a_re.shape
    c = b_re.shape[-1]
    gpb = LANES // c
    nblk = ng // gpb
    L = S5_CHUNK
    blk = lambda a: a.reshape((nblk, gpb) + a.shape[1:])
    dt = jnp.exp(log_dt)[:, None]
    mag = jnp.exp(dt * a_re)
    ab_re = mag * jnp.cos(dt * a_im)
    ab_im = mag * jnp.sin(dt * a_im)
    den = a_re * a_re + a_im * a_im
    f_re = ((ab_re - 1.0) * a_re + ab_im * a_im) / den
    f_im = (ab_im * a_re - (ab_re - 1.0) * a_im) / den
    bb_re = f_re[..., None] * b_re - f_im[..., None] * b_im
    bb_im = f_re[..., None] * b_im + f_im[..., None] * b_re
    jj = jnp.arange(L + 1, dtype=F32)[None, :, None]
    ang = blk(dt * a_im).reshape(nblk, 1, gpb * p) * jj
    pmag = jnp.exp(blk(dt * a_re).reshape(nblk, 1, gpb * p) * jj)
    pw_re = pmag * jnp.cos(ang)
    pw_im = pmag * jnp.sin(ang)
    bbc_re = jnp.swapaxes(blk(bb_re), 2, 3).reshape(nblk, LANES, p)
    bbc_im = jnp.swapaxes(blk(bb_im), 2, 3).reshape(nblk, LANES, p)
    cc_re = c_re.reshape(nblk, LANES, p)
    cc_im = c_im.reshape(nblk, LANES, p)
    d_blk = jnp.tile(d_skip.reshape(nblk, LANES), (1, L)).reshape(nblk, 1, L * LANES)
    return bbc_re, bbc_im, cc_re, cc_im, pw_re, pw_im, d_blk


def _s5_kernel(u_ref, bbr_ref, bbi_ref, ctr_ref, cti_ref, pwr_ref, pwi_ref, d_ref, o_ref,
               t_ref, we_ref, vt_ref, e_ref, hin_ref, *, nseq):
    L = S5_CHUNK
    ew = e_ref.shape[1]
    rows = e_ref.shape[0] // nseq
    seq = rows * L
    ph = ew // 2

    @pl.when(pl.program_id(1) == 0)
    def _build_operands():
        zblock = jnp.zeros((LANES, LANES), BF16)
        for k in range(L // 2):
            t_ref[(2 * k + 1) * LANES:(2 * k + 2) * LANES, 2 * k * LANES:(2 * k + 1) * LANES] = zblock
        p = bbr_ref.shape[2]
        gpb = ph // p
        shift_c = (LANES // gpb).bit_length() - 1
        shift_p = p.bit_length() - 1
        assert (1 << shift_c) * gpb == LANES and (1 << shift_p) == p

        def expand(a):
            wide = jnp.concatenate([a] * gpb, axis=1)
            rg = lax.shift_right_logical(lax.broadcasted_iota(jnp.int32, wide.shape, 0), shift_c)
            cg = lax.shift_right_logical(lax.broadcasted_iota(jnp.int32, wide.shape, 1), shift_p)
            return jnp.where(rg == cg, wide, 0.0)

        bbr, bbi = expand(bbr_ref[0]), expand(bbi_ref[0])
        ctr, cti = expand(ctr_ref[0]), expand(cti_ref[0])
        cc_hi, cc_lo = _split_bf16(jnp.concatenate([ctr.T, -cti.T], axis=0))
        for lp in range(L):
            j = L - 1 - lp
            p_re, p_im = pwr_ref[0, j:j + 1, :], pwi_ref[0, j:j + 1, :]
            w = jnp.concatenate([bbr * p_re - bbi * p_im, bbr * p_im + bbi * p_re], axis=1)
            w_hi, w_lo = _split_bf16(w)
            we_ref[lp * LANES:(lp + 1) * LANES, :] = w_hi
            k = (jnp.dot(w_hi, cc_hi, preferred_element_type=F32)
                 + jnp.dot(w_lo, cc_hi, preferred_element_type=F32)
                 + jnp.dot(w_hi, cc_lo, preferred_element_type=F32)).astype(BF16)
            for l1 in range(L - j):
                t_ref[l1 * LANES:(l1 + 1) * LANES, (l1 + j) * LANES:(l1 + j + 1) * LANES] = k
        for l in range(L):
            p_re, p_im = pwr_ref[0, l + 1:l + 2, :], pwi_ref[0, l + 1:l + 2, :]
            v = jnp.concatenate([ctr * p_re - cti * p_im, -(ctr * p_im + cti * p_re)], axis=1)
            vt_ref[l * LANES:(l + 1) * LANES, :] = v.astype(BF16)

    xcat = jnp.concatenate(
        [jnp.concatenate([u_ref[0, pl.ds(q * seq + l, rows, stride=L), :] for l in range(L)], axis=1)
         for q in range(nseq)], axis=0)
    xb = xcat.astype(BF16)
    e_ref[...] = jnp.dot(xb, we_ref[...], preferred_element_type=F32)
    ar = pwr_ref[0, L:L + 1, :]
    ai = pwi_ref[0, L:L + 1, :]

    def body(r, carry):
        out = []
        for q in range(nseq):
            hr, hi = carry[2 * q], carry[2 * q + 1]
            hin_ref[pl.ds(q * rows + r, 1), 0:ph] = hr
            hin_ref[pl.ds(q * rows + r, 1), ph:ew] = hi
            er = e_ref[pl.ds(q * rows + r, 1), 0:ph]
            ei = e_ref[pl.ds(q * rows + r, 1), ph:ew]
            out += [ar * hr - ai * hi + er, ar * hi + ai * hr + ei]
        return tuple(out)

    zero = jnp.zeros((1, ph), F32)
    lax.fori_loop(0, rows, body, (zero,) * (2 * nseq))
    tw = 2 * LANES
    y = jnp.concatenate([jnp.dot(xb[:, :c0 + tw], t_ref[0:c0 + tw, c0:c0 + tw], preferred_element_type=F32)
                         for c0 in range(0, L * LANES, tw)], axis=1)
    y = y + lax.dot_general(hin_ref[...].astype(BF16), vt_ref[...], (((1,), (1,)), ((), ())),
                            preferred_element_type=F32)
    y = _gelu_tanh(y + d_ref[0] * xcat)
    for q in range(nseq):
        for l in range(L):
            o_ref[0, pl.ds(q * seq + l, rows, stride=L), :] = y[q * rows:(q + 1) * rows, l * LANES:(l + 1) * LANES]


def _s5_mix(u, tables, nbatch):
    nblk, n, _ = u.shape
    L = S5_CHUNK
    sw = tables[4].shape[2]
    seq = n // nbatch
    nseq = S5_SEQ_PER_STEP if nbatch % S5_SEQ_PER_STEP == 0 else 1
    rows = nseq * (seq // L)
    slab = lambda a: pl.BlockSpec((1,) + a.shape[1:], lambda g, b: (g, 0, 0))
    return pl.pallas_call(
        functools.partial(_s5_kernel, nseq=nseq),
        grid=(nblk, nbatch // nseq),
        in_specs=[pl.BlockSpec((1, nseq * seq, LANES), lambda g, b: (g, b, 0))] + [slab(a) for a in tables],
        out_specs=pl.BlockSpec((1, nseq * seq, LANES), lambda g, b: (g, b, 0)),
        out_shape=jax.ShapeDtypeStruct((nblk, n, LANES), F32),
        scratch_shapes=[
            pltpu.VMEM((L * LANES, L * LANES), BF16),
            pltpu.VMEM((L * LANES, 2 * sw), BF16),
            pltpu.VMEM((L * LANES, 2 * sw), BF16),
            pltpu.VMEM((rows, 2 * sw), F32),
            pltpu.VMEM((rows, 2 * sw), F32),
        ],
        compiler_params=_params(("arbitrary", "arbitrary")),
        name="s5_chunked_scan",
    )(u, *tables)


def _glu_kernel(y_ref, w_ref, b_ref, h_ref, g1_ref, n2g_ref, sc_ref, sh_ref, wrh_ref, wrl_ref, br_ref,
                h3_ref, t_ref, rt_ref):
    d = h_ref.shape[1]
    y = jnp.concatenate([y_ref[g] for g in range(y_ref.shape[0])], axis=1)
    z = jnp.dot(y.astype(BF16), w_ref[...], preferred_element_type=F32) + b_ref[...]
    h3 = h_ref[...] + g1_ref[0] * (z[:, :d] * jax.nn.sigmoid(z[:, d:]))
    h3_ref[...] = h3
    t = _rms_mod(h3, n2g_ref[...], sc_ref[0], sh_ref[0])
    rows = t.shape[0]
    for j in range(TOKEN_ROWS):
        t_ref[pl.ds(j, rows, stride=TOKEN_ROWS), :] = t[:, j * LANES:(j + 1) * LANES]
    t_hi, t_lo = _split_bf16(t)
    logits = (jnp.dot(t_hi, wrh_ref[...], preferred_element_type=F32)
              + jnp.dot(t_lo, wrh_ref[...], preferred_element_type=F32)
              + jnp.dot(t_hi, wrl_ref[...], preferred_element_type=F32)) + br_ref[...]
    lane = lax.broadcasted_iota(jnp.int32, logits.shape, 1).astype(F32)

    def top1(v):
        m = jnp.max(v, axis=-1, keepdims=True)
        return m, jnp.min(jnp.where(v == m, lane, float(LANES)), axis=-1, keepdims=True)

    m1, i1 = top1(logits)
    m2, i2 = top1(jnp.where(lane == i1, -jnp.inf, logits))
    e2 = jnp.exp(m2 - m1)
    den = 1.0 + e2
    rt_ref[...] = jnp.where(lane == 0.0, 1.0 / den,
                            jnp.where(lane == 1.0, e2 / den,
                                      jnp.where(lane == 2.0, i1, jnp.where(lane == 3.0, i2, 0.0))))


def _glu_router(y, w_glu, b_glu, h, g1, n2g, sc2, sh2, w_router, b_router, seq):
    n, d = h.shape
    ne = w_router.shape[1]
    assert ne <= LANES
    tm = _divisor(seq, 512, 8)
    tpb = seq // tm
    bmap = lambda i: (i // tpb, 0, 0)
    tile = pl.BlockSpec((tm, d), lambda i: (i, 0))
    wr_hi, wr_lo = _split_bf16(jnp.pad(w_router, ((0, 0), (0, LANES - ne))))
    br = jnp.pad(b_router, (0, LANES - ne), constant_values=-1e30).reshape(1, LANES)
    return pl.pallas_call(
        _glu_kernel,
        grid=(n // tm,),
        in_specs=[
            pl.BlockSpec((d // LANES, tm, LANES), lambda i: (0, i, 0)),
            _resident((d, 2 * d)),
            _resident((1, 2 * d)),
            tile,
            pl.BlockSpec((1, 1, d), bmap),
            _resident((1, d)),
            pl.BlockSpec((1, 1, d), bmap),
            pl.BlockSpec((1, 1, d), bmap),
            _resident((d, LANES)),
            _resident((d, LANES)),
            _resident((1, LANES)),
        ],
        out_specs=[tile, pl.BlockSpec((tm * TOKEN_ROWS, LANES), lambda i: (i, 0)),
                   pl.BlockSpec((tm, LANES), lambda i: (i, 0))],
        out_shape=[
            jax.ShapeDtypeStruct((n, d), F32),
            jax.ShapeDtypeStruct((n * TOKEN_ROWS, LANES), F32),
            jax.ShapeDtypeStruct((n, LANES), F32),
        ],
        compiler_params=_params(("arbitrary",)),
        name="s5_glu_router",
    )(y, w_glu.astype(BF16), b_glu.reshape(1, 2 * d), h, g1, n2g.reshape(1, d), sc2, sh2,
      wr_hi, wr_lo, br)


def _token_copy(src_ref, src_tok, dst_ref, dst_tok, sem):
    s = pl.multiple_of(src_tok * TOKEN_ROWS, TOKEN_ROWS)
    t = pl.multiple_of(dst_tok * TOKEN_ROWS, TOKEN_ROWS)
    return pltpu.make_async_copy(src_ref.at[pl.ds(s, TOKEN_ROWS), :], dst_ref.at[pl.ds(t, TOKEN_ROWS), :], sem)


def _tokens_wait(src_ref, dst_ref, dst_tok, ntok, sem):
    t = pl.multiple_of(dst_tok * TOKEN_ROWS, TOKEN_ROWS)
    pltpu.make_async_copy(src_ref.at[pl.ds(0, ntok * TOKEN_ROWS), :],
                          dst_ref.at[pl.ds(t, ntok * TOKEN_ROWS), :], sem).wait()


def _expert_kernel(te_ref, tv_ref, nxt_ref, first_ref, t8_ref, wg_ref, wu_ref, wd_ref, o_ref,
                   xbuf_ref, xb_ref, acc_ref, sems, *, fchunk, nf):
    del te_ref
    i = pl.program_id(0)
    f = pl.program_id(1)
    nt = pl.num_programs(0)
    tm = acc_ref.shape[0]
    slot = lax.rem(i, 2)

    def gather(idx_ref, into):
        def body(j, c):
            for u in range(DMA_UNROLL):
                r = j * DMA_UNROLL + u
                _token_copy(t8_ref, idx_ref[0, 0, r], xbuf_ref, into * tm + r, sems.at[into]).start(priority=u % 2)
            return c
        lax.fori_loop(0, tm // DMA_UNROLL, body, 0)

    @pl.when(jnp.logical_and(i == 0, f == 0))
    def _():
        gather(first_ref, 0)

    @pl.when(jnp.logical_and(f == 0, jnp.logical_and(i + 1 < nt, tv_ref[jnp.minimum(i + 1, nt - 1)] == 1)))
    def _():
        gather(nxt_ref, 1 - slot)

    @pl.when(jnp.logical_and(tv_ref[i] == 0, f == nf - 1))
    def _():
        o_ref[...] = jnp.zeros_like(o_ref)

    @pl.when(tv_ref[i] == 1)
    def _():
        tf = wg_ref.shape[2]
        next_used = jnp.logical_and(i + 1 < nt, tv_ref[jnp.minimum(i + 1, nt - 1)] == 1)

        def to_rows(s):
            base = pl.multiple_of(s * (tm * TOKEN_ROWS), TOKEN_ROWS)
            rows0 = pl.multiple_of(s * tm, tm)
            for j in range(TOKEN_ROWS):
                xb_ref[pl.ds(rows0, tm), j * LANES:(j + 1) * LANES] = (
                    xbuf_ref[pl.ds(base + j, tm, stride=TOKEN_ROWS), :].astype(BF16))

        def ffn():
            x = xb_ref[pl.ds(pl.multiple_of(slot * tm, tm), tm), :]
            y = None
            for f0 in range(0, tf, fchunk):
                f1 = min(f0 + fchunk, tf)
                g = jnp.dot(x, wg_ref[0, :, f0:f1], preferred_element_type=F32)
                u = jnp.dot(x, wu_ref[0, :, f0:f1], preferred_element_type=F32)
                a = (_silu(g) * u).astype(BF16)
                yy = jnp.dot(a, wd_ref[0, f0:f1, :], preferred_element_type=F32)
                y = yy if y is None else y + yy
            return y

        def emit(total):
            for j in range(TOKEN_ROWS):
                o_ref[pl.ds(j, tm, stride=TOKEN_ROWS), :] = total[:, j * LANES:(j + 1) * LANES]

        @pl.when(jnp.logical_and(i == 0, f == 0))
        def _():
            _tokens_wait(t8_ref, xbuf_ref, 0, tm, sems.at[0])
            to_rows(0)

        @pl.when(jnp.logical_and(f == nf - 1, next_used))
        def _():
            _tokens_wait(t8_ref, xbuf_ref, (1 - slot) * tm, tm, sems.at[1 - slot])

        if nf == 1:
            y = ffn()
            to_rows(1 - slot)
            emit(y)
        else:
            @pl.when(f == 0)
            def _():
                acc_ref[...] = ffn()

            @pl.when(jnp.logical_and(f > 0, f < nf - 1))
            def _():
                acc_ref[...] += ffn()

            @pl.when(f == nf - 1)
            def _():
                y = ffn()
                to_rows(1 - slot)
                emit(acc_ref[...] + y)


def _experts(t8, src_tok, tile_expert, tile_valid, wg, wu, wd, tm):
    n_tiles = src_tok.shape[0]
    d = TOKEN_ROWS * LANES
    dff = wg.shape[2]
    tf = _divisor(dff, 1792, 256)
    nf = dff // tf
    assert tm % (nf * DMA_UNROLL) == 0
    fidx = lambda i, f, tv: f * tv[i] + (nf - 1) * (1 - tv[i])
    smem_tile = lambda imap: pl.BlockSpec((1, 1, tm), imap, memory_space=pltpu.SMEM)
    grid_spec = pltpu.PrefetchScalarGridSpec(
        num_scalar_prefetch=2,
        grid=(n_tiles, nf),
        in_specs=[
            smem_tile(lambda i, f, te, tv: (jnp.minimum(i + 1, n_tiles - 1), 0, 0)),
            smem_tile(lambda i, f, te, tv: (0, 0, 0)),
            pl.BlockSpec(memory_space=pl.ANY),
            pl.BlockSpec((1, d, tf), lambda i, f, te, tv: (te[i], 0, fidx(i, f, tv))),
            pl.BlockSpec((1, d, tf), lambda i, f, te, tv: (te[i], 0, fidx(i, f, tv))),
            pl.BlockSpec((1, tf, d), lambda i, f, te, tv: (te[i], fidx(i, f, tv), 0)),
        ],
        out_specs=pl.BlockSpec((tm * TOKEN_ROWS, LANES), lambda i, f, te, tv: (i, 0)),
        scratch_shapes=[
            pltpu.VMEM((2 * tm * TOKEN_ROWS, LANES), F32),
            pltpu.VMEM((2 * tm, d), BF16),
            pltpu.VMEM((tm, d), F32),
            pltpu.SemaphoreType.DMA((2,)),
        ],
    )
    return pl.pallas_call(
        functools.partial(_expert_kernel, fchunk=1024, nf=nf),
        grid_spec=grid_spec,
        out_shape=jax.ShapeDtypeStruct((n_tiles * tm * TOKEN_ROWS, LANES), F32),
        compiler_params=_params(("arbitrary", "arbitrary")),
        name="moe_experts",
    )(tile_expert, tile_valid, src_tok, src_tok, t8, wg, wu, wd)


def _combine_kernel(nxt_ref, first_ref, ys_ref, h_ref, rt_ref, g2_ref, fg_ref, o_ref, ybuf_ref, sems):
    i = pl.program_id(0)
    ns = pl.num_programs(0)
    tc = h_ref.shape[0]
    npair = TOP_K * tc
    slot = lax.rem(i, 2)

    def gather(idx_ref, into):
        def body(j, c):
            for u in range(DMA_UNROLL):
                r = j * DMA_UNROLL + u
                for k in range(TOP_K):
                    _token_copy(ys_ref, idx_ref[0, 0, TOP_K * r + k], ybuf_ref, into * npair + k * tc + r,
                                sems.at[into]).start(priority=k % 2)
            return c
        lax.fori_loop(0, tc // DMA_UNROLL, body, 0)

    @pl.when(i == 0)
    def _():
        gather(first_ref, 0)

    @pl.when(i + 1 < ns)
    def _():
        gather(nxt_ref, 1 - slot)

    _tokens_wait(ys_ref, ybuf_ref, slot * npair, npair, sems.at[slot])
    rt = rt_ref[...]
    base = pl.multiple_of(slot * (npair * TOKEN_ROWS), TOKEN_ROWS)
    pieces = []
    for j in range(TOKEN_ROWS):
        yj = None
        for k in range(TOP_K):
            v = ybuf_ref[pl.ds(base + k * tc * TOKEN_ROWS + j, tc, stride=TOKEN_ROWS), :]
            yj = rt[:, k:k + 1] * v if yj is None else yj + rt[:, k:k + 1] * v
        pieces.append(yj)
    y = jnp.concatenate(pieces, axis=1)
    h4 = h_ref[...] + g2_ref[0] * y
    ms = jnp.mean(h4 * h4, axis=-1, keepdims=True)
    o_ref[...] = h4 * lax.rsqrt(ms + NORM_EPS) * fg_ref[...]


def _combine(ys8, dest, h2d, rt, g2, final_g, seq):
    n, d = h2d.shape
    tc = _divisor(seq, 512, DMA_UNROLL)
    tpb = seq // tc
    ns = n // tc
    dest3 = dest.reshape(ns, 1, TOP_K * tc)
    smem_tile = lambda imap: pl.BlockSpec((1, 1, TOP_K * tc), imap, memory_space=pltpu.SMEM)
    return pl.pallas_call(
        _combine_kernel,
        grid=(ns,),
        in_specs=[
            smem_tile(lambda i: (jnp.minimum(i + 1, ns - 1), 0, 0)),
            smem_tile(lambda i: (0, 0, 0)),
            pl.BlockSpec(memory_space=pl.ANY),
            pl.BlockSpec((tc, d), lambda i: (i, 0)),
            pl.BlockSpec((tc, LANES), lambda i: (i, 0)),
            pl.BlockSpec((1, 1, d), lambda i: (i // tpb, 0, 0)),
            _resident((1, d)),
        ],
        out_specs=pl.BlockSpec((tc, d), lambda i: (i, 0)),
        out_shape=jax.ShapeDtypeStruct((n, d), F32),
        scratch_shapes=[pltpu.VMEM((2 * TOP_K * tc * TOKEN_ROWS, LANES), F32), pltpu.SemaphoreType.DMA((2,))],
        compiler_params=_params(("arbitrary",)),
        name="moe_combine_norm",
    )(dest3, dest3, ys8, h2d, rt, g2, final_g.reshape(1, d))


def _route(rt, n_experts, tm):
    n = rt.shape[0]
    npairs = n * TOP_K
    pair_expert = rt[:, 2:2 + TOP_K].astype(jnp.int32).reshape(npairs)
    onehot = (pair_expert[:, None] == jnp.arange(n_experts, dtype=jnp.int32)[None, :]).astype(jnp.int32)
    csum = jnp.cumsum(onehot, axis=0)
    rank = jnp.sum(onehot * csum, axis=1) - 1
    counts = csum[-1]
    tiles = (counts + tm - 1) // tm
    tile_end = jnp.cumsum(tiles)
    tile_start = tile_end - tiles
    dest = jnp.sum(onehot * tile_start[None, :], axis=1) * tm + rank
    n_tiles = npairs // tm + n_experts
    ti = jnp.arange(n_tiles, dtype=jnp.int32)
    n_active = tile_end[-1]
    tile_valid = (ti < n_active).astype(jnp.int32)
    last_used = jnp.minimum(ti, n_active - 1)
    expert_of = jnp.sum((last_used[:, None] >= tile_end[None, :]).astype(jnp.int32), axis=1)
    tile_expert = jnp.minimum(expert_of, n_experts - 1)
    order_tok = (jnp.sort(pair_expert * npairs + jnp.arange(npairs, dtype=jnp.int32)) % npairs) // TOP_K
    nrows = n_tiles * tm
    count_start = jnp.cumsum(counts) - counts
    padded = jnp.concatenate([jnp.zeros((nrows,), jnp.int32), order_tok, jnp.zeros((nrows,), jnp.int32)])
    row = jnp.arange(nrows, dtype=jnp.int32)
    src_tok = jnp.zeros((nrows,), jnp.int32)
    for e in range(n_experts):
        first_row = tile_start[e] * tm
        shifted = lax.dynamic_slice(padded, (nrows + count_start[e] - first_row,), (nrows,))
        mine = jnp.logical_and(row >= first_row, row < first_row + counts[e])
        src_tok = jnp.where(mine, shifted, src_tok)
    return dest.astype(jnp.int32), src_tok.reshape(n_tiles, 1, tm), tile_expert, tile_valid


def kernel(x, c, mod_w, mod_b, norm1_g, norm2_g, conv_w_pw1, conv_b_pw1, conv_w_dw, conv_b_dw, conv_ln_g, conv_ln_b, conv_w_pw2, conv_b_pw2, ssm_a_re, ssm_a_im, ssm_log_dt, ssm_b_re, ssm_b_im, ssm_c_re, ssm_c_im, ssm_d, ssm_w_glu, ssm_b_glu, ffn_w_gate, ffn_w_up, ffn_w_down, moe_w_router, moe_b_router, moe_w_gate, moe_w_up, moe_w_down, final_norm_g):
    nb, seq, d = x.shape
    n = nb * seq
    assert mod_w.shape[0] == 2 and seq % S5_CHUNK == 0 and d == TOKEN_ROWS * LANES
    n_experts = moe_w_router.shape[-1]

    mod = _modulation(c, mod_w, mod_b)
    parts = [[mod[i, :, k * d:(k + 1) * d].reshape(nb, 1, d) for k in range(6)] for i in range(2)]
    sh1a, sc1a, g1a, sh2a, sc2a, g2a = parts[0]
    sh1b, sc1b, g1b, sh2b, sc2b, g2b = parts[1]

    u, w_pw2, ffn_wg, ffn_wu, ffn_wd, w_glu = _pw1(
        x.reshape(n, d), norm1_g[0], sc1a, sh1a, conv_w_pw1[0], conv_b_pw1[0], seq,
        [conv_w_pw2[0], ffn_w_gate[0], ffn_w_up[0], ffn_w_down[0], ssm_w_glu[0]])
    h1, moe_wg, moe_wu = _conv_block(u.reshape(nb, seq, d), x, conv_w_dw[0], conv_b_dw[0], conv_ln_g[0],
                                     conv_ln_b[0], w_pw2, conv_b_pw2[0], g1a, moe_w_gate[0], moe_w_up[0])
    h2, u1, moe_wd = _dense_ffn(h1.reshape(n, d), norm2_g[0], sc2a, sh2a, g2a, ffn_wg, ffn_wu, ffn_wd,
                                norm1_g[1], sc1b, sh1b, seq, moe_w_down[0])

    tables = _s5_tables(ssm_a_re[0], ssm_a_im[0], ssm_log_dt[0], ssm_b_re[0], ssm_b_im[0],
                        ssm_c_re[0], ssm_c_im[0], ssm_d[0])
    y1 = _s5_mix(u1, tables, nb)
    h3, t8, rt = _glu_router(y1, w_glu, ssm_b_glu[0], h2, g1b, norm2_g[1], sc2b, sh2b,
                             moe_w_router[0], moe_b_router[0], seq)
    tm = 512
    dest, src_tok, tile_expert, tile_valid = _route(rt, n_experts, tm)
    ys8 = _experts(t8, src_tok, tile_expert, tile_valid, moe_wg, moe_wu, moe_wd, tm)
    out = _combine(ys8, dest, h3, rt, g2b, final_norm_g, seq)
    return out.reshape(nb, seq, d)
```

```python
import functools

import jax
import jax.numpy as jnp
from jax import lax
from jax.experimental import pallas as pl
from jax.experimental.pallas import tpu as pltpu

F32 = jnp.float32
BF16 = jnp.bfloat16
HIGHEST = lax.Precision.HIGHEST

NORM_EPS = 1e-6
TOP_K = 2
LANES = 128
S5_CHUNK = 16
S5_SEQ_PER_STEP = 2
CONV_HALO = 32
CONV_ROWS = 64
TOKEN_ROWS = 8
DMA_UNROLL = 8
VMEM_LIMIT = 56 * 2**20


def _params(sem):
    return pltpu.CompilerParams(dimension_semantics=sem, vmem_limit_bytes=VMEM_LIMIT)


def _divisor(n, cap, mult):
    best = None
    for d in range(mult, min(n, cap) + 1, mult):
        if n % d == 0:
            best = d
    assert best is not None, (n, cap, mult)
    return best


def _resident(shape):
    nd = len(shape)
    return pl.BlockSpec(shape, lambda *_: (0,) * nd, pipeline_mode=pl.Buffered(1))


def _rms_mod(x, g, sc, sh):
    ms = jnp.mean(x * x, axis=-1, keepdims=True)
    return (x * lax.rsqrt(ms + NORM_EPS) * g) * (1.0 + sc) + sh


def _silu(x):
    return x * jax.nn.sigmoid(x)


def _gelu_tanh(x):
    c = 0.7978845608028654
    return 0.5 * x * (1.0 + jnp.tanh(c * (x + 0.044715 * (x * x * x))))


def _split_bf16(x):
    hi = x.astype(BF16)
    return hi, (x - hi.astype(F32)).astype(BF16)


def _mod_kernel(c_ref, w_ref, b_ref, o_ref):
    cond = _silu(c_ref[...])
    o_ref[0] = jnp.dot(cond, w_ref[0], preferred_element_type=F32, precision=HIGHEST) + b_ref[0]


def _modulation(c, mod_w, mod_b):
    depth, d, d6 = mod_w.shape
    nb = c.shape[0]
    rows = -(-nb // 8) * 8
    c8 = jnp.pad(c, ((0, rows - nb), (0, 0)))
    tn = _divisor(d6, 1536, LANES)
    out = pl.pallas_call(
        _mod_kernel,
        grid=(depth, d6 // tn),
        in_specs=[
            pl.BlockSpec((rows, d), lambda i, j: (0, 0)),
            pl.BlockSpec((1, d, tn), lambda i, j: (i, 0, j)),
            pl.BlockSpec((1, 1, tn), lambda i, j: (i, 0, j)),
        ],
        out_specs=pl.BlockSpec((1, rows, tn), lambda i, j: (i, 0, j)),
        out_shape=jax.ShapeDtypeStruct((depth, rows, d6), F32),
        compiler_params=_params(("arbitrary", "arbitrary")),
        name="adaln_mod",
    )(c8, mod_w, mod_b.reshape(depth, 1, d6))
    return out[:, :nb, :]


def _cast_rider(w, nsteps, step):
    w2 = w.reshape(-1, w.shape[-1])
    rows, cols = w2.shape
    slabs = max(s for s in range(1, nsteps + 1) if nsteps % s == 0 and rows % (16 * s) == 0)
    rep = nsteps // slabs
    spec = pl.BlockSpec((rows // slabs, cols), lambda *g: (step(*g) // rep, 0))
    return w2, spec, jax.ShapeDtypeStruct((rows, cols), BF16)


def _pw1_kernel(x_ref, g_ref, sc_ref, sh_ref, w_ref, b_ref, *refs):
    nr = (len(refs) - 1) // 2
    o_ref = refs[nr]
    for cw_ref, co_ref in zip(refs[:nr], refs[nr + 1:]):
        co_ref[...] = cw_ref[...].astype(BF16)
    x = x_ref[...]
    d = x.shape[1]
    y = _rms_mod(x, g_ref[...], sc_ref[0], sh_ref[0])
    u = jnp.dot(y.astype(BF16), w_ref[...], preferred_element_type=F32) + b_ref[...]
    o_ref[...] = u[:, :d] * jax.nn.sigmoid(u[:, d:])


def _pw1(x2, norm_g, sc, sh, w, b, seq, riders):
    n, d = x2.shape
    tm = _divisor(seq, 512, 8)
    tpb = seq // tm
    bmap = lambda i: (i // tpb, 0, 0)
    cast = [_cast_rider(r, n // tm, lambda i: i) for r in riders]
    out = pl.pallas_call(
        _pw1_kernel,
        grid=(n // tm,),
        in_specs=[
            pl.BlockSpec((tm, d), lambda i: (i, 0)),
            _resident((1, d)),
            pl.BlockSpec((1, 1, d), bmap),
            pl.BlockSpec((1, 1, d), bmap),
            _resident((d, 2 * d)),
            _resident((1, 2 * d)),
        ] + [c[1] for c in cast],
        out_specs=[pl.BlockSpec((tm, d), lambda i: (i, 0))] + [c[1] for c in cast],
        out_shape=[jax.ShapeDtypeStruct((n, d), F32)] + [c[2] for c in cast],
        compiler_params=_params(("arbitrary",)),
        name="conv_pw1_glu",
    )(x2, norm_g.reshape(1, d), sc, sh, w.astype(BF16), b.reshape(1, 2 * d), *[c[0] for c in cast])
    return [out[0]] + [o.reshape(r.shape) for o, r in zip(out[1:], riders)]


def _conv_kernel(cur_ref, prev_ref, wdw_ref, bdw_ref, lng_ref, lnb_ref, w2_ref, b2_ref, h_ref, g1_ref,
                 cwa_ref, cwb_ref, o_ref, coa_ref, cob_ref, buf_ref, cv_ref, *, taps):
    tm, d = cv_ref.shape
    i = pl.program_id(1)
    for cw_ref, co_ref in ((cwa_ref, coa_ref), (cwb_ref, cob_ref)):
        _, _, kk, tf = co_ref.shape
        for e in range(co_ref.shape[0]):
            for q in range(co_ref.shape[1]):
                co_ref[e, q] = cw_ref[e * kk:(e + 1) * kk, q * tf:(q + 1) * tf].astype(BF16)
    buf_ref[0:CONV_HALO, :] = jnp.where(i > 0, prev_ref[0], 0.0)
    buf_ref[CONV_HALO:, :] = cur_ref[0]
    off0 = CONV_HALO - (taps - 1)
    span = CONV_ROWS + CONV_HALO
    for c in range(d // LANES):
        lanes = slice(c * LANES, (c + 1) * LANES)

        def body(k, carry, lanes=lanes):
            r0 = pl.multiple_of(k * CONV_ROWS, CONV_ROWS)
            v = buf_ref[pl.ds(r0, span), lanes]
            acc = jnp.broadcast_to(bdw_ref[:, lanes], (CONV_ROWS, LANES))
            for s in range(8):
                xs = v if s == 0 else pltpu.roll(v, span - s, axis=0)
                for q in range(span // 8):
                    o = 8 * q + s
                    if o < off0 or o > off0 + taps - 1:
                        continue
                    wrow = wdw_ref[o - off0:o - off0 + 1, lanes]
                    acc = acc + wrow * xs[8 * q:8 * q + CONV_ROWS, :]
            cv_ref[pl.ds(r0, CONV_ROWS), lanes] = acc
            return carry

        lax.fori_loop(0, tm // CONV_ROWS, body, 0)
    v = cv_ref[...]
    mu = jnp.mean(v, axis=-1, keepdims=True)
    xc = v - mu
    var = jnp.mean(xc * xc, axis=-1, keepdims=True)
    y = _silu(xc * lax.rsqrt(var + NORM_EPS) * lng_ref[...] + lnb_ref[...])
    z = jnp.dot(y.astype(BF16), w2_ref[...], preferred_element_type=F32) + b2_ref[...]
    o_ref[0] = h_ref[0] + g1_ref[0] * z


def _conv_block(u3, x3, w_dw, b_dw, ln_g, ln_b, w2, b2, g1, rider_a, rider_b, col_chunks):
    nb, seq, d = x3.shape
    taps = w_dw.shape[0]
    assert taps - 1 <= CONV_HALO and d % LANES == 0
    tm = _divisor(seq, 512, CONV_ROWS)
    hb = tm // CONV_HALO
    tps = seq // tm
    wpad = jnp.pad(w_dw, ((0, -taps % 8), (0, 0)))
    row = lambda a: a.reshape(1, d)
    step = lambda b, i: b * tps + i
    ra2, ra_spec, _ = _cast_rider(rider_a, nb * tps, step)
    rb2, rb_spec, _ = _cast_rider(rider_b, nb * tps, step)

    def chunked_out(rider, in_spec):
        ne, k, fdim = rider.shape
        slab = in_spec.block_shape[0]
        assert fdim % col_chunks == 0 and slab * nb * tps == ne * k
        if slab >= k:
            assert slab % k == 0
            spec = pl.BlockSpec((slab // k, col_chunks, k, fdim // col_chunks), lambda b, i: (step(b, i), 0, 0, 0))
        else:
            assert k % slab == 0
            per_expert = k // slab
            spec = pl.BlockSpec((1, col_chunks, slab, fdim // col_chunks),
                                lambda b, i: (step(b, i) // per_expert, 0, step(b, i) % per_expert, 0))
        return spec, jax.ShapeDtypeStruct((ne, col_chunks, k, fdim // col_chunks), BF16)

    (ca_spec, ra_shape), (cb_spec, rb_shape) = chunked_out(rider_a, ra_spec), chunked_out(rider_b, rb_spec)
    return pl.pallas_call(
        functools.partial(_conv_kernel, taps=taps),
        grid=(nb, seq // tm),
        in_specs=[
            pl.BlockSpec((1, tm, d), lambda b, i: (b, i, 0)),
            pl.BlockSpec((1, CONV_HALO, d), lambda b, i: (b, jnp.maximum(i * hb - 1, 0), 0)),
            _resident(wpad.shape),
            _resident((1, d)),
            _resident((1, d)),
            _resident((1, d)),
            _resident((d, d)),
            _resident((1, d)),
            pl.BlockSpec((1, tm, d), lambda b, i: (b, i, 0)),
            pl.BlockSpec((1, 1, d), lambda b, i: (b, 0, 0)),
            ra_spec,
            rb_spec,
        ],
        out_specs=[pl.BlockSpec((1, tm, d), lambda b, i: (b, i, 0)), ca_spec, cb_spec],
        out_shape=[jax.ShapeDtypeStruct((nb, seq, d), F32), ra_shape, rb_shape],
        scratch_shapes=[pltpu.VMEM((tm + CONV_HALO, d), F32), pltpu.VMEM((tm, d), F32)],
        compiler_params=_params(("arbitrary", "arbitrary")),
        name="conv_dw_ln_pw2",
    )(u3, u3, wpad, row(b_dw), row(ln_g), row(ln_b), w2.astype(BF16), row(b2), x3, g1, ra2, rb2)


def _ffn_kernel(h_ref, n2g_ref, sc_ref, sh_ref, g2_ref, wg_ref, wu_ref, wd_ref, n1g_ref, sc1_ref, sh1_ref, cw_ref,
                h2_ref, u_ref, co_ref, *, fchunk):
    co_ref[...] = cw_ref[...].astype(BF16)
    h = h_ref[...]
    t = _rms_mod(h, n2g_ref[...], sc_ref[0], sh_ref[0]).astype(BF16)
    dff = wg_ref.shape[1]
    acc = None
    for f0 in range(0, dff, fchunk):
        f1 = min(f0 + fchunk, dff)
        g = jnp.dot(t, wg_ref[:, f0:f1], preferred_element_type=F32)
        u = jnp.dot(t, wu_ref[:, f0:f1], preferred_element_type=F32)
        a = (_silu(g) * u).astype(BF16)
        y = jnp.dot(a, wd_ref[f0:f1, :], preferred_element_type=F32)
        acc = y if acc is None else acc + y
    h2 = h + g2_ref[0] * acc
    h2_ref[...] = h2
    u = _rms_mod(h2, n1g_ref[...], sc1_ref[0], sh1_ref[0])
    for g in range(u_ref.shape[0]):
        u_ref[g] = u[:, g * LANES:(g + 1) * LANES]


def _dense_ffn(h, n2g, sc2, sh2, g2, wg, wu, wd, n1g, sc1, sh1, seq, rider):
    n, d = h.shape
    nblk = d // LANES
    dff = wg.shape[1]
    tm = _divisor(seq, 512, 8)
    tpb = seq // tm
    bmap = lambda i: (i // tpb, 0, 0)
    tile = pl.BlockSpec((tm, d), lambda i: (i, 0))
    rider2, rider_spec, rider_shape = _cast_rider(rider, n // tm, lambda i: i)
    h2, u, casted = pl.pallas_call(
        functools.partial(_ffn_kernel, fchunk=512),
        grid=(n // tm,),
        in_specs=[
            tile,
            _resident((1, d)),
            pl.BlockSpec((1, 1, d), bmap),
            pl.BlockSpec((1, 1, d), bmap),
            pl.BlockSpec((1, 1, d), bmap),
            _resident((d, dff)),
            _resident((d, dff)),
            _resident((dff, d)),
            _resident((1, d)),
            pl.BlockSpec((1, 1, d), bmap),
            pl.BlockSpec((1, 1, d), bmap),
            rider_spec,
        ],
        out_specs=[tile, pl.BlockSpec((nblk, tm, LANES), lambda i: (0, i, 0)), rider_spec],
        out_shape=[jax.ShapeDtypeStruct((n, d), F32), jax.ShapeDtypeStruct((nblk, n, LANES), F32), rider_shape],
        compiler_params=_params(("arbitrary",)),
        name="dense_swiglu",
    )(h, n2g.reshape(1, d), sc2, sh2, g2, wg.astype(BF16), wu.astype(BF16), wd.astype(BF16),
      n1g.reshape(1, d), sc1, sh1, rider2)
    return h2, u, casted.reshape(rider.shape)


def _s5_tables(a_re, a_im, log_dt, b_re, b_im, c_re, c_im, d_skip):
    ng, p = a_re.shape
    c = b_re.shape[-1]
    gpb = LANES // c
    nblk = ng // gpb
    L = S5_CHUNK
    blk = lambda a: a.reshape((nblk, gpb) + a.shape[1:])
    dt = jnp.exp(log_dt)[:, None]
    mag = jnp.exp(dt * a_re)
    ab_re = mag * jnp.cos(dt * a_im)
    ab_im = mag * jnp.sin(dt * a_im)
    den = a_re * a_re + a_im * a_im
    f_re = ((ab_re - 1.0) * a_re + ab_im * a_im) / den
    f_im = (ab_im * a_re - (ab_re - 1.0) * a_im) / den
    bb_re = f_re[..., None] * b_re - f_im[..., None] * b_im
    bb_im = f_re[..., None] * b_im + f_im[..., None] * b_re
    jj = jnp.arange(L + 1, dtype=F32)[None, :, None]
    ang = blk(dt * a_im).reshape(nblk, 1, gpb * p) * jj
    pmag = jnp.exp(blk(dt * a_re).reshape(nblk, 1, gpb * p) * jj)
    pw_re = pmag * jnp.cos(ang)
    pw_im = pmag * jnp.sin(ang)
    bbc_re = jnp.swapaxes(blk(bb_re), 2, 3).reshape(nblk, LANES, p)
    bbc_im = jnp.swapaxes(blk(bb_im), 2, 3).reshape(nblk, LANES, p)
    cc_re = c_re.reshape(nblk, LANES, p)
    cc_im = c_im.reshape(nblk, LANES, p)
    d_blk = jnp.tile(d_skip.reshape(nblk, LANES), (1, L)).reshape(nblk, 1, L * LANES)
    return bbc_re, bbc_im, cc_re, cc_im, pw_re, pw_im, d_blk


def _s5_kernel(u_ref, bbr_ref, bbi_ref, ctr_ref, cti_ref, pwr_ref, pwi_ref, d_ref, o_ref,
               t_ref, we_ref, vt_ref, e_ref, hin_ref, *, nseq):
    L = S5_CHUNK
    ew = e_ref.shape[1]
    rows = e_ref.shape[0] // nseq
    seq = rows * L
    ph = ew // 2

    @pl.when(pl.program_id(1) == 0)
    def _build_operands():
        zblock = jnp.zeros((LANES, LANES), BF16)
        for k in range(L // 2):
            t_ref[(2 * k + 1) * LANES:(2 * k + 2) * LANES, 2 * k * LANES:(2 * k + 1) * LANES] = zblock
        p = bbr_ref.shape[2]
        gpb = ph // p
        shift_c = (LANES // gpb).bit_length() - 1
        shift_p = p.bit_length() - 1
        assert (1 << shift_c) * gpb == LANES and (1 << shift_p) == p

        def expand(a):
            wide = jnp.concatenate([a] * gpb, axis=1)
            rg = lax.shift_right_logical(lax.broadcasted_iota(jnp.int32, wide.shape, 0), shift_c)
            cg = lax.shift_right_logical(lax.broadcasted_iota(jnp.int32, wide.shape, 1), shift_p)
            return jnp.where(rg == cg, wide, 0.0)

        bbr, bbi = expand(bbr_ref[0]), expand(bbi_ref[0])
        ctr, cti = expand(ctr_ref[0]), expand(cti_ref[0])
        cc_hi, cc_lo = _split_bf16(jnp.concatenate([ctr.T, -cti.T], axis=0))
        for lp in range(L):
            j = L - 1 - lp
            p_re, p_im = pwr_ref[0, j:j + 1, :], pwi_ref[0, j:j + 1, :]
            w = jnp.concatenate([bbr * p_re - bbi * p_im, bbr * p_im + bbi * p_re], axis=1)
            w_hi, w_lo = _split_bf16(w)
            we_ref[lp * LANES:(lp + 1) * LANES, :] = w_hi
            k = (jnp.dot(w_hi, cc_hi, preferred_element_type=F32)
                 + jnp.dot(w_lo, cc_hi, preferred_element_type=F32)
                 + jnp.dot(w_hi, cc_lo, preferred_element_type=F32)).astype(BF16)
            for l1 in range(L - j):
                t_ref[l1 * LANES:(l1 + 1) * LANES, (l1 + j) * LANES:(l1 + j + 1) * LANES] = k
        for l in range(L):
            p_re, p_im = pwr_ref[0, l + 1:l + 2, :], pwi_ref[0, l + 1:l + 2, :]
            v = jnp.concatenate([ctr * p_re - cti * p_im, -(ctr * p_im + cti * p_re)], axis=1)
            vt_ref[l * LANES:(l + 1) * LANES, :] = v.astype(BF16)

    xcat = jnp.concatenate(
        [jnp.concatenate([u_ref[0, pl.ds(q * seq + l, rows, stride=L), :] for l in range(L)], axis=1)
         for q in range(nseq)], axis=0)
    xb = xcat.astype(BF16)
    e_ref[...] = jnp.dot(xb, we_ref[...], preferred_element_type=F32)
    ar = pwr_ref[0, L:L + 1, :]
    ai = pwi_ref[0, L:L + 1, :]

    def body(r, carry):
        out = []
        for q in range(nseq):
            hr, hi = carry[2 * q], carry[2 * q + 1]
            hin_ref[pl.ds(q * rows + r, 1), 0:ph] = hr
            hin_ref[pl.ds(q * rows + r, 1), ph:ew] = hi
            er = e_ref[pl.ds(q * rows + r, 1), 0:ph]
            ei = e_ref[pl.ds(q * rows + r, 1), ph:ew]
            out += [ar * hr - ai * hi + er, ar * hi + ai * hr + ei]
        return tuple(out)

    zero = jnp.zeros((1, ph), F32)
    lax.fori_loop(0, rows, body, (zero,) * (2 * nseq))
    tw = 2 * LANES
    y = jnp.concatenate([jnp.dot(xb[:, :c0 + tw], t_ref[0:c0 + tw, c0:c0 + tw], preferred_element_type=F32)
                         for c0 in range(0, L * LANES, tw)], axis=1)
    y = y + lax.dot_general(hin_ref[...].astype(BF16), vt_ref[...], (((1,), (1,)), ((), ())),
                            preferred_element_type=F32)
    y = _gelu_tanh(y + d_ref[0] * xcat)
    for q in range(nseq):
        for l in range(L):
            o_ref[0, pl.ds(q * seq + l, rows, stride=L), :] = y[q * rows:(q + 1) * rows, l * LANES:(l + 1) * LANES]


def _s5_mix(u, tables, nbatch):
    nblk, n, _ = u.shape
    L = S5_CHUNK
    sw = tables[4].shape[2]
    seq = n // nbatch
    nseq = S5_SEQ_PER_STEP if nbatch % S5_SEQ_PER_STEP == 0 else 1
    rows = nseq * (seq // L)
    slab = lambda a: pl.BlockSpec((1,) + a.shape[1:], lambda g, b: (g, 0, 0))
    return pl.pallas_call(
        functools.partial(_s5_kernel, nseq=nseq),
        grid=(nblk, nbatch // nseq),
        in_specs=[pl.BlockSpec((1, nseq * seq, LANES), lambda g, b: (g, b, 0))] + [slab(a) for a in tables],
        out_specs=pl.BlockSpec((1, nseq * seq, LANES), lambda g, b: (g, b, 0)),
        out_shape=jax.ShapeDtypeStruct((nblk, n, LANES), F32),
        scratch_shapes=[
            pltpu.VMEM((L * LANES, L * LANES), BF16),
            pltpu.VMEM((L * LANES, 2 * sw), BF16),
            pltpu.VMEM((L * LANES, 2 * sw), BF16),
            pltpu.VMEM((rows, 2 * sw), F32),
            pltpu.VMEM((rows, 2 * sw), F32),
        ],
        compiler_params=_params(("arbitrary", "arbitrary")),
        name="s5_chunked_scan",
    )(u, *tables)


def _glu_kernel(y_ref, w_ref, b_ref, h_ref, g1_ref, n2g_ref, sc_ref, sh_ref, wrh_ref, wrl_ref, br_ref,
                h3_ref, t_ref, rt_ref):
    d = h_ref.shape[1]
    y = jnp.concatenate([y_ref[g] for g in range(y_ref.shape[0])], axis=1)
    z = jnp.dot(y.astype(BF16), w_ref[...], preferred_element_type=F32) + b_ref[...]
    h3 = h_ref[...] + g1_ref[0] * (z[:, :d] * jax.nn.sigmoid(z[:, d:]))
    h3_ref[...] = h3
    t = _rms_mod(h3, n2g_ref[...], sc_ref[0], sh_ref[0])
    rows = t.shape[0]
    for j in range(TOKEN_ROWS):
        t_ref[pl.ds(j, rows, stride=TOKEN_ROWS), :] = t[:, j * LANES:(j + 1) * LANES]
    t_hi, t_lo = _split_bf16(t)
    logits = (jnp.dot(t_hi, wrh_ref[...], preferred_element_type=F32)
              + jnp.dot(t_lo, wrh_ref[...], preferred_element_type=F32)
              + jnp.dot(t_hi, wrl_ref[...], preferred_element_type=F32)) + br_ref[...]
    lane = lax.broadcasted_iota(jnp.int32, logits.shape, 1).astype(F32)

    def top1(v):
        m = jnp.max(v, axis=-1, keepdims=True)
        return m, jnp.min(jnp.where(v == m, lane, float(LANES)), axis=-1, keepdims=True)

    m1, i1 = top1(logits)
    m2, i2 = top1(jnp.where(lane == i1, -jnp.inf, logits))
    e2 = jnp.exp(m2 - m1)
    den = 1.0 + e2
    rt_ref[...] = jnp.where(lane == 0.0, 1.0 / den,
                            jnp.where(lane == 1.0, e2 / den,
                                      jnp.where(lane == 2.0, i1, jnp.where(lane == 3.0, i2, 0.0))))


def _glu_router(y, w_glu, b_glu, h, g1, n2g, sc2, sh2, w_router, b_router, seq):
    n, d = h.shape
    ne = w_router.shape[1]
    assert ne <= LANES
    tm = _divisor(seq, 512, 8)
    tpb = seq // tm
    bmap = lambda i: (i // tpb, 0, 0)
    tile = pl.BlockSpec((tm, d), lambda i: (i, 0))
    wr_hi, wr_lo = _split_bf16(jnp.pad(w_router, ((0, 0), (0, LANES - ne))))
    br = jnp.pad(b_router, (0, LANES - ne), constant_values=-1e30).reshape(1, LANES)
    return pl.pallas_call(
        _glu_kernel,
        grid=(n // tm,),
        in_specs=[
            pl.BlockSpec((d // LANES, tm, LANES), lambda i: (0, i, 0)),
            _resident((d, 2 * d)),
            _resident((1, 2 * d)),
            tile,
            pl.BlockSpec((1, 1, d), bmap),
            _resident((1, d)),
            pl.BlockSpec((1, 1, d), bmap),
            pl.BlockSpec((1, 1, d), bmap),
            _resident((d, LANES)),
            _resident((d, LANES)),
            _resident((1, LANES)),
        ],
        out_specs=[tile, pl.BlockSpec((tm * TOKEN_ROWS, LANES), lambda i: (i, 0)),
                   pl.BlockSpec((tm, LANES), lambda i: (i, 0))],
        out_shape=[
            jax.ShapeDtypeStruct((n, d), F32),
            jax.ShapeDtypeStruct((n * TOKEN_ROWS, LANES), F32),
            jax.ShapeDtypeStruct((n, LANES), F32),
        ],
        compiler_params=_params(("arbitrary",)),
        name="s5_glu_router",
    )(y, w_glu.astype(BF16), b_glu.reshape(1, 2 * d), h, g1, n2g.reshape(1, d), sc2, sh2,
      wr_hi, wr_lo, br)


def _token_copy(src_ref, src_tok, dst_ref, dst_tok, sem):
    s = pl.multiple_of(src_tok * TOKEN_ROWS, TOKEN_ROWS)
    t = pl.multiple_of(dst_tok * TOKEN_ROWS, TOKEN_ROWS)
    return pltpu.make_async_copy(src_ref.at[pl.ds(s, TOKEN_ROWS), :], dst_ref.at[pl.ds(t, TOKEN_ROWS), :], sem)


def _tokens_wait(src_ref, dst_ref, dst_tok, ntok, sem):
    t = pl.multiple_of(dst_tok * TOKEN_ROWS, TOKEN_ROWS)
    pltpu.make_async_copy(src_ref.at[pl.ds(0, ntok * TOKEN_ROWS), :],
                          dst_ref.at[pl.ds(t, ntok * TOKEN_ROWS), :], sem).wait()


def _expert_kernel(te_ref, tv_ref, nxt_ref, first_ref, t8_ref, wg_ref, wu_ref, wd_ref, o_ref,
                   xbuf_ref, xb_ref, acc_ref, sems, *, fchunk, nf):
    del te_ref
    i = pl.program_id(0)
    f = pl.program_id(1)
    nt = pl.num_programs(0)
    tm = acc_ref.shape[0]
    slot = lax.rem(i, 2)

    def gather(idx_ref, into):
        def body(j, c):
            for u in range(DMA_UNROLL):
                r = j * DMA_UNROLL + u
                _token_copy(t8_ref, idx_ref[0, 0, r], xbuf_ref, into * tm + r, sems.at[into]).start(priority=u % 2)
            return c
        lax.fori_loop(0, tm // DMA_UNROLL, body, 0)

    @pl.when(jnp.logical_and(i == 0, f == 0))
    def _():
        gather(first_ref, 0)

    @pl.when(jnp.logical_and(f == 0, jnp.logical_and(i + 1 < nt, tv_ref[jnp.minimum(i + 1, nt - 1)] == 1)))
    def _():
        gather(nxt_ref, 1 - slot)

    @pl.when(jnp.logical_and(tv_ref[i] == 0, f == nf - 1))
    def _():
        o_ref[...] = jnp.zeros_like(o_ref)

    @pl.when(tv_ref[i] == 1)
    def _():
        tf = wg_ref.shape[3]
        next_used = jnp.logical_and(i + 1 < nt, tv_ref[jnp.minimum(i + 1, nt - 1)] == 1)

        def to_rows(s):
            base = pl.multiple_of(s * (tm * TOKEN_ROWS), TOKEN_ROWS)
            rows0 = pl.multiple_of(s * tm, tm)
            for j in range(TOKEN_ROWS):
                xb_ref[pl.ds(rows0, tm), j * LANES:(j + 1) * LANES] = (
                    xbuf_ref[pl.ds(base + j, tm, stride=TOKEN_ROWS), :].astype(BF16))

        def ffn():
            x = xb_ref[pl.ds(pl.multiple_of(slot * tm, tm), tm), :]
            y = None
            for f0 in range(0, tf, fchunk):
                f1 = min(f0 + fchunk, tf)
                g = jnp.dot(x, wg_ref[0, 0, :, f0:f1], preferred_element_type=F32)
                u = jnp.dot(x, wu_ref[0, 0, :, f0:f1], preferred_element_type=F32)
                a = (_silu(g) * u).astype(BF16)
                yy = jnp.dot(a, wd_ref[0, f0:f1, :], preferred_element_type=F32)
                y = yy if y is None else y + yy
            return y

        def emit(total):
            for j in range(TOKEN_ROWS):
                o_ref[pl.ds(j, tm, stride=TOKEN_ROWS), :] = total[:, j * LANES:(j + 1) * LANES]

        @pl.when(jnp.logical_and(i == 0, f == 0))
        def _():
            _tokens_wait(t8_ref, xbuf_ref, 0, tm, sems.at[0])
            to_rows(0)

        @pl.when(jnp.logical_and(f == nf - 1, next_used))
        def _():
            _tokens_wait(t8_ref, xbuf_ref, (1 - slot) * tm, tm, sems.at[1 - slot])

        if nf == 1:
            y = ffn()
            to_rows(1 - slot)
            emit(y)
        else:
            @pl.when(f == 0)
            def _():
                acc_ref[...] = ffn()

            @pl.when(jnp.logical_and(f > 0, f < nf - 1))
            def _():
                acc_ref[...] += ffn()

            @pl.when(f == nf - 1)
            def _():
                y = ffn()
                to_rows(1 - slot)
                emit(acc_ref[...] + y)


def _experts(t8, src_tok, tile_expert, tile_valid, wg, wu, wd, tm):
    n_tiles = src_tok.shape[0]
    d = TOKEN_ROWS * LANES
    _, nf, _, tf = wg.shape
    assert tm % DMA_UNROLL == 0 and wd.shape[1] == nf * tf
    fidx = lambda i, f, tv: f * tv[i] + (nf - 1) * (1 - tv[i])
    smem_tile = lambda imap: pl.BlockSpec((1, 1, tm), imap, memory_space=pltpu.SMEM)
    grid_spec = pltpu.PrefetchScalarGridSpec(
        num_scalar_prefetch=2,
        grid=(n_tiles, nf),
        in_specs=[
            smem_tile(lambda i, f, te, tv: (jnp.minimum(i + 1, n_tiles - 1), 0, 0)),
            smem_tile(lambda i, f, te, tv: (0, 0, 0)),
            pl.BlockSpec(memory_space=pl.ANY),
            pl.BlockSpec((1, 1, d, tf), lambda i, f, te, tv: (te[i], fidx(i, f, tv), 0, 0)),
            pl.BlockSpec((1, 1, d, tf), lambda i, f, te, tv: (te[i], fidx(i, f, tv), 0, 0)),
            pl.BlockSpec((1, tf, d), lambda i, f, te, tv: (te[i], fidx(i, f, tv), 0)),
        ],
        out_specs=pl.BlockSpec((tm * TOKEN_ROWS, LANES), lambda i, f, te, tv: (i, 0)),
        scratch_shapes=[
            pltpu.VMEM((2 * tm * TOKEN_ROWS, LANES), F32),
            pltpu.VMEM((2 * tm, d), BF16),
            pltpu.VMEM((tm, d), F32),
            pltpu.SemaphoreType.DMA((2,)),
        ],
    )
    return pl.pallas_call(
        functools.partial(_expert_kernel, fchunk=1024, nf=nf),
        grid_spec=grid_spec,
        out_shape=jax.ShapeDtypeStruct((n_tiles * tm * TOKEN_ROWS, LANES), F32),
        compiler_params=_params(("arbitrary", "arbitrary")),
        name="moe_experts",
    )(tile_expert, tile_valid, src_tok, src_tok, t8, wg, wu, wd)


def _combine_kernel(nxt_ref, first_ref, ys_ref, h_ref, rt_ref, g2_ref, fg_ref, o_ref, ybuf_ref, sems):
    i = pl.program_id(0)
    ns = pl.num_programs(0)
    tc = h_ref.shape[0]
    npair = TOP_K * tc
    slot = lax.rem(i, 2)

    def gather(idx_ref, into):
        def body(j, c):
            for u in range(DMA_UNROLL):
                r = j * DMA_UNROLL + u
                for k in range(TOP_K):
                    _token_copy(ys_ref, idx_ref[0, 0, TOP_K * r + k], ybuf_ref, into * npair + k * tc + r,
                                sems.at[into]).start(priority=k % 2)
            return c
        lax.fori_loop(0, tc // DMA_UNROLL, body, 0)

    @pl.when(i == 0)
    def _():
        gather(first_ref, 0)

    @pl.when(i + 1 < ns)
    def _():
        gather(nxt_ref, 1 - slot)

    _tokens_wait(ys_ref, ybuf_ref, slot * npair, npair, sems.at[slot])
    rt = rt_ref[...]
    base = pl.multiple_of(slot * (npair * TOKEN_ROWS), TOKEN_ROWS)
    pieces = []
    for j in range(TOKEN_ROWS):
        yj = None
        for k in range(TOP_K):
            v = ybuf_ref[pl.ds(base + k * tc * TOKEN_ROWS + j, tc, stride=TOKEN_ROWS), :]
            yj = rt[:, k:k + 1] * v if yj is None else yj + rt[:, k:k + 1] * v
        pieces.append(yj)
    y = jnp.concatenate(pieces, axis=1)
    h4 = h_ref[...] + g2_ref[0] * y
    ms = jnp.mean(h4 * h4, axis=-1, keepdims=True)
    o_ref[...] = h4 * lax.rsqrt(ms + NORM_EPS) * fg_ref[...]


def _combine(ys8, dest, h2d, rt, g2, final_g, seq):
    n, d = h2d.shape
    tc = _divisor(seq, 512, DMA_UNROLL)
    tpb = seq // tc
    ns = n // tc
    dest3 = dest.reshape(ns, 1, TOP_K * tc)
    smem_tile = lambda imap: pl.BlockSpec((1, 1, TOP_K * tc), imap, memory_space=pltpu.SMEM)
    return pl.pallas_call(
        _combine_kernel,
        grid=(ns,),
        in_specs=[
            smem_tile(lambda i: (jnp.minimum(i + 1, ns - 1), 0, 0)),
            smem_tile(lambda i: (0, 0, 0)),
            pl.BlockSpec(memory_space=pl.ANY),
            pl.BlockSpec((tc, d), lambda i: (i, 0)),
            pl.BlockSpec((tc, LANES), lambda i: (i, 0)),
            pl.BlockSpec((1, 1, d), lambda i: (i // tpb, 0, 0)),
            _resident((1, d)),
        ],
        out_specs=pl.BlockSpec((tc, d), lambda i: (i, 0)),
        out_shape=jax.ShapeDtypeStruct((n, d), F32),
        scratch_shapes=[pltpu.VMEM((2 * TOP_K * tc * TOKEN_ROWS, LANES), F32), pltpu.SemaphoreType.DMA((2,))],
        compiler_params=_params(("arbitrary",)),
        name="moe_combine_norm",
    )(dest3, dest3, ys8, h2d, rt, g2, final_g.reshape(1, d))


def _route(rt, n_experts, tm):
    n = rt.shape[0]
    npairs = n * TOP_K
    pair_expert = rt[:, 2:2 + TOP_K].astype(jnp.int32).reshape(npairs)
    onehot = (pair_expert[:, None] == jnp.arange(n_experts, dtype=jnp.int32)[None, :]).astype(jnp.int32)
    csum = jnp.cumsum(onehot, axis=0)
    rank = jnp.sum(onehot * csum, axis=1) - 1
    counts = csum[-1]
    tiles = (counts + tm - 1) // tm
    tile_end = jnp.cumsum(tiles)
    tile_start = tile_end - tiles
    dest = jnp.sum(onehot * tile_start[None, :], axis=1) * tm + rank
    n_tiles = npairs // tm + n_experts
    ti = jnp.arange(n_tiles, dtype=jnp.int32)
    n_active = tile_end[-1]
    tile_valid = (ti < n_active).astype(jnp.int32)
    last_used = jnp.minimum(ti, n_active - 1)
    expert_of = jnp.sum((last_used[:, None] >= tile_end[None, :]).astype(jnp.int32), axis=1)
    tile_expert = jnp.minimum(expert_of, n_experts - 1)
    order_tok = (jnp.sort(pair_expert * npairs + jnp.arange(npairs, dtype=jnp.int32)) % npairs) // TOP_K
    nrows = n_tiles * tm
    count_start = jnp.cumsum(counts) - counts
    padded = jnp.concatenate([jnp.zeros((nrows,), jnp.int32), order_tok, jnp.zeros((nrows,), jnp.int32)])
    row = jnp.arange(nrows, dtype=jnp.int32)
    src_tok = jnp.zeros((nrows,), jnp.int32)
    for e in range(n_experts):
        first_row = tile_start[e] * tm
        shifted = lax.dynamic_slice(padded, (nrows + count_start[e] - first_row,), (nrows,))
        mine = jnp.logical_and(row >= first_row, row < first_row + counts[e])
        src_tok = jnp.where(mine, shifted, src_tok)
    return dest.astype(jnp.int32), src_tok.reshape(n_tiles, 1, tm), tile_expert, tile_valid


def kernel(x, c, mod_w, mod_b, norm1_g, norm2_g, conv_w_pw1, conv_b_pw1, conv_w_dw, conv_b_dw, conv_ln_g, conv_ln_b, conv_w_pw2, conv_b_pw2, ssm_a_re, ssm_a_im, ssm_log_dt, ssm_b_re, ssm_b_im, ssm_c_re, ssm_c_im, ssm_d, ssm_w_glu, ssm_b_glu, ffn_w_gate, ffn_w_up, ffn_w_down, moe_w_router, moe_b_router, moe_w_gate, moe_w_up, moe_w_down, final_norm_g):
    nb, seq, d = x.shape
    n = nb * seq
    assert mod_w.shape[0] == 2 and seq % S5_CHUNK == 0 and d == TOKEN_ROWS * LANES
    n_experts = moe_w_router.shape[-1]

    mod = _modulation(c, mod_w, mod_b)
    parts = [[mod[i, :, k * d:(k + 1) * d].reshape(nb, 1, d) for k in range(6)] for i in range(2)]
    sh1a, sc1a, g1a, sh2a, sc2a, g2a = parts[0]
    sh1b, sc1b, g1b, sh2b, sc2b, g2b = parts[1]

    u, w_pw2, ffn_wg, ffn_wu, ffn_wd, w_glu = _pw1(
        x.reshape(n, d), norm1_g[0], sc1a, sh1a, conv_w_pw1[0], conv_b_pw1[0], seq,
        [conv_w_pw2[0], ffn_w_gate[0], ffn_w_up[0], ffn_w_down[0], ssm_w_glu[0]])
    h1, moe_wg, moe_wu = _conv_block(u.reshape(nb, seq, d), x, conv_w_dw[0], conv_b_dw[0], conv_ln_g[0],
                                     conv_ln_b[0], w_pw2, conv_b_pw2[0], g1a, moe_w_gate[0], moe_w_up[0],
                                     moe_w_gate.shape[-1] // _divisor(moe_w_gate.shape[-1], 1792, 256))
    h2, u1, moe_wd = _dense_ffn(h1.reshape(n, d), norm2_g[0], sc2a, sh2a, g2a, ffn_wg, ffn_wu, ffn_wd,
                                norm1_g[1], sc1b, sh1b, seq, moe_w_down[0])

    tables = _s5_tables(ssm_a_re[0], ssm_a_im[0], ssm_log_dt[0], ssm_b_re[0], ssm_b_im[0],
                        ssm_c_re[0], ssm_c_im[0], ssm_d[0])
    y1 = _s5_mix(u1, tables, nb)
    h3, t8, rt = _glu_router(y1, w_glu, ssm_b_glu[0], h2, g1b, norm2_g[1], sc2b, sh2b,
                             moe_w_router[0], moe_b_router[0], seq)
    tm = 512
    dest, src_tok, tile_expert, tile_valid = _route(rt, n_experts, tm)
    ys8 = _experts(t8, src_tok, tile_expert, tile_valid, moe_wg, moe_wu, moe_wd, tm)
    out = _combine(ys8, dest, h3, rt, g2b, final_norm_g, seq)
    return out.reshape(nb, seq, d)
```

```python
import functools

import jax
import jax.numpy as jnp
from jax import lax
from jax.experimental import pallas as pl
from jax.experimental.pallas import tpu as pltpu

F32 = jnp.float32
BF16 = jnp.bfloat16
HIGHEST = lax.Precision.HIGHEST

NORM_EPS = 1e-6
TOP_K = 2
LANES = 128
S5_CHUNK = 16
S5_SEQ_PER_STEP = 2
CONV_HALO = 32
CONV_ROWS = 64
TOKEN_ROWS = 8
DMA_UNROLL = 8
VMEM_LIMIT = 56 * 2**20


def _params(sem):
    return pltpu.CompilerParams(dimension_semantics=sem, vmem_limit_bytes=VMEM_LIMIT)


def _divisor(n, cap, mult):
    best = None
    for d in range(mult, min(n, cap) + 1, mult):
        if n % d == 0:
            best = d
    assert best is not None, (n, cap, mult)
    return best


def _resident(shape):
    nd = len(shape)
    return pl.BlockSpec(shape, lambda *_: (0,) * nd, pipeline_mode=pl.Buffered(1))


def _rms_mod(x, g, sc, sh):
    ms = jnp.mean(x * x, axis=-1, keepdims=True)
    return (x * lax.rsqrt(ms + NORM_EPS) * g) * (1.0 + sc) + sh


def _silu(x):
    return x * jax.nn.sigmoid(x)


def _gelu_tanh(x):
    c = 0.7978845608028654
    return 0.5 * x * (1.0 + jnp.tanh(c * (x + 0.044715 * (x * x * x))))


def _split_bf16(x):
    hi = x.astype(BF16)
    return hi, (x - hi.astype(F32)).astype(BF16)


def _mod_kernel(c_ref, w_ref, b_ref, o_ref):
    cond = _silu(c_ref[...])
    o_ref[0] = jnp.dot(cond, w_ref[0], preferred_element_type=F32, precision=HIGHEST) + b_ref[0]


def _modulation(c, mod_w, mod_b):
    depth, d, d6 = mod_w.shape
    nb = c.shape[0]
    rows = -(-nb // 8) * 8
    c8 = jnp.pad(c, ((0, rows - nb), (0, 0)))
    tn = _divisor(d6, 1536, LANES)
    out = pl.pallas_call(
        _mod_kernel,
        grid=(depth, d6 // tn),
        in_specs=[
            pl.BlockSpec((rows, d), lambda i, j: (0, 0)),
            pl.BlockSpec((1, d, tn), lambda i, j: (i, 0, j)),
            pl.BlockSpec((1, 1, tn), lambda i, j: (i, 0, j)),
        ],
        out_specs=pl.BlockSpec((1, rows, tn), lambda i, j: (i, 0, j)),
        out_shape=jax.ShapeDtypeStruct((depth, rows, d6), F32),
        compiler_params=_params(("arbitrary", "arbitrary")),
        name="adaln_mod",
    )(c8, mod_w, mod_b.reshape(depth, 1, d6))
    return out[:, :nb, :]


def _cast_rider(w, nsteps, step):
    w2 = w.reshape(-1, w.shape[-1])
    rows, cols = w2.shape
    slabs = max(s for s in range(1, nsteps + 1) if nsteps % s == 0 and rows % (16 * s) == 0)
    rep = nsteps // slabs
    spec = pl.BlockSpec((rows // slabs, cols), lambda *g: (step(*g) // rep, 0))
    return w2, spec, jax.ShapeDtypeStruct((rows, cols), BF16)


def _pw1_kernel(x_ref, g_ref, sc_ref, sh_ref, w_ref, b_ref, *refs):
    nr = (len(refs) - 1) // 2
    o_ref = refs[nr]
    for cw_ref, co_ref in zip(refs[:nr], refs[nr + 1:]):
        co_ref[...] = cw_ref[...].astype(BF16)
    x = x_ref[...]
    d = x.shape[1]
    y = _rms_mod(x, g_ref[...], sc_ref[0], sh_ref[0])
    u = jnp.dot(y.astype(BF16), w_ref[...], preferred_element_type=F32) + b_ref[...]
    o_ref[...] = u[:, :d] * jax.nn.sigmoid(u[:, d:])


def _pw1(x2, norm_g, sc, sh, w, b, seq, riders):
    n, d = x2.shape
    tm = _divisor(seq, 512, 8)
    tpb = seq // tm
    bmap = lambda i: (i // tpb, 0, 0)
    cast = [_cast_rider(r, n // tm, lambda i: i) for r in riders]
    out = pl.pallas_call(
        _pw1_kernel,
        grid=(n // tm,),
        in_specs=[
            pl.BlockSpec((tm, d), lambda i: (i, 0)),
            _resident((1, d)),
            pl.BlockSpec((1, 1, d), bmap),
            pl.BlockSpec((1, 1, d), bmap),
            _resident((d, 2 * d)),
            _resident((1, 2 * d)),
        ] + [c[1] for c in cast],
        out_specs=[pl.BlockSpec((tm, d), lambda i: (i, 0))] + [c[1] for c in cast],
        out_shape=[jax.ShapeDtypeStruct((n, d), F32)] + [c[2] for c in cast],
        compiler_params=_params(("arbitrary",)),
        name="conv_pw1_glu",
    )(x2, norm_g.reshape(1, d), sc, sh, w.astype(BF16), b.reshape(1, 2 * d), *[c[0] for c in cast])
    return [out[0]] + [o.reshape(r.shape) for o, r in zip(out[1:], riders)]


def _conv_kernel(cur_ref, prev_ref, wdw_ref, bdw_ref, lng_ref, lnb_ref, w2_ref, b2_ref, h_ref, g1_ref,
                 cwa_ref, cwb_ref, o_ref, coa_ref, cob_ref, buf_ref, cv_ref, *, taps):
    tm, d = cv_ref.shape
    i = pl.program_id(1)
    coa_ref[...] = cwa_ref[...].astype(BF16)
    cob_ref[...] = cwb_ref[...].astype(BF16)
    buf_ref[0:CONV_HALO, :] = jnp.where(i > 0, prev_ref[0], 0.0)
    buf_ref[CONV_HALO:, :] = cur_ref[0]
    off0 = CONV_HALO - (taps - 1)
    span = CONV_ROWS + CONV_HALO
    for c in range(d // LANES):
        lanes = slice(c * LANES, (c + 1) * LANES)

        def body(k, carry, lanes=lanes):
            r0 = pl.multiple_of(k * CONV_ROWS, CONV_ROWS)
            v = buf_ref[pl.ds(r0, span), lanes]
            acc = jnp.broadcast_to(bdw_ref[:, lanes], (CONV_ROWS, LANES))
            for s in range(8):
                xs = v if s == 0 else pltpu.roll(v, span - s, axis=0)
                for q in range(span // 8):
                    o = 8 * q + s
                    if o < off0 or o > off0 + taps - 1:
                        continue
                    wrow = wdw_ref[o - off0:o - off0 + 1, lanes]
                    acc = acc + wrow * xs[8 * q:8 * q + CONV_ROWS, :]
            cv_ref[pl.ds(r0, CONV_ROWS), lanes] = acc
            return carry

        lax.fori_loop(0, tm // CONV_ROWS, body, 0)
    v = cv_ref[...]
    mu = jnp.mean(v, axis=-1, keepdims=True)
    xc = v - mu
    var = jnp.mean(xc * xc, axis=-1, keepdims=True)
    y = _silu(xc * lax.rsqrt(var + NORM_EPS) * lng_ref[...] + lnb_ref[...])
    z = jnp.dot(y.astype(BF16), w2_ref[...], preferred_element_type=F32) + b2_ref[...]
    o_ref[0] = h_ref[0] + g1_ref[0] * z


def _conv_block(u3, x3, w_dw, b_dw, ln_g, ln_b, w2, b2, g1, rider_a, rider_b):
    nb, seq, d = x3.shape
    taps = w_dw.shape[0]
    assert taps - 1 <= CONV_HALO and d % LANES == 0
    tm = _divisor(seq, 512, CONV_ROWS)
    hb = tm // CONV_HALO
    tps = seq // tm
    wpad = jnp.pad(w_dw, ((0, -taps % 8), (0, 0)))
    row = lambda a: a.reshape(1, d)
    step = lambda b, i: b * tps + i
    ra2, ra_spec, ra_shape = _cast_rider(rider_a, nb * tps, step)
    rb2, rb_spec, rb_shape = _cast_rider(rider_b, nb * tps, step)
    out, cast_a, cast_b = pl.pallas_call(
        functools.partial(_conv_kernel, taps=taps),
        grid=(nb, seq // tm),
        in_specs=[
            pl.BlockSpec((1, tm, d), lambda b, i: (b, i, 0)),
            pl.BlockSpec((1, CONV_HALO, d), lambda b, i: (b, jnp.maximum(i * hb - 1, 0), 0)),
            _resident(wpad.shape),
            _resident((1, d)),
            _resident((1, d)),
            _resident((1, d)),
            _resident((d, d)),
            _resident((1, d)),
            pl.BlockSpec((1, tm, d), lambda b, i: (b, i, 0)),
            pl.BlockSpec((1, 1, d), lambda b, i: (b, 0, 0)),
            ra_spec,
            rb_spec,
        ],
        out_specs=[pl.BlockSpec((1, tm, d), lambda b, i: (b, i, 0)), ra_spec, rb_spec],
        out_shape=[jax.ShapeDtypeStruct((nb, seq, d), F32), ra_shape, rb_shape],
        scratch_shapes=[pltpu.VMEM((tm + CONV_HALO, d), F32), pltpu.VMEM((tm, d), F32)],
        compiler_params=_params(("arbitrary", "arbitrary")),
        name="conv_dw_ln_pw2",
    )(u3, u3, wpad, row(b_dw), row(ln_g), row(ln_b), w2.astype(BF16), row(b2), x3, g1, ra2, rb2)
    return out, cast_a.reshape(rider_a.shape), cast_b.reshape(rider_b.shape)


def _ffn_kernel(h_ref, n2g_ref, sc_ref, sh_ref, g2_ref, wg_ref, wu_ref, wd_ref, n1g_ref, sc1_ref, sh1_ref, cw_ref,
                h2_ref, u_ref, co_ref, *, fchunk):
    co_ref[...] = cw_ref[...].astype(BF16)
    h = h_ref[...]
    t = _rms_mod(h, n2g_ref[...], sc_ref[0], sh_ref[0]).astype(BF16)
    dff = wg_ref.shape[1]
    acc = None
    for f0 in range(0, dff, fchunk):
        f1 = min(f0 + fchunk, dff)
        g = jnp.dot(t, wg_ref[:, f0:f1], preferred_element_type=F32)
        u = jnp.dot(t, wu_ref[:, f0:f1], preferred_element_type=F32)
        a = (_silu(g) * u).astype(BF16)
        y = jnp.dot(a, wd_ref[f0:f1, :], preferred_element_type=F32)
        acc = y if acc is None else acc + y
    h2 = h + g2_ref[0] * acc
    h2_ref[...] = h2
    u = _rms_mod(h2, n1g_ref[...], sc1_ref[0], sh1_ref[0])
    for g in range(u_ref.shape[0]):
        u_ref[g] = u[:, g * LANES:(g + 1) * LANES]


def _dense_ffn(h, n2g, sc2, sh2, g2, wg, wu, wd, n1g, sc1, sh1, seq, rider):
    n, d = h.shape
    nblk = d // LANES
    dff = wg.shape[1]
    tm = _divisor(seq, 512, 8)
    tpb = seq // tm
    bmap = lambda i: (i // tpb, 0, 0)
    tile = pl.BlockSpec((tm, d), lambda i: (i, 0))
    rider2, rider_spec, rider_shape = _cast_rider(rider, n // tm, lambda i: i)
    h2, u, casted = pl.pallas_call(
        functools.partial(_ffn_kernel, fchunk=512),
        grid=(n // tm,),
        in_specs=[
            tile,
            _resident((1, d)),
            pl.BlockSpec((1, 1, d), bmap),
            pl.BlockSpec((1, 1, d), bmap),
            pl.BlockSpec((1, 1, d), bmap),
            _resident((d, dff)),
            _resident((d, dff)),
            _resident((dff, d)),
            _resident((1, d)),
            pl.BlockSpec((1, 1, d), bmap),
            pl.BlockSpec((1, 1, d), bmap),
            rider_spec,
        ],
        out_specs=[tile, pl.BlockSpec((nblk, tm, LANES), lambda i: (0, i, 0)), rider_spec],
        out_shape=[jax.ShapeDtypeStruct((n, d), F32), jax.ShapeDtypeStruct((nblk, n, LANES), F32), rider_shape],
        compiler_params=_params(("arbitrary",)),
        name="dense_swiglu",
    )(h, n2g.reshape(1, d), sc2, sh2, g2, wg.astype(BF16), wu.astype(BF16), wd.astype(BF16),
      n1g.reshape(1, d), sc1, sh1, rider2)
    return h2, u, casted.reshape(rider.shape)


def _s5_tables(a_re, a_im, log_dt, b_re, b_im, c_re, c_im, d_skip):
    ng, p = a_re.shape
    c = b_re.shape[-1]
    gpb = LANES // c
    nblk = ng // gpb
    L = S5_CHUNK
    blk = lambda a: a.reshape((nblk, gpb) + a.shape[1:])
    dt = jnp.exp(log_dt)[:, None]
    mag = jnp.exp(dt * a_re)
    ab_re = mag * jnp.cos(dt * a_im)
    ab_im = mag * jnp.sin(dt * a_im)
    den = a_re * a_re + a_im * a_im
    f_re = ((ab_re - 1.0) * a_re + ab_im * a_im) / den
    f_im = (ab_im * a_re - (ab_re - 1.0) * a_im) / den
    bb_re = f_re[..., None] * b_re - f_im[..., None] * b_im
    bb_im = f_re[..., None] * b_im + f_im[..., None] * b_re
    jj = jnp.arange(L + 1, dtype=F32)[None, :, None]
    ang = blk(dt * a_im).reshape(nblk, 1, gpb * p) * jj
    pmag = jnp.exp(blk(dt * a_re).reshape(nblk, 1, gpb * p) * jj)
    pw_re = pmag * jnp.cos(ang)
    pw_im = pmag * jnp.sin(ang)
    bbc_re = jnp.swapaxes(blk(bb_re), 2, 3).reshape(nblk, LANES, p)
    bbc_im = jnp.swapaxes(blk(bb_im), 2, 3).reshape(nblk, LANES, p)
    cc_re = c_re.reshape(nblk, LANES, p)
    cc_im = c_im.reshape(nblk, LANES, p)
    d_blk = jnp.tile(d_skip.reshape(nblk, LANES), (1, L)).reshape(nblk, 1, L * LANES)
    return bbc_re, bbc_im, cc_re, cc_im, pw_re, pw_im, d_blk


def _s5_kernel(u_ref, bbr_ref, bbi_ref, ctr_ref, cti_ref, pwr_ref, pwi_ref, d_ref, o_ref,
               t_ref, we_ref, vt_ref, e_ref, hin_ref, *, nseq):
    L = S5_CHUNK
    ew = e_ref.shape[1]
    rows = e_ref.shape[0] // nseq
    seq = rows * L
    ph = ew // 2

    @pl.when(pl.program_id(1) == 0)
    def _build_operands():
        zblock = jnp.zeros((LANES, LANES), BF16)
        for k in range(L // 2):
            t_ref[(2 * k + 1) * LANES:(2 * k + 2) * LANES, 2 * k * LANES:(2 * k + 1) * LANES] = zblock
        p = bbr_ref.shape[2]
        gpb = ph // p
        shift_c = (LANES // gpb).bit_length() - 1
        shift_p = p.bit_length() - 1
        assert (1 << shift_c) * gpb == LANES and (1 << shift_p) == p

        def expand(a):
            wide = jnp.concatenate([a] * gpb, axis=1)
            rg = lax.shift_right_logical(lax.broadcasted_iota(jnp.int32, wide.shape, 0), shift_c)
            cg = lax.shift_right_logical(lax.broadcasted_iota(jnp.int32, wide.shape, 1), shift_p)
            return jnp.where(rg == cg, wide, 0.0)

        bbr, bbi = expand(bbr_ref[0]), expand(bbi_ref[0])
        ctr, cti = expand(ctr_ref[0]), expand(cti_ref[0])
        cc_hi, cc_lo = _split_bf16(jnp.concatenate([ctr.T, -cti.T], axis=0))
        for lp in range(L):
            j = L - 1 - lp
            p_re, p_im = pwr_ref[0, j:j + 1, :], pwi_ref[0, j:j + 1, :]
            w = jnp.concatenate([bbr * p_re - bbi * p_im, bbr * p_im + bbi * p_re], axis=1)
            w_hi, w_lo = _split_bf16(w)
            we_ref[lp * LANES:(lp + 1) * LANES, :] = w_hi
            k = (jnp.dot(w_hi, cc_hi, preferred_element_type=F32)
                 + jnp.dot(w_lo, cc_hi, preferred_element_type=F32)
                 + jnp.dot(w_hi, cc_lo, preferred_element_type=F32)).astype(BF16)
            for l1 in range(L - j):
                t_ref[l1 * LANES:(l1 + 1) * LANES, (l1 + j) * LANES:(l1 + j + 1) * LANES] = k
        for l in range(L):
            p_re, p_im = pwr_ref[0, l + 1:l + 2, :], pwi_ref[0, l + 1:l + 2, :]
            v = jnp.concatenate([ctr * p_re - cti * p_im, -(ctr * p_im + cti * p_re)], axis=1)
            vt_ref[l * LANES:(l + 1) * LANES, :] = v.astype(BF16)

    xcat = jnp.concatenate(
        [jnp.concatenate([u_ref[0, pl.ds(q * seq + l, rows, stride=L), :] for l in range(L)], axis=1)
         for q in range(nseq)], axis=0)
    xb = xcat.astype(BF16)
    e_ref[...] = jnp.dot(xb, we_ref[...], preferred_element_type=F32)
    ar = pwr_ref[0, L:L + 1, :]
    ai = pwi_ref[0, L:L + 1, :]

    def body(r, carry):
        out = []
        for q in range(nseq):
            hr, hi = carry[2 * q], carry[2 * q + 1]
            hin_ref[pl.ds(q * rows + r, 1), 0:ph] = hr
            hin_ref[pl.ds(q * rows + r, 1), ph:ew] = hi
            er = e_ref[pl.ds(q * rows + r, 1), 0:ph]
            ei = e_ref[pl.ds(q * rows + r, 1), ph:ew]
            out += [ar * hr - ai * hi + er, ar * hi + ai * hr + ei]
        return tuple(out)

    zero = jnp.zeros((1, ph), F32)
    lax.fori_loop(0, rows, body, (zero,) * (2 * nseq))
    tw = 2 * LANES
    y = jnp.concatenate([jnp.dot(xb[:, :c0 + tw], t_ref[0:c0 + tw, c0:c0 + tw], preferred_element_type=F32)
                         for c0 in range(0, L * LANES, tw)], axis=1)
    y = y + lax.dot_general(hin_ref[...].astype(BF16), vt_ref[...], (((1,), (1,)), ((), ())),
                            preferred_element_type=F32)
    y = _gelu_tanh(y + d_ref[0] * xcat)
    for q in range(nseq):
        for l in range(L):
            o_ref[0, pl.ds(q * seq + l, rows, stride=L), :] = y[q * rows:(q + 1) * rows, l * LANES:(l + 1) * LANES]


def _s5_mix(u, tables, nbatch):
    nblk, n, _ = u.shape
    L = S5_CHUNK
    sw = tables[4].shape[2]
    seq = n // nbatch
    nseq = S5_SEQ_PER_STEP if nbatch % S5_SEQ_PER_STEP == 0 else 1
    rows = nseq * (seq // L)
    slab = lambda a: pl.BlockSpec((1,) + a.shape[1:], lambda g, b: (g, 0, 0))
    return pl.pallas_call(
        functools.partial(_s5_kernel, nseq=nseq),
        grid=(nblk, nbatch // nseq),
        in_specs=[pl.BlockSpec((1, nseq * seq, LANES), lambda g, b: (g, b, 0))] + [slab(a) for a in tables],
        out_specs=pl.BlockSpec((1, nseq * seq, LANES), lambda g, b: (g, b, 0)),
        out_shape=jax.ShapeDtypeStruct((nblk, n, LANES), F32),
        scratch_shapes=[
            pltpu.VMEM((L * LANES, L * LANES), BF16),
            pltpu.VMEM((L * LANES, 2 * sw), BF16),
            pltpu.VMEM((L * LANES, 2 * sw), BF16),
            pltpu.VMEM((rows, 2 * sw), F32),
            pltpu.VMEM((rows, 2 * sw), F32),
        ],
        compiler_params=_params(("arbitrary", "arbitrary")),
        name="s5_chunked_scan",
    )(u, *tables)


def _glu_kernel(y_ref, w_ref, b_ref, h_ref, g1_ref, n2g_ref, sc_ref, sh_ref, wrh_ref, wrl_ref, br_ref,
                h3_ref, t_ref, rt_ref):
    d = h_ref.shape[1]
    y = jnp.concatenate([y_ref[g] for g in range(y_ref.shape[0])], axis=1)
    z = jnp.dot(y.astype(BF16), w_ref[...], preferred_element_type=F32) + b_ref[...]
    h3 = h_ref[...] + g1_ref[0] * (z[:, :d] * jax.nn.sigmoid(z[:, d:]))
    h3_ref[...] = h3
    t = _rms_mod(h3, n2g_ref[...], sc_ref[0], sh_ref[0])
    rows = t.shape[0]
    for j in range(TOKEN_ROWS):
        t_ref[pl.ds(j, rows, stride=TOKEN_ROWS), :] = t[:, j * LANES:(j + 1) * LANES]
    t_hi, t_lo = _split_bf16(t)
    logits = (jnp.dot(t_hi, wrh_ref[...], preferred_element_type=F32)
              + jnp.dot(t_lo, wrh_ref[...], preferred_element_type=F32)
              + jnp.dot(t_hi, wrl_ref[...], preferred_element_type=F32)) + br_ref[...]
    lane = lax.broadcasted_iota(jnp.int32, logits.shape, 1).astype(F32)

    def top1(v):
        m = jnp.max(v, axis=-1, keepdims=True)
        return m, jnp.min(jnp.where(v == m, lane, float(LANES)), axis=-1, keepdims=True)

    m1, i1 = top1(logits)
    m2, i2 = top1(jnp.where(lane == i1, -jnp.inf, logits))
    e2 = jnp.exp(m2 - m1)
    den = 1.0 + e2
    rt_ref[...] = jnp.where(lane == 0.0, 1.0 / den,
                            jnp.where(lane == 1.0, e2 / den,
                                      jnp.where(lane == 2.0, i1, jnp.where(lane == 3.0, i2, 0.0))))


def _glu_router(y, w_glu, b_glu, h, g1, n2g, sc2, sh2, w_router, b_router, seq):
    n, d = h.shape
    ne = w_router.shape[1]
    assert ne <= LANES
    tm = _divisor(seq, 512, 8)
    tpb = seq // tm
    bmap = lambda i: (i // tpb, 0, 0)
    tile = pl.BlockSpec((tm, d), lambda i: (i, 0))
    wr_hi, wr_lo = _split_bf16(jnp.pad(w_router, ((0, 0), (0, LANES - ne))))
    br = jnp.pad(b_router, (0, LANES - ne), constant_values=-1e30).reshape(1, LANES)
    return pl.pallas_call(
        _glu_kernel,
        grid=(n // tm,),
        in_specs=[
            pl.BlockSpec((d // LANES, tm, LANES), lambda i: (0, i, 0)),
            _resident((d, 2 * d)),
            _resident((1, 2 * d)),
            tile,
            pl.BlockSpec((1, 1, d), bmap),
            _resident((1, d)),
            pl.BlockSpec((1, 1, d), bmap),
            pl.BlockSpec((1, 1, d), bmap),
            _resident((d, LANES)),
            _resident((d, LANES)),
            _resident((1, LANES)),
        ],
        out_specs=[tile, pl.BlockSpec((tm * TOKEN_ROWS, LANES), lambda i: (i, 0)),
                   pl.BlockSpec((tm, LANES), lambda i: (i, 0))],
        out_shape=[
            jax.ShapeDtypeStruct((n, d), F32),
            jax.ShapeDtypeStruct((n * TOKEN_ROWS, LANES), F32),
            jax.ShapeDtypeStruct((n, LANES), F32),
        ],
        compiler_params=_params(("arbitrary",)),
        name="s5_glu_router",
    )(y, w_glu.astype(BF16), b_glu.reshape(1, 2 * d), h, g1, n2g.reshape(1, d), sc2, sh2,
      wr_hi, wr_lo, br)


def _token_copy(src_ref, src_tok, dst_ref, dst_tok, sem):
    s = pl.multiple_of(src_tok * TOKEN_ROWS, TOKEN_ROWS)
    t = pl.multiple_of(dst_tok * TOKEN_ROWS, TOKEN_ROWS)
    return pltpu.make_async_copy(src_ref.at[pl.ds(s, TOKEN_ROWS), :], dst_ref.at[pl.ds(t, TOKEN_ROWS), :], sem)


def _tokens_wait(src_ref, dst_ref, dst_tok, ntok, sem):
    t = pl.multiple_of(dst_tok * TOKEN_ROWS, TOKEN_ROWS)
    pltpu.make_async_copy(src_ref.at[pl.ds(0, ntok * TOKEN_ROWS), :],
                          dst_ref.at[pl.ds(t, ntok * TOKEN_ROWS), :], sem).wait()


def _expert_kernel(te_ref, tv_ref, nxt_ref, first_ref, t8_ref, wg_ref, wu_ref, wd_ref, o_ref,
                   xbuf_ref, xb_ref, acc_ref, sems, *, fchunk, nf):
    del te_ref
    i = pl.program_id(0)
    f = pl.program_id(1)
    nt = pl.num_programs(0)
    tm = acc_ref.shape[0]
    slot = lax.rem(i, 2)

    def gather(idx_ref, into):
        def body(j, c):
            for u in range(DMA_UNROLL):
                r = j * DMA_UNROLL + u
                _token_copy(t8_ref, idx_ref[0, 0, r], xbuf_ref, into * tm + r, sems.at[into]).start(priority=u % 2)
            return c
        lax.fori_loop(0, tm // DMA_UNROLL, body, 0)

    @pl.when(jnp.logical_and(i == 0, f == 0))
    def _():
        gather(first_ref, 0)

    @pl.when(jnp.logical_and(f == 0, jnp.logical_and(i + 1 < nt, tv_ref[jnp.minimum(i + 1, nt - 1)] == 1)))
    def _():
        gather(nxt_ref, 1 - slot)

    @pl.when(jnp.logical_and(tv_ref[i] == 0, f == nf - 1))
    def _():
        o_ref[...] = jnp.zeros_like(o_ref)

    @pl.when(tv_ref[i] == 1)
    def _():
        tf = wg_ref.shape[2]
        next_used = jnp.logical_and(i + 1 < nt, tv_ref[jnp.minimum(i + 1, nt - 1)] == 1)

        def to_rows(s):
            base = pl.multiple_of(s * (tm * TOKEN_ROWS), TOKEN_ROWS)
            rows0 = pl.multiple_of(s * tm, tm)
            for j in range(TOKEN_ROWS):
                xb_ref[pl.ds(rows0, tm), j * LANES:(j + 1) * LANES] = (
                    xbuf_ref[pl.ds(base + j, tm, stride=TOKEN_ROWS), :].astype(BF16))

        def ffn():
            x = xb_ref[pl.ds(pl.multiple_of(slot * tm, tm), tm), :]
            y = None
            for f0 in range(0, tf, fchunk):
                f1 = min(f0 + fchunk, tf)
                g = jnp.dot(x, wg_ref[0, :, f0:f1], preferred_element_type=F32)
                u = jnp.dot(x, wu_ref[0, :, f0:f1], preferred_element_type=F32)
                a = (_silu(g) * u).astype(BF16)
                yy = jnp.dot(a, wd_ref[0, f0:f1, :], preferred_element_type=F32)
                y = yy if y is None else y + yy
            return y

        def emit(total):
            for j in range(TOKEN_ROWS):
                o_ref[pl.ds(j, tm, stride=TOKEN_ROWS), :] = total[:, j * LANES:(j + 1) * LANES]

        @pl.when(jnp.logical_and(i == 0, f == 0))
        def _():
            _tokens_wait(t8_ref, xbuf_ref, 0, tm, sems.at[0])
            to_rows(0)

        @pl.when(jnp.logical_and(f == nf - 1, next_used))
        def _():
            _tokens_wait(t8_ref, xbuf_ref, (1 - slot) * tm, tm, sems.at[1 - slot])

        if nf == 1:
            y = ffn()
            to_rows(1 - slot)
            emit(y)
        else:
            @pl.when(f == 0)
            def _():
                acc_ref[...] = ffn()

            @pl.when(jnp.logical_and(f > 0, f < nf - 1))
            def _():
                acc_ref[...] += ffn()

            @pl.when(f == nf - 1)
            def _():
                y = ffn()
                to_rows(1 - slot)
                emit(acc_ref[...] + y)


def _experts(t8, src_tok, tile_expert, tile_valid, wg, wu, wd, tm):
    n_tiles = src_tok.shape[0]
    d = TOKEN_ROWS * LANES
    dff = wg.shape[2]
    tf = _divisor(dff, 1792, 256)
    nf = dff // tf
    assert tm % (nf * DMA_UNROLL) == 0
    fidx = lambda i, f, tv: f * tv[i] + (nf - 1) * (1 - tv[i])
    smem_tile = lambda imap: pl.BlockSpec((1, 1, tm), imap, memory_space=pltpu.SMEM)
    grid_spec = pltpu.PrefetchScalarGridSpec(
        num_scalar_prefetch=2,
        grid=(n_tiles, nf),
        in_specs=[
            smem_tile(lambda i, f, te, tv: (jnp.minimum(i + 1, n_tiles - 1), 0, 0)),
            smem_tile(lambda i, f, te, tv: (0, 0, 0)),
            pl.BlockSpec(memory_space=pl.ANY),
            pl.BlockSpec((1, d, tf), lambda i, f, te, tv: (te[i], 0, fidx(i, f, tv))),
            pl.BlockSpec((1, d, tf), lambda i, f, te, tv: (te[i], 0, fidx(i, f, tv))),
            pl.BlockSpec((1, tf, d), lambda i, f, te, tv: (te[i], fidx(i, f, tv), 0)),
        ],
        out_specs=pl.BlockSpec((tm * TOKEN_ROWS, LANES), lambda i, f, te, tv: (i, 0)),
        scratch_shapes=[
            pltpu.VMEM((2 * tm * TOKEN_ROWS, LANES), F32),
            pltpu.VMEM((2 * tm, d), BF16),
            pltpu.VMEM((tm, d), F32),
            pltpu.SemaphoreType.DMA((2,)),
        ],
    )
    return pl.pallas_call(
        functools.partial(_expert_kernel, fchunk=1024, nf=nf),
        grid_spec=grid_spec,
        out_shape=jax.ShapeDtypeStruct((n_tiles * tm * TOKEN_ROWS, LANES), F32),
        compiler_params=_params(("arbitrary", "arbitrary")),
        name="moe_experts",
    )(tile_expert, tile_valid, src_tok, src_tok, t8, wg, wu, wd)


def _dispatch_kernel(nxt_ref, first_ref, t8_ref, o_ref, xbuf_ref, sems):
    i = pl.program_id(0)
    ns = pl.num_programs(0)
    tm = o_ref.shape[0]
    slot = lax.rem(i, 2)

    def gather(idx_ref, into):
        def body(j, c):
            for u in range(DMA_UNROLL):
                r = j * DMA_UNROLL + u
                _token_copy(t8_ref, idx_ref[0, 0, r], xbuf_ref, into * tm + r, sems.at[into]).start(priority=u % 2)
            return c
        lax.fori_loop(0, tm // DMA_UNROLL, body, 0)

    @pl.when(i == 0)
    def _():
        gather(first_ref, 0)

    @pl.when(i + 1 < ns)
    def _():
        gather(nxt_ref, 1 - slot)

    _tokens_wait(t8_ref, xbuf_ref, slot * tm, tm, sems.at[slot])
    base = pl.multiple_of(slot * (tm * TOKEN_ROWS), TOKEN_ROWS)
    for j in range(TOKEN_ROWS):
        o_ref[:, j * LANES:(j + 1) * LANES] = xbuf_ref[pl.ds(base + j, tm, stride=TOKEN_ROWS), :].astype(BF16)


def _dispatch(t8, src_tok):
    n_tiles, _, tm = src_tok.shape
    d = TOKEN_ROWS * LANES
    smem_tile = lambda imap: pl.BlockSpec((1, 1, tm), imap, memory_space=pltpu.SMEM)
    return pl.pallas_call(
        _dispatch_kernel,
        grid=(n_tiles,),
        in_specs=[
            smem_tile(lambda i: (jnp.minimum(i + 1, n_tiles - 1), 0, 0)),
            smem_tile(lambda i: (0, 0, 0)),
            pl.BlockSpec(memory_space=pl.ANY),
        ],
        out_specs=pl.BlockSpec((tm, d), lambda i: (i, 0)),
        out_shape=jax.ShapeDtypeStruct((n_tiles * tm, d), BF16),
        scratch_shapes=[pltpu.VMEM((2 * tm * TOKEN_ROWS, LANES), F32), pltpu.SemaphoreType.DMA((2,))],
        compiler_params=_params(("arbitrary",)),
        name="moe_dispatch",
    )(src_tok, src_tok, t8)


def _expert_rows_kernel(te_ref, tv_ref, x_ref, wg_ref, wu_ref, wd_ref, o_ref, acc_ref, *, fchunk, nf):
    del te_ref
    i = pl.program_id(0)
    f = pl.program_id(1)
    tm = acc_ref.shape[0]
    tf = wg_ref.shape[2]

    def ffn():
        x = x_ref[...]
        y = None
        for f0 in range(0, tf, fchunk):
            f1 = min(f0 + fchunk, tf)
            g = jnp.dot(x, wg_ref[0, :, f0:f1], preferred_element_type=F32)
            u = jnp.dot(x, wu_ref[0, :, f0:f1], preferred_element_type=F32)
            a = (_silu(g) * u).astype(BF16)
            yy = jnp.dot(a, wd_ref[0, f0:f1, :], preferred_element_type=F32)
            y = yy if y is None else y + yy
        return y

    def emit(total):
        for j in range(TOKEN_ROWS):
            o_ref[pl.ds(j, tm, stride=TOKEN_ROWS), :] = total[:, j * LANES:(j + 1) * LANES]

    @pl.when(jnp.logical_and(tv_ref[i] == 0, f == nf - 1))
    def _():
        o_ref[...] = jnp.zeros_like(o_ref)

    @pl.when(tv_ref[i] == 1)
    def _():
        if nf == 1:
            emit(ffn())
        else:
            @pl.when(f == 0)
            def _():
                acc_ref[...] = ffn()

            @pl.when(jnp.logical_and(f > 0, f < nf - 1))
            def _():
                acc_ref[...] += ffn()

            @pl.when(f == nf - 1)
            def _():
                emit(acc_ref[...] + ffn())


def _experts_rows(xs, tile_expert, tile_valid, wg, wu, wd, tm):
    nrows, d = xs.shape
    n_tiles = nrows // tm
    dff = wg.shape[2]
    tf = _divisor(dff, 1792, 256)
    nf = dff // tf
    fidx = lambda i, f, tv: f * tv[i] + (nf - 1) * (1 - tv[i])
    grid_spec = pltpu.PrefetchScalarGridSpec(
        num_scalar_prefetch=2,
        grid=(n_tiles, nf),
        in_specs=[
            pl.BlockSpec((tm, d), lambda i, f, te, tv: (i, 0)),
            pl.BlockSpec((1, d, tf), lambda i, f, te, tv: (te[i], 0, fidx(i, f, tv))),
            pl.BlockSpec((1, d, tf), lambda i, f, te, tv: (te[i], 0, fidx(i, f, tv))),
            pl.BlockSpec((1, tf, d), lambda i, f, te, tv: (te[i], fidx(i, f, tv), 0)),
        ],
        out_specs=pl.BlockSpec((tm * TOKEN_ROWS, LANES), lambda i, f, te, tv: (i, 0)),
        scratch_shapes=[pltpu.VMEM((tm, d), F32)],
    )
    return pl.pallas_call(
        functools.partial(_expert_rows_kernel, fchunk=1024, nf=nf),
        grid_spec=grid_spec,
        out_shape=jax.ShapeDtypeStruct((nrows * TOKEN_ROWS, LANES), F32),
        compiler_params=_params(("arbitrary", "arbitrary")),
        name="moe_experts",
    )(tile_expert, tile_valid, xs, wg, wu, wd)


def _combine_kernel(nxt_ref, first_ref, ys_ref, h_ref, rt_ref, g2_ref, fg_ref, o_ref, ybuf_ref, sems):
    i = pl.program_id(0)
    ns = pl.num_programs(0)
    tc = h_ref.shape[0]
    npair = TOP_K * tc
    slot = lax.rem(i, 2)

    def gather(idx_ref, into):
        def body(j, c):
            for u in range(DMA_UNROLL):
                r = j * DMA_UNROLL + u
                for k in range(TOP_K):
                    _token_copy(ys_ref, idx_ref[0, 0, TOP_K * r + k], ybuf_ref, into * npair + k * tc + r,
                                sems.at[into]).start(priority=k % 2)
            return c
        lax.fori_loop(0, tc // DMA_UNROLL, body, 0)

    @pl.when(i == 0)
    def _():
        gather(first_ref, 0)

    @pl.when(i + 1 < ns)
    def _():
        gather(nxt_ref, 1 - slot)

    _tokens_wait(ys_ref, ybuf_ref, slot * npair, npair, sems.at[slot])
    rt = rt_ref[...]
    base = pl.multiple_of(slot * (npair * TOKEN_ROWS), TOKEN_ROWS)
    pieces = []
    for j in range(TOKEN_ROWS):
        yj = None
        for k in range(TOP_K):
            v = ybuf_ref[pl.ds(base + k * tc * TOKEN_ROWS + j, tc, stride=TOKEN_ROWS), :]
            yj = rt[:, k:k + 1] * v if yj is None else yj + rt[:, k:k + 1] * v
        pieces.append(yj)
    y = jnp.concatenate(pieces, axis=1)
    h4 = h_ref[...] + g2_ref[0] * y
    ms = jnp.mean(h4 * h4, axis=-1, keepdims=True)
    o_ref[...] = h4 * lax.rsqrt(ms + NORM_EPS) * fg_ref[...]


def _combine(ys8, dest, h2d, rt, g2, final_g, seq):
    n, d = h2d.shape
    tc = _divisor(seq, 512, DMA_UNROLL)
    tpb = seq // tc
    ns = n // tc
    dest3 = dest.reshape(ns, 1, TOP_K * tc)
    smem_tile = lambda imap: pl.BlockSpec((1, 1, TOP_K * tc), imap, memory_space=pltpu.SMEM)
    return pl.pallas_call(
        _combine_kernel,
        grid=(ns,),
        in_specs=[
            smem_tile(lambda i: (jnp.minimum(i + 1, ns - 1), 0, 0)),
            smem_tile(lambda i: (0, 0, 0)),
            pl.BlockSpec(memory_space=pl.ANY),
            pl.BlockSpec((tc, d), lambda i: (i, 0)),
            pl.BlockSpec((tc, LANES), lambda i: (i, 0)),
            pl.BlockSpec((1, 1, d), lambda i: (i // tpb, 0, 0)),
            _resident((1, d)),
        ],
        out_specs=pl.BlockSpec((tc, d), lambda i: (i, 0)),
        out_shape=jax.ShapeDtypeStruct((n, d), F32),
        scratch_shapes=[pltpu.VMEM((2 * TOP_K * tc * TOKEN_ROWS, LANES), F32), pltpu.SemaphoreType.DMA((2,))],
        compiler_params=_params(("arbitrary",)),
        name="moe_combine_norm",
    )(dest3, dest3, ys8, h2d, rt, g2, final_g.reshape(1, d))


def _route(rt, n_experts, tm):
    n = rt.shape[0]
    npairs = n * TOP_K
    pair_expert = rt[:, 2:2 + TOP_K].astype(jnp.int32).reshape(npairs)
    onehot = (pair_expert[:, None] == jnp.arange(n_experts, dtype=jnp.int32)[None, :]).astype(jnp.int32)
    csum = jnp.cumsum(onehot, axis=0)
    rank = jnp.sum(onehot * csum, axis=1) - 1
    counts = csum[-1]
    tiles = (counts + tm - 1) // tm
    tile_end = jnp.cumsum(tiles)
    tile_start = tile_end - tiles
    dest = jnp.sum(onehot * tile_start[None, :], axis=1) * tm + rank
    n_tiles = npairs // tm + n_experts
    ti = jnp.arange(n_tiles, dtype=jnp.int32)
    n_active = tile_end[-1]
    tile_valid = (ti < n_active).astype(jnp.int32)
    last_used = jnp.minimum(ti, n_active - 1)
    expert_of = jnp.sum((last_used[:, None] >= tile_end[None, :]).astype(jnp.int32), axis=1)
    tile_expert = jnp.minimum(expert_of, n_experts - 1)
    order_tok = (jnp.sort(pair_expert * npairs + jnp.arange(npairs, dtype=jnp.int32)) % npairs) // TOP_K
    nrows = n_tiles * tm
    count_start = jnp.cumsum(counts) - counts
    padded = jnp.concatenate([jnp.zeros((nrows,), jnp.int32), order_tok, jnp.zeros((nrows,), jnp.int32)])
    row = jnp.arange(nrows, dtype=jnp.int32)
    src_tok = jnp.zeros((nrows,), jnp.int32)
    for e in range(n_experts):
        first_row = tile_start[e] * tm
        shifted = lax.dynamic_slice(padded, (nrows + count_start[e] - first_row,), (nrows,))
        mine = jnp.logical_and(row >= first_row, row < first_row + counts[e])
        src_tok = jnp.where(mine, shifted, src_tok)
    return dest.astype(jnp.int32), src_tok.reshape(n_tiles, 1, tm), tile_expert, tile_valid


def kernel(x, c, mod_w, mod_b, norm1_g, norm2_g, conv_w_pw1, conv_b_pw1, conv_w_dw, conv_b_dw, conv_ln_g, conv_ln_b, conv_w_pw2, conv_b_pw2, ssm_a_re, ssm_a_im, ssm_log_dt, ssm_b_re, ssm_b_im, ssm_c_re, ssm_c_im, ssm_d, ssm_w_glu, ssm_b_glu, ffn_w_gate, ffn_w_up, ffn_w_down, moe_w_router, moe_b_router, moe_w_gate, moe_w_up, moe_w_down, final_norm_g):
    nb, seq, d = x.shape
    n = nb * seq
    assert mod_w.shape[0] == 2 and seq % S5_CHUNK == 0 and d == TOKEN_ROWS * LANES
    n_experts = moe_w_router.shape[-1]

    mod = _modulation(c, mod_w, mod_b)
    parts = [[mod[i, :, k * d:(k + 1) * d].reshape(nb, 1, d) for k in range(6)] for i in range(2)]
    sh1a, sc1a, g1a, sh2a, sc2a, g2a = parts[0]
    sh1b, sc1b, g1b, sh2b, sc2b, g2b = parts[1]

    u, w_pw2, ffn_wg, ffn_wu, ffn_wd, w_glu = _pw1(
        x.reshape(n, d), norm1_g[0], sc1a, sh1a, conv_w_pw1[0], conv_b_pw1[0], seq,
        [conv_w_pw2[0], ffn_w_gate[0], ffn_w_up[0], ffn_w_down[0], ssm_w_glu[0]])
    h1, moe_wg, moe_wu = _conv_block(u.reshape(nb, seq, d), x, conv_w_dw[0], conv_b_dw[0], conv_ln_g[0],
                                     conv_ln_b[0], w_pw2, conv_b_pw2[0], g1a, moe_w_gate[0], moe_w_up[0])
    h2, u1, moe_wd = _dense_ffn(h1.reshape(n, d), norm2_g[0], sc2a, sh2a, g2a, ffn_wg, ffn_wu, ffn_wd,
                                norm1_g[1], sc1b, sh1b, seq, moe_w_down[0])

    tables = _s5_tables(ssm_a_re[0], ssm_a_im[0], ssm_log_dt[0], ssm_b_re[0], ssm_b_im[0],
                        ssm_c_re[0], ssm_c_im[0], ssm_d[0])
    y1 = _s5_mix(u1, tables, nb)
    h3, t8, rt = _glu_router(y1, w_glu, ssm_b_glu[0], h2, g1b, norm2_g[1], sc2b, sh2b,
                             moe_w_router[0], moe_b_router[0], seq)
    tm = 512
    dest, src_tok, tile_expert, tile_valid = _route(rt, n_experts, tm)
    xs = _dispatch(t8, src_tok)
    ys8 = _experts_rows(xs, tile_expert, tile_valid, moe_wg, moe_wu, moe_wd, tm)
    out = _combine(ys8, dest, h3, rt, g2b, final_norm_g, seq)
    return out.reshape(nb, seq, d)
```
